```python
import math
import jax, jax.numpy as jnp
from jax import lax
import numpy as np

D_MODEL = 1024
BATCH = 8
SEQ = 4096
DEPTH = 4

CHUNK = 64
Q_BLOCK = 128
EPS = 1e-6
NEG_INF = -1e30
D_MIX = D_MODEL
HEAD_DIM = 64
W_SSM = D_MIX // 4
W_SB = D_MIX // 4
W_CH = D_MIX // 4
W_DF = D_MIX - W_SSM - W_SB - W_CH
SSM_GROUP = 16
SSM_GROUPS = W_SSM // SSM_GROUP
SSM_STATE = 64
DT_MIN = 1e-3
DT_MAX = 1e-1
H_SB = W_SB // HEAD_DIM
H_CH = W_CH // HEAD_DIM
H_DF = W_DF // HEAD_DIM
DF_QK_DIM = HEAD_DIM // 2
CH_LEFT_CHUNKS = 8
CH_BAND = CH_LEFT_CHUNKS + 1
REL_CLIP = 128
N_MIX_HEADS = D_MIX // HEAD_DIM
IN_COLS = W_SSM + 3 * W_SB + 3 * W_CH + 3 * W_DF
D_FF = 11 * D_MODEL // 4
N_EXPERTS = 8
TOP_K = 2
D_FF_EXPERT = 7 * D_MODEL // 2
N_DENSE = (DEPTH + 1) // 2
N_MOE = DEPTH // 2

kernel_name = 'hybrid_parallel_heads_streaming_block'


def rms_norm(x, gain):
    xf = x.astype(jnp.float32)
    y = xf * lax.rsqrt(jnp.mean(xf * xf, axis=-1, keepdims=True) + EPS)
    return (y * gain.astype(jnp.float32)).astype(x.dtype)


def split_columns(proj):
    sizes = (W_SSM, W_SB, W_SB, W_SB, W_CH, W_CH, W_CH, W_DF, W_DF, W_DF)
    parts, off = [], 0
    for w in sizes:
        parts.append(proj[..., off:off + w])
        off += w
    return parts


def s5_mixer(u, lam_re, lam_im, log_dt, b_re, b_im, c_re, c_im, d_skip, w_glu, b_glu):
    bsz, seq, _ = u.shape
    f32 = jnp.float32
    uf = u.astype(f32).reshape(bsz, seq, SSM_GROUPS, SSM_GROUP)
    lam = lax.complex(lam_re.astype(f32), lam_im.astype(f32))
    dt = jnp.exp(log_dt.astype(f32))[:, None]
    lam_bar = jnp.exp(lam * dt)
    b_mat = lax.complex(b_re.astype(f32), b_im.astype(f32))
    c_mat = lax.complex(c_re.astype(f32), c_im.astype(f32))
    b_bar = ((lam_bar - 1.0) / lam)[..., None] * b_mat
    bu = jnp.einsum('bsgp,gnp->sbgn', uf.astype(jnp.complex64), b_bar)
    a = jnp.broadcast_to(lam_bar, (seq, 1) + lam_bar.shape)

    def combine(e1, e2):
        a1, b1 = e1
        a2, b2 = e2
        return a1 * a2, a2 * b1 + b2

    _, states = lax.associative_scan(combine, (a, bu), axis=0)
    y = jnp.einsum('sbgn,gpn->bsgp', states, c_mat).real + d_skip.astype(f32).reshape(SSM_GROUPS, SSM_GROUP) * uf
    y = jax.nn.gelu(y.reshape(bsz, seq, W_SSM))
    y = y * jax.nn.sigmoid(y @ w_glu + b_glu)
    return y.astype(u.dtype)


def stick_breaking_attention(q, k, v):
    bsz, seq, h, d = q.shape
    nb = seq // Q_BLOCK
    scale = d ** -0.5
    q_blocks = jnp.moveaxis(q.reshape(bsz, nb, Q_BLOCK, h, d), 1, 0)
    key_pos = jnp.arange(seq)

    def block(args):
        qb, bi = args
        q_pos = bi * Q_BLOCK + jnp.arange(Q_BLOCK)
        z = jnp.einsum('bqhd,bkhd->bhqk', qb, k).astype(jnp.float32) * scale
        before = key_pos[None, :] < q_pos[:, None]
        log_1m = jnp.where(before, jax.nn.log_sigmoid(-z), 0.0)
        between = lax.cumsum(log_1m, axis=3, reverse=True) - log_1m
        w = jnp.where(before, jnp.exp(jax.nn.log_sigmoid(z) + between), 0.0)
        return jnp.einsum('bhqk,bkhd->bqhd', w.astype(v.dtype), v)

    out = lax.map(block, (q_blocks, jnp.arange(nb)))
    return jnp.moveaxis(out, 0, 1).reshape(bsz, seq, h * d)


def chunked_band_attention(q, k, v, rel_bias):
    bsz, seq, h, d = q.shape
    nc = seq // CHUNK
    scale = d ** -0.5
    qc = q.reshape(bsz, nc, CHUNK, h, d)
    pad = ((0, 0), (CH_LEFT_CHUNKS * CHUNK, 0), (0, 0), (0, 0))
    kp = jnp.pad(k, pad).reshape(bsz, nc + CH_LEFT_CHUNKS, CHUNK, h, d)
    vp = jnp.pad(v, pad).reshape(bsz, nc + CH_LEFT_CHUNKS, CHUNK, h, d)
    band_idx = jnp.arange(nc)[:, None] + jnp.arange(CH_BAND)[None, :]
    kb = kp[:, band_idx].reshape(bsz, nc, CH_BAND * CHUNK, h, d)
    vb = vp[:, band_idx].reshape(bsz, nc, CH_BAND * CHUNK, h, d)
    key_valid = jnp.repeat(band_idx >= CH_LEFT_CHUNKS, CHUNK, axis=1)
    rel = CH_LEFT_CHUNKS * CHUNK + jnp.arange(CHUNK)[:, None] - jnp.arange(CH_BAND * CHUNK)[None, :]
    bias = rel_bias.astype(jnp.float32)[:, jnp.clip(rel, -REL_CLIP, REL_CLIP) + REL_CLIP]
    s = jnp.einsum('bcqhd,bckhd->bhcqk', qc, kb).astype(jnp.float32) * scale + bias[:, None]
    s = jnp.where(key_valid[:, None, :], s, NEG_INF)
    p = jax.nn.softmax(s, axis=-1)
    out = jnp.einsum('bhcqk,bckhd->bcqhd', p.astype(v.dtype), vb)
    return out.reshape(bsz, seq, h * d)


def differential_attention(q, k, v, lam, slopes):
    bsz, seq, h, _, dq = q.shape
    nb = seq // Q_BLOCK
    scale = dq ** -0.5
    q_blocks = jnp.moveaxis(q.reshape(bsz, nb, Q_BLOCK, h, 2, dq), 1, 0)
    key_pos = jnp.arange(seq)

    def block(args):
        qb, bi = args
        q_pos = bi * Q_BLOCK + jnp.arange(Q_BLOCK)
        s = jnp.einsum('bqhid,bkhid->ibhqk', qb, k).astype(jnp.float32) * scale
        dist = jnp.abs(q_pos[:, None] - key_pos[None, :]).astype(jnp.float32)
        s = s - slopes[:, None, None] * dist
        allowed = (key_pos[None, :] // CHUNK) <= (q_pos[:, None] // CHUNK)
        p = jax.nn.softmax(jnp.where(allowed, s, NEG_INF), axis=-1)
        w = p[0] - lam * p[1]
        return jnp.einsum('bhqk,bkhd->bqhd', w.astype(v.dtype), v)

    out = lax.map(block, (q_blocks, jnp.arange(nb)))
    return jnp.moveaxis(out, 0, 1).reshape(bsz, seq, h * v.shape[-1])


def swiglu(h, w1, w3, w2):
    return (jax.nn.silu(h @ w1) * (h @ w3)) @ w2


def moe_ffn(h, w_router, w1, w3, w2):
    bsz, seq, dm = h.shape
    tok = h.reshape(bsz * seq, dm)
    logits = (tok @ w_router).astype(jnp.float32)
    top_vals, top_idx = lax.top_k(logits, TOP_K)
    gates = jax.nn.softmax(top_vals, axis=-1)
    dense_gate = jnp.sum(jax.nn.one_hot(top_idx, N_EXPERTS, dtype=jnp.float32) * gates[..., None], axis=1)
    out = jnp.zeros_like(tok)
    for e in range(N_EXPERTS):
        out = out + dense_gate[:, e:e + 1].astype(tok.dtype) * swiglu(tok, w1[e], w3[e], w2[e])
    return out.reshape(bsz, seq, dm)


def setup_inputs(seed: int = 0) -> dict:
    key = jax.random.key(seed)
    ks = iter(jax.random.split(key, 40))
    f32 = jnp.float32

    def nrm(shape, scale):
        return scale * jax.random.normal(next(ks), shape, f32)

    def gain(shape):
        return 1.0 + 0.05 * jax.random.normal(next(ks), shape, f32)

    res_scale = (2.0 * DEPTH) ** -0.5
    return {
        'x': nrm((BATCH, SEQ, D_MODEL), 1.0),
        'norm_mix_g': gain((DEPTH, D_MODEL)),
        'w_in': nrm((DEPTH, D_MODEL, IN_COLS), D_MODEL ** -0.5),
        'ssm_lam_re': -0.5 + nrm((DEPTH, SSM_GROUPS, SSM_STATE), 0.01),
        'ssm_lam_im': jnp.pi * jnp.arange(SSM_STATE, dtype=f32) + nrm((DEPTH, SSM_GROUPS, SSM_STATE), 0.01),
        'ssm_log_dt': jax.random.uniform(next(ks), (DEPTH, SSM_GROUPS), f32, math.log(DT_MIN), math.log(DT_MAX)),
        'ssm_b_re': nrm((DEPTH, SSM_GROUPS, SSM_STATE, SSM_GROUP), (2.0 * SSM_GROUP) ** -0.5),
        'ssm_b_im': nrm((DEPTH, SSM_GROUPS, SSM_STATE, SSM_GROUP), (2.0 * SSM_GROUP) ** -0.5),
        'ssm_c_re': nrm((DEPTH, SSM_GROUPS, SSM_GROUP, SSM_STATE), (2.0 * SSM_STATE) ** -0.5),
        'ssm_c_im': nrm((DEPTH, SSM_GROUPS, SSM_GROUP, SSM_STATE), (2.0 * SSM_STATE) ** -0.5),
        'ssm_d': nrm((DEPTH, W_SSM), 1.0),
        'ssm_w_glu': nrm((DEPTH, W_SSM, W_SSM), W_SSM ** -0.5),
        'ssm_b_glu': nrm((DEPTH, W_SSM), 0.02),
        'ch_q_norm_g': gain((DEPTH, HEAD_DIM)),
        'ch_k_norm_g': gain((DEPTH, HEAD_DIM)),
        'ch_rel_bias': nrm((DEPTH, H_CH, 2 * REL_CLIP + 1), 0.1),
        'df_q_norm_g': gain((DEPTH, 2, DF_QK_DIM)),
        'df_k_norm_g': gain((DEPTH, 2, DF_QK_DIM)),
        'df_lambda': nrm((DEPTH, 4, DF_QK_DIM), 0.1),
        'out_norm_g': gain((DEPTH, D_MIX)),
        'w_out': nrm((DEPTH, D_MIX, D_MODEL), D_MIX ** -0.5 * res_scale),
        'norm_ffn_g': gain((DEPTH, D_MODEL)),
        'ffn_w1': nrm((N_DENSE, D_MODEL, D_FF), D_MODEL ** -0.5),
        'ffn_w3': nrm((N_DENSE, D_MODEL, D_FF), D_MODEL ** -0.5),
        'ffn_w2': nrm((N_DENSE, D_FF, D_MODEL), D_FF ** -0.5 * res_scale),
        'moe_router': nrm((N_MOE, D_MODEL, N_EXPERTS), D_MODEL ** -0.5),
        'moe_w1': nrm((N_MOE, N_EXPERTS, D_MODEL, D_FF_EXPERT), D_MODEL ** -0.5),
        'moe_w3': nrm((N_MOE, N_EXPERTS, D_MODEL, D_FF_EXPERT), D_MODEL ** -0.5),
        'moe_w2': nrm((N_MOE, N_EXPERTS, D_FF_EXPERT, D_MODEL), D_FF_EXPERT ** -0.5 * res_scale),
    }


def reference(x, norm_mix_g, w_in, ssm_lam_re, ssm_lam_im, ssm_log_dt, ssm_b_re, ssm_b_im, ssm_c_re, ssm_c_im,
              ssm_d, ssm_w_glu, ssm_b_glu, ch_q_norm_g, ch_k_norm_g, ch_rel_bias, df_q_norm_g, df_k_norm_g,
              df_lambda, out_norm_g, w_out, norm_ffn_g, ffn_w1, ffn_w3, ffn_w2, moe_router, moe_w1, moe_w3, moe_w2):
    bsz, seq, _ = x.shape
    slopes = jnp.exp2(-8.0 * jnp.arange(1, H_DF + 1, dtype=jnp.float32) / H_DF)

    def heads(t, n):
        return t.reshape(bsz, seq, n, HEAD_DIM)

    for layer in range(DEPTH):
        h = rms_norm(x, norm_mix_g[layer])
        proj = h @ w_in[layer]
        u, q_sb, k_sb, v_sb, q_ch, k_ch, v_ch, q_df, k_df, v_df = split_columns(proj)

        y_ssm = s5_mixer(u, ssm_lam_re[layer], ssm_lam_im[layer], ssm_log_dt[layer], ssm_b_re[layer],
                         ssm_b_im[layer], ssm_c_re[layer], ssm_c_im[layer], ssm_d[layer],
                         ssm_w_glu[layer], ssm_b_glu[layer])

        y_sb = stick_breaking_attention(heads(q_sb, H_SB), heads(k_sb, H_SB), heads(v_sb, H_SB))

        q_c = rms_norm(heads(q_ch, H_CH), ch_q_norm_g[layer])
        k_c = rms_norm(heads(k_ch, H_CH), ch_k_norm_g[layer])
        y_ch = chunked_band_attention(q_c, k_c, heads(v_ch, H_CH), ch_rel_bias[layer])

        q_d = rms_norm(q_df.reshape(bsz, seq, H_DF, 2, DF_QK_DIM), df_q_norm_g[layer])
        k_d = rms_norm(k_df.reshape(bsz, seq, H_DF, 2, DF_QK_DIM), df_k_norm_g[layer])
        lambda_init = 0.8 - 0.6 * math.exp(-0.3 * layer)
        lam_p = df_lambda[layer].astype(jnp.float32)
        lam = jnp.exp(jnp.sum(lam_p[0] * lam_p[1])) - jnp.exp(jnp.sum(lam_p[2] * lam_p[3])) + lambda_init
        y_df = differential_attention(q_d, k_d, heads(v_df, H_DF), lam, slopes)

        mix = jnp.concatenate([y_ssm, y_sb, y_ch, y_df], axis=-1).reshape(bsz, seq, N_MIX_HEADS, HEAD_DIM)
        head_scale = jnp.concatenate([jnp.ones((N_MIX_HEADS - H_DF,), jnp.float32),
                                      jnp.full((H_DF,), 1.0 - lambda_init, jnp.float32)])
        mix = rms_norm(mix, out_norm_g[layer].reshape(N_MIX_HEADS, HEAD_DIM)) * head_scale[:, None].astype(x.dtype)
        x = x + mix.reshape(bsz, seq, D_MIX) @ w_out[layer]

        h2 = rms_norm(x, norm_ffn_g[layer])
        idx = layer // 2
        if layer % 2 == 0:
            x = x + swiglu(h2, ffn_w1[idx], ffn_w3[idx], ffn_w2[idx])
        else:
            x = x + moe_ffn(h2, moe_router[idx], moe_w1[idx], moe_w3[idx], moe_w2[idx])
    return x
```

```python
import functools
import math

import jax
import jax.numpy as jnp
from jax import lax
from jax.experimental import pallas as pl
from jax.experimental.pallas import tpu as pltpu

F32 = jnp.float32
BF16 = jnp.bfloat16

EPS = 1e-6
NEG_INF = -1e30
HEAD_DIM = 64
GROUP_W = 256
CHUNK = 64
SSM_GROUP = 16
SSM_STATE = 64
DF_QK_DIM = 32
CH_LEFT_CHUNKS = 8
REL_CLIP = 128
N_EXPERTS = 8
LOG2E = 1.4426950408889634

VMEM_LIMIT_BYTES = 56 * 1024 * 1024
SUBLANES = 8
LANES = 128

TM_PROJ = 512
TQ_ATT = 128
TK_ATT = 128
SSM_STEPS = 64
TM_EXPERT = 512
TF_EXPERT = 896


def _cparams(sem):
    return pltpu.CompilerParams(dimension_semantics=sem, vmem_limit_bytes=VMEM_LIMIT_BYTES)


def _dot(a, b):
    return jnp.dot(a, b, preferred_element_type=F32)


def _dot_nt(a, b):
    return lax.dot_general(a, b, (((1,), (1,)), ((), ())), preferred_element_type=F32)


def _split_dot(v, m):
    hi = v.astype(BF16)
    lo = (v - hi.astype(F32)).astype(BF16)
    return _dot(hi, m) + _dot(lo, m)


def _group_ms(v, ind, inv_size):
    return _split_dot(v * v, ind) * inv_size


def _block_indicator(width, group):
    r = jnp.arange(width)[:, None] // group
    c = jnp.arange(width)[None, :] // group
    return (r == c).astype(BF16)


def _in_proj_kernel(x_ref, g_ref, w_ref, qkg_ref, ind64_ref, ind32_ref, u_ref, qkv_ref):
    x = x_ref[...]
    ms = jnp.mean(x * x, axis=-1, keepdims=True)
    h = (x * lax.rsqrt(ms + EPS) * g_ref[...]).astype(BF16)

    def cols(c):
        return _dot(h, w_ref[:, c * GROUP_W:(c + 1) * GROUP_W])

    def put(c, val):
        qkv_ref[:, (c - 1) * GROUP_W:c * GROUP_W] = val.astype(BF16)

    u_ref[...] = cols(0)
    put(1, cols(1) * 0.125)
    put(2, cols(2))
    put(3, cols(3))
    for c, gi in ((4, 0), (5, 1)):
        a = cols(c)
        ms_h = _group_ms(a, ind64_ref[...], 1.0 / HEAD_DIM)
        put(c, a * lax.rsqrt(ms_h + EPS) * qkg_ref[gi:gi + 1, :])
    put(6, cols(6))
    for c, gi in ((7, 2), (8, 3)):
        a = cols(c)
        ms_h = _group_ms(a, ind32_ref[...], 1.0 / DF_QK_DIM)
        put(c, a * lax.rsqrt(ms_h + EPS) * qkg_ref[gi:gi + 1, :])
    put(9, cols(9))


def _in_proj(x2, g, w, qkg, ind64, ind32, bsz, seq):
    n, dm = x2.shape
    tm = TM_PROJ
    ns = seq // tm
    ncols = w.shape[1]
    const = lambda i: (0, 0)
    return pl.pallas_call(
        _in_proj_kernel,
        out_shape=(jax.ShapeDtypeStruct((seq, bsz * GROUP_W), F32),
                   jax.ShapeDtypeStruct((n, ncols - GROUP_W), BF16)),
        grid=(n // tm,),
        in_specs=[
            pl.BlockSpec((tm, dm), lambda i: (i, 0)),
            pl.BlockSpec((1, dm), const),
            pl.BlockSpec((dm, ncols), const),
            pl.BlockSpec((4, GROUP_W), const),
            pl.BlockSpec((GROUP_W, GROUP_W), const),
            pl.BlockSpec((GROUP_W, GROUP_W), const),
        ],
        out_specs=(
            pl.BlockSpec((tm, GROUP_W), lambda i: (i % ns, i // ns)),
            pl.BlockSpec((tm, ncols - GROUP_W), lambda i: (i, 0)),
        ),
        compiler_params=_cparams(("arbitrary",)),
        name="in_proj",
    )(x2, g, w, qkg, ind64, ind32)


def _s5_kernel(u_ref, wb_ref, lre_ref, lim_ref, wc_ref, d_ref, wg_ref, bg_ref, y_ref, bu_ref, st_ref,
               *, steps, nstate):
    @pl.when(pl.program_id(0) == 0)
    def _():
        st_ref[...] = jnp.zeros_like(st_ref)

    u = u_ref[...]
    bu_ref[...] = _dot(u.astype(BF16), wb_ref[...])
    lre = lre_ref[...]
    lim = lim_ref[...]

    def step(t, carry):
        sre, sim = carry
        r = pl.multiple_of(t * SUBLANES, SUBLANES)
        bre = bu_ref[pl.ds(r, SUBLANES), 0:nstate]
        bim = bu_ref[pl.ds(r, SUBLANES), nstate:2 * nstate]
        nre = lre * sre - lim * sim + bre
        nim = lre * sim + lim * sre + bim
        bu_ref[pl.ds(r, SUBLANES), 0:nstate] = nre
        bu_ref[pl.ds(r, SUBLANES), nstate:2 * nstate] = nim
        return nre, nim

    sre, sim = lax.fori_loop(0, steps, step, (st_ref[:, 0:nstate], st_ref[:, nstate:2 * nstate]))
    st_ref[:, 0:nstate] = sre
    st_ref[:, nstate:2 * nstate] = sim

    y = _dot(bu_ref[...].astype(BF16), wc_ref[...]) + d_ref[...] * u
    y = jax.nn.gelu(y)
    gate = jax.nn.sigmoid(_dot(y.astype(BF16), wg_ref[...]) + bg_ref[...])
    y_ref[...] = (y * gate).astype(y_ref.dtype)


def _s5(u_tm, wb, lre, lim, wc, d, wg, bg, bsz):
    rows, w = u_tm.shape
    assert bsz == SUBLANES
    blk = SSM_STEPS * bsz
    nstate2 = wb.shape[1]
    const = lambda c: (0, 0)
    return pl.pallas_call(
        functools.partial(_s5_kernel, steps=SSM_STEPS, nstate=nstate2 // 2),
        out_shape=jax.ShapeDtypeStruct((rows, w), BF16),
        grid=(rows // blk,),
        in_specs=[
            pl.BlockSpec((blk, w), lambda c: (c, 0)),
            pl.BlockSpec((w, nstate2), const),
            pl.BlockSpec((bsz, nstate2 // 2), const),
            pl.BlockSpec((bsz, nstate2 // 2), const),
            pl.BlockSpec((nstate2, w), const),
            pl.BlockSpec((1, w), const),
            pl.BlockSpec((w, w), const),
            pl.BlockSpec((1, w), const),
        ],
        out_specs=pl.BlockSpec((blk, w), lambda c: (c, 0)),
        scratch_shapes=[pltpu.VMEM((blk, nstate2), F32), pltpu.VMEM((bsz, nstate2), F32)],
        compiler_params=_cparams(("arbitrary",)),
        name="s5_mixer",
    )(u_tm, wb, lre, lim, wc, d, wg, bg)


def _s5_params(lam_re, lam_im, log_dt, b_re, b_im, c_re, c_im, bsz):
    g, n = lam_re.shape
    dt = jnp.exp(log_dt.astype(F32))[:, None]
    xr = lam_re * dt
    th = lam_im * dt
    er = jnp.exp(xr)
    lbr = er * jnp.cos(th)
    lbi = er * jnp.sin(th)
    ar = jnp.expm1(xr) * jnp.cos(th) - 2.0 * jnp.sin(0.5 * th) ** 2
    ai = lbi
    den = lam_re * lam_re + lam_im * lam_im
    fr = (ar * lam_re + ai * lam_im) / den
    fi = (ai * lam_re - ar * lam_im) / den
    bbr = fr[..., None] * b_re - fi[..., None] * b_im
    bbi = fr[..., None] * b_im + fi[..., None] * b_re
    eye = jnp.eye(g, dtype=F32)
    p = b_re.shape[-1]
    wb_re = jnp.einsum("gnp,gh->gphn", bbr, eye).reshape(g * p, g * n)
    wb_im = jnp.einsum("gnp,gh->gphn", bbi, eye).reshape(g * p, g * n)
    wb = jnp.concatenate([wb_re, wb_im], axis=1).astype(BF16)
    wc_re = jnp.einsum("gpn,gh->gnhp", c_re, eye).reshape(g * n, g * p)
    wc_im = jnp.einsum("gpn,gh->gnhp", c_im, eye).reshape(g * n, g * p)
    wc = jnp.concatenate([wc_re, -wc_im], axis=0).astype(BF16)
    lre = jnp.broadcast_to(lbr.reshape(1, g * n), (bsz, g * n))
    lim = jnp.broadcast_to(lbi.reshape(1, g * n), (bsz, g * n))
    return wb, lre, lim, wc


def _iota2(shape, dim):
    return lax.broadcasted_iota(jnp.int32, shape, dim)


def _sb_kernel(q_ref, k_ref, v_ref, o_ref, *, tq, tk, nheads):
    i = pl.program_id(1)
    r = _iota2((tq, tk), 0)
    c = _iota2((tq, tk), 1)
    before = c < r
    later = (_iota2((tk, tk), 0) > _iota2((tk, tk), 1)).astype(BF16)

    for h in range(nheads):
        hs = slice(h * HEAD_DIM, (h + 1) * HEAD_DIM)
        qh = q_ref[:, hs]

        def tile(j, carry, acc, diagonal):
            row0 = pl.multiple_of(j * tk, tk)
            kh = k_ref[pl.ds(row0, tk), hs]
            vh = v_ref[pl.ds(row0, tk), hs]
            z = _dot_nt(qh, kh)
            sp = jnp.maximum(z, 0.0) + jnp.log(1.0 + jnp.exp(-jnp.abs(z)))
            nl1m = jnp.where(before, sp, 0.0) if diagonal else sp
            between = _split_dot(nl1m, later) + carry
            w = jnp.exp(z - sp - between)
            if diagonal:
                w = jnp.where(before, w, 0.0)
            acc = acc + _dot(w.astype(BF16), vh)
            carry = carry + jnp.sum(nl1m, axis=-1, keepdims=True)
            return carry, acc

        carry, acc = tile(i, jnp.zeros((tq, 1), F32), jnp.zeros((tq, HEAD_DIM), F32), True)

        def body(n, ca):
            return tile(i - 1 - n, ca[0], ca[1], False)

        carry, acc = lax.fori_loop(0, i, body, (carry, acc))
        o_ref[:, hs] = acc.astype(o_ref.dtype)


def _ch_kernel(q_ref, k_ref, v_ref, bias_ref, o_ref, *, tq, tk, nheads, nback):
    i = pl.program_id(1)
    for h in range(nheads):
        hs = slice(h * HEAD_DIM, (h + 1) * HEAD_DIM)
        qh = q_ref[:, hs]

        def body(n, mla):
            m, l, acc = mla
            row0 = pl.multiple_of((i - n) * tk, tk)
            kh = k_ref[pl.ds(row0, tk), hs]
            vh = v_ref[pl.ds(row0, tk), hs]
            s = _dot_nt(qh, kh) + bias_ref[h, n]
            m_new = jnp.maximum(m, jnp.max(s, axis=-1, keepdims=True))
            alpha = jnp.exp(m - m_new)
            p = jnp.exp(s - m_new)
            l = alpha * l + jnp.sum(p, axis=-1, keepdims=True)
            acc = alpha * acc + _dot(p.astype(BF16), vh)
            return m_new, l, acc

        init = (jnp.full((tq, 1), NEG_INF, F32), jnp.zeros((tq, 1), F32), jnp.zeros((tq, HEAD_DIM), F32))
        m, l, acc = lax.fori_loop(0, jnp.minimum(i, nback) + 1, body, init)
        o_ref[:, hs] = (acc / l).astype(o_ref.dtype)


def _df_kernel(q_ref, k_ref, v_ref, lam_ref, o_ref, *, tq, tk, nheads, slopes2):
    i = pl.program_id(1)
    r = _iota2((tq, tk), 0)
    c = _iota2((tq, tk), 1)
    dmat = (r - jnp.abs(r - c)).astype(F32)
    allowed = jnp.right_shift(c, 6) <= jnp.right_shift(r, 6)
    colpos = _iota2((1, tk), 1).astype(F32)
    lane = _iota2((1, GROUP_W), 1)
    q = q_ref[...]
    lam = lam_ref[...]

    for h in range(nheads):
        hs = slice(h * HEAD_DIM, (h + 1) * HEAD_DIM)
        sl = slopes2[h]
        parts = []
        for half in range(2):
            lo = h * HEAD_DIM + half * DF_QK_DIM
            qm = jnp.where((lane >= lo) & (lane < lo + DF_QK_DIM), q, jnp.zeros_like(q))

            row_d = pl.multiple_of(i * tk, tk)
            s = _dot_nt(qm, k_ref[pl.ds(row_d, tk), :]) + sl * dmat
            s = jnp.where(allowed, s, NEG_INF)
            m = jnp.max(s, axis=-1, keepdims=True)
            p = jnp.exp2(s - m)
            l = jnp.sum(p, axis=-1, keepdims=True)
            acc = _dot(p.astype(BF16), v_ref[pl.ds(row_d, tk), hs])

            def body(n, mla, qm=qm):
                m, l, acc = mla
                j = i - 1 - n
                row0 = pl.multiple_of(j * tk, tk)
                koff = ((j - i) * tk).astype(F32)
                s = _dot_nt(qm, k_ref[pl.ds(row0, tk), :]) + sl * (colpos + koff)
                m_new = jnp.maximum(m, jnp.max(s, axis=-1, keepdims=True))
                alpha = jnp.exp2(m - m_new)
                p = jnp.exp2(s - m_new)
                l = alpha * l + jnp.sum(p, axis=-1, keepdims=True)
                acc = alpha * acc + _dot(p.astype(BF16), v_ref[pl.ds(row0, tk), hs])
                return m_new, l, acc

            m, l, acc = lax.fori_loop(0, i, body, (m, l, acc))
            parts.append(acc / l)
        o_ref[:, hs] = (parts[0] - lam * parts[1]).astype(o_ref.dtype)


def _attention_call(kernel, qkv, extra, extra_specs, col0, bsz, seq, name):
    tq = TQ_ATT
    nq = seq // tq
    n = qkv.shape[0]
    return pl.pallas_call(
        kernel,
        out_shape=jax.ShapeDtypeStruct((n, GROUP_W), BF16),
        grid=(bsz, nq),
        in_specs=[
            pl.BlockSpec((tq, GROUP_W), lambda b, i: (b * nq + i, col0)),
            pl.BlockSpec((seq, GROUP_W), lambda b, i: (b, col0 + 1)),
            pl.BlockSpec((seq, GROUP_W), lambda b, i: (b, col0 + 2)),
        ] + extra_specs,
        out_specs=pl.BlockSpec((tq, GROUP_W), lambda b, i: (b * nq + i, 0)),
        compiler_params=_cparams(("arbitrary", "arbitrary")),
        name=name,
    )(qkv, qkv, qkv, *extra)


def _ch_bias_tiles(rel_bias, tq, tk):
    nback = CH_LEFT_CHUNKS * CHUNK // tk
    n = jnp.arange(nback + 1)[:, None, None]
    r = jnp.arange(tq)[None, :, None]
    c = jnp.arange(tk)[None, None, :]
    delta = (n * tk + r) // CHUNK - c // CHUNK
    valid = (delta >= 0) & (delta <= CH_LEFT_CHUNKS)
    rel_key = (CH_LEFT_CHUNKS - delta) * CHUNK + c % CHUNK
    rel = CH_LEFT_CHUNKS * CHUNK + r % CHUNK - rel_key
    idx = jnp.clip(rel, -REL_CLIP, REL_CLIP) + REL_CLIP
    bias = rel_bias.astype(F32)[:, idx]
    return jnp.where(valid[None], bias, NEG_INF), nback


def _out_proj_kernel(ys_ref, ysb_ref, ych_ref, ydf_ref, g_ref, ind_ref, w_ref, x_ref, o_ref):
    acc = x_ref[...]
    for gi, y_ref in enumerate((ys_ref, ysb_ref, ych_ref, ydf_ref)):
        cs = slice(gi * GROUP_W, (gi + 1) * GROUP_W)
        y = y_ref[...].astype(F32)
        ms = _group_ms(y, ind_ref[...], 1.0 / HEAD_DIM)
        yn = (y * lax.rsqrt(ms + EPS) * g_ref[:, cs]).astype(BF16)
        acc = acc + _dot(yn, w_ref[cs, :])
    o_ref[...] = acc


def _out_proj(y_ssm_tm, y_sb, y_ch, y_df, g, ind64, w, x2, bsz, seq):
    n, dm = x2.shape
    tm = TM_PROJ
    ns = seq // tm
    const = lambda i: (0, 0)
    tile = pl.BlockSpec((tm, GROUP_W), lambda i: (i, 0))
    return pl.pallas_call(
        _out_proj_kernel,
        out_shape=jax.ShapeDtypeStruct((n, dm), F32),
        grid=(n // tm,),
        in_specs=[
            pl.BlockSpec((tm, GROUP_W), lambda i: (i % ns, i // ns)),
            tile, tile, tile,
            pl.BlockSpec((1, dm), const),
            pl.BlockSpec((GROUP_W, GROUP_W), const),
            pl.BlockSpec((dm, dm), const),
            pl.BlockSpec((tm, dm), lambda i: (i, 0)),
        ],
        out_specs=pl.BlockSpec((tm, dm), lambda i: (i, 0)),
        compiler_params=_cparams(("arbitrary",)),
        name="out_proj",
    )(y_ssm_tm, y_sb, y_ch, y_df, g, ind64, w, x2)


def _ffn_kernel(x_ref, g_ref, w1_ref, w3_ref, w2_ref, o_ref, hid_ref, *, tf):
    x = x_ref[...]
    ms = jnp.mean(x * x, axis=-1, keepdims=True)
    h = (x * lax.rsqrt(ms + EPS) * g_ref[...]).astype(BF16)
    dff = w1_ref.shape[1]
    for c in range(dff // tf):
        cs = slice(c * tf, (c + 1) * tf)
        a = _dot(h, w1_ref[:, cs])
        b = _dot(h, w3_ref[:, cs])
        hid_ref[:, cs] = (jax.nn.silu(a) * b).astype(BF16)
    o_ref[...] = x + _dot(hid_ref[...], w2_ref[...])


def _ffn(x2, g, w1, w3, w2):
    n, dm = x2.shape
    dff = w1.shape[1]
    tm = TM_PROJ
    const = lambda i: (0, 0)
    return pl.pallas_call(
        functools.partial(_ffn_kernel, tf=GROUP_W),
        out_shape=jax.ShapeDtypeStruct((n, dm), F32),
        grid=(n // tm,),
        in_specs=[
            pl.BlockSpec((tm, dm), lambda i: (i, 0)),
            pl.BlockSpec((1, dm), const),
            pl.BlockSpec((dm, dff), const),
            pl.BlockSpec((dm, dff), const),
            pl.BlockSpec((dff, dm), const),
        ],
        out_specs=pl.BlockSpec((tm, dm), lambda i: (i, 0)),
        scratch_shapes=[pltpu.VMEM((tm, dff), BF16)],
        compiler_params=_cparams(("arbitrary",)),
        name="ffn_dense",
    )(x2, g, w1, w3, w2)


def _router_kernel(x_ref, g_ref, wr_ref, tri_ref, h_ref, route_ref, cnt_ref, carry_ref):
    @pl.when(pl.program_id(0) == 0)
    def _():
        carry_ref[...] = jnp.zeros_like(carry_ref)

    x = x_ref[...]
    ms = jnp.mean(x * x, axis=-1, keepdims=True)
    h = x * lax.rsqrt(ms + EPS) * g_ref[...]
    h_hi = h.astype(BF16)
    h_ref[...] = h_hi
    h_lo = (h - h_hi.astype(F32)).astype(BF16)
    logits = _dot(h_hi, wr_ref[0]) + _dot(h_lo, wr_ref[0]) + _dot(h_hi, wr_ref[1])

    lane = _iota2(logits.shape, 1).astype(F32)
    lg = jnp.where(lane < N_EXPERTS, logits, -jnp.inf)
    m1 = jnp.max(lg, axis=-1, keepdims=True)
    e1 = jnp.min(jnp.where(lg == m1, lane, float(LANES)), axis=-1, keepdims=True)
    lg2 = jnp.where(lane == e1, -jnp.inf, lg)
    m2 = jnp.max(lg2, axis=-1, keepdims=True)
    e2 = jnp.min(jnp.where(lg2 == m2, lane, float(LANES)), axis=-1, keepdims=True)
    t = jnp.exp(m2 - m1)
    g1 = 1.0 / (1.0 + t)
    g2 = t / (1.0 + t)

    hot1 = lane == e1
    hot2 = lane == e2
    sel = jnp.where(hot1 | hot2, 1.0, 0.0).astype(BF16)
    prior = _dot(tri_ref[...], sel) + carry_ref[...]
    r1 = jnp.sum(jnp.where(hot1, prior, 0.0), axis=-1, keepdims=True)
    r2 = jnp.sum(jnp.where(hot2, prior, 0.0), axis=-1, keepdims=True)
    carry_ref[...] = carry_ref[...] + jnp.sum(sel.astype(F32), axis=0, keepdims=True)
    cnt_ref[...] = carry_ref[...]

    out = jnp.where(lane == 0, e1, 0.0)
    out = jnp.where(lane == 1, e2, out)
    out = jnp.where(lane == 2, g1, out)
    out = jnp.where(lane == 3, g2, out)
    out = jnp.where(lane == 4, r1, out)
    out = jnp.where(lane == 5, r2, out)
    route_ref[...] = out


def _router(x2, g, wr2, tri):
    n, dm = x2.shape
    tm = TM_PROJ
    const = lambda i: (0, 0)
    return pl.pallas_call(
        _router_kernel,
        out_shape=(jax.ShapeDtypeStruct((n, dm), BF16),
                   jax.ShapeDtypeStruct((n, LANES), F32),
                   jax.ShapeDtypeStruct((1, LANES), F32)),
        grid=(n // tm,),
        in_specs=[
            pl.BlockSpec((tm, dm), lambda i: (i, 0)),
            pl.BlockSpec((1, dm), const),
            pl.BlockSpec((2, dm, LANES), lambda i: (0, 0, 0)),
            pl.BlockSpec((tm, tm), const),
        ],
        out_specs=(
            pl.BlockSpec((tm, dm), lambda i: (i, 0)),
            pl.BlockSpec((tm, LANES), lambda i: (i, 0)),
            pl.BlockSpec((1, LANES), const),
        ),
        scratch_shapes=[pltpu.VMEM((1, LANES), F32)],
        compiler_params=_cparams(("arbitrary",)),
        name="moe_router",
    )(x2, g, wr2, tri)


def _experts_kernel(te_ref, nused_ref, xs_ref, gate_ref, w1_ref, w3_ref, w2_ref, o_ref, acc_ref):
    t = pl.program_id(0)
    c = pl.program_id(1)
    nc = pl.num_programs(1)

    @pl.when(t < nused_ref[0])
    def _():
        x = xs_ref[...]
        a = _dot(x, w1_ref[0])
        b = _dot(x, w3_ref[0])
        hid = (jax.nn.silu(a) * b).astype(BF16)
        part = _dot(hid, w2_ref[0])

        @pl.when(c == 0)
        def _():
            acc_ref[...] = part

        @pl.when(c > 0)
        def _():
            acc_ref[...] = acc_ref[...] + part

        @pl.when(c == nc - 1)
        def _():
            o_ref[...] = acc_ref[...] * gate_ref[...]

    @pl.when(jnp.logical_and(t >= nused_ref[0], c == nc - 1))
    def _():
        o_ref[...] = jnp.zeros_like(o_ref)


def _experts(tile_expert, nused, xs, gates, w1, w3, w2):
    rows, dm = xs.shape
    dff = w1.shape[2]
    tm, tf = TM_EXPERT, TF_EXPERT
    nt, nc = rows // tm, dff // tf

    def cc(t, c, nu):
        return jnp.where(t < nu[0], c, nc - 1)

    grid_spec = pltpu.PrefetchScalarGridSpec(
        num_scalar_prefetch=2,
        grid=(nt, nc),
        in_specs=[
            pl.BlockSpec((tm, dm), lambda t, c, te, nu: (t, 0)),
            pl.BlockSpec((tm, 1), lambda t, c, te, nu: (t, 0)),
            pl.BlockSpec((1, dm, tf), lambda t, c, te, nu: (te[t], 0, cc(t, c, nu))),
            pl.BlockSpec((1, dm, tf), lambda t, c, te, nu: (te[t], 0, cc(t, c, nu))),
            pl.BlockSpec((1, tf, dm), lambda t, c, te, nu: (te[t], cc(t, c, nu), 0)),
        ],
        out_specs=pl.BlockSpec((tm, dm), lambda t, c, te, nu: (t, 0)),
        scratch_shapes=[pltpu.VMEM((tm, dm), F32)],
    )
    return pl.pallas_call(
        _experts_kernel,
        out_shape=jax.ShapeDtypeStruct((rows, dm), F32),
        grid_spec=grid_spec,
        compiler_params=_cparams(("arbitrary", "arbitrary")),
        name="moe_experts",
    )(tile_expert, nused, xs, gates, w1, w3, w2)


def _moe(x2, g, w_router, w1, w3, w2, tri):
    n, dm = x2.shape
    wr = jnp.zeros((dm, LANES), F32).at[:, :N_EXPERTS].set(w_router.astype(F32))
    wr_hi = wr.astype(BF16)
    wr_lo = (wr - wr_hi.astype(F32)).astype(BF16)
    h, route, counts = _router(x2, g, jnp.stack([wr_hi, wr_lo]), tri)

    tm = TM_EXPERT
    rows = 2 * n + N_EXPERTS * tm
    nt = rows // tm
    e1 = route[:, 0].astype(jnp.int32)
    e2 = route[:, 1].astype(jnp.int32)
    cnt = counts[0, :N_EXPERTS].astype(jnp.int32)
    padded = ((cnt + tm - 1) // tm) * tm
    ends = jnp.cumsum(padded)
    starts = ends - padded
    d1 = starts[e1] + route[:, 4].astype(jnp.int32)
    d2 = starts[e2] + route[:, 5].astype(jnp.int32)
    nused = (ends[-1] // tm).astype(jnp.int32)
    tile_start = jnp.arange(nt, dtype=jnp.int32) * tm
    tile_expert = jnp.minimum(jnp.sum(tile_start[:, None] >= ends[None, :], axis=1), N_EXPERTS - 1)
    last_expert = tile_expert[jnp.maximum(nused - 1, 0)]
    tile_expert = jnp.where(jnp.arange(nt) < nused, tile_expert, last_expert).astype(jnp.int32)

    tok = jnp.arange(n, dtype=jnp.int32)
    src = jnp.zeros((rows,), jnp.int32).at[d1].set(tok).at[d2].set(tok)
    gate_rows = jnp.zeros((rows,), F32).at[d1].set(route[:, 2]).at[d2].set(route[:, 3])
    xs = jnp.take(h, src, axis=0)
    ys = _experts(tile_expert, nused.reshape(1), xs, gate_rows[:, None], w1, w3, w2)
    return x2 + jnp.take(ys, d1, axis=0) + jnp.take(ys, d2, axis=0)


def kernel(x, norm_mix_g, w_in, ssm_lam_re, ssm_lam_im, ssm_log_dt, ssm_b_re, ssm_b_im, ssm_c_re, ssm_c_im,
           ssm_d, ssm_w_glu, ssm_b_glu, ch_q_norm_g, ch_k_norm_g, ch_rel_bias, df_q_norm_g, df_k_norm_g,
           df_lambda, out_norm_g, w_out, norm_ffn_g, ffn_w1, ffn_w3, ffn_w2, moe_router, moe_w1, moe_w3, moe_w2):
    bsz, seq, dm = x.shape
    depth = w_in.shape[0]
    n = bsz * seq
    nheads = GROUP_W // HEAD_DIM
    ind64 = _block_indicator(GROUP_W, HEAD_DIM)
    ind32 = _block_indicator(GROUP_W, DF_QK_DIM)
    tri = (jnp.arange(TM_PROJ)[:, None] > jnp.arange(TM_PROJ)[None, :]).astype(BF16)
    slopes = [2.0 ** (-8.0 * (h + 1) / nheads) for h in range(nheads)]
    slopes2 = tuple(s * LOG2E for s in slopes)

    x2 = x.reshape(n, dm)
    for layer in range(depth):
        qkg = jnp.stack([
            jnp.tile(ch_q_norm_g[layer].astype(F32), nheads) * HEAD_DIM ** -0.5,
            jnp.tile(ch_k_norm_g[layer].astype(F32), nheads),
            jnp.tile(df_q_norm_g[layer].astype(F32).reshape(-1), nheads) * (DF_QK_DIM ** -0.5 * LOG2E),
            jnp.tile(df_k_norm_g[layer].astype(F32).reshape(-1), nheads),
        ])
        u_tm, qkv = _in_proj(x2, norm_mix_g[layer].reshape(1, dm), w_in[layer].astype(BF16), qkg,
                             ind64, ind32, bsz, seq)

        wb, lre, lim, wc = _s5_params(ssm_lam_re[layer], ssm_lam_im[layer], ssm_log_dt[layer],
                                      ssm_b_re[layer], ssm_b_im[layer], ssm_c_re[layer], ssm_c_im[layer], bsz)
        y_ssm = _s5(u_tm.reshape(seq * bsz, GROUP_W), wb, lre, lim, wc, ssm_d[layer].reshape(1, GROUP_W),
                    ssm_w_glu[layer].astype(BF16), ssm_b_glu[layer].reshape(1, GROUP_W), bsz)
        y_ssm = y_ssm.reshape(seq, bsz * GROUP_W)

        y_sb = _attention_call(
            functools.partial(_sb_kernel, tq=TQ_ATT, tk=TK_ATT, nheads=nheads),
            qkv, (), [], 0, bsz, seq, "sb_attention")

        bias, nback = _ch_bias_tiles(ch_rel_bias[layer], TQ_ATT, TK_ATT)
        y_ch = _attention_call(
            functools.partial(_ch_kernel, tq=TQ_ATT, tk=TK_ATT, nheads=nheads, nback=nback),
            qkv, (bias,), [pl.BlockSpec(bias.shape, lambda b, i: (0, 0, 0, 0))], 3, bsz, seq, "ch_attention")

        lambda_init = 0.8 - 0.6 * math.exp(-0.3 * layer)
        lam_p = df_lambda[layer].astype(F32)
        lam = jnp.exp(jnp.sum(lam_p[0] * lam_p[1])) - jnp.exp(jnp.sum(lam_p[2] * lam_p[3])) + lambda_init
        lam_row = jnp.full((1, HEAD_DIM), lam, F32)
        y_df = _attention_call(
            functools.partial(_df_kernel, tq=TQ_ATT, tk=TK_ATT, nheads=nheads, slopes2=slopes2),
            qkv, (lam_row,), [pl.BlockSpec((1, HEAD_DIM), lambda b, i: (0, 0))], 6, bsz, seq, "df_attention")

        head_scale = jnp.concatenate([jnp.ones((dm - GROUP_W,), F32),
                                      jnp.full((GROUP_W,), 1.0 - lambda_init, F32)])
        g_out = (out_norm_g[layer].astype(F32) * head_scale).reshape(1, dm)
        x2 = _out_proj(y_ssm, y_sb, y_ch, y_df, g_out, ind64, w_out[layer].astype(BF16), x2, bsz, seq)

        idx = layer // 2
        g_ffn = norm_ffn_g[layer].reshape(1, dm)
        if layer % 2 == 0:
            x2 = _ffn(x2, g_ffn, ffn_w1[idx].astype(BF16), ffn_w3[idx].astype(BF16), ffn_w2[idx].astype(BF16))
        else:
            x2 = _moe(x2, g_ffn, moe_router[idx], moe_w1[idx].astype(BF16), moe_w3[idx].astype(BF16),
                      moe_w2[idx].astype(BF16), tri)
    return x2.reshape(bsz, seq, dm)
```

```python
import functools
import math

import jax
import jax.numpy as jnp
from jax import lax
from jax.experimental import pallas as pl
from jax.experimental.pallas import tpu as pltpu

F32 = jnp.float32
BF16 = jnp.bfloat16

EPS = 1e-6
NEG_INF = -1e30
HEAD_DIM = 64
GROUP_W = 256
CHUNK = 64
SSM_GROUP = 16
SSM_STATE = 64
DF_QK_DIM = 32
CH_LEFT_CHUNKS = 8
REL_CLIP = 128
N_EXPERTS = 8
LOG2E = 1.4426950408889634

VMEM_LIMIT_BYTES = 56 * 1024 * 1024
SUBLANES = 8
LANES = 128

TM_PROJ = 512
TQ_ATT = 256
TK_ATT = 256
SSM_STEPS = 64
TM_EXPERT = 512
TF_EXPERT = 896


def _cparams(sem):
    return pltpu.CompilerParams(dimension_semantics=sem, vmem_limit_bytes=VMEM_LIMIT_BYTES)


def _dot(a, b):
    return jnp.dot(a, b, preferred_element_type=F32)


def _dot_nt(a, b):
    return lax.dot_general(a, b, (((1,), (1,)), ((), ())), preferred_element_type=F32)


def _split_dot(v, m):
    hi = v.astype(BF16)
    lo = (v - hi.astype(F32)).astype(BF16)
    return _dot(hi, m) + _dot(lo, m)


def _group_ms(v, ind, inv_size):
    return _split_dot(v * v, ind) * inv_size


def _block_indicator(width, group):
    r = jnp.arange(width)[:, None] // group
    c = jnp.arange(width)[None, :] // group
    return (r == c).astype(BF16)


def _in_proj_kernel(x_ref, g_ref, w_ref, qkg_ref, ind64_ref, ind32_ref, u_ref, qkv_ref):
    x = x_ref[...]
    ms = jnp.mean(x * x, axis=-1, keepdims=True)
    h = (x * lax.rsqrt(ms + EPS) * g_ref[...]).astype(BF16)

    def cols(c):
        return _dot(h, w_ref[:, c * GROUP_W:(c + 1) * GROUP_W])

    def put(c, val):
        qkv_ref[:, (c - 1) * GROUP_W:c * GROUP_W] = val.astype(BF16)

    u_ref[...] = cols(0)
    put(1, cols(1) * 0.125)
    put(2, cols(2))
    put(3, cols(3))
    for c, gi in ((4, 0), (5, 1)):
        a = cols(c)
        ms_h = _group_ms(a, ind64_ref[...], 1.0 / HEAD_DIM)
        put(c, a * lax.rsqrt(ms_h + EPS) * qkg_ref[gi:gi + 1, :])
    put(6, cols(6))
    for c, gi in ((7, 2), (8, 3)):
        a = cols(c)
        ms_h = _group_ms(a, ind32_ref[...], 1.0 / DF_QK_DIM)
        put(c, a * lax.rsqrt(ms_h + EPS) * qkg_ref[gi:gi + 1, :])
    put(9, cols(9))


def _in_proj(x2, g, w, qkg, ind64, ind32, bsz, seq):
    n, dm = x2.shape
    tm = TM_PROJ
    ns = seq // tm
    ncols = w.shape[1]
    const = lambda i: (0, 0)
    return pl.pallas_call(
        _in_proj_kernel,
        out_shape=(jax.ShapeDtypeStruct((seq, bsz * GROUP_W), F32),
                   jax.ShapeDtypeStruct((n, ncols - GROUP_W), BF16)),
        grid=(n // tm,),
        in_specs=[
            pl.BlockSpec((tm, dm), lambda i: (i, 0)),
            pl.BlockSpec((1, dm), const),
            pl.BlockSpec((dm, ncols), const),
            pl.BlockSpec((4, GROUP_W), const),
            pl.BlockSpec((GROUP_W, GROUP_W), const),
            pl.BlockSpec((GROUP_W, GROUP_W), const),
        ],
        out_specs=(
            pl.BlockSpec((tm, GROUP_W), lambda i: (i % ns, i // ns)),
            pl.BlockSpec((tm, ncols - GROUP_W), lambda i: (i, 0)),
        ),
        compiler_params=_cparams(("arbitrary",)),
        name="in_proj",
    )(x2, g, w, qkg, ind64, ind32)


def _s5_kernel(u_ref, wb_ref, lre_ref, lim_ref, wc_ref, d_ref, wg_ref, bg_ref, y_ref, bu_ref, st_ref,
               *, steps, nstate):
    @pl.when(pl.program_id(0) == 0)
    def _():
        st_ref[...] = jnp.zeros_like(st_ref)

    u = u_ref[...]
    bu_ref[...] = _dot(u.astype(BF16), wb_ref[...])
    lre = lre_ref[...]
    lim = lim_ref[...]

    def step(t, carry):
        sre, sim = carry
        r = pl.multiple_of(t * SUBLANES, SUBLANES)
        bre = bu_ref[pl.ds(r, SUBLANES), 0:nstate]
        bim = bu_ref[pl.ds(r, SUBLANES), nstate:2 * nstate]
        nre = lre * sre - lim * sim + bre
        nim = lre * sim + lim * sre + bim
        bu_ref[pl.ds(r, SUBLANES), 0:nstate] = nre
        bu_ref[pl.ds(r, SUBLANES), nstate:2 * nstate] = nim
        return nre, nim

    sre, sim = lax.fori_loop(0, steps, step, (st_ref[:, 0:nstate], st_ref[:, nstate:2 * nstate]))
    st_ref[:, 0:nstate] = sre
    st_ref[:, nstate:2 * nstate] = sim

    y = _dot(bu_ref[...].astype(BF16), wc_ref[...]) + d_ref[...] * u
    y = jax.nn.gelu(y)
    gate = jax.nn.sigmoid(_dot(y.astype(BF16), wg_ref[...]) + bg_ref[...])
    y_ref[...] = (y * gate).astype(y_ref.dtype)


def _s5(u_tm, wb, lre, lim, wc, d, wg, bg, bsz):
    rows, w = u_tm.shape
    assert bsz == SUBLANES
    blk = SSM_STEPS * bsz
    nstate2 = wb.shape[1]
    const = lambda c: (0, 0)
    return pl.pallas_call(
        functools.partial(_s5_kernel, steps=SSM_STEPS, nstate=nstate2 // 2),
        out_shape=jax.ShapeDtypeStruct((rows, w), BF16),
        grid=(rows // blk,),
        in_specs=[
            pl.BlockSpec((blk, w), lambda c: (c, 0)),
            pl.BlockSpec((w, nstate2), const),
            pl.BlockSpec((bsz, nstate2 // 2), const),
            pl.BlockSpec((bsz, nstate2 // 2), const),
            pl.BlockSpec((nstate2, w), const),
            pl.BlockSpec((1, w), const),
            pl.BlockSpec((w, w), const),
            pl.BlockSpec((1, w), const),
        ],
        out_specs=pl.BlockSpec((blk, w), lambda c: (c, 0)),
        scratch_shapes=[pltpu.VMEM((blk, nstate2), F32), pltpu.VMEM((bsz, nstate2), F32)],
        compiler_params=_cparams(("arbitrary",)),
        name="s5_mixer",
    )(u_tm, wb, lre, lim, wc, d, wg, bg)


def _s5_params(lam_re, lam_im, log_dt, b_re, b_im, c_re, c_im, bsz):
    g, n = lam_re.shape
    dt = jnp.exp(log_dt.astype(F32))[:, None]
    xr = lam_re * dt
    th = lam_im * dt
    er = jnp.exp(xr)
    lbr = er * jnp.cos(th)
    lbi = er * jnp.sin(th)
    ar = jnp.expm1(xr) * jnp.cos(th) - 2.0 * jnp.sin(0.5 * th) ** 2
    ai = lbi
    den = lam_re * lam_re + lam_im * lam_im
    fr = (ar * lam_re + ai * lam_im) / den
    fi = (ai * lam_re - ar * lam_im) / den
    bbr = fr[..., None] * b_re - fi[..., None] * b_im
    bbi = fr[..., None] * b_im + fi[..., None] * b_re
    eye = jnp.eye(g, dtype=F32)
    p = b_re.shape[-1]
    wb_re = jnp.einsum("gnp,gh->gphn", bbr, eye).reshape(g * p, g * n)
    wb_im = jnp.einsum("gnp,gh->gphn", bbi, eye).reshape(g * p, g * n)
    wb = jnp.concatenate([wb_re, wb_im], axis=1).astype(BF16)
    wc_re = jnp.einsum("gpn,gh->gnhp", c_re, eye).reshape(g * n, g * p)
    wc_im = jnp.einsum("gpn,gh->gnhp", c_im, eye).reshape(g * n, g * p)
    wc = jnp.concatenate([wc_re, -wc_im], axis=0).astype(BF16)
    lre = jnp.broadcast_to(lbr.reshape(1, g * n), (bsz, g * n))
    lim = jnp.broadcast_to(lbi.reshape(1, g * n), (bsz, g * n))
    return wb, lre, lim, wc


VA_ROWS = 2 * HEAD_DIM


def _iota2(shape, dim):
    return lax.broadcasted_iota(jnp.int32, shape, dim)


def _stack_masked_q(q_ref, qs_ref, tq, ngroups, width):
    lane = _iota2((1, GROUP_W), 1)
    q = q_ref[...]
    for g in range(ngroups):
        keep = (lane >= g * width) & (lane < (g + 1) * width)
        qs_ref[g * tq:(g + 1) * tq, :] = jnp.where(keep, q, jnp.zeros_like(q))


def _sb_kernel(q_ref, k_ref, vt_ref, later_ref, o_ref, qs_ref, carry_ref, acc_ref, *, tq, tk, nheads):
    i = pl.program_id(1)
    _stack_masked_q(q_ref, qs_ref, tq, nheads, HEAD_DIM)

    def tile(j, diagonal):
        row0 = pl.multiple_of(j * tk, tk)
        zt = _dot_nt(k_ref[pl.ds(row0, tk), :], qs_ref[...])
        if diagonal:
            before = _iota2((tk, tq), 0) < _iota2((tk, tq), 1)
        for h in range(nheads):
            hs = slice(h * HEAD_DIM, (h + 1) * HEAD_DIM)
            z = zt[:, h * tq:(h + 1) * tq]
            sp = jnp.maximum(z, 0.0) + jnp.log(1.0 + jnp.exp(-jnp.abs(z)))
            nl1m = jnp.where(before, sp, 0.0) if diagonal else sp
            hi = nl1m.astype(BF16)
            lo = (nl1m - hi.astype(F32)).astype(BF16)
            inside = _dot(later_ref[...], hi) + _dot(later_ref[...], lo)
            between = inside if diagonal else inside + carry_ref[h:h + 1, :]
            w = jnp.exp(z - sp - between)
            if diagonal:
                w = jnp.where(before, w, 0.0)
            pv = _dot(vt_ref[j, hs, :], w.astype(BF16))
            total = inside[0:1, :] + nl1m[0:1, :]
            if diagonal:
                acc_ref[hs, :] = pv
                carry_ref[h:h + 1, :] = total
            else:
                acc_ref[hs, :] = acc_ref[hs, :] + pv
                carry_ref[h:h + 1, :] = carry_ref[h:h + 1, :] + total

    tile(i, True)

    def body(n, c):
        tile(i - 1 - n, False)
        return c

    lax.fori_loop(0, i, body, 0)
    o_ref[...] = acc_ref[...].astype(o_ref.dtype)


def _softmax_tile(s, off, va, g, first, m_ref, acc_ref):
    rows = slice(g * VA_ROWS, (g + 1) * VA_ROWS)
    blk_max = jnp.max(s, axis=0, keepdims=True)
    if first:
        m_new = blk_max
        acc_ref[rows, :] = _dot(va, jnp.exp2(s - m_new).astype(BF16))
    else:
        m_old = m_ref[g:g + 1, :]
        m_new = jnp.maximum(m_old, blk_max + off)
        p = jnp.exp2(s - (m_new - off)).astype(BF16)
        acc_ref[rows, :] = jnp.exp2(m_old - m_new) * acc_ref[rows, :] + _dot(va, p)
    m_ref[g:g + 1, :] = m_new


def _normalised(acc_ref, g):
    base = g * VA_ROWS
    return acc_ref[base:base + HEAD_DIM, :] / acc_ref[base + HEAD_DIM:base + HEAD_DIM + 1, :]


def _ch_kernel(q_ref, k_ref, va_ref, bias_ref, o_ref, qs_ref, m_ref, acc_ref, *, tq, tk, nheads, nback):
    i = pl.program_id(1)
    _stack_masked_q(q_ref, qs_ref, tq, nheads, HEAD_DIM)

    def tile(n, first):
        j = i - n
        row0 = pl.multiple_of(j * tk, tk)
        st = _dot_nt(k_ref[pl.ds(row0, tk), :], qs_ref[...])
        for h in range(nheads):
            s = st[:, h * tq:(h + 1) * tq] + bias_ref[h, n]
            _softmax_tile(s, 0.0, va_ref[j, h * VA_ROWS:(h + 1) * VA_ROWS, :], h, first, m_ref, acc_ref)

    tile(0, True)

    def body(n, c):
        tile(n, False)
        return c

    lax.fori_loop(1, jnp.minimum(i, nback) + 1, body, 0)
    for h in range(nheads):
        o_ref[h * HEAD_DIM:(h + 1) * HEAD_DIM, :] = _normalised(acc_ref, h).astype(o_ref.dtype)


def _df_kernel(lam_ref, q_ref, k_ref, va_ref, kb_ref, db_ref, o_ref, qs_ref, m_ref, acc_ref,
               *, tq, tk, nheads, slopes2):
    i = pl.program_id(1)
    _stack_masked_q(q_ref, qs_ref, tq, 2 * nheads, DF_QK_DIM)

    def tile(j, diagonal):
        row0 = pl.multiple_of(j * tk, tk)
        st = _dot_nt(k_ref[pl.ds(row0, tk), :], qs_ref[...])
        for g in range(2 * nheads):
            h = g // 2
            va = va_ref[j, h * VA_ROWS:(h + 1) * VA_ROWS, :]
            if diagonal:
                _softmax_tile(st[:, g * tq:(g + 1) * tq] + db_ref[h], 0.0, va, g, True, m_ref, acc_ref)
            else:
                off = slopes2[h] * ((j - i) * tk).astype(F32)
                _softmax_tile(st[:, g * tq:(g + 1) * tq] + kb_ref[h], off, va, g, False, m_ref, acc_ref)

    tile(i, True)

    def body(n, c):
        tile(i - 1 - n, False)
        return c

    lax.fori_loop(0, i, body, 0)
    lam = lam_ref[0]
    for h in range(nheads):
        out = _normalised(acc_ref, 2 * h) - lam * _normalised(acc_ref, 2 * h + 1)
        o_ref[h * HEAD_DIM:(h + 1) * HEAD_DIM, :] = out.astype(o_ref.dtype)


def _attention_call(kernel, qkv, col0, vt, extra, extra_specs, scratch, bsz, seq, name, smem=()):
    tq, tk = TQ_ATT, TK_ATT
    assert tq == tk
    nq = seq // tq
    vrows = vt.shape[2]
    return pl.pallas_call(
        kernel,
        out_shape=jax.ShapeDtypeStruct((bsz, GROUP_W, seq), BF16),
        grid=(bsz, nq),
        in_specs=[pl.BlockSpec(memory_space=pltpu.SMEM) for _ in smem] + [
            pl.BlockSpec((tq, GROUP_W), lambda b, i: (b * nq + i, col0)),
            pl.BlockSpec((seq, GROUP_W), lambda b, i: (b, col0 + 1)),
            pl.BlockSpec((None, seq // tk, vrows, tk), lambda b, i: (b, 0, 0, 0)),
        ] + extra_specs,
        out_specs=pl.BlockSpec((None, GROUP_W, tq), lambda b, i: (b, 0, i)),
        scratch_shapes=scratch,
        compiler_params=_cparams(("arbitrary", "arbitrary")),
        name=name,
    )(*smem, qkv, qkv, vt, *extra)


def _values_t(v, bsz, seq, nheads, ones_row):
    if ones_row:
        v4 = v.reshape(bsz, seq, nheads, HEAD_DIM)
        pad = jnp.zeros((bsz, seq, nheads, VA_ROWS - HEAD_DIM - 1), v.dtype)
        v = jnp.concatenate([v4, jnp.ones((bsz, seq, nheads, 1), v.dtype), pad], axis=-1)
    rows = v.size // (bsz * seq)
    return v.reshape(bsz, seq // TK_ATT, TK_ATT, rows).transpose(0, 1, 3, 2)


def _tokens_major(o, n):
    return o.transpose(0, 2, 1).reshape(n, GROUP_W)


def _ch_bias_tiles(rel_bias, tq, tk):
    nback = CH_LEFT_CHUNKS * CHUNK // tk
    n = jnp.arange(nback + 1)[:, None, None]
    s = jnp.arange(tk)[None, :, None]
    t = jnp.arange(tq)[None, None, :]
    delta = (n * tk + t) // CHUNK - s // CHUNK
    valid = (delta >= 0) & (delta <= CH_LEFT_CHUNKS)
    rel_key = (CH_LEFT_CHUNKS - delta) * CHUNK + s % CHUNK
    rel = CH_LEFT_CHUNKS * CHUNK + t % CHUNK - rel_key
    idx = jnp.clip(rel, -REL_CLIP, REL_CLIP) + REL_CLIP
    bias = rel_bias.astype(F32)[:, idx] * LOG2E
    return jnp.where(valid[None], bias, NEG_INF), nback


def _df_bias_tiles(slopes2, tq, tk):
    s = jnp.arange(tk, dtype=jnp.int32)[:, None]
    t = jnp.arange(tq, dtype=jnp.int32)[None, :]
    sl = jnp.asarray(slopes2, F32)[:, None, None]
    kb = sl * jnp.broadcast_to(s, (tk, tq)).astype(F32)
    allowed = (s // CHUNK) <= (t // CHUNK)
    db = jnp.where(allowed[None], sl * (t - jnp.abs(t - s)).astype(F32), NEG_INF)
    return kb, db


def _out_proj_kernel(ys_ref, ysb_ref, ych_ref, ydf_ref, g_ref, ind_ref, w_ref, x_ref, o_ref):
    acc = x_ref[...]
    for gi, y_ref in enumerate((ys_ref, ysb_ref, ych_ref, ydf_ref)):
        cs = slice(gi * GROUP_W, (gi + 1) * GROUP_W)
        y = y_ref[...].astype(F32)
        ms = _group_ms(y, ind_ref[...], 1.0 / HEAD_DIM)
        yn = (y * lax.rsqrt(ms + EPS) * g_ref[:, cs]).astype(BF16)
        acc = acc + _dot(yn, w_ref[cs, :])
    o_ref[...] = acc


def _out_proj(y_ssm_tm, y_sb, y_ch, y_df, g, ind64, w, x2, bsz, seq):
    n, dm = x2.shape
    tm = TM_PROJ
    ns = seq // tm
    const = lambda i: (0, 0)
    tile = pl.BlockSpec((tm, GROUP_W), lambda i: (i, 0))
    return pl.pallas_call(
        _out_proj_kernel,
        out_shape=jax.ShapeDtypeStruct((n, dm), F32),
        grid=(n // tm,),
        in_specs=[
            pl.BlockSpec((tm, GROUP_W), lambda i: (i % ns, i // ns)),
            tile, tile, tile,
            pl.BlockSpec((1, dm), const),
            pl.BlockSpec((GROUP_W, GROUP_W), const),
            pl.BlockSpec((dm, dm), const),
            pl.BlockSpec((tm, dm), lambda i: (i, 0)),
        ],
        out_specs=pl.BlockSpec((tm, dm), lambda i: (i, 0)),
        compiler_params=_cparams(("arbitrary",)),
        name="out_proj",
    )(y_ssm_tm, y_sb, y_ch, y_df, g, ind64, w, x2)


def _ffn_kernel(x_ref, g_ref, w1_ref, w3_ref, w2_ref, o_ref, hid_ref, *, tf):
    x = x_ref[...]
    ms = jnp.mean(x * x, axis=-1, keepdims=True)
    h = (x * lax.rsqrt(ms + EPS) * g_ref[...]).astype(BF16)
    dff = w1_ref.shape[1]
    for c in range(dff // tf):
        cs = slice(c * tf, (c + 1) * tf)
        a = _dot(h, w1_ref[:, cs])
        b = _dot(h, w3_ref[:, cs])
        hid_ref[:, cs] = (jax.nn.silu(a) * b).astype(BF16)
    o_ref[...] = x + _dot(hid_ref[...], w2_ref[...])


def _ffn(x2, g, w1, w3, w2):
    n, dm = x2.shape
    dff = w1.shape[1]
    tm = TM_PROJ
    const = lambda i: (0, 0)
    return pl.pallas_call(
        functools.partial(_ffn_kernel, tf=GROUP_W),
        out_shape=jax.ShapeDtypeStruct((n, dm), F32),
        grid=(n // tm,),
        in_specs=[
            pl.BlockSpec((tm, dm), lambda i: (i, 0)),
            pl.BlockSpec((1, dm), const),
            pl.BlockSpec((dm, dff), const),
            pl.BlockSpec((dm, dff), const),
            pl.BlockSpec((dff, dm), const),
        ],
        out_specs=pl.BlockSpec((tm, dm), lambda i: (i, 0)),
        scratch_shapes=[pltpu.VMEM((tm, dff), BF16)],
        compiler_params=_cparams(("arbitrary",)),
        name="ffn_dense",
    )(x2, g, w1, w3, w2)


def _router_kernel(x_ref, g_ref, wr_ref, tri_ref, h_ref, route_ref, cnt_ref, carry_ref):
    @pl.when(pl.program_id(0) == 0)
    def _():
        carry_ref[...] = jnp.zeros_like(carry_ref)

    x = x_ref[...]
    ms = jnp.mean(x * x, axis=-1, keepdims=True)
    h = x * lax.rsqrt(ms + EPS) * g_ref[...]
    h_hi = h.astype(BF16)
    h_ref[...] = h_hi
    h_lo = (h - h_hi.astype(F32)).astype(BF16)
    logits = _dot(h_hi, wr_ref[0]) + _dot(h_lo, wr_ref[0]) + _dot(h_hi, wr_ref[1])

    lane = _iota2(logits.shape, 1).astype(F32)
    lg = jnp.where(lane < N_EXPERTS, logits, -jnp.inf)
    m1 = jnp.max(lg, axis=-1, keepdims=True)
    e1 = jnp.min(jnp.where(lg == m1, lane, float(LANES)), axis=-1, keepdims=True)
    lg2 = jnp.where(lane == e1, -jnp.inf, lg)
    m2 = jnp.max(lg2, axis=-1, keepdims=True)
    e2 = jnp.min(jnp.where(lg2 == m2, lane, float(LANES)), axis=-1, keepdims=True)
    t = jnp.exp(m2 - m1)
    g1 = 1.0 / (1.0 + t)
    g2 = t / (1.0 + t)

    hot1 = lane == e1
    hot2 = lane == e2
    sel = jnp.where(hot1 | hot2, 1.0, 0.0).astype(BF16)
    prior = _dot(tri_ref[...], sel) + carry_ref[...]
    r1 = jnp.sum(jnp.where(hot1, prior, 0.0), axis=-1, keepdims=True)
    r2 = jnp.sum(jnp.where(hot2, prior, 0.0), axis=-1, keepdims=True)
    carry_ref[...] = carry_ref[...] + jnp.sum(sel.astype(F32), axis=0, keepdims=True)
    cnt_ref[...] = carry_ref[...]

    out = jnp.where(lane == 0, e1, 0.0)
    out = jnp.where(lane == 1, e2, out)
    out = jnp.where(lane == 2, g1, out)
    out = jnp.where(lane == 3, g2, out)
    out = jnp.where(lane == 4, r1, out)
    out = jnp.where(lane == 5, r2, out)
    route_ref[...] = out


def _router(x2, g, wr2, tri):
    n, dm = x2.shape
    tm = TM_PROJ
    const = lambda i: (0, 0)
    return pl.pallas_call(
        _router_kernel,
        out_shape=(jax.ShapeDtypeStruct((n, dm), BF16),
                   jax.ShapeDtypeStruct((n, LANES), F32),
                   jax.ShapeDtypeStruct((1, LANES), F32)),
        grid=(n // tm,),
        in_specs=[
            pl.BlockSpec((tm, dm), lambda i: (i, 0)),
            pl.BlockSpec((1, dm), const),
            pl.BlockSpec((2, dm, LANES), lambda i: (0, 0, 0)),
            pl.BlockSpec((tm, tm), const),
        ],
        out_specs=(
            pl.BlockSpec((tm, dm), lambda i: (i, 0)),
            pl.BlockSpec((tm, LANES), lambda i: (i, 0)),
            pl.BlockSpec((1, LANES), const),
        ),
        scratch_shapes=[pltpu.VMEM((1, LANES), F32)],
        compiler_params=_cparams(("arbitrary",)),
        name="moe_router",
    )(x2, g, wr2, tri)


def _experts_kernel(te_ref, nused_ref, xs_ref, gate_ref, w1_ref, w3_ref, w2_ref, o_ref, acc_ref):
    t = pl.program_id(0)
    c = pl.program_id(1)
    nc = pl.num_programs(1)

    @pl.when(t < nused_ref[0])
    def _():
        x = xs_ref[...]
        a = _dot(x, w1_ref[0])
        b = _dot(x, w3_ref[0])
        hid = (jax.nn.silu(a) * b).astype(BF16)
        part = _dot(hid, w2_ref[0])

        @pl.when(c == 0)
        def _():
            acc_ref[...] = part

        @pl.when(c > 0)
        def _():
            acc_ref[...] = acc_ref[...] + part

        @pl.when(c == nc - 1)
        def _():
            o_ref[...] = acc_ref[...] * gate_ref[...]

    @pl.when(jnp.logical_and(t >= nused_ref[0], c == nc - 1))
    def _():
        o_ref[...] = jnp.zeros_like(o_ref)


def _experts(tile_expert, nused, xs, gates, w1, w3, w2):
    rows, dm = xs.shape
    dff = w1.shape[2]
    tm, tf = TM_EXPERT, TF_EXPERT
    nt, nc = rows // tm, dff // tf

    def cc(t, c, nu):
        return jnp.where(t < nu[0], c, nc - 1)

    grid_spec = pltpu.PrefetchScalarGridSpec(
        num_scalar_prefetch=2,
        grid=(nt, nc),
        in_specs=[
            pl.BlockSpec((tm, dm), lambda t, c, te, nu: (t, 0)),
            pl.BlockSpec((tm, 1), lambda t, c, te, nu: (t, 0)),
            pl.BlockSpec((1, dm, tf), lambda t, c, te, nu: (te[t], 0, cc(t, c, nu))),
            pl.BlockSpec((1, dm, tf), lambda t, c, te, nu: (te[t], 0, cc(t, c, nu))),
            pl.BlockSpec((1, tf, dm), lambda t, c, te, nu: (te[t], cc(t, c, nu), 0)),
        ],
        out_specs=pl.BlockSpec((tm, dm), lambda t, c, te, nu: (t, 0)),
        scratch_shapes=[pltpu.VMEM((tm, dm), F32)],
    )
    return pl.pallas_call(
        _experts_kernel,
        out_shape=jax.ShapeDtypeStruct((rows, dm), F32),
        grid_spec=grid_spec,
        compiler_params=_cparams(("arbitrary", "arbitrary")),
        name="moe_experts",
    )(tile_expert, nused, xs, gates, w1, w3, w2)


def _moe(x2, g, w_router, w1, w3, w2, tri):
    n, dm = x2.shape
    wr = jnp.zeros((dm, LANES), F32).at[:, :N_EXPERTS].set(w_router.astype(F32))
    wr_hi = wr.astype(BF16)
    wr_lo = (wr - wr_hi.astype(F32)).astype(BF16)
    h, route, counts = _router(x2, g, jnp.stack([wr_hi, wr_lo]), tri)

    tm = TM_EXPERT
    rows = 2 * n + N_EXPERTS * tm
    nt = rows // tm
    e1 = route[:, 0].astype(jnp.int32)
    e2 = route[:, 1].astype(jnp.int32)
    cnt = counts[0, :N_EXPERTS].astype(jnp.int32)
    padded = ((cnt + tm - 1) // tm) * tm
    ends = jnp.cumsum(padded)
    starts = ends - padded
    d1 = starts[e1] + route[:, 4].astype(jnp.int32)
    d2 = starts[e2] + route[:, 5].astype(jnp.int32)
    nused = (ends[-1] // tm).astype(jnp.int32)
    tile_start = jnp.arange(nt, dtype=jnp.int32) * tm
    tile_expert = jnp.minimum(jnp.sum(tile_start[:, None] >= ends[None, :], axis=1), N_EXPERTS - 1)
    last_expert = tile_expert[jnp.maximum(nused - 1, 0)]
    tile_expert = jnp.where(jnp.arange(nt) < nused, tile_expert, last_expert).astype(jnp.int32)

    tok = jnp.arange(n, dtype=jnp.int32)
    src = jnp.zeros((rows,), jnp.int32).at[d1].set(tok).at[d2].set(tok)
    gate_rows = jnp.zeros((rows,), F32).at[d1].set(route[:, 2]).at[d2].set(route[:, 3])
    xs = jnp.take(h, src, axis=0)
    ys = _experts(tile_expert, nused.reshape(1), xs, gate_rows[:, None], w1, w3, w2)
    return x2 + jnp.take(ys, d1, axis=0) + jnp.take(ys, d2, axis=0)


def kernel(x, norm_mix_g, w_in, ssm_lam_re, ssm_lam_im, ssm_log_dt, ssm_b_re, ssm_b_im, ssm_c_re, ssm_c_im,
           ssm_d, ssm_w_glu, ssm_b_glu, ch_q_norm_g, ch_k_norm_g, ch_rel_bias, df_q_norm_g, df_k_norm_g,
           df_lambda, out_norm_g, w_out, norm_ffn_g, ffn_w1, ffn_w3, ffn_w2, moe_router, moe_w1, moe_w3, moe_w2):
    bsz, seq, dm = x.shape
    depth = w_in.shape[0]
    n = bsz * seq
    nheads = GROUP_W // HEAD_DIM
    ind64 = _block_indicator(GROUP_W, HEAD_DIM)
    ind32 = _block_indicator(GROUP_W, DF_QK_DIM)
    tri = (jnp.arange(TM_PROJ)[:, None] > jnp.arange(TM_PROJ)[None, :]).astype(BF16)
    slopes = [2.0 ** (-8.0 * (h + 1) / nheads) for h in range(nheads)]
    slopes2 = tuple(s * LOG2E for s in slopes)
    df_kb, df_db = _df_bias_tiles(slopes2, TQ_ATT, TK_ATT)
    later = (jnp.arange(TK_ATT)[:, None] < jnp.arange(TK_ATT)[None, :]).astype(BF16)

    x2 = x.reshape(n, dm)
    for layer in range(depth):
        qkg = jnp.stack([
            jnp.tile(ch_q_norm_g[layer].astype(F32), nheads) * (HEAD_DIM ** -0.5 * LOG2E),
            jnp.tile(ch_k_norm_g[layer].astype(F32), nheads),
            jnp.tile(df_q_norm_g[layer].astype(F32).reshape(-1), nheads) * (DF_QK_DIM ** -0.5 * LOG2E),
            jnp.tile(df_k_norm_g[layer].astype(F32).reshape(-1), nheads),
        ])
        u_tm, qkv = _in_proj(x2, norm_mix_g[layer].reshape(1, dm), w_in[layer].astype(BF16), qkg,
                             ind64, ind32, bsz, seq)

        wb, lre, lim, wc = _s5_params(ssm_lam_re[layer], ssm_lam_im[layer], ssm_log_dt[layer],
                                      ssm_b_re[layer], ssm_b_im[layer], ssm_c_re[layer], ssm_c_im[layer], bsz)
        y_ssm = _s5(u_tm.reshape(seq * bsz, GROUP_W), wb, lre, lim, wc, ssm_d[layer].reshape(1, GROUP_W),
                    ssm_w_glu[layer].astype(BF16), ssm_b_glu[layer].reshape(1, GROUP_W), bsz)
        y_ssm = y_ssm.reshape(seq, bsz * GROUP_W)

        tq, tk = TQ_ATT, TK_ATT
        const3 = lambda b, i: (0, 0, 0)
        y_sb = _attention_call(
            functools.partial(_sb_kernel, tq=tq, tk=tk, nheads=nheads),
            qkv, 0, _values_t(qkv[:, 2 * GROUP_W:3 * GROUP_W], bsz, seq, nheads, False),
            (later,), [pl.BlockSpec((tk, tk), lambda b, i: (0, 0))],
            [pltpu.VMEM((nheads * tq, GROUP_W), BF16), pltpu.VMEM((SUBLANES, tq), F32),
             pltpu.VMEM((GROUP_W, tq), F32)],
            bsz, seq, "sb_attention")

        bias, nback = _ch_bias_tiles(ch_rel_bias[layer], tq, tk)
        y_ch = _attention_call(
            functools.partial(_ch_kernel, tq=tq, tk=tk, nheads=nheads, nback=nback),
            qkv, 3, _values_t(qkv[:, 5 * GROUP_W:6 * GROUP_W], bsz, seq, nheads, True),
            (bias,), [pl.BlockSpec(bias.shape, lambda b, i: (0, 0, 0, 0))],
            [pltpu.VMEM((nheads * tq, GROUP_W), BF16), pltpu.VMEM((SUBLANES, tq), F32),
             pltpu.VMEM((nheads * VA_ROWS, tq), F32)],
            bsz, seq, "ch_attention")

        lambda_init = 0.8 - 0.6 * math.exp(-0.3 * layer)
        lam_p = df_lambda[layer].astype(F32)
        lam = jnp.exp(jnp.sum(lam_p[0] * lam_p[1])) - jnp.exp(jnp.sum(lam_p[2] * lam_p[3])) + lambda_init
        y_df = _attention_call(
            functools.partial(_df_kernel, tq=tq, tk=tk, nheads=nheads, slopes2=slopes2),
            qkv, 6, _values_t(qkv[:, 8 * GROUP_W:9 * GROUP_W], bsz, seq, nheads, True),
            (df_kb, df_db), [pl.BlockSpec(df_kb.shape, const3), pl.BlockSpec(df_db.shape, const3)],
            [pltpu.VMEM((2 * nheads * tq, GROUP_W), BF16), pltpu.VMEM((2 * nheads, tq), F32),
             pltpu.VMEM((2 * nheads * VA_ROWS, tq), F32)],
            bsz, seq, "df_attention", smem=(lam.reshape(1),))
        y_sb, y_ch, y_df = (_tokens_major(y, n) for y in (y_sb, y_ch, y_df))

        head_scale = jnp.concatenate([jnp.ones((dm - GROUP_W,), F32),
                                      jnp.full((GROUP_W,), 1.0 - lambda_init, F32)])
        g_out = (out_norm_g[layer].astype(F32) * head_scale).reshape(1, dm)
        x2 = _out_proj(y_ssm, y_sb, y_ch, y_df, g_out, ind64, w_out[layer].astype(BF16), x2, bsz, seq)

        idx = layer // 2
        g_ffn = norm_ffn_g[layer].reshape(1, dm)
        if layer % 2 == 0:
            x2 = _ffn(x2, g_ffn, ffn_w1[idx].astype(BF16), ffn_w3[idx].astype(BF16), ffn_w2[idx].astype(BF16))
        else:
            x2 = _moe(x2, g_ffn, moe_router[idx], moe_w1[idx].astype(BF16), moe_w3[idx].astype(BF16),
                      moe_w2[idx].astype(BF16), tri)
    return x2.reshape(bsz, seq, dm)
```

```python
import functools
import math

import jax
import jax.numpy as jnp
from jax import lax
from jax.experimental import pallas as pl
from jax.experimental.pallas import tpu as pltpu

F32 = jnp.float32
BF16 = jnp.bfloat16

EPS = 1e-6
NEG_INF = -1e30
HEAD_DIM = 64
GROUP_W = 256
CHUNK = 64
SSM_GROUP = 16
SSM_STATE = 64
DF_QK_DIM = 32
CH_LEFT_CHUNKS = 8
REL_CLIP = 128
N_EXPERTS = 8
LOG2E = 1.4426950408889634

VMEM_LIMIT_BYTES = 56 * 1024 * 1024
SUBLANES = 8
LANES = 128

TM_PROJ = 512
TQ_ATT = 256
TK_ATT = 256
SSM_STEPS = 64
TM_EXPERT = 512
TF_EXPERT = 896


def _cparams(sem):
    return pltpu.CompilerParams(dimension_semantics=sem, vmem_limit_bytes=VMEM_LIMIT_BYTES)


def _dot(a, b):
    return jnp.dot(a, b, preferred_element_type=F32)


def _dot_nt(a, b):
    return lax.dot_general(a, b, (((1,), (1,)), ((), ())), preferred_element_type=F32)


def _split_dot(v, m):
    hi = v.astype(BF16)
    lo = (v - hi.astype(F32)).astype(BF16)
    return _dot(hi, m) + _dot(lo, m)


def _group_ms(v, ind, inv_size):
    return _split_dot(v * v, ind) * inv_size


def _block_indicator(width, group):
    r = jnp.arange(width)[:, None] // group
    c = jnp.arange(width)[None, :] // group
    return (r == c).astype(BF16)


def _in_proj_kernel(x_ref, g_ref, w_ref, qkg_ref, ind64_ref, ind32_ref, u_ref, qkv_ref):
    x = x_ref[...]
    ms = jnp.mean(x * x, axis=-1, keepdims=True)
    h = (x * lax.rsqrt(ms + EPS) * g_ref[...]).astype(BF16)

    def cols(c):
        return _dot(h, w_ref[:, c * GROUP_W:(c + 1) * GROUP_W])

    def put(c, val):
        qkv_ref[:, (c - 1) * GROUP_W:c * GROUP_W] = val.astype(BF16)

    u_ref[...] = cols(0)
    put(1, cols(1) * (HEAD_DIM ** -0.5 * LOG2E))
    put(2, cols(2))
    put(3, cols(3))
    for c, gi in ((4, 0), (5, 1)):
        a = cols(c)
        ms_h = _group_ms(a, ind64_ref[...], 1.0 / HEAD_DIM)
        put(c, a * lax.rsqrt(ms_h + EPS) * qkg_ref[gi:gi + 1, :])
    put(6, cols(6))
    for c, gi in ((7, 2), (8, 3)):
        a = cols(c)
        ms_h = _group_ms(a, ind32_ref[...], 1.0 / DF_QK_DIM)
        put(c, a * lax.rsqrt(ms_h + EPS) * qkg_ref[gi:gi + 1, :])
    put(9, cols(9))


def _in_proj(x2, g, w, qkg, ind64, ind32, bsz, seq):
    n, dm = x2.shape
    tm = TM_PROJ
    ns = seq // tm
    ncols = w.shape[1]
    const = lambda i: (0, 0)
    return pl.pallas_call(
        _in_proj_kernel,
        out_shape=(jax.ShapeDtypeStruct((seq, bsz * GROUP_W), F32),
                   jax.ShapeDtypeStruct((n, ncols - GROUP_W), BF16)),
        grid=(n // tm,),
        in_specs=[
            pl.BlockSpec((tm, dm), lambda i: (i, 0)),
            pl.BlockSpec((1, dm), const),
            pl.BlockSpec((dm, ncols), const),
            pl.BlockSpec((4, GROUP_W), const),
            pl.BlockSpec((GROUP_W, GROUP_W), const),
            pl.BlockSpec((GROUP_W, GROUP_W), const),
        ],
        out_specs=(
            pl.BlockSpec((tm, GROUP_W), lambda i: (i % ns, i // ns)),
            pl.BlockSpec((tm, ncols - GROUP_W), lambda i: (i, 0)),
        ),
        compiler_params=_cparams(("arbitrary",)),
        name="in_proj",
    )(x2, g, w, qkg, ind64, ind32)


def _s5_kernel(u_ref, wb_ref, lre_ref, lim_ref, wc_ref, d_ref, wg_ref, bg_ref, y_ref, bu_ref, st_ref,
               *, steps, nstate):
    @pl.when(pl.program_id(0) == 0)
    def _():
        st_ref[...] = jnp.zeros_like(st_ref)

    u = u_ref[...]
    bu_ref[...] = _dot(u.astype(BF16), wb_ref[...])
    lre = lre_ref[...]
    lim = lim_ref[...]

    def step(t, carry):
        sre, sim = carry
        r = pl.multiple_of(t * SUBLANES, SUBLANES)
        bre = bu_ref[pl.ds(r, SUBLANES), 0:nstate]
        bim = bu_ref[pl.ds(r, SUBLANES), nstate:2 * nstate]
        nre = lre * sre - lim * sim + bre
        nim = lre * sim + lim * sre + bim
        bu_ref[pl.ds(r, SUBLANES), 0:nstate] = nre
        bu_ref[pl.ds(r, SUBLANES), nstate:2 * nstate] = nim
        return nre, nim

    sre, sim = lax.fori_loop(0, steps, step, (st_ref[:, 0:nstate], st_ref[:, nstate:2 * nstate]))
    st_ref[:, 0:nstate] = sre
    st_ref[:, nstate:2 * nstate] = sim

    y = _dot(bu_ref[...].astype(BF16), wc_ref[...]) + d_ref[...] * u
    y = jax.nn.gelu(y)
    gate = jax.nn.sigmoid(_dot(y.astype(BF16), wg_ref[...]) + bg_ref[...])
    y_ref[...] = (y * gate).astype(y_ref.dtype)


def _s5(u_tm, wb, lre, lim, wc, d, wg, bg, bsz):
    rows, w = u_tm.shape
    assert bsz == SUBLANES
    blk = SSM_STEPS * bsz
    nstate2 = wb.shape[1]
    const = lambda c: (0, 0)
    return pl.pallas_call(
        functools.partial(_s5_kernel, steps=SSM_STEPS, nstate=nstate2 // 2),
        out_shape=jax.ShapeDtypeStruct((rows, w), BF16),
        grid=(rows // blk,),
        in_specs=[
            pl.BlockSpec((blk, w), lambda c: (c, 0)),
            pl.BlockSpec((w, nstate2), const),
            pl.BlockSpec((bsz, nstate2 // 2), const),
            pl.BlockSpec((bsz, nstate2 // 2), const),
            pl.BlockSpec((nstate2, w), const),
            pl.BlockSpec((1, w), const),
            pl.BlockSpec((w, w), const),
            pl.BlockSpec((1, w), const),
        ],
        out_specs=pl.BlockSpec((blk, w), lambda c: (c, 0)),
        scratch_shapes=[pltpu.VMEM((blk, nstate2), F32), pltpu.VMEM((bsz, nstate2), F32)],
        compiler_params=_cparams(("arbitrary",)),
        name="s5_mixer",
    )(u_tm, wb, lre, lim, wc, d, wg, bg)


def _s5_params(lam_re, lam_im, log_dt, b_re, b_im, c_re, c_im, bsz):
    g, n = lam_re.shape
    dt = jnp.exp(log_dt.astype(F32))[:, None]
    xr = lam_re * dt
    th = lam_im * dt
    er = jnp.exp(xr)
    lbr = er * jnp.cos(th)
    lbi = er * jnp.sin(th)
    ar = jnp.expm1(xr) * jnp.cos(th) - 2.0 * jnp.sin(0.5 * th) ** 2
    ai = lbi
    den = lam_re * lam_re + lam_im * lam_im
    fr = (ar * lam_re + ai * lam_im) / den
    fi = (ai * lam_re - ar * lam_im) / den
    bbr = fr[..., None] * b_re - fi[..., None] * b_im
    bbi = fr[..., None] * b_im + fi[..., None] * b_re
    eye = jnp.eye(g, dtype=F32)
    p = b_re.shape[-1]
    wb_re = jnp.einsum("gnp,gh->gphn", bbr, eye).reshape(g * p, g * n)
    wb_im = jnp.einsum("gnp,gh->gphn", bbi, eye).reshape(g * p, g * n)
    wb = jnp.concatenate([wb_re, wb_im], axis=1).astype(BF16)
    wc_re = jnp.einsum("gpn,gh->gnhp", c_re, eye).reshape(g * n, g * p)
    wc_im = jnp.einsum("gpn,gh->gnhp", c_im, eye).reshape(g * n, g * p)
    wc = jnp.concatenate([wc_re, -wc_im], axis=0).astype(BF16)
    lre = jnp.broadcast_to(lbr.reshape(1, g * n), (bsz, g * n))
    lim = jnp.broadcast_to(lbi.reshape(1, g * n), (bsz, g * n))
    return wb, lre, lim, wc


VA_ROWS = 2 * HEAD_DIM


def _iota2(shape, dim):
    return lax.broadcasted_iota(jnp.int32, shape, dim)


def _stack_masked_q(q_ref, qs_ref, tq, ngroups, width):
    lane = _iota2((1, GROUP_W), 1)
    q = q_ref[...]
    for g in range(ngroups):
        keep = (lane >= g * width) & (lane < (g + 1) * width)
        qs_ref[g * tq:(g + 1) * tq, :] = jnp.where(keep, q, jnp.zeros_like(q))


def _key_scores(k_ref, qs_ref, j, tk):
    row0 = pl.multiple_of(jnp.maximum(j, 0) * tk, tk)
    return _dot_nt(k_ref[pl.ds(row0, tk), :], qs_ref[...])


def _paired_key_tiles(i, score_fn, tile_fn, za_ref, zb_ref):
    def body(n, c):
        ja = i - 1 - 2 * n
        zb_ref[...] = score_fn(ja - 1)
        tile_fn(ja, za_ref, True)
        za_ref[...] = score_fn(ja - 2)
        tile_fn(ja - 1, zb_ref, ja >= 1)
        return c

    lax.fori_loop(0, (i + 1) // 2, body, 0)


def _sb_kernel(q_ref, k_ref, vt_ref, later_ref, o_ref, qs_ref, carry_ref, acc_ref, za_ref, zb_ref,
               *, tq, tk, nheads):
    i = pl.program_id(1)
    _stack_masked_q(q_ref, qs_ref, tq, nheads, HEAD_DIM)
    score_fn = functools.partial(_key_scores, k_ref, qs_ref, tk=tk)

    def tile(j, zt, valid, diagonal=False):
        if diagonal:
            before = _iota2((tk, tq), 0) < _iota2((tk, tq), 1)
        else:
            penalty = jnp.where(valid, 0.0, -NEG_INF)
            keep = jnp.where(valid, 1.0, 0.0)
        jv = jnp.maximum(j, 0)

        def scan_stage(h):
            z = zt[:, h * tq:(h + 1) * tq]
            neg_abs = lax.bitcast_convert_type(
                lax.bitcast_convert_type(z, jnp.uint32) | jnp.uint32(0x80000000), F32)
            sp = jnp.maximum(z, 0.0) + jnp.log2(1.0 + jnp.exp2(neg_abs))
            nl1m = jnp.where(before, sp, 0.0) if diagonal else sp
            hi = nl1m.astype(BF16)
            lo = (nl1m - hi.astype(F32)).astype(BF16)
            inside = _dot(later_ref[...], jnp.concatenate([hi, lo], axis=0))
            return z - sp, inside, nl1m[0:1, :]

        def value_stage(h, logit, inside, first_row):
            hs = slice(h * HEAD_DIM, (h + 1) * HEAD_DIM)
            between = inside if diagonal else inside + (carry_ref[h:h + 1, :] + penalty)
            w = jnp.exp2(logit - between)
            if diagonal:
                w = jnp.where(before, w, 0.0)
            pv = _dot(vt_ref[jv, hs, :], w.astype(BF16))
            total = inside[0:1, :] + first_row
            if diagonal:
                acc_ref[hs, :] = pv
                carry_ref[h:h + 1, :] = total
            else:
                acc_ref[hs, :] = acc_ref[hs, :] + pv
                carry_ref[h:h + 1, :] = carry_ref[h:h + 1, :] + keep * total

        pending = scan_stage(0)
        for h in range(1, nheads):
            upcoming = scan_stage(h)
            value_stage(h - 1, *pending)
            pending = upcoming
        value_stage(nheads - 1, *pending)

    za_ref[...] = score_fn(i - 1)
    tile(i, score_fn(i), True, diagonal=True)
    _paired_key_tiles(i, score_fn, tile, za_ref, zb_ref)
    o_ref[...] = acc_ref[...].astype(o_ref.dtype)


def _softmax_tile(st, col0, bias, off, va, g, first, m_ref, acc_ref):
    rows = slice(g * VA_ROWS, (g + 1) * VA_ROWS)
    parts, alphas = [], []
    for c0 in range(0, bias.shape[1], LANES):
        cs = slice(c0, c0 + LANES)
        s = st[:, col0 + c0:col0 + c0 + LANES] + bias[:, cs]
        blk_max = jnp.max(s, axis=0, keepdims=True)
        if first:
            m_new = blk_max
            parts.append(jnp.exp2(s - m_new).astype(BF16))
        else:
            m_old = m_ref[g:g + 1, cs]
            m_new = jnp.maximum(m_old, blk_max + off)
            parts.append(jnp.exp2(s - (m_new - off)).astype(BF16))
            alphas.append(jnp.exp2(m_old - m_new))
        m_ref[g:g + 1, cs] = m_new
    pv = _dot(va, jnp.concatenate(parts, axis=1))
    if first:
        acc_ref[rows, :] = pv
    else:
        acc_ref[rows, :] = jnp.concatenate(alphas, axis=1) * acc_ref[rows, :] + pv


def _normalised(acc_ref, g):
    base = g * VA_ROWS
    return acc_ref[base:base + HEAD_DIM, :] / acc_ref[base + HEAD_DIM:base + HEAD_DIM + 1, :]


def _ch_kernel(q_ref, k_ref, va_ref, bias_ref, o_ref, qs_ref, m_ref, acc_ref, *, tq, tk, nheads, nback):
    i = pl.program_id(1)
    _stack_masked_q(q_ref, qs_ref, tq, nheads, HEAD_DIM)

    def tile(n, first):
        j = i - n
        row0 = pl.multiple_of(j * tk, tk)
        st = _dot_nt(k_ref[pl.ds(row0, tk), :], qs_ref[...])
        for h in range(nheads):
            _softmax_tile(st, h * tq, bias_ref.at[h, n], 0.0, va_ref[j, h * VA_ROWS:(h + 1) * VA_ROWS, :],
                          h, first, m_ref, acc_ref)

    tile(0, True)

    def body(n, c):
        tile(n, False)
        return c

    lax.fori_loop(1, jnp.minimum(i, nback) + 1, body, 0)
    for h in range(nheads):
        o_ref[h * HEAD_DIM:(h + 1) * HEAD_DIM, :] = _normalised(acc_ref, h).astype(o_ref.dtype)


def _df_kernel(lam_ref, q_ref, k_ref, va_ref, kb_ref, db_ref, o_ref, qs_ref, m_ref, acc_ref, za_ref, zb_ref,
               *, tq, tk, nheads, slopes2):
    i = pl.program_id(1)
    _stack_masked_q(q_ref, qs_ref, tq, 2 * nheads, DF_QK_DIM)
    score_fn = functools.partial(_key_scores, k_ref, qs_ref, tk=tk)

    def tile(j, st, valid, diagonal=False):
        jv = jnp.maximum(j, 0)
        for g in range(2 * nheads):
            h = g // 2
            va = va_ref[jv, h * VA_ROWS:(h + 1) * VA_ROWS, :]
            if diagonal:
                _softmax_tile(st, g * tq, db_ref.at[h], 0.0, va, g, True, m_ref, acc_ref)
            else:
                off = jnp.where(valid, slopes2[h] * ((j - i) * tk).astype(F32), NEG_INF)
                _softmax_tile(st, g * tq, kb_ref.at[h], off, va, g, False, m_ref, acc_ref)

    za_ref[...] = score_fn(i - 1)
    tile(i, score_fn(i), True, diagonal=True)
    _paired_key_tiles(i, score_fn, tile, za_ref, zb_ref)
    lam = lam_ref[0]
    for h in range(nheads):
        out = _normalised(acc_ref, 2 * h) - lam * _normalised(acc_ref, 2 * h + 1)
        o_ref[h * HEAD_DIM:(h + 1) * HEAD_DIM, :] = out.astype(o_ref.dtype)


def _attention_call(kernel, qkv, col0, vt, extra, extra_specs, scratch, bsz, seq, name, smem=()):
    tq, tk = TQ_ATT, TK_ATT
    assert tq == tk
    nq = seq // tq
    vrows = vt.shape[2]
    return pl.pallas_call(
        kernel,
        out_shape=jax.ShapeDtypeStruct((bsz, GROUP_W, seq), BF16),
        grid=(bsz, nq),
        in_specs=[pl.BlockSpec(memory_space=pltpu.SMEM) for _ in smem] + [
            pl.BlockSpec((tq, GROUP_W), lambda b, i: (b * nq + i, col0)),
            pl.BlockSpec((seq, GROUP_W), lambda b, i: (b, col0 + 1)),
            pl.BlockSpec((None, seq // tk, vrows, tk), lambda b, i: (b, 0, 0, 0)),
        ] + extra_specs,
        out_specs=pl.BlockSpec((None, GROUP_W, tq), lambda b, i: (b, 0, i)),
        scratch_shapes=scratch,
        compiler_params=_cparams(("arbitrary", "arbitrary")),
        name=name,
    )(*smem, qkv, qkv, vt, *extra)


def _values_t(v, bsz, seq, nheads, ones_row):
    if ones_row:
        v4 = v.reshape(bsz, seq, nheads, HEAD_DIM)
        pad = jnp.zeros((bsz, seq, nheads, VA_ROWS - HEAD_DIM - 1), v.dtype)
        v = jnp.concatenate([v4, jnp.ones((bsz, seq, nheads, 1), v.dtype), pad], axis=-1)
    rows = v.size // (bsz * seq)
    return v.reshape(bsz, seq // TK_ATT, TK_ATT, rows).transpose(0, 1, 3, 2)


def _tokens_major(o, n):
    return o.transpose(0, 2, 1).reshape(n, GROUP_W)


def _ch_bias_tiles(rel_bias, tq, tk):
    nback = CH_LEFT_CHUNKS * CHUNK // tk
    n = jnp.arange(nback + 1)[:, None, None]
    s = jnp.arange(tk)[None, :, None]
    t = jnp.arange(tq)[None, None, :]
    delta = (n * tk + t) // CHUNK - s // CHUNK
    valid = (delta >= 0) & (delta <= CH_LEFT_CHUNKS)
    rel_key = (CH_LEFT_CHUNKS - delta) * CHUNK + s % CHUNK
    rel = CH_LEFT_CHUNKS * CHUNK + t % CHUNK - rel_key
    idx = jnp.clip(rel, -REL_CLIP, REL_CLIP) + REL_CLIP
    bias = rel_bias.astype(F32)[:, idx] * LOG2E
    return jnp.where(valid[None], bias, NEG_INF), nback


def _df_bias_tiles(slopes2, tq, tk):
    s = jnp.arange(tk, dtype=jnp.int32)[:, None]
    t = jnp.arange(tq, dtype=jnp.int32)[None, :]
    sl = jnp.asarray(slopes2, F32)[:, None, None]
    kb = sl * jnp.broadcast_to(s, (tk, tq)).astype(F32)
    allowed = (s // CHUNK) <= (t // CHUNK)
    db = jnp.where(allowed[None], sl * (t - jnp.abs(t - s)).astype(F32), NEG_INF)
    return kb, db


def _out_proj_kernel(ys_ref, ysb_ref, ych_ref, ydf_ref, g_ref, ind_ref, w_ref, x_ref, o_ref):
    acc = x_ref[...]
    for gi, y_ref in enumerate((ys_ref, ysb_ref, ych_ref, ydf_ref)):
        cs = slice(gi * GROUP_W, (gi + 1) * GROUP_W)
        y = y_ref[...].astype(F32)
        ms = _group_ms(y, ind_ref[...], 1.0 / HEAD_DIM)
        yn = (y * lax.rsqrt(ms + EPS) * g_ref[:, cs]).astype(BF16)
        acc = acc + _dot(yn, w_ref[cs, :])
    o_ref[...] = acc


def _out_proj(y_ssm_tm, y_sb, y_ch, y_df, g, ind64, w, x2, bsz, seq):
    n, dm = x2.shape
    tm = TM_PROJ
    ns = seq // tm
    const = lambda i: (0, 0)
    tile = pl.BlockSpec((tm, GROUP_W), lambda i: (i, 0))
    return pl.pallas_call(
        _out_proj_kernel,
        out_shape=jax.ShapeDtypeStruct((n, dm), F32),
        grid=(n // tm,),
        in_specs=[
            pl.BlockSpec((tm, GROUP_W), lambda i: (i % ns, i // ns)),
            tile, tile, tile,
            pl.BlockSpec((1, dm), const),
            pl.BlockSpec((GROUP_W, GROUP_W), const),
            pl.BlockSpec((dm, dm), const),
            pl.BlockSpec((tm, dm), lambda i: (i, 0)),
        ],
        out_specs=pl.BlockSpec((tm, dm), lambda i: (i, 0)),
        compiler_params=_cparams(("arbitrary",)),
        name="out_proj",
    )(y_ssm_tm, y_sb, y_ch, y_df, g, ind64, w, x2)


def _ffn_kernel(x_ref, g_ref, w1_ref, w3_ref, w2_ref, o_ref, hid_ref, *, tf):
    x = x_ref[...]
    ms = jnp.mean(x * x, axis=-1, keepdims=True)
    h = (x * lax.rsqrt(ms + EPS) * g_ref[...]).astype(BF16)
    dff = w1_ref.shape[1]
    for c in range(dff // tf):
        cs = slice(c * tf, (c + 1) * tf)
        a = _dot(h, w1_ref[:, cs])
        b = _dot(h, w3_ref[:, cs])
        hid_ref[:, cs] = (jax.nn.silu(a) * b).astype(BF16)
    o_ref[...] = x + _dot(hid_ref[...], w2_ref[...])


def _ffn(x2, g, w1, w3, w2):
    n, dm = x2.shape
    dff = w1.shape[1]
    tm = TM_PROJ
    const = lambda i: (0, 0)
    return pl.pallas_call(
        functools.partial(_ffn_kernel, tf=GROUP_W),
        out_shape=jax.ShapeDtypeStruct((n, dm), F32),
        grid=(n // tm,),
        in_specs=[
            pl.BlockSpec((tm, dm), lambda i: (i, 0)),
            pl.BlockSpec((1, dm), const),
            pl.BlockSpec((dm, dff), const),
            pl.BlockSpec((dm, dff), const),
            pl.BlockSpec((dff, dm), const),
        ],
        out_specs=pl.BlockSpec((tm, dm), lambda i: (i, 0)),
        scratch_shapes=[pltpu.VMEM((tm, dff), BF16)],
        compiler_params=_cparams(("arbitrary",)),
        name="ffn_dense",
    )(x2, g, w1, w3, w2)


def _router_kernel(x_ref, g_ref, wr_ref, tri_ref, h_ref, route_ref, cnt_ref, carry_ref):
    @pl.when(pl.program_id(0) == 0)
    def _():
        carry_ref[...] = jnp.zeros_like(carry_ref)

    x = x_ref[...]
    ms = jnp.mean(x * x, axis=-1, keepdims=True)
    h = x * lax.rsqrt(ms + EPS) * g_ref[...]
    h_hi = h.astype(BF16)
    h_ref[...] = h_hi
    h_lo = (h - h_hi.astype(F32)).astype(BF16)
    logits = _dot(h_hi, wr_ref[0]) + _dot(h_lo, wr_ref[0]) + _dot(h_hi, wr_ref[1])

    lane = _iota2(logits.shape, 1).astype(F32)
    lg = jnp.where(lane < N_EXPERTS, logits, -jnp.inf)
    m1 = jnp.max(lg, axis=-1, keepdims=True)
    e1 = jnp.min(jnp.where(lg == m1, lane, float(LANES)), axis=-1, keepdims=True)
    lg2 = jnp.where(lane == e1, -jnp.inf, lg)
    m2 = jnp.max(lg2, axis=-1, keepdims=True)
    e2 = jnp.min(jnp.where(lg2 == m2, lane, float(LANES)), axis=-1, keepdims=True)
    t = jnp.exp(m2 - m1)
    g1 = 1.0 / (1.0 + t)
    g2 = t / (1.0 + t)

    hot1 = lane == e1
    hot2 = lane == e2
    sel = jnp.where(hot1 | hot2, 1.0, 0.0).astype(BF16)
    prior = _dot(tri_ref[...], sel) + carry_ref[...]
    r1 = jnp.sum(jnp.where(hot1, prior, 0.0), axis=-1, keepdims=True)
    r2 = jnp.sum(jnp.where(hot2, prior, 0.0), axis=-1, keepdims=True)
    carry_ref[...] = carry_ref[...] + jnp.sum(sel.astype(F32), axis=0, keepdims=True)
    cnt_ref[...] = carry_ref[...]

    out = jnp.where(lane == 0, e1, 0.0)
    out = jnp.where(lane == 1, e2, out)
    out = jnp.where(lane == 2, g1, out)
    out = jnp.where(lane == 3, g2, out)
    out = jnp.where(lane == 4, r1, out)
    out = jnp.where(lane == 5, r2, out)
    route_ref[...] = out


def _router(x2, g, wr2, tri):
    n, dm = x2.shape
    tm = TM_PROJ
    const = lambda i: (0, 0)
    return pl.pallas_call(
        _router_kernel,
        out_shape=(jax.ShapeDtypeStruct((n, dm), BF16),
                   jax.ShapeDtypeStruct((n, LANES), F32),
                   jax.ShapeDtypeStruct((1, LANES), F32)),
        grid=(n // tm,),
        in_specs=[
            pl.BlockSpec((tm, dm), lambda i: (i, 0)),
            pl.BlockSpec((1, dm), const),
            pl.BlockSpec((2, dm, LANES), lambda i: (0, 0, 0)),
            pl.BlockSpec((tm, tm), const),
        ],
        out_specs=(
            pl.BlockSpec((tm, dm), lambda i: (i, 0)),
            pl.BlockSpec((tm, LANES), lambda i: (i, 0)),
            pl.BlockSpec((1, LANES), const),
        ),
        scratch_shapes=[pltpu.VMEM((1, LANES), F32)],
        compiler_params=_cparams(("arbitrary",)),
        name="moe_router",
    )(x2, g, wr2, tri)


def _experts_kernel(te_ref, nused_ref, xs_ref, gate_ref, w1_ref, w3_ref, w2_ref, o_ref, acc_ref):
    t = pl.program_id(0)
    c = pl.program_id(1)
    nc = pl.num_programs(1)

    @pl.when(t < nused_ref[0])
    def _():
        x = xs_ref[...]
        a = _dot(x, w1_ref[0])
        b = _dot(x, w3_ref[0])
        hid = (jax.nn.silu(a) * b).astype(BF16)
        part = _dot(hid, w2_ref[0])

        @pl.when(c == 0)
        def _():
            acc_ref[...] = part

        @pl.when(c > 0)
        def _():
            acc_ref[...] = acc_ref[...] + part

        @pl.when(c == nc - 1)
        def _():
            o_ref[...] = acc_ref[...] * gate_ref[...]

    @pl.when(jnp.logical_and(t >= nused_ref[0], c == nc - 1))
    def _():
        o_ref[...] = jnp.zeros_like(o_ref)


def _experts(tile_expert, nused, xs, gates, w1, w3, w2):
    rows, dm = xs.shape
    dff = w1.shape[2]
    tm, tf = TM_EXPERT, TF_EXPERT
    nt, nc = rows // tm, dff // tf

    def cc(t, c, nu):
        return jnp.where(t < nu[0], c, nc - 1)

    grid_spec = pltpu.PrefetchScalarGridSpec(
        num_scalar_prefetch=2,
        grid=(nt, nc),
        in_specs=[
            pl.BlockSpec((tm, dm), lambda t, c, te, nu: (t, 0)),
            pl.BlockSpec((tm, 1), lambda t, c, te, nu: (t, 0)),
            pl.BlockSpec((1, dm, tf), lambda t, c, te, nu: (te[t], 0, cc(t, c, nu))),
            pl.BlockSpec((1, dm, tf), lambda t, c, te, nu: (te[t], 0, cc(t, c, nu))),
            pl.BlockSpec((1, tf, dm), lambda t, c, te, nu: (te[t], cc(t, c, nu), 0)),
        ],
        out_specs=pl.BlockSpec((tm, dm), lambda t, c, te, nu: (t, 0)),
        scratch_shapes=[pltpu.VMEM((tm, dm), F32)],
    )
    return pl.pallas_call(
        _experts_kernel,
        out_shape=jax.ShapeDtypeStruct((rows, dm), F32),
        grid_spec=grid_spec,
        compiler_params=_cparams(("arbitrary", "arbitrary")),
        name="moe_experts",
    )(tile_expert, nused, xs, gates, w1, w3, w2)


def _moe(x2, g, w_router, w1, w3, w2, tri):
    n, dm = x2.shape
    wr = jnp.zeros((dm, LANES), F32).at[:, :N_EXPERTS].set(w_router.astype(F32))
    wr_hi = wr.astype(BF16)
    wr_lo = (wr - wr_hi.astype(F32)).astype(BF16)
    h, route, counts = _router(x2, g, jnp.stack([wr_hi, wr_lo]), tri)

    tm = TM_EXPERT
    rows = 2 * n + N_EXPERTS * tm
    nt = rows // tm
    e1 = route[:, 0].astype(jnp.int32)
    e2 = route[:, 1].astype(jnp.int32)
    cnt = counts[0, :N_EXPERTS].astype(jnp.int32)
    padded = ((cnt + tm - 1) // tm) * tm
    ends = jnp.cumsum(padded)
    starts = ends - padded
    d1 = starts[e1] + route[:, 4].astype(jnp.int32)
    d2 = starts[e2] + route[:, 5].astype(jnp.int32)
    nused = (ends[-1] // tm).astype(jnp.int32)
    tile_start = jnp.arange(nt, dtype=jnp.int32) * tm
    tile_expert = jnp.minimum(jnp.sum(tile_start[:, None] >= ends[None, :], axis=1), N_EXPERTS - 1)
    last_expert = tile_expert[jnp.maximum(nused - 1, 0)]
    tile_expert = jnp.where(jnp.arange(nt) < nused, tile_expert, last_expert).astype(jnp.int32)

    tok = jnp.arange(n, dtype=jnp.int32)
    src = jnp.zeros((rows,), jnp.int32).at[d1].set(tok).at[d2].set(tok)
    gate_rows = jnp.zeros((rows,), F32).at[d1].set(route[:, 2]).at[d2].set(route[:, 3])
    xs = jnp.take(h, src, axis=0)
    ys = _experts(tile_expert, nused.reshape(1), xs, gate_rows[:, None], w1, w3, w2)
    return x2 + jnp.take(ys, d1, axis=0) + jnp.take(ys, d2, axis=0)


def kernel(x, norm_mix_g, w_in, ssm_lam_re, ssm_lam_im, ssm_log_dt, ssm_b_re, ssm_b_im, ssm_c_re, ssm_c_im,
           ssm_d, ssm_w_glu, ssm_b_glu, ch_q_norm_g, ch_k_norm_g, ch_rel_bias, df_q_norm_g, df_k_norm_g,
           df_lambda, out_norm_g, w_out, norm_ffn_g, ffn_w1, ffn_w3, ffn_w2, moe_router, moe_w1, moe_w3, moe_w2):
    bsz, seq, dm = x.shape
    depth = w_in.shape[0]
    n = bsz * seq
    nheads = GROUP_W // HEAD_DIM
    ind64 = _block_indicator(GROUP_W, HEAD_DIM)
    ind32 = _block_indicator(GROUP_W, DF_QK_DIM)
    tri = (jnp.arange(TM_PROJ)[:, None] > jnp.arange(TM_PROJ)[None, :]).astype(BF16)
    slopes = [2.0 ** (-8.0 * (h + 1) / nheads) for h in range(nheads)]
    slopes2 = tuple(s * LOG2E for s in slopes)
    df_kb, df_db = _df_bias_tiles(slopes2, TQ_ATT, TK_ATT)
    later = (jnp.arange(TK_ATT)[:, None] < jnp.arange(TK_ATT)[None, :]).astype(BF16)
    later = jnp.concatenate([later, later], axis=1)

    x2 = x.reshape(n, dm)
    for layer in range(depth):
        qkg = jnp.stack([
            jnp.tile(ch_q_norm_g[layer].astype(F32), nheads) * (HEAD_DIM ** -0.5 * LOG2E),
            jnp.tile(ch_k_norm_g[layer].astype(F32), nheads),
            jnp.tile(df_q_norm_g[layer].astype(F32).reshape(-1), nheads) * (DF_QK_DIM ** -0.5 * LOG2E),
            jnp.tile(df_k_norm_g[layer].astype(F32).reshape(-1), nheads),
        ])
        u_tm, qkv = _in_proj(x2, norm_mix_g[layer].reshape(1, dm), w_in[layer].astype(BF16), qkg,
                             ind64, ind32, bsz, seq)

        wb, lre, lim, wc = _s5_params(ssm_lam_re[layer], ssm_lam_im[layer], ssm_log_dt[layer],
                                      ssm_b_re[layer], ssm_b_im[layer], ssm_c_re[layer], ssm_c_im[layer], bsz)
        y_ssm = _s5(u_tm.reshape(seq * bsz, GROUP_W), wb, lre, lim, wc, ssm_d[layer].reshape(1, GROUP_W),
                    ssm_w_glu[layer].astype(BF16), ssm_b_glu[layer].reshape(1, GROUP_W), bsz)
        y_ssm = y_ssm.reshape(seq, bsz * GROUP_W)

        tq, tk = TQ_ATT, TK_ATT
        const3 = lambda b, i: (0, 0, 0)
        y_sb = _attention_call(
            functools.partial(_sb_kernel, tq=tq, tk=tk, nheads=nheads),
            qkv, 0, _values_t(qkv[:, 2 * GROUP_W:3 * GROUP_W], bsz, seq, nheads, False),
            (later,), [pl.BlockSpec((tk, 2 * tk), lambda b, i: (0, 0))],
            [pltpu.VMEM((nheads * tq, GROUP_W), BF16), pltpu.VMEM((SUBLANES, tq), F32),
             pltpu.VMEM((GROUP_W, tq), F32),
             pltpu.VMEM((tk, nheads * tq), F32), pltpu.VMEM((tk, nheads * tq), F32)],
            bsz, seq, "sb_attention")

        bias, nback = _ch_bias_tiles(ch_rel_bias[layer], tq, tk)
        y_ch = _attention_call(
            functools.partial(_ch_kernel, tq=tq, tk=tk, nheads=nheads, nback=nback),
            qkv, 3, _values_t(qkv[:, 5 * GROUP_W:6 * GROUP_W], bsz, seq, nheads, True),
            (bias,), [pl.BlockSpec(bias.shape, lambda b, i: (0, 0, 0, 0))],
            [pltpu.VMEM((nheads * tq, GROUP_W), BF16), pltpu.VMEM((SUBLANES, tq), F32),
             pltpu.VMEM((nheads * VA_ROWS, tq), F32)],
            bsz, seq, "ch_attention")

        lambda_init = 0.8 - 0.6 * math.exp(-0.3 * layer)
        lam_p = df_lambda[layer].astype(F32)
        lam = jnp.exp(jnp.sum(lam_p[0] * lam_p[1])) - jnp.exp(jnp.sum(lam_p[2] * lam_p[3])) + lambda_init
        y_df = _attention_call(
            functools.partial(_df_kernel, tq=tq, tk=tk, nheads=nheads, slopes2=slopes2),
            qkv, 6, _values_t(qkv[:, 8 * GROUP_W:9 * GROUP_W], bsz, seq, nheads, True),
            (df_kb, df_db), [pl.BlockSpec(df_kb.shape, const3), pl.BlockSpec(df_db.shape, const3)],
            [pltpu.VMEM((2 * nheads * tq, GROUP_W), BF16), pltpu.VMEM((2 * nheads, tq), F32),
             pltpu.VMEM((2 * nheads * VA_ROWS, tq), F32),
             pltpu.VMEM((tk, 2 * nheads * tq), F32), pltpu.VMEM((tk, 2 * nheads * tq), F32)],
            bsz, seq, "df_attention", smem=(lam.reshape(1),))
        y_sb, y_ch, y_df = (_tokens_major(y, n) for y in (y_sb, y_ch, y_df))

        head_scale = jnp.concatenate([jnp.ones((dm - GROUP_W,), F32),
                                      jnp.full((GROUP_W,), 1.0 - lambda_init, F32)])
        g_out = (out_norm_g[layer].astype(F32) * head_scale).reshape(1, dm)
        x2 = _out_proj(y_ssm, y_sb, y_ch, y_df, g_out, ind64, w_out[layer].astype(BF16), x2, bsz, seq)

        idx = layer // 2
        g_ffn = norm_ffn_g[layer].reshape(1, dm)
        if layer % 2 == 0:
            x2 = _ffn(x2, g_ffn, ffn_w1[idx].astype(BF16), ffn_w3[idx].astype(BF16), ffn_w2[idx].astype(BF16))
        else:
            x2 = _moe(x2, g_ffn, moe_router[idx], moe_w1[idx].astype(BF16), moe_w3[idx].astype(BF16),
                      moe_w2[idx].astype(BF16), tri)
    return x2.reshape(bsz, seq, dm)
```

```python
import functools
import math

import jax
import jax.numpy as jnp
from jax import lax
from jax.experimental import pallas as pl
from jax.experimental.pallas import tpu as pltpu

F32 = jnp.float32
BF16 = jnp.bfloat16

EPS = 1e-6
NEG_INF = -1e30
HEAD_DIM = 64
GROUP_W = 256
CHUNK = 64
SSM_GROUP = 16
SSM_STATE = 64
DF_QK_DIM = 32
CH_LEFT_CHUNKS = 8
REL_CLIP = 128
N_EXPERTS = 8
LOG2E = 1.4426950408889634

VMEM_LIMIT_BYTES = 56 * 1024 * 1024
SUBLANES = 8
LANES = 128

TM_PROJ = 512
TQ_ATT = 256
TK_ATT = 256
SSM_STEPS = 64
TM_EXPERT = 512
TF_EXPERT = 896


def _cparams(sem):
    return pltpu.CompilerParams(dimension_semantics=sem, vmem_limit_bytes=VMEM_LIMIT_BYTES)


def _dot(a, b):
    return jnp.dot(a, b, preferred_element_type=F32)


def _dot_nt(a, b):
    return lax.dot_general(a, b, (((1,), (1,)), ((), ())), preferred_element_type=F32)


def _split_dot(v, m):
    hi = v.astype(BF16)
    lo = (v - hi.astype(F32)).astype(BF16)
    return _dot(hi, m) + _dot(lo, m)


def _group_ms(v, ind, inv_size):
    return _split_dot(v * v, ind) * inv_size


def _block_indicator(width, group):
    r = jnp.arange(width)[:, None] // group
    c = jnp.arange(width)[None, :] // group
    return (r == c).astype(BF16)


def _in_proj_kernel(x_ref, g_ref, w_ref, qkg_ref, ind64_ref, ind32_ref, u_ref, qkv_ref):
    x = x_ref[...]
    ms = jnp.mean(x * x, axis=-1, keepdims=True)
    h = (x * lax.rsqrt(ms + EPS) * g_ref[...]).astype(BF16)

    def cols(c):
        return _dot(h, w_ref[:, c * GROUP_W:(c + 1) * GROUP_W])

    def put(c, val):
        qkv_ref[:, (c - 1) * GROUP_W:c * GROUP_W] = val.astype(BF16)

    u_ref[...] = cols(0)
    put(1, cols(1) * (HEAD_DIM ** -0.5 * LOG2E))
    put(2, cols(2))
    put(3, cols(3))
    for c, gi in ((4, 0), (5, 1)):
        a = cols(c)
        ms_h = _group_ms(a, ind64_ref[...], 1.0 / HEAD_DIM)
        put(c, a * lax.rsqrt(ms_h + EPS) * qkg_ref[gi:gi + 1, :])
    put(6, cols(6))
    for c, gi in ((7, 2), (8, 3)):
        a = cols(c)
        ms_h = _group_ms(a, ind32_ref[...], 1.0 / DF_QK_DIM)
        put(c, a * lax.rsqrt(ms_h + EPS) * qkg_ref[gi:gi + 1, :])
    put(9, cols(9))


def _in_proj(x2, g, w, qkg, ind64, ind32, bsz, seq):
    n, dm = x2.shape
    tm = TM_PROJ
    ns = seq // tm
    ncols = w.shape[1]
    const = lambda i: (0, 0)
    return pl.pallas_call(
        _in_proj_kernel,
        out_shape=(jax.ShapeDtypeStruct((seq, bsz * GROUP_W), F32),
                   jax.ShapeDtypeStruct((n, ncols - GROUP_W), BF16)),
        grid=(n // tm,),
        in_specs=[
            pl.BlockSpec((tm, dm), lambda i: (i, 0)),
            pl.BlockSpec((1, dm), const),
            pl.BlockSpec((dm, ncols), const),
            pl.BlockSpec((4, GROUP_W), const),
            pl.BlockSpec((GROUP_W, GROUP_W), const),
            pl.BlockSpec((GROUP_W, GROUP_W), const),
        ],
        out_specs=(
            pl.BlockSpec((tm, GROUP_W), lambda i: (i % ns, i // ns)),
            pl.BlockSpec((tm, ncols - GROUP_W), lambda i: (i, 0)),
        ),
        compiler_params=_cparams(("arbitrary",)),
        name="in_proj",
    )(x2, g, w, qkg, ind64, ind32)


def _s5_kernel(u_ref, wb_ref, lre_ref, lim_ref, wc_ref, d_ref, wg_ref, bg_ref, y_ref, bu_ref, st_ref,
               *, steps, nstate):
    @pl.when(pl.program_id(0) == 0)
    def _():
        st_ref[...] = jnp.zeros_like(st_ref)

    u = u_ref[...]
    bu_ref[...] = _dot(u.astype(BF16), wb_ref[...])
    lre = lre_ref[...]
    lim = lim_ref[...]

    def step(t, carry):
        sre, sim = carry
        r = pl.multiple_of(t * SUBLANES, SUBLANES)
        bre = bu_ref[pl.ds(r, SUBLANES), 0:nstate]
        bim = bu_ref[pl.ds(r, SUBLANES), nstate:2 * nstate]
        nre = lre * sre - lim * sim + bre
        nim = lre * sim + lim * sre + bim
        bu_ref[pl.ds(r, SUBLANES), 0:nstate] = nre
        bu_ref[pl.ds(r, SUBLANES), nstate:2 * nstate] = nim
        return nre, nim

    sre, sim = lax.fori_loop(0, steps, step, (st_ref[:, 0:nstate], st_ref[:, nstate:2 * nstate]))
    st_ref[:, 0:nstate] = sre
    st_ref[:, nstate:2 * nstate] = sim

    y = _dot(bu_ref[...].astype(BF16), wc_ref[...]) + d_ref[...] * u
    y = jax.nn.gelu(y)
    gate = jax.nn.sigmoid(_dot(y.astype(BF16), wg_ref[...]) + bg_ref[...])
    y_ref[...] = (y * gate).astype(y_ref.dtype)


def _s5(u_tm, wb, lre, lim, wc, d, wg, bg, bsz):
    rows, w = u_tm.shape
    assert bsz == SUBLANES
    blk = SSM_STEPS * bsz
    nstate2 = wb.shape[1]
    const = lambda c: (0, 0)
    return pl.pallas_call(
        functools.partial(_s5_kernel, steps=SSM_STEPS, nstate=nstate2 // 2),
        out_shape=jax.ShapeDtypeStruct((rows, w), BF16),
        grid=(rows // blk,),
        in_specs=[
            pl.BlockSpec((blk, w), lambda c: (c, 0)),
            pl.BlockSpec((w, nstate2), const),
            pl.BlockSpec((bsz, nstate2 // 2), const),
            pl.BlockSpec((bsz, nstate2 // 2), const),
            pl.BlockSpec((nstate2, w), const),
            pl.BlockSpec((1, w), const),
            pl.BlockSpec((w, w), const),
            pl.BlockSpec((1, w), const),
        ],
        out_specs=pl.BlockSpec((blk, w), lambda c: (c, 0)),
        scratch_shapes=[pltpu.VMEM((blk, nstate2), F32), pltpu.VMEM((bsz, nstate2), F32)],
        compiler_params=_cparams(("arbitrary",)),
        name="s5_mixer",
    )(u_tm, wb, lre, lim, wc, d, wg, bg)


def _s5_params(lam_re, lam_im, log_dt, b_re, b_im, c_re, c_im, bsz):
    g, n = lam_re.shape
    dt = jnp.exp(log_dt.astype(F32))[:, None]
    xr = lam_re * dt
    th = lam_im * dt
    er = jnp.exp(xr)
    lbr = er * jnp.cos(th)
    lbi = er * jnp.sin(th)
    ar = jnp.expm1(xr) * jnp.cos(th) - 2.0 * jnp.sin(0.5 * th) ** 2
    ai = lbi
    den = lam_re * lam_re + lam_im * lam_im
    fr = (ar * lam_re + ai * lam_im) / den
    fi = (ai * lam_re - ar * lam_im) / den
    bbr = fr[..., None] * b_re - fi[..., None] * b_im
    bbi = fr[..., None] * b_im + fi[..., None] * b_re
    eye = jnp.eye(g, dtype=F32)
    p = b_re.shape[-1]
    wb_re = jnp.einsum("gnp,gh->gphn", bbr, eye).reshape(g * p, g * n)
    wb_im = jnp.einsum("gnp,gh->gphn", bbi, eye).reshape(g * p, g * n)
    wb = jnp.concatenate([wb_re, wb_im], axis=1).astype(BF16)
    wc_re = jnp.einsum("gpn,gh->gnhp", c_re, eye).reshape(g * n, g * p)
    wc_im = jnp.einsum("gpn,gh->gnhp", c_im, eye).reshape(g * n, g * p)
    wc = jnp.concatenate([wc_re, -wc_im], axis=0).astype(BF16)
    lre = jnp.broadcast_to(lbr.reshape(1, g * n), (bsz, g * n))
    lim = jnp.broadcast_to(lbi.reshape(1, g * n), (bsz, g * n))
    return wb, lre, lim, wc


VA_ROWS = 2 * HEAD_DIM


def _iota2(shape, dim):
    return lax.broadcasted_iota(jnp.int32, shape, dim)


def _stack_masked_q(q_ref, qs_ref, tq, ngroups, width):
    lane = _iota2((1, GROUP_W), 1)
    q = q_ref[...]
    for g in range(ngroups):
        keep = (lane >= g * width) & (lane < (g + 1) * width)
        qs_ref[g * tq:(g + 1) * tq, :] = jnp.where(keep, q, jnp.zeros_like(q))


def _key_scores(k_ref, qs_ref, j, tk):
    row0 = pl.multiple_of(jnp.maximum(j, 0) * tk, tk)
    return _dot_nt(k_ref[pl.ds(row0, tk), :], qs_ref[...])


def _paired_key_tiles(i, score_fn, tile_fn, za_ref, zb_ref):
    def body(n, c):
        ja = i - 1 - 2 * n
        zb_ref[...] = score_fn(ja - 1)
        tile_fn(ja, za_ref, True)
        za_ref[...] = score_fn(ja - 2)
        tile_fn(ja - 1, zb_ref, ja >= 1)
        return c

    lax.fori_loop(0, (i + 1) // 2, body, 0)


def _sb_kernel(q_ref, k_ref, vt_ref, later_ref, o_ref, qs_ref, carry_ref, acc_ref, za_ref, zb_ref,
               *, tq, tk, nheads):
    i = pl.program_id(1)
    _stack_masked_q(q_ref, qs_ref, tq, nheads, HEAD_DIM)
    score_fn = functools.partial(_key_scores, k_ref, qs_ref, tk=tk)

    def tile(j, zt, valid, diagonal=False):
        if diagonal:
            before = _iota2((tk, tq), 0) < _iota2((tk, tq), 1)
        else:
            penalty = jnp.where(valid, 0.0, -NEG_INF)
            keep = jnp.where(valid, 1.0, 0.0)
        jv = jnp.maximum(j, 0)

        def scan_stage(h):
            z = zt[:, h * tq:(h + 1) * tq]
            neg_abs = lax.bitcast_convert_type(
                lax.bitcast_convert_type(z, jnp.uint32) | jnp.uint32(0x80000000), F32)
            sp = jnp.maximum(z, 0.0) + jnp.log2(1.0 + jnp.exp2(neg_abs))
            nl1m = jnp.where(before, sp, 0.0) if diagonal else sp
            hi = nl1m.astype(BF16)
            lo = (nl1m - hi.astype(F32)).astype(BF16)
            inside = _dot(later_ref[...], jnp.concatenate([hi, lo], axis=0))
            return z - sp, inside, nl1m[0:1, :]

        def value_stage(h, logit, inside, first_row):
            hs = slice(h * HEAD_DIM, (h + 1) * HEAD_DIM)
            between = inside if diagonal else inside + (carry_ref[h:h + 1, :] + penalty)
            w = jnp.exp2(logit - between)
            if diagonal:
                w = jnp.where(before, w, 0.0)
            pv = _dot(vt_ref[jv, hs, :], w.astype(BF16))
            total = inside[0:1, :] + first_row
            if diagonal:
                acc_ref[hs, :] = pv
                carry_ref[h:h + 1, :] = total
            else:
                acc_ref[hs, :] = acc_ref[hs, :] + pv
                carry_ref[h:h + 1, :] = carry_ref[h:h + 1, :] + keep * total

        pending = scan_stage(0)
        for h in range(1, nheads):
            upcoming = scan_stage(h)
            value_stage(h - 1, *pending)
            pending = upcoming
        value_stage(nheads - 1, *pending)

    za_ref[...] = score_fn(i - 1)
    tile(i, score_fn(i), True, diagonal=True)
    _paired_key_tiles(i, score_fn, tile, za_ref, zb_ref)
    o_ref[...] = acc_ref[...].astype(o_ref.dtype)


def _softmax_tile(st, col0, bias, off, va, g, first, m_ref, acc_ref):
    rows = slice(g * VA_ROWS, (g + 1) * VA_ROWS)
    parts, alphas = [], []
    for c0 in range(0, bias.shape[1], LANES):
        cs = slice(c0, c0 + LANES)
        s = st[:, col0 + c0:col0 + c0 + LANES] + bias[:, cs]
        blk_max = jnp.max(s, axis=0, keepdims=True)
        if first:
            m_new = blk_max
            parts.append(jnp.exp2(s - m_new).astype(BF16))
        else:
            m_old = m_ref[g:g + 1, cs]
            m_new = jnp.maximum(m_old, blk_max + off)
            parts.append(jnp.exp2(s - (m_new - off)).astype(BF16))
            alphas.append(jnp.exp2(m_old - m_new))
        m_ref[g:g + 1, cs] = m_new
    pv = _dot(va, jnp.concatenate(parts, axis=1))
    if first:
        acc_ref[rows, :] = pv
    else:
        acc_ref[rows, :] = jnp.concatenate(alphas, axis=1) * acc_ref[rows, :] + pv


def _normalised(acc_ref, g):
    base = g * VA_ROWS
    return acc_ref[base:base + HEAD_DIM, :] / acc_ref[base + HEAD_DIM:base + HEAD_DIM + 1, :]


def _ch_kernel(q_ref, k_ref, va_ref, bias_ref, o_ref, qs_ref, m_ref, acc_ref, *, tq, tk, nheads, nback):
    i = pl.program_id(1)
    _stack_masked_q(q_ref, qs_ref, tq, nheads, HEAD_DIM)

    def tile(n, first):
        j = i - n
        row0 = pl.multiple_of(j * tk, tk)
        st = _dot_nt(k_ref[pl.ds(row0, tk), :], qs_ref[...])
        for h in range(nheads):
            _softmax_tile(st, h * tq, bias_ref.at[h, n], 0.0, va_ref[j, h * VA_ROWS:(h + 1) * VA_ROWS, :],
                          h, first, m_ref, acc_ref)

    tile(0, True)

    def body(n, c):
        tile(n, False)
        return c

    lax.fori_loop(1, jnp.minimum(i, nback) + 1, body, 0)
    for h in range(nheads):
        o_ref[h * HEAD_DIM:(h + 1) * HEAD_DIM, :] = _normalised(acc_ref, h).astype(o_ref.dtype)


def _df_kernel(lam_ref, q_ref, k_ref, va_ref, kb_ref, db_ref, o_ref, qs_ref, m_ref, acc_ref, za_ref, zb_ref,
               *, tq, tk, nheads, slopes2):
    i = pl.program_id(1)
    _stack_masked_q(q_ref, qs_ref, tq, 2 * nheads, DF_QK_DIM)
    score_fn = functools.partial(_key_scores, k_ref, qs_ref, tk=tk)

    def tile(j, st, valid, diagonal=False):
        jv = jnp.maximum(j, 0)
        for g in range(2 * nheads):
            h = g // 2
            va = va_ref[jv, h * VA_ROWS:(h + 1) * VA_ROWS, :]
            if diagonal:
                _softmax_tile(st, g * tq, db_ref.at[h], 0.0, va, g, True, m_ref, acc_ref)
            else:
                off = jnp.where(valid, slopes2[h] * ((j - i) * tk).astype(F32), NEG_INF)
                _softmax_tile(st, g * tq, kb_ref.at[h], off, va, g, False, m_ref, acc_ref)

    za_ref[...] = score_fn(i - 1)
    tile(i, score_fn(i), True, diagonal=True)
    _paired_key_tiles(i, score_fn, tile, za_ref, zb_ref)
    lam = lam_ref[0]
    for h in range(nheads):
        out = _normalised(acc_ref, 2 * h) - lam * _normalised(acc_ref, 2 * h + 1)
        o_ref[h * HEAD_DIM:(h + 1) * HEAD_DIM, :] = out.astype(o_ref.dtype)


def _attention_call(kernel, qkv, col0, vt, extra, extra_specs, scratch, bsz, seq, name, smem=()):
    tq, tk = TQ_ATT, TK_ATT
    assert tq == tk
    nq = seq // tq
    vrows = vt.shape[2]
    return pl.pallas_call(
        kernel,
        out_shape=jax.ShapeDtypeStruct((bsz, GROUP_W, seq), BF16),
        grid=(bsz, nq),
        in_specs=[pl.BlockSpec(memory_space=pltpu.SMEM) for _ in smem] + [
            pl.BlockSpec((tq, GROUP_W), lambda b, i: (b * nq + i, col0)),
            pl.BlockSpec((seq, GROUP_W), lambda b, i: (b, col0 + 1)),
            pl.BlockSpec((None, seq // tk, vrows, tk), lambda b, i: (b, 0, 0, 0)),
        ] + extra_specs,
        out_specs=pl.BlockSpec((None, GROUP_W, tq), lambda b, i: (b, 0, i)),
        scratch_shapes=scratch,
        compiler_params=_cparams(("arbitrary", "arbitrary")),
        name=name,
    )(*smem, qkv, qkv, vt, *extra)


def _values_t(v, bsz, seq, nheads, ones_row):
    if ones_row:
        v4 = v.reshape(bsz, seq, nheads, HEAD_DIM)
        pad = jnp.zeros((bsz, seq, nheads, VA_ROWS - HEAD_DIM - 1), v.dtype)
        v = jnp.concatenate([v4, jnp.ones((bsz, seq, nheads, 1), v.dtype), pad], axis=-1)
    rows = v.size // (bsz * seq)
    return v.reshape(bsz, seq // TK_ATT, TK_ATT, rows).transpose(0, 1, 3, 2)


def _tokens_major(o, n):
    return o.transpose(0, 2, 1).reshape(n, GROUP_W)


def _ch_bias_tiles(rel_bias, tq, tk):
    nback = CH_LEFT_CHUNKS * CHUNK // tk
    n = jnp.arange(nback + 1)[:, None, None]
    s = jnp.arange(tk)[None, :, None]
    t = jnp.arange(tq)[None, None, :]
    delta = (n * tk + t) // CHUNK - s // CHUNK
    valid = (delta >= 0) & (delta <= CH_LEFT_CHUNKS)
    rel_key = (CH_LEFT_CHUNKS - delta) * CHUNK + s % CHUNK
    rel = CH_LEFT_CHUNKS * CHUNK + t % CHUNK - rel_key
    idx = jnp.clip(rel, -REL_CLIP, REL_CLIP) + REL_CLIP
    bias = rel_bias.astype(F32)[:, idx] * LOG2E
    return jnp.where(valid[None], bias, NEG_INF), nback


def _df_bias_tiles(slopes2, tq, tk):
    s = jnp.arange(tk, dtype=jnp.int32)[:, None]
    t = jnp.arange(tq, dtype=jnp.int32)[None, :]
    sl = jnp.asarray(slopes2, F32)[:, None, None]
    kb = sl * jnp.broadcast_to(s, (tk, tq)).astype(F32)
    allowed = (s // CHUNK) <= (t // CHUNK)
    db = jnp.where(allowed[None], sl * (t - jnp.abs(t - s)).astype(F32), NEG_INF)
    return kb, db


def _out_proj_kernel(ys_ref, ysb_ref, ych_ref, ydf_ref, g_ref, ind_ref, w_ref, x_ref, o_ref):
    acc = x_ref[...]
    for gi, y_ref in enumerate((ys_ref, ysb_ref, ych_ref, ydf_ref)):
        cs = slice(gi * GROUP_W, (gi + 1) * GROUP_W)
        y = y_ref[...].astype(F32)
        ms = _group_ms(y, ind_ref[...], 1.0 / HEAD_DIM)
        yn = (y * lax.rsqrt(ms + EPS) * g_ref[:, cs]).astype(BF16)
        acc = acc + _dot(yn, w_ref[cs, :])
    o_ref[...] = acc


def _out_proj(y_ssm_tm, y_sb, y_ch, y_df, g, ind64, w, x2, bsz, seq):
    n, dm = x2.shape
    tm = TM_PROJ
    ns = seq // tm
    const = lambda i: (0, 0)
    tile = pl.BlockSpec((tm, GROUP_W), lambda i: (i, 0))
    return pl.pallas_call(
        _out_proj_kernel,
        out_shape=jax.ShapeDtypeStruct((n, dm), F32),
        grid=(n // tm,),
        in_specs=[
            pl.BlockSpec((tm, GROUP_W), lambda i: (i % ns, i // ns)),
            tile, tile, tile,
            pl.BlockSpec((1, dm), const),
            pl.BlockSpec((GROUP_W, GROUP_W), const),
            pl.BlockSpec((dm, dm), const),
            pl.BlockSpec((tm, dm), lambda i: (i, 0)),
        ],
        out_specs=pl.BlockSpec((tm, dm), lambda i: (i, 0)),
        compiler_params=_cparams(("arbitrary",)),
        name="out_proj",
    )(y_ssm_tm, y_sb, y_ch, y_df, g, ind64, w, x2)


def _ffn_kernel(x_ref, g_ref, w1_ref, w3_ref, w2_ref, o_ref, hid_ref, *, tf):
    x = x_ref[...]
    ms = jnp.mean(x * x, axis=-1, keepdims=True)
    h = (x * lax.rsqrt(ms + EPS) * g_ref[...]).astype(BF16)
    dff = w1_ref.shape[1]
    for c in range(dff // tf):
        cs = slice(c * tf, (c + 1) * tf)
        a = _dot(h, w1_ref[:, cs])
        b = _dot(h, w3_ref[:, cs])
        hid_ref[:, cs] = (jax.nn.silu(a) * b).astype(BF16)
    o_ref[...] = x + _dot(hid_ref[...], w2_ref[...])


def _ffn(x2, g, w1, w3, w2):
    n, dm = x2.shape
    dff = w1.shape[1]
    tm = TM_PROJ
    const = lambda i: (0, 0)
    return pl.pallas_call(
        functools.partial(_ffn_kernel, tf=GROUP_W),
        out_shape=jax.ShapeDtypeStruct((n, dm), F32),
        grid=(n // tm,),
        in_specs=[
            pl.BlockSpec((tm, dm), lambda i: (i, 0)),
            pl.BlockSpec((1, dm), const),
            pl.BlockSpec((dm, dff), const),
            pl.BlockSpec((dm, dff), const),
            pl.BlockSpec((dff, dm), const),
        ],
        out_specs=pl.BlockSpec((tm, dm), lambda i: (i, 0)),
        scratch_shapes=[pltpu.VMEM((tm, dff), BF16)],
        compiler_params=_cparams(("arbitrary",)),
        name="ffn_dense",
    )(x2, g, w1, w3, w2)


def _router_kernel(x_ref, g_ref, wr_ref, tri_ref, h_ref, route_ref, cnt_ref, carry_ref):
    @pl.when(pl.program_id(0) == 0)
    def _():
        carry_ref[...] = jnp.zeros_like(carry_ref)

    x = x_ref[...]
    ms = jnp.mean(x * x, axis=-1, keepdims=True)
    h = x * lax.rsqrt(ms + EPS) * g_ref[...]
    h_hi = h.astype(BF16)
    half = h.shape[1] // 2
    bits = lax.bitcast_convert_type(h_hi.astype(F32), jnp.uint32)
    h_ref[...] = bits[:, :half] | lax.shift_right_logical(bits[:, half:], jnp.uint32(16))
    h_lo = (h - h_hi.astype(F32)).astype(BF16)
    logits = _dot(h_hi, wr_ref[0]) + _dot(h_lo, wr_ref[0]) + _dot(h_hi, wr_ref[1])

    lane = _iota2(logits.shape, 1).astype(F32)
    lg = jnp.where(lane < N_EXPERTS, logits, -jnp.inf)
    m1 = jnp.max(lg, axis=-1, keepdims=True)
    e1 = jnp.min(jnp.where(lg == m1, lane, float(LANES)), axis=-1, keepdims=True)
    lg2 = jnp.where(lane == e1, -jnp.inf, lg)
    m2 = jnp.max(lg2, axis=-1, keepdims=True)
    e2 = jnp.min(jnp.where(lg2 == m2, lane, float(LANES)), axis=-1, keepdims=True)
    t = jnp.exp(m2 - m1)
    g1 = 1.0 / (1.0 + t)
    g2 = t / (1.0 + t)

    hot1 = lane == e1
    hot2 = lane == e2
    sel = jnp.where(hot1 | hot2, 1.0, 0.0).astype(BF16)
    prior = _dot(tri_ref[...], sel) + carry_ref[...]
    r1 = jnp.sum(jnp.where(hot1, prior, 0.0), axis=-1, keepdims=True)
    r2 = jnp.sum(jnp.where(hot2, prior, 0.0), axis=-1, keepdims=True)
    carry_ref[...] = carry_ref[...] + jnp.sum(sel.astype(F32), axis=0, keepdims=True)
    cnt_ref[...] = carry_ref[...]

    out = jnp.where(lane == 0, e1, 0.0)
    out = jnp.where(lane == 1, e2, out)
    out = jnp.where(lane == 2, g1, out)
    out = jnp.where(lane == 3, g2, out)
    out = jnp.where(lane == 4, r1, out)
    out = jnp.where(lane == 5, r2, out)
    route_ref[...] = out


def _router(x2, g, wr2, tri):
    n, dm = x2.shape
    tm = TM_PROJ
    const = lambda i: (0, 0)
    return pl.pallas_call(
        _router_kernel,
        out_shape=(jax.ShapeDtypeStruct((n, dm // 2), jnp.uint32),
                   jax.ShapeDtypeStruct((n, LANES), F32),
                   jax.ShapeDtypeStruct((1, LANES), F32)),
        grid=(n // tm,),
        in_specs=[
            pl.BlockSpec((tm, dm), lambda i: (i, 0)),
            pl.BlockSpec((1, dm), const),
            pl.BlockSpec((2, dm, LANES), lambda i: (0, 0, 0)),
            pl.BlockSpec((tm, tm), const),
        ],
        out_specs=(
            pl.BlockSpec((tm, dm // 2), lambda i: (i, 0)),
            pl.BlockSpec((tm, LANES), lambda i: (i, 0)),
            pl.BlockSpec((1, LANES), const),
        ),
        scratch_shapes=[pltpu.VMEM((1, LANES), F32)],
        compiler_params=_cparams(("arbitrary",)),
        name="moe_router",
    )(x2, g, wr2, tri)


def _row_copy(src_ref, src_row, dst_ref, dst_row, sem):
    return pltpu.make_async_copy(src_ref.at[pl.ds(src_row, 1)], dst_ref.at[pl.ds(dst_row, 1)], sem)


def _dispatch_kernel(d1_ref, d2_ref, cnt_ref, h_ref, xs_ref, zero_ref, sem, tail_sem, *, tm, cap, nexp):
    def issue(r, c):
        _row_copy(h_ref, r, xs_ref, d1_ref[0, 0, r], sem).start()
        _row_copy(h_ref, r, xs_ref, d2_ref[0, 0, r], sem).start()
        return c

    lax.fori_loop(0, tm, issue, 0, unroll=8)

    def aligned_end(e):
        return ((cnt_ref[e] + (SUBLANES - 1)) // SUBLANES) * SUBLANES

    def tail_copy(e):
        row0 = pl.multiple_of(e * cap + aligned_end(e), SUBLANES)
        return pltpu.make_async_copy(zero_ref, xs_ref.at[pl.ds(row0, tm)], tail_sem)

    def tail_rows(e, fn):
        for k in range(SUBLANES - 1):
            @pl.when(cnt_ref[e] + k < aligned_end(e))
            def _():
                fn(_row_copy(zero_ref, 0, xs_ref, e * cap + cnt_ref[e] + k, tail_sem))

    @pl.when(pl.program_id(0) == pl.num_programs(0) - 1)
    def _():
        zero_ref[...] = jnp.zeros_like(zero_ref)
        for e in range(nexp):
            tail_copy(e).start()
            tail_rows(e, lambda cp: cp.start())
        for e in range(nexp):
            tail_copy(e).wait()
            tail_rows(e, lambda cp: cp.wait())

    for _ in range(2):
        pltpu.make_async_copy(h_ref, xs_ref.at[pl.ds(0, tm)], sem).wait()


def _dispatch(d1, d2, cnt, h, cap):
    n, wd = h.shape
    tm = TM_PROJ
    nt = n // tm
    idx_spec = pl.BlockSpec((1, 1, tm), lambda i: (i, 0, 0), memory_space=pltpu.SMEM)
    return pl.pallas_call(
        functools.partial(_dispatch_kernel, tm=tm, cap=cap, nexp=N_EXPERTS),
        out_shape=jax.ShapeDtypeStruct((N_EXPERTS * cap, wd), h.dtype),
        grid=(nt,),
        in_specs=[idx_spec, idx_spec, pl.BlockSpec(memory_space=pltpu.SMEM),
                  pl.BlockSpec((tm, wd), lambda i: (i, 0))],
        out_specs=pl.BlockSpec(memory_space=pl.ANY),
        scratch_shapes=[pltpu.VMEM((tm, wd), h.dtype), pltpu.SemaphoreType.DMA, pltpu.SemaphoreType.DMA],
        compiler_params=pltpu.CompilerParams(dimension_semantics=("arbitrary",),
                                             vmem_limit_bytes=VMEM_LIMIT_BYTES, disable_bounds_checks=True),
        name="moe_dispatch",
    )(d1.reshape(nt, 1, tm), d2.reshape(nt, 1, tm), cnt, h)


def _combine_kernel(d1_ref, d2_ref, x_ref, route_ref, ys_ref, o_ref, y1_ref, y2_ref, sem, *, tm):
    def issue(r, c):
        _row_copy(ys_ref, d1_ref[0, 0, r], y1_ref, r, sem).start()
        _row_copy(ys_ref, d2_ref[0, 0, r], y2_ref, r, sem).start()
        return c

    lax.fori_loop(0, tm, issue, 0, unroll=8)
    for buf in (y1_ref, y2_ref):
        pltpu.make_async_copy(ys_ref.at[pl.ds(0, tm)], buf, sem).wait()
    o_ref[...] = x_ref[...] + route_ref[:, 2:3] * y1_ref[...] + route_ref[:, 3:4] * y2_ref[...]


def _combine(d1, d2, x2, route, ys):
    n, dm = x2.shape
    tm = TM_PROJ
    nt = n // tm
    idx_spec = pl.BlockSpec((1, 1, tm), lambda i: (i, 0, 0), memory_space=pltpu.SMEM)
    return pl.pallas_call(
        functools.partial(_combine_kernel, tm=tm),
        out_shape=jax.ShapeDtypeStruct((n, dm), F32),
        grid=(nt,),
        in_specs=[idx_spec, idx_spec, pl.BlockSpec((tm, dm), lambda i: (i, 0)),
                  pl.BlockSpec((tm, LANES), lambda i: (i, 0)), pl.BlockSpec(memory_space=pl.ANY)],
        out_specs=pl.BlockSpec((tm, dm), lambda i: (i, 0)),
        scratch_shapes=[pltpu.VMEM((tm, dm), F32), pltpu.VMEM((tm, dm), F32), pltpu.SemaphoreType.DMA],
        compiler_params=pltpu.CompilerParams(dimension_semantics=("arbitrary",),
                                             vmem_limit_bytes=VMEM_LIMIT_BYTES, disable_bounds_checks=True),
        name="moe_combine",
    )(d1.reshape(nt, 1, tm), d2.reshape(nt, 1, tm), x2, route, ys)


def _experts_kernel(te_ref, blk_ref, nused_ref, xs_ref, w1_ref, w3_ref, w2_ref, o_ref, x_ref, acc_ref):
    t = pl.program_id(0)
    c = pl.program_id(1)
    nc = pl.num_programs(1)

    @pl.when(jnp.logical_and(t < nused_ref[0], c == 0))
    def _():
        w = xs_ref[...]
        half = w.shape[1]
        x_ref[:, :half] = lax.bitcast_convert_type(w & jnp.uint32(0xFFFF0000), F32).astype(BF16)
        x_ref[:, half:] = lax.bitcast_convert_type(lax.shift_left(w, jnp.uint32(16)), F32).astype(BF16)

    @pl.when(t < nused_ref[0])
    def _():
        x = x_ref[...]
        a = _dot(x, w1_ref[0])
        b = _dot(x, w3_ref[0])
        hid = (jax.nn.silu(a) * b).astype(BF16)
        part = _dot(hid, w2_ref[0])

        @pl.when(c == 0)
        def _():
            acc_ref[...] = part

        @pl.when(c > 0)
        def _():
            acc_ref[...] = acc_ref[...] + part

        @pl.when(c == nc - 1)
        def _():
            o_ref[...] = acc_ref[...]

    @pl.when(jnp.logical_and(t >= nused_ref[0], c == nc - 1))
    def _():
        o_ref[...] = jnp.zeros_like(o_ref)


def _experts(tile_expert, tile_block, nused, xs, w1, w3, w2, nt):
    rows, wd = xs.shape
    dm = 2 * wd
    dff = w1.shape[2]
    tm, tf = TM_EXPERT, TF_EXPERT
    nc = dff // tf
    spare = rows // tm

    def cc(t, c, nu):
        return jnp.where(t < nu[0], c, nc - 1)

    grid_spec = pltpu.PrefetchScalarGridSpec(
        num_scalar_prefetch=3,
        grid=(nt, nc),
        in_specs=[
            pl.BlockSpec((tm, wd), lambda t, c, te, tb, nu: (tb[t], 0)),
            pl.BlockSpec((1, dm, tf), lambda t, c, te, tb, nu: (te[t], 0, cc(t, c, nu))),
            pl.BlockSpec((1, dm, tf), lambda t, c, te, tb, nu: (te[t], 0, cc(t, c, nu))),
            pl.BlockSpec((1, tf, dm), lambda t, c, te, tb, nu: (te[t], cc(t, c, nu), 0)),
        ],
        out_specs=pl.BlockSpec((tm, dm), lambda t, c, te, tb, nu: (jnp.where(t < nu[0], tb[t], spare), 0)),
        scratch_shapes=[pltpu.VMEM((tm, dm), BF16), pltpu.VMEM((tm, dm), F32)],
    )
    return pl.pallas_call(
        _experts_kernel,
        out_shape=jax.ShapeDtypeStruct((rows + tm, dm), F32),
        grid_spec=grid_spec,
        compiler_params=_cparams(("arbitrary", "arbitrary")),
        name="moe_experts",
    )(tile_expert, tile_block, nused, xs, w1, w3, w2)


def _moe(x2, g, w_router, w1, w3, w2, tri):
    n, dm = x2.shape
    wr = jnp.zeros((dm, LANES), F32).at[:, :N_EXPERTS].set(w_router.astype(F32))
    wr_hi = wr.astype(BF16)
    wr_lo = (wr - wr_hi.astype(F32)).astype(BF16)
    h, route, counts = _router(x2, g, jnp.stack([wr_hi, wr_lo]), tri)

    tm = TM_EXPERT
    assert n % tm == 0 and TM_PROJ == tm
    cap = n + tm
    d1 = route[:, 0].astype(jnp.int32) * cap + route[:, 4].astype(jnp.int32)
    d2 = route[:, 1].astype(jnp.int32) * cap + route[:, 5].astype(jnp.int32)
    cnt = counts[0, :N_EXPERTS].astype(jnp.int32)

    nt = 2 * n // tm + N_EXPERTS
    ntile = (cnt + tm - 1) // tm
    ends = jnp.cumsum(ntile)
    nused = ends[-1]
    t = jnp.minimum(jnp.arange(nt, dtype=jnp.int32), nused - 1)
    tile_expert = jnp.sum(t[:, None] >= ends[None, :], axis=1).astype(jnp.int32)
    tile_block = tile_expert * (cap // tm) + t - (ends - ntile)[tile_expert]

    xs = _dispatch(d1, d2, cnt, h, cap)
    ys = _experts(tile_expert, tile_block.astype(jnp.int32), nused.reshape(1).astype(jnp.int32), xs, w1, w3, w2, nt)
    return _combine(d1, d2, x2, route, ys)


def kernel(x, norm_mix_g, w_in, ssm_lam_re, ssm_lam_im, ssm_log_dt, ssm_b_re, ssm_b_im, ssm_c_re, ssm_c_im,
           ssm_d, ssm_w_glu, ssm_b_glu, ch_q_norm_g, ch_k_norm_g, ch_rel_bias, df_q_norm_g, df_k_norm_g,
           df_lambda, out_norm_g, w_out, norm_ffn_g, ffn_w1, ffn_w3, ffn_w2, moe_router, moe_w1, moe_w3, moe_w2):
    bsz, seq, dm = x.shape
    depth = w_in.shape[0]
    n = bsz * seq
    nheads = GROUP_W // HEAD_DIM
    ind64 = _block_indicator(GROUP_W, HEAD_DIM)
    ind32 = _block_indicator(GROUP_W, DF_QK_DIM)
    tri = (jnp.arange(TM_PROJ)[:, None] > jnp.arange(TM_PROJ)[None, :]).astype(BF16)
    slopes = [2.0 ** (-8.0 * (h + 1) / nheads) for h in range(nheads)]
    slopes2 = tuple(s * LOG2E for s in slopes)
    df_kb, df_db = _df_bias_tiles(slopes2, TQ_ATT, TK_ATT)
    later = (jnp.arange(TK_ATT)[:, None] < jnp.arange(TK_ATT)[None, :]).astype(BF16)
    later = jnp.concatenate([later, later], axis=1)

    x2 = x.reshape(n, dm)
    for layer in range(depth):
        qkg = jnp.stack([
            jnp.tile(ch_q_norm_g[layer].astype(F32), nheads) * (HEAD_DIM ** -0.5 * LOG2E),
            jnp.tile(ch_k_norm_g[layer].astype(F32), nheads),
            jnp.tile(df_q_norm_g[layer].astype(F32).reshape(-1), nheads) * (DF_QK_DIM ** -0.5 * LOG2E),
            jnp.tile(df_k_norm_g[layer].astype(F32).reshape(-1), nheads),
        ])
        u_tm, qkv = _in_proj(x2, norm_mix_g[layer].reshape(1, dm), w_in[layer].astype(BF16), qkg,
                             ind64, ind32, bsz, seq)

        wb, lre, lim, wc = _s5_params(ssm_lam_re[layer], ssm_lam_im[layer], ssm_log_dt[layer],
                                      ssm_b_re[layer], ssm_b_im[layer], ssm_c_re[layer], ssm_c_im[layer], bsz)
        y_ssm = _s5(u_tm.reshape(seq * bsz, GROUP_W), wb, lre, lim, wc, ssm_d[layer].reshape(1, GROUP_W),
                    ssm_w_glu[layer].astype(BF16), ssm_b_glu[layer].reshape(1, GROUP_W), bsz)
        y_ssm = y_ssm.reshape(seq, bsz * GROUP_W)

        tq, tk = TQ_ATT, TK_ATT
        const3 = lambda b, i: (0, 0, 0)
        y_sb = _attention_call(
            functools.partial(_sb_kernel, tq=tq, tk=tk, nheads=nheads),
            qkv, 0, _values_t(qkv[:, 2 * GROUP_W:3 * GROUP_W], bsz, seq, nheads, False),
            (later,), [pl.BlockSpec((tk, 2 * tk), lambda b, i: (0, 0))],
            [pltpu.VMEM((nheads * tq, GROUP_W), BF16), pltpu.VMEM((SUBLANES, tq), F32),
             pltpu.VMEM((GROUP_W, tq), F32),
             pltpu.VMEM((tk, nheads * tq), F32), pltpu.VMEM((tk, nheads * tq), F32)],
            bsz, seq, "sb_attention")

        bias, nback = _ch_bias_tiles(ch_rel_bias[layer], tq, tk)
        y_ch = _attention_call(
            functools.partial(_ch_kernel, tq=tq, tk=tk, nheads=nheads, nback=nback),
            qkv, 3, _values_t(qkv[:, 5 * GROUP_W:6 * GROUP_W], bsz, seq, nheads, True),
            (bias,), [pl.BlockSpec(bias.shape, lambda b, i: (0, 0, 0, 0))],
            [pltpu.VMEM((nheads * tq, GROUP_W), BF16), pltpu.VMEM((SUBLANES, tq), F32),
             pltpu.VMEM((nheads * VA_ROWS, tq), F32)],
            bsz, seq, "ch_attention")

        lambda_init = 0.8 - 0.6 * math.exp(-0.3 * layer)
        lam_p = df_lambda[layer].astype(F32)
        lam = jnp.exp(jnp.sum(lam_p[0] * lam_p[1])) - jnp.exp(jnp.sum(lam_p[2] * lam_p[3])) + lambda_init
        y_df = _attention_call(
            functools.partial(_df_kernel, tq=tq, tk=tk, nheads=nheads, slopes2=slopes2),
            qkv, 6, _values_t(qkv[:, 8 * GROUP_W:9 * GROUP_W], bsz, seq, nheads, True),
            (df_kb, df_db), [pl.BlockSpec(df_kb.shape, const3), pl.BlockSpec(df_db.shape, const3)],
            [pltpu.VMEM((2 * nheads * tq, GROUP_W), BF16), pltpu.VMEM((2 * nheads, tq), F32),
             pltpu.VMEM((2 * nheads * VA_ROWS, tq), F32),
             pltpu.VMEM((tk, 2 * nheads * tq), F32), pltpu.VMEM((tk, 2 * nheads * tq), F32)],
            bsz, seq, "df_attention", smem=(lam.reshape(1),))
        y_sb, y_ch, y_df = (_tokens_major(y, n) for y in (y_sb, y_ch, y_df))

        head_scale = jnp.concatenate([jnp.ones((dm - GROUP_W,), F32),
                                      jnp.full((GROUP_W,), 1.0 - lambda_init, F32)])
        g_out = (out_norm_g[layer].astype(F32) * head_scale).reshape(1, dm)
        x2 = _out_proj(y_ssm, y_sb, y_ch, y_df, g_out, ind64, w_out[layer].astype(BF16), x2, bsz, seq)

        idx = layer // 2
        g_ffn = norm_ffn_g[layer].reshape(1, dm)
        if layer % 2 == 0:
            x2 = _ffn(x2, g_ffn, ffn_w1[idx].astype(BF16), ffn_w3[idx].astype(BF16), ffn_w2[idx].astype(BF16))
        else:
            x2 = _moe(x2, g_ffn, moe_router[idx], moe_w1[idx].astype(BF16), moe_w3[idx].astype(BF16),
                      moe_w2[idx].astype(BF16), tri)
    return x2.reshape(bsz, seq, dm)
```

```python
import functools
import math

import jax
import jax.numpy as jnp
import numpy as np
from jax import lax
from jax.experimental import pallas as pl
from jax.experimental.pallas import tpu as pltpu

F32 = jnp.float32
BF16 = jnp.bfloat16

EPS = 1e-6
NEG_INF = -1e30
HEAD_DIM = 64
GROUP_W = 256
CHUNK = 64
SSM_GROUP = 16
SSM_STATE = 64
DF_QK_DIM = 32
CH_LEFT_CHUNKS = 8
REL_CLIP = 128
N_EXPERTS = 8
LOG2E = 1.4426950408889634

VMEM_LIMIT_BYTES = 56 * 1024 * 1024
SUBLANES = 8
LANES = 128

TM_PROJ = 512
TQ_ATT = 256
TK_ATT = 256
SSM_STEPS = 64
TM_EXPERT = 512
TF_EXPERT = 1792


def _cparams(sem):
    return pltpu.CompilerParams(dimension_semantics=sem, vmem_limit_bytes=VMEM_LIMIT_BYTES)


def _dot(a, b):
    return jnp.dot(a, b, preferred_element_type=F32)


def _dot_nt(a, b):
    return lax.dot_general(a, b, (((1,), (1,)), ((), ())), preferred_element_type=F32)


def _split_dot(v, m):
    hi = v.astype(BF16)
    lo = (v - hi.astype(F32)).astype(BF16)
    return _dot(hi, m) + _dot(lo, m)


def _group_ms(v, ind, inv_size):
    return _split_dot(v * v, ind) * inv_size


def _block_indicator(width, group):
    r = jnp.arange(width)[:, None] // group
    c = jnp.arange(width)[None, :] // group
    return (r == c).astype(BF16)


def _in_proj_kernel(x_ref, g_ref, w_ref, qkg_ref, ind64_ref, ind32_ref, u_ref, qkv_ref):
    x = x_ref[...]
    ms = jnp.mean(x * x, axis=-1, keepdims=True)
    h = (x * lax.rsqrt(ms + EPS) * g_ref[...]).astype(BF16)

    def cols(c):
        return _dot(h, w_ref[:, c * GROUP_W:(c + 1) * GROUP_W])

    def put(c, val):
        qkv_ref[:, (c - 1) * GROUP_W:c * GROUP_W] = val.astype(BF16)

    u_ref[...] = cols(0)
    put(1, cols(1) * (HEAD_DIM ** -0.5 * LOG2E))
    put(2, cols(2))
    put(3, cols(3))
    for c, gi in ((4, 0), (5, 1)):
        a = cols(c)
        ms_h = _group_ms(a, ind64_ref[...], 1.0 / HEAD_DIM)
        put(c, a * lax.rsqrt(ms_h + EPS) * qkg_ref[gi:gi + 1, :])
    put(6, cols(6))
    for c, gi in ((7, 2), (8, 3)):
        a = cols(c)
        ms_h = _group_ms(a, ind32_ref[...], 1.0 / DF_QK_DIM)
        put(c, a * lax.rsqrt(ms_h + EPS) * qkg_ref[gi:gi + 1, :])
    put(9, cols(9))


def _in_proj(x2, g, w, qkg, ind64, ind32, bsz, seq):
    n, dm = x2.shape
    tm = TM_PROJ
    ns = seq // tm
    ncols = w.shape[1]
    const = lambda i: (0, 0)
    return pl.pallas_call(
        _in_proj_kernel,
        out_shape=(jax.ShapeDtypeStruct((seq, bsz * GROUP_W), F32),
                   jax.ShapeDtypeStruct((n, ncols - GROUP_W), BF16)),
        grid=(n // tm,),
        in_specs=[
            pl.BlockSpec((tm, dm), lambda i: (i, 0)),
            pl.BlockSpec((1, dm), const),
            pl.BlockSpec((dm, ncols), const),
            pl.BlockSpec((4, GROUP_W), const),
            pl.BlockSpec((GROUP_W, GROUP_W), const),
            pl.BlockSpec((GROUP_W, GROUP_W), const),
        ],
        out_specs=(
            pl.BlockSpec((tm, GROUP_W), lambda i: (i % ns, i // ns)),
            pl.BlockSpec((tm, ncols - GROUP_W), lambda i: (i, 0)),
        ),
        compiler_params=_cparams(("arbitrary",)),
        name="in_proj",
    )(x2, g, w, qkg, ind64, ind32)


def _s5_kernel(u_ref, wb_ref, lre_ref, lim_ref, wc_ref, d_ref, wg_ref, bg_ref, y_ref, bu_ref, st_ref,
               *, steps, nstate):
    @pl.when(pl.program_id(0) == 0)
    def _():
        st_ref[...] = jnp.zeros_like(st_ref)

    u = u_ref[...]
    bu_ref[...] = _dot(u.astype(BF16), wb_ref[...])
    lre = lre_ref[...]
    lim = lim_ref[...]

    def step(t, carry):
        sre, sim = carry
        r = pl.multiple_of(t * SUBLANES, SUBLANES)
        bre = bu_ref[pl.ds(r, SUBLANES), 0:nstate]
        bim = bu_ref[pl.ds(r, SUBLANES), nstate:2 * nstate]
        nre = lre * sre - lim * sim + bre
        nim = lre * sim + lim * sre + bim
        bu_ref[pl.ds(r, SUBLANES), 0:nstate] = nre
        bu_ref[pl.ds(r, SUBLANES), nstate:2 * nstate] = nim
        return nre, nim

    sre, sim = lax.fori_loop(0, steps, step, (st_ref[:, 0:nstate], st_ref[:, nstate:2 * nstate]))
    st_ref[:, 0:nstate] = sre
    st_ref[:, nstate:2 * nstate] = sim

    y = _dot(bu_ref[...].astype(BF16), wc_ref[...]) + d_ref[...] * u
    y = jax.nn.gelu(y)
    gate = jax.nn.sigmoid(_dot(y.astype(BF16), wg_ref[...]) + bg_ref[...])
    y_ref[...] = (y * gate).astype(y_ref.dtype)


def _s5(u_tm, wb, lre, lim, wc, d, wg, bg, bsz):
    rows, w = u_tm.shape
    assert bsz == SUBLANES
    blk = SSM_STEPS * bsz
    nstate2 = wb.shape[1]
    const = lambda c: (0, 0)
    return pl.pallas_call(
        functools.partial(_s5_kernel, steps=SSM_STEPS, nstate=nstate2 // 2),
        out_shape=jax.ShapeDtypeStruct((rows, w), BF16),
        grid=(rows // blk,),
        in_specs=[
            pl.BlockSpec((blk, w), lambda c: (c, 0)),
            pl.BlockSpec((w, nstate2), const),
            pl.BlockSpec((bsz, nstate2 // 2), const),
            pl.BlockSpec((bsz, nstate2 // 2), const),
            pl.BlockSpec((nstate2, w), const),
            pl.BlockSpec((1, w), const),
            pl.BlockSpec((w, w), const),
            pl.BlockSpec((1, w), const),
        ],
        out_specs=pl.BlockSpec((blk, w), lambda c: (c, 0)),
        scratch_shapes=[pltpu.VMEM((blk, nstate2), F32), pltpu.VMEM((bsz, nstate2), F32)],
        compiler_params=_cparams(("arbitrary",)),
        name="s5_mixer",
    )(u_tm, wb, lre, lim, wc, d, wg, bg)


def _s5_params(lam_re, lam_im, log_dt, b_re, b_im, c_re, c_im, bsz):
    g, n = lam_re.shape
    dt = jnp.exp(log_dt.astype(F32))[:, None]
    xr = lam_re * dt
    th = lam_im * dt
    er = jnp.exp(xr)
    lbr = er * jnp.cos(th)
    lbi = er * jnp.sin(th)
    ar = jnp.expm1(xr) * jnp.cos(th) - 2.0 * jnp.sin(0.5 * th) ** 2
    ai = lbi
    den = lam_re * lam_re + lam_im * lam_im
    fr = (ar * lam_re + ai * lam_im) / den
    fi = (ai * lam_re - ar * lam_im) / den
    bbr = fr[..., None] * b_re - fi[..., None] * b_im
    bbi = fr[..., None] * b_im + fi[..., None] * b_re
    eye = jnp.eye(g, dtype=F32)
    p = b_re.shape[-1]
    wb_re = jnp.einsum("gnp,gh->gphn", bbr, eye).reshape(g * p, g * n)
    wb_im = jnp.einsum("gnp,gh->gphn", bbi, eye).reshape(g * p, g * n)
    wb = jnp.concatenate([wb_re, wb_im], axis=1).astype(BF16)
    wc_re = jnp.einsum("gpn,gh->gnhp", c_re, eye).reshape(g * n, g * p)
    wc_im = jnp.einsum("gpn,gh->gnhp", c_im, eye).reshape(g * n, g * p)
    wc = jnp.concatenate([wc_re, -wc_im], axis=0).astype(BF16)
    lre = jnp.broadcast_to(lbr.reshape(1, g * n), (bsz, g * n))
    lim = jnp.broadcast_to(lbi.reshape(1, g * n), (bsz, g * n))
    return wb, lre, lim, wc


VA_ROWS = 2 * HEAD_DIM


def _iota2(shape, dim):
    return lax.broadcasted_iota(jnp.int32, shape, dim)


def _stack_masked_q(q_ref, qs_ref, tq, ngroups, width):
    lane = _iota2((1, GROUP_W), 1)
    q = q_ref[...]
    for g in range(ngroups):
        keep = (lane >= g * width) & (lane < (g + 1) * width)
        qs_ref[g * tq:(g + 1) * tq, :] = jnp.where(keep, q, jnp.zeros_like(q))


def _key_scores(k_ref, qs_ref, j, tk):
    row0 = pl.multiple_of(jnp.maximum(j, 0) * tk, tk)
    return _dot_nt(k_ref[pl.ds(row0, tk), :], qs_ref[...])


def _paired_key_tiles(i, score_fn, tile_fn, za_ref, zb_ref):
    def body(n, c):
        ja = i - 1 - 2 * n
        zb_ref[...] = score_fn(ja - 1)
        tile_fn(ja, za_ref, True)
        za_ref[...] = score_fn(ja - 2)
        tile_fn(ja - 1, zb_ref, ja >= 1)
        return c

    lax.fori_loop(0, (i + 1) // 2, body, 0)


def _sb_kernel(q_ref, k_ref, vt_ref, later_ref, o_ref, qs_ref, carry_ref, acc_ref, za_ref, zb_ref,
               *, tq, tk, nheads):
    i = pl.program_id(1)
    _stack_masked_q(q_ref, qs_ref, tq, nheads, HEAD_DIM)
    score_fn = functools.partial(_key_scores, k_ref, qs_ref, tk=tk)

    def tile(j, zt, valid, diagonal=False):
        if diagonal:
            before = _iota2((tk, tq), 0) < _iota2((tk, tq), 1)
        else:
            penalty = jnp.where(valid, 0.0, -NEG_INF)
            keep = jnp.where(valid, 1.0, 0.0)
        jv = jnp.maximum(j, 0)

        def scan_stage(h):
            z = zt[:, h * tq:(h + 1) * tq]
            neg_abs = lax.bitcast_convert_type(
                lax.bitcast_convert_type(z, jnp.uint32) | jnp.uint32(0x80000000), F32)
            sp = jnp.maximum(z, 0.0) + jnp.log2(1.0 + jnp.exp2(neg_abs))
            nl1m = jnp.where(before, sp, 0.0) if diagonal else sp
            hi = nl1m.astype(BF16)
            lo = (nl1m - hi.astype(F32)).astype(BF16)
            inside = _dot(later_ref[...], jnp.concatenate([hi, lo], axis=0))
            return z - sp, inside, nl1m[0:1, :]

        def value_stage(h, logit, inside, first_row):
            hs = slice(h * HEAD_DIM, (h + 1) * HEAD_DIM)
            between = inside if diagonal else inside + (carry_ref[h:h + 1, :] + penalty)
            w = jnp.exp2(logit - between)
            if diagonal:
                w = jnp.where(before, w, 0.0)
            pv = _dot(vt_ref[jv, hs, :], w.astype(BF16))
            total = inside[0:1, :] + first_row
            if diagonal:
                acc_ref[hs, :] = pv
                carry_ref[h:h + 1, :] = total
            else:
                acc_ref[hs, :] = acc_ref[hs, :] + pv
                carry_ref[h:h + 1, :] = carry_ref[h:h + 1, :] + keep * total

        pending = scan_stage(0)
        for h in range(1, nheads):
            upcoming = scan_stage(h)
            value_stage(h - 1, *pending)
            pending = upcoming
        value_stage(nheads - 1, *pending)

    za_ref[...] = score_fn(i - 1)
    tile(i, score_fn(i), True, diagonal=True)
    _paired_key_tiles(i, score_fn, tile, za_ref, zb_ref)
    o_ref[...] = acc_ref[...].astype(o_ref.dtype)


def _softmax_tile(st, col0, bias, off, va, g, first, m_ref, acc_ref):
    rows = slice(g * VA_ROWS, (g + 1) * VA_ROWS)
    parts, alphas = [], []
    for c0 in range(0, bias.shape[1], LANES):
        cs = slice(c0, c0 + LANES)
        s = st[:, col0 + c0:col0 + c0 + LANES] + bias[:, cs]
        blk_max = jnp.max(s, axis=0, keepdims=True)
        if first:
            m_new = blk_max
            parts.append(jnp.exp2(s - m_new).astype(BF16))
        else:
            m_old = m_ref[g:g + 1, cs]
            m_new = jnp.maximum(m_old, blk_max + off)
            parts.append(jnp.exp2(s - (m_new - off)).astype(BF16))
            alphas.append(jnp.exp2(m_old - m_new))
        m_ref[g:g + 1, cs] = m_new
    pv = _dot(va, jnp.concatenate(parts, axis=1))
    if first:
        acc_ref[rows, :] = pv
    else:
        acc_ref[rows, :] = jnp.concatenate(alphas, axis=1) * acc_ref[rows, :] + pv


def _normalised(acc_ref, g):
    base = g * VA_ROWS
    return acc_ref[base:base + HEAD_DIM, :] / acc_ref[base + HEAD_DIM:base + HEAD_DIM + 1, :]


def _ch_kernel(q_ref, k_ref, va_ref, bias_ref, o_ref, qs_ref, m_ref, acc_ref, *, tq, tk, nheads, nback):
    i = pl.program_id(1)
    _stack_masked_q(q_ref, qs_ref, tq, nheads, HEAD_DIM)

    def tile(n, first):
        j = i - n
        row0 = pl.multiple_of(j * tk, tk)
        st = _dot_nt(k_ref[pl.ds(row0, tk), :], qs_ref[...])
        for h in range(nheads):
            _softmax_tile(st, h * tq, bias_ref.at[h, n], 0.0, va_ref[j, h * VA_ROWS:(h + 1) * VA_ROWS, :],
                          h, first, m_ref, acc_ref)

    tile(0, True)

    def body(n, c):
        tile(n, False)
        return c

    lax.fori_loop(1, jnp.minimum(i, nback) + 1, body, 0)
    for h in range(nheads):
        o_ref[h * HEAD_DIM:(h + 1) * HEAD_DIM, :] = _normalised(acc_ref, h).astype(o_ref.dtype)


def _df_kernel(lam_ref, q_ref, k_ref, va_ref, kb_ref, db_ref, o_ref, qs_ref, m_ref, acc_ref, za_ref, zb_ref,
               *, tq, tk, nheads, slopes2):
    i = pl.program_id(1)
    _stack_masked_q(q_ref, qs_ref, tq, 2 * nheads, DF_QK_DIM)
    score_fn = functools.partial(_key_scores, k_ref, qs_ref, tk=tk)

    def tile(j, st, valid, diagonal=False):
        jv = jnp.maximum(j, 0)
        for g in range(2 * nheads):
            h = g // 2
            va = va_ref[jv, h * VA_ROWS:(h + 1) * VA_ROWS, :]
            if diagonal:
                _softmax_tile(st, g * tq, db_ref.at[h], 0.0, va, g, True, m_ref, acc_ref)
            else:
                off = jnp.where(valid, slopes2[h] * ((j - i) * tk).astype(F32), NEG_INF)
                _softmax_tile(st, g * tq, kb_ref.at[h], off, va, g, False, m_ref, acc_ref)

    za_ref[...] = score_fn(i - 1)
    tile(i, score_fn(i), True, diagonal=True)
    _paired_key_tiles(i, score_fn, tile, za_ref, zb_ref)
    lam = lam_ref[0]
    for h in range(nheads):
        out = _normalised(acc_ref, 2 * h) - lam * _normalised(acc_ref, 2 * h + 1)
        o_ref[h * HEAD_DIM:(h + 1) * HEAD_DIM, :] = out.astype(o_ref.dtype)


def _attention_call(kernel, qkv, col0, vt, extra, extra_specs, scratch, bsz, seq, name, smem=()):
    tq, tk = TQ_ATT, TK_ATT
    assert tq == tk
    nq = seq // tq
    vrows = vt.shape[2]
    return pl.pallas_call(
        kernel,
        out_shape=jax.ShapeDtypeStruct((bsz, GROUP_W, seq), BF16),
        grid=(bsz, nq),
        in_specs=[pl.BlockSpec(memory_space=pltpu.SMEM) for _ in smem] + [
            pl.BlockSpec((tq, GROUP_W), lambda b, i: (b * nq + i, col0)),
            pl.BlockSpec((seq, GROUP_W), lambda b, i: (b, col0 + 1)),
            pl.BlockSpec((None, seq // tk, vrows, tk), lambda b, i: (b, 0, 0, 0)),
        ] + extra_specs,
        out_specs=pl.BlockSpec((None, GROUP_W, tq), lambda b, i: (b, 0, i)),
        scratch_shapes=scratch,
        compiler_params=_cparams(("arbitrary", "arbitrary")),
        name=name,
    )(*smem, qkv, qkv, vt, *extra)


def _values_t(v, bsz, seq, nheads, ones_row):
    if ones_row:
        v4 = v.reshape(bsz, seq, nheads, HEAD_DIM)
        pad = jnp.zeros((bsz, seq, nheads, VA_ROWS - HEAD_DIM - 1), v.dtype)
        v = jnp.concatenate([v4, jnp.ones((bsz, seq, nheads, 1), v.dtype), pad], axis=-1)
    rows = v.size // (bsz * seq)
    return v.reshape(bsz, seq // TK_ATT, TK_ATT, rows).transpose(0, 1, 3, 2)


def _tokens_major(o, n):
    return o.transpose(0, 2, 1).reshape(n, GROUP_W)


def _ch_bias_tiles(rel_bias, tq, tk):
    assert tq == tk
    nback = CH_LEFT_CHUNKS * CHUNK // tk
    nh = rel_bias.shape[0]
    n = np.arange(nback + 1)[:, None, None]
    s = np.arange(tk)[None, :, None]
    t = np.arange(tq)[None, None, :]
    delta = (n * tk + t) // CHUNK - s // CHUNK
    valid = (delta >= 0) & (delta <= CH_LEFT_CHUNKS)
    table = rel_bias.astype(F32) * LOG2E
    lo, hi = -(tk - 1), nback * tk + tq - 1
    ext = jnp.concatenate([jnp.repeat(table[:, :1], -REL_CLIP - lo, axis=1), table,
                           jnp.repeat(table[:, -1:], hi - REL_CLIP, axis=1)], axis=1)
    period = tk + tq
    tiles = []
    for m in range(nback + 1):
        w = ext[:, m * tk:m * tk + period - 1]
        w = jnp.concatenate([w, jnp.zeros((nh, 1), F32)], axis=1)
        skew = jnp.tile(w, (1, tk))[:, :tk * (period - 1)].reshape(nh, tk, period - 1)
        tiles.append(skew[:, :, tk - 1:tk - 1 + tq])
    bias = jnp.stack(tiles, axis=1)
    return jnp.where(jnp.asarray(valid)[None], bias, NEG_INF), nback


def _df_bias_tiles(slopes2, tq, tk):
    s = jnp.arange(tk, dtype=jnp.int32)[:, None]
    t = jnp.arange(tq, dtype=jnp.int32)[None, :]
    sl = jnp.asarray(slopes2, F32)[:, None, None]
    kb = sl * jnp.broadcast_to(s, (tk, tq)).astype(F32)
    allowed = (s // CHUNK) <= (t // CHUNK)
    db = jnp.where(allowed[None], sl * (t - jnp.abs(t - s)).astype(F32), NEG_INF)
    return kb, db


def _out_proj_kernel(ys_ref, ysb_ref, ych_ref, ydf_ref, g_ref, ind_ref, w_ref, x_ref, o_ref):
    acc = x_ref[...]
    for gi, y_ref in enumerate((ys_ref, ysb_ref, ych_ref, ydf_ref)):
        cs = slice(gi * GROUP_W, (gi + 1) * GROUP_W)
        y = y_ref[...].astype(F32)
        ms = _group_ms(y, ind_ref[...], 1.0 / HEAD_DIM)
        yn = (y * lax.rsqrt(ms + EPS) * g_ref[:, cs]).astype(BF16)
        acc = acc + _dot(yn, w_ref[cs, :])
    o_ref[...] = acc


def _out_proj(y_ssm_tm, y_sb, y_ch, y_df, g, ind64, w, x2, bsz, seq):
    n, dm = x2.shape
    tm = TM_PROJ
    ns = seq // tm
    const = lambda i: (0, 0)
    tile = pl.BlockSpec((tm, GROUP_W), lambda i: (i, 0))
    return pl.pallas_call(
        _out_proj_kernel,
        out_shape=jax.ShapeDtypeStruct((n, dm), F32),
        grid=(n // tm,),
        in_specs=[
            pl.BlockSpec((tm, GROUP_W), lambda i: (i % ns, i // ns)),
            tile, tile, tile,
            pl.BlockSpec((1, dm), const),
            pl.BlockSpec((GROUP_W, GROUP_W), const),
            pl.BlockSpec((dm, dm), const),
            pl.BlockSpec((tm, dm), lambda i: (i, 0)),
        ],
        out_specs=pl.BlockSpec((tm, dm), lambda i: (i, 0)),
        compiler_params=_cparams(("arbitrary",)),
        name="out_proj",
    )(y_ssm_tm, y_sb, y_ch, y_df, g, ind64, w, x2)


def _ffn_kernel(x_ref, g_ref, w1_ref, w3_ref, w2_ref, o_ref, hid_ref, *, tf):
    x = x_ref[...]
    ms = jnp.mean(x * x, axis=-1, keepdims=True)
    h = (x * lax.rsqrt(ms + EPS) * g_ref[...]).astype(BF16)
    dff = w1_ref.shape[1]
    for c in range(dff // tf):
        cs = slice(c * tf, (c + 1) * tf)
        a = _dot(h, w1_ref[:, cs])
        b = _dot(h, w3_ref[:, cs])
        hid_ref[:, cs] = (jax.nn.silu(a) * b).astype(BF16)
    o_ref[...] = x + _dot(hid_ref[...], w2_ref[...])


def _ffn(x2, g, w1, w3, w2):
    n, dm = x2.shape
    dff = w1.shape[1]
    tm = TM_PROJ
    const = lambda i: (0, 0)
    return pl.pallas_call(
        functools.partial(_ffn_kernel, tf=GROUP_W),
        out_shape=jax.ShapeDtypeStruct((n, dm), F32),
        grid=(n // tm,),
        in_specs=[
            pl.BlockSpec((tm, dm), lambda i: (i, 0)),
            pl.BlockSpec((1, dm), const),
            pl.BlockSpec((dm, dff), const),
            pl.BlockSpec((dm, dff), const),
            pl.BlockSpec((dff, dm), const),
        ],
        out_specs=pl.BlockSpec((tm, dm), lambda i: (i, 0)),
        scratch_shapes=[pltpu.VMEM((tm, dff), BF16)],
        compiler_params=_cparams(("arbitrary",)),
        name="ffn_dense",
    )(x2, g, w1, w3, w2)


def _router_kernel(x_ref, g_ref, wr_ref, tri_ref, h_ref, route_ref, cnt_ref, carry_ref):
    @pl.when(pl.program_id(0) == 0)
    def _():
        carry_ref[...] = jnp.zeros_like(carry_ref)

    x = x_ref[...]
    ms = jnp.mean(x * x, axis=-1, keepdims=True)
    h = x * lax.rsqrt(ms + EPS) * g_ref[...]
    h_hi = h.astype(BF16)
    half = h.shape[1] // 2
    bits = lax.bitcast_convert_type(h_hi.astype(F32), jnp.uint32)
    h_ref[...] = bits[:, :half] | lax.shift_right_logical(bits[:, half:], jnp.uint32(16))
    h_lo = (h - h_hi.astype(F32)).astype(BF16)
    logits = _dot(h_hi, wr_ref[0]) + _dot(h_lo, wr_ref[0]) + _dot(h_hi, wr_ref[1])

    lane = _iota2(logits.shape, 1).astype(F32)
    lg = jnp.where(lane < N_EXPERTS, logits, -jnp.inf)
    m1 = jnp.max(lg, axis=-1, keepdims=True)
    e1 = jnp.min(jnp.where(lg == m1, lane, float(LANES)), axis=-1, keepdims=True)
    lg2 = jnp.where(lane == e1, -jnp.inf, lg)
    m2 = jnp.max(lg2, axis=-1, keepdims=True)
    e2 = jnp.min(jnp.where(lg2 == m2, lane, float(LANES)), axis=-1, keepdims=True)
    t = jnp.exp(m2 - m1)
    g1 = 1.0 / (1.0 + t)
    g2 = t / (1.0 + t)

    hot1 = lane == e1
    hot2 = lane == e2
    sel = jnp.where(hot1 | hot2, 1.0, 0.0).astype(BF16)
    prior = _dot(tri_ref[...], sel) + carry_ref[...]
    r1 = jnp.sum(jnp.where(hot1, prior, 0.0), axis=-1, keepdims=True)
    r2 = jnp.sum(jnp.where(hot2, prior, 0.0), axis=-1, keepdims=True)
    carry_ref[...] = carry_ref[...] + jnp.sum(sel.astype(F32), axis=0, keepdims=True)
    cnt_ref[...] = carry_ref[...]

    out = jnp.where(lane == 0, e1, 0.0)
    out = jnp.where(lane == 1, e2, out)
    out = jnp.where(lane == 2, g1, out)
    out = jnp.where(lane == 3, g2, out)
    out = jnp.where(lane == 4, r1, out)
    out = jnp.where(lane == 5, r2, out)
    route_ref[...] = out


def _router(x2, g, wr2, tri):
    n, dm = x2.shape
    tm = TM_PROJ
    const = lambda i: (0, 0)
    return pl.pallas_call(
        _router_kernel,
        out_shape=(jax.ShapeDtypeStruct((n, dm // 2), jnp.uint32),
                   jax.ShapeDtypeStruct((n, LANES), F32),
                   jax.ShapeDtypeStruct((1, LANES), F32)),
        grid=(n // tm,),
        in_specs=[
            pl.BlockSpec((tm, dm), lambda i: (i, 0)),
            pl.BlockSpec((1, dm), const),
            pl.BlockSpec((2, dm, LANES), lambda i: (0, 0, 0)),
            pl.BlockSpec((tm, tm), const),
        ],
        out_specs=(
            pl.BlockSpec((tm, dm // 2), lambda i: (i, 0)),
            pl.BlockSpec((tm, LANES), lambda i: (i, 0)),
            pl.BlockSpec((1, LANES), const),
        ),
        scratch_shapes=[pltpu.VMEM((1, LANES), F32)],
        compiler_params=_cparams(("arbitrary",)),
        name="moe_router",
    )(x2, g, wr2, tri)


def _row_copy(src_ref, src_row, dst_ref, dst_row, sem):
    return pltpu.make_async_copy(src_ref.at[pl.ds(src_row, 1)], dst_ref.at[pl.ds(dst_row, 1)], sem)


def _dispatch_kernel(d1_ref, d2_ref, cnt_ref, h_ref, xs_ref, zero_ref, sem, tail_sem, *, tm, cap, nexp):
    def issue(r, c):
        _row_copy(h_ref, r, xs_ref, d1_ref[0, 0, r], sem).start()
        _row_copy(h_ref, r, xs_ref, d2_ref[0, 0, r], sem).start()
        return c

    lax.fori_loop(0, tm, issue, 0, unroll=8)

    def aligned_end(e):
        return ((cnt_ref[e] + (SUBLANES - 1)) // SUBLANES) * SUBLANES

    def tail_copy(e):
        row0 = pl.multiple_of(e * cap + aligned_end(e), SUBLANES)
        return pltpu.make_async_copy(zero_ref, xs_ref.at[pl.ds(row0, tm)], tail_sem)

    def tail_rows(e, fn):
        for k in range(SUBLANES - 1):
            @pl.when(cnt_ref[e] + k < aligned_end(e))
            def _():
                fn(_row_copy(zero_ref, 0, xs_ref, e * cap + cnt_ref[e] + k, tail_sem))

    @pl.when(pl.program_id(0) == pl.num_programs(0) - 1)
    def _():
        zero_ref[...] = jnp.zeros_like(zero_ref)
        for e in range(nexp):
            tail_copy(e).start()
            tail_rows(e, lambda cp: cp.start())
        for e in range(nexp):
            tail_copy(e).wait()
            tail_rows(e, lambda cp: cp.wait())

    for _ in range(2):
        pltpu.make_async_copy(h_ref, xs_ref.at[pl.ds(0, tm)], sem).wait()


def _dispatch(d1, d2, cnt, h, cap):
    n, wd = h.shape
    tm = TM_PROJ
    nt = n // tm
    idx_spec = pl.BlockSpec((1, 1, tm), lambda i: (i, 0, 0), memory_space=pltpu.SMEM)
    return pl.pallas_call(
        functools.partial(_dispatch_kernel, tm=tm, cap=cap, nexp=N_EXPERTS),
        out_shape=jax.ShapeDtypeStruct((N_EXPERTS * cap, wd), h.dtype),
        grid=(nt,),
        in_specs=[idx_spec, idx_spec, pl.BlockSpec(memory_space=pltpu.SMEM),
                  pl.BlockSpec((tm, wd), lambda i: (i, 0))],
        out_specs=pl.BlockSpec(memory_space=pl.ANY),
        scratch_shapes=[pltpu.VMEM((tm, wd), h.dtype), pltpu.SemaphoreType.DMA, pltpu.SemaphoreType.DMA],
        compiler_params=pltpu.CompilerParams(dimension_semantics=("arbitrary",),
                                             vmem_limit_bytes=VMEM_LIMIT_BYTES, disable_bounds_checks=True),
        name="moe_dispatch",
    )(d1.reshape(nt, 1, tm), d2.reshape(nt, 1, tm), cnt, h)


def _combine_kernel(d1_ref, d2_ref, x_ref, route_ref, ys_ref, o_ref, y1_ref, y2_ref, sem, *, tm):
    def issue(r, c):
        _row_copy(ys_ref, d1_ref[0, 0, r], y1_ref, r, sem).start()
        _row_copy(ys_ref, d2_ref[0, 0, r], y2_ref, r, sem).start()
        return c

    lax.fori_loop(0, tm, issue, 0, unroll=8)
    for buf in (y1_ref, y2_ref):
        pltpu.make_async_copy(ys_ref.at[pl.ds(0, tm)], buf, sem).wait()
    o_ref[...] = x_ref[...] + route_ref[:, 2:3] * y1_ref[...] + route_ref[:, 3:4] * y2_ref[...]


def _combine(d1, d2, x2, route, ys):
    n, dm = x2.shape
    tm = TM_PROJ
    nt = n // tm
    idx_spec = pl.BlockSpec((1, 1, tm), lambda i: (i, 0, 0), memory_space=pltpu.SMEM)
    return pl.pallas_call(
        functools.partial(_combine_kernel, tm=tm),
        out_shape=jax.ShapeDtypeStruct((n, dm), F32),
        grid=(nt,),
        in_specs=[idx_spec, idx_spec, pl.BlockSpec((tm, dm), lambda i: (i, 0)),
                  pl.BlockSpec((tm, LANES), lambda i: (i, 0)), pl.BlockSpec(memory_space=pl.ANY)],
        out_specs=pl.BlockSpec((tm, dm), lambda i: (i, 0)),
        scratch_shapes=[pltpu.VMEM((tm, dm), F32), pltpu.VMEM((tm, dm), F32), pltpu.SemaphoreType.DMA],
        compiler_params=pltpu.CompilerParams(dimension_semantics=("arbitrary",),
                                             vmem_limit_bytes=VMEM_LIMIT_BYTES, disable_bounds_checks=True),
        name="moe_combine",
    )(d1.reshape(nt, 1, tm), d2.reshape(nt, 1, tm), x2, route, ys)


def _experts_kernel(te_ref, blk_ref, nused_ref, xs_ref, w1_ref, w3_ref, w2_ref, o_ref, x_ref, hid_ref, acc_ref):
    t = pl.program_id(0)
    c = pl.program_id(1)
    nc = pl.num_programs(1)

    @pl.when(jnp.logical_and(t < nused_ref[0], c == 0))
    def _():
        w = xs_ref[...]
        half = w.shape[1]
        x_ref[:, :half] = lax.bitcast_convert_type(w & jnp.uint32(0xFFFF0000), F32).astype(BF16)
        x_ref[:, half:] = lax.bitcast_convert_type(lax.shift_left(w, jnp.uint32(16)), F32).astype(BF16)

    @pl.when(t < nused_ref[0])
    def _():
        x = x_ref[...]
        tf = w1_ref.shape[2]
        for s0 in range(0, tf, GROUP_W):
            cs = slice(s0, s0 + GROUP_W)
            a = _dot(x, w1_ref[0, :, cs])
            b = _dot(x, w3_ref[0, :, cs])
            hid_ref[:, cs] = (jax.nn.silu(a) * b).astype(BF16)
        part = _dot(hid_ref[...], w2_ref[0])

        @pl.when(c == 0)
        def _():
            acc_ref[...] = part

        @pl.when(c > 0)
        def _():
            acc_ref[...] = acc_ref[...] + part

        @pl.when(c == nc - 1)
        def _():
            o_ref[...] = acc_ref[...]

    @pl.when(jnp.logical_and(t >= nused_ref[0], c == nc - 1))
    def _():
        o_ref[...] = jnp.zeros_like(o_ref)


def _experts(tile_expert, tile_block, nused, xs, w1, w3, w2, nt):
    rows, wd = xs.shape
    dm = 2 * wd
    dff = w1.shape[2]
    tm, tf = TM_EXPERT, TF_EXPERT
    nc = dff // tf
    spare = rows // tm

    def cc(t, c, nu):
        return jnp.where(t < nu[0], c, nc - 1)

    grid_spec = pltpu.PrefetchScalarGridSpec(
        num_scalar_prefetch=3,
        grid=(nt, nc),
        in_specs=[
            pl.BlockSpec((tm, wd), lambda t, c, te, tb, nu: (tb[t], 0)),
            pl.BlockSpec((1, dm, tf), lambda t, c, te, tb, nu: (te[t], 0, cc(t, c, nu))),
            pl.BlockSpec((1, dm, tf), lambda t, c, te, tb, nu: (te[t], 0, cc(t, c, nu))),
            pl.BlockSpec((1, tf, dm), lambda t, c, te, tb, nu: (te[t], cc(t, c, nu), 0)),
        ],
        out_specs=pl.BlockSpec((tm, dm), lambda t, c, te, tb, nu: (jnp.where(t < nu[0], tb[t], spare), 0)),
        scratch_shapes=[pltpu.VMEM((tm, dm), BF16), pltpu.VMEM((tm, tf), BF16), pltpu.VMEM((tm, dm), F32)],
    )
    return pl.pallas_call(
        _experts_kernel,
        out_shape=jax.ShapeDtypeStruct((rows + tm, dm), F32),
        grid_spec=grid_spec,
        compiler_params=_cparams(("arbitrary", "arbitrary")),
        name="moe_experts",
    )(tile_expert, tile_block, nused, xs, w1, w3, w2)


def _moe(x2, g, w_router, w1, w3, w2, tri):
    n, dm = x2.shape
    wr = jnp.zeros((dm, LANES), F32).at[:, :N_EXPERTS].set(w_router.astype(F32))
    wr_hi = wr.astype(BF16)
    wr_lo = (wr - wr_hi.astype(F32)).astype(BF16)
    h, route, counts = _router(x2, g, jnp.stack([wr_hi, wr_lo]), tri)

    tm = TM_EXPERT
    assert n % tm == 0 and TM_PROJ == tm
    cap = n + tm
    d1 = route[:, 0].astype(jnp.int32) * cap + route[:, 4].astype(jnp.int32)
    d2 = route[:, 1].astype(jnp.int32) * cap + route[:, 5].astype(jnp.int32)
    cnt = counts[0, :N_EXPERTS].astype(jnp.int32)

    nt = 2 * n // tm + N_EXPERTS
    ntile = (cnt + tm - 1) // tm
    ends = jnp.cumsum(ntile)
    nused = ends[-1]
    t = jnp.minimum(jnp.arange(nt, dtype=jnp.int32), nused - 1)
    tile_expert = jnp.sum(t[:, None] >= ends[None, :], axis=1).astype(jnp.int32)
    tile_block = tile_expert * (cap // tm) + t - (ends - ntile)[tile_expert]

    xs = _dispatch(d1, d2, cnt, h, cap)
    ys = _experts(tile_expert, tile_block.astype(jnp.int32), nused.reshape(1).astype(jnp.int32), xs, w1, w3, w2, nt)
    return _combine(d1, d2, x2, route, ys)


def kernel(x, norm_mix_g, w_in, ssm_lam_re, ssm_lam_im, ssm_log_dt, ssm_b_re, ssm_b_im, ssm_c_re, ssm_c_im,
           ssm_d, ssm_w_glu, ssm_b_glu, ch_q_norm_g, ch_k_norm_g, ch_rel_bias, df_q_norm_g, df_k_norm_g,
           df_lambda, out_norm_g, w_out, norm_ffn_g, ffn_w1, ffn_w3, ffn_w2, moe_router, moe_w1, moe_w3, moe_w2):
    bsz, seq, dm = x.shape
    depth = w_in.shape[0]
    n = bsz * seq
    nheads = GROUP_W // HEAD_DIM
    ind64 = _block_indicator(GROUP_W, HEAD_DIM)
    ind32 = _block_indicator(GROUP_W, DF_QK_DIM)
    tri = (jnp.arange(TM_PROJ)[:, None] > jnp.arange(TM_PROJ)[None, :]).astype(BF16)
    slopes = [2.0 ** (-8.0 * (h + 1) / nheads) for h in range(nheads)]
    slopes2 = tuple(s * LOG2E for s in slopes)
    df_kb, df_db = _df_bias_tiles(slopes2, TQ_ATT, TK_ATT)
    later = (jnp.arange(TK_ATT)[:, None] < jnp.arange(TK_ATT)[None, :]).astype(BF16)
    later = jnp.concatenate([later, later], axis=1)

    x2 = x.reshape(n, dm)
    for layer in range(depth):
        qkg = jnp.stack([
            jnp.tile(ch_q_norm_g[layer].astype(F32), nheads) * (HEAD_DIM ** -0.5 * LOG2E),
            jnp.tile(ch_k_norm_g[layer].astype(F32), nheads),
            jnp.tile(df_q_norm_g[layer].astype(F32).reshape(-1), nheads) * (DF_QK_DIM ** -0.5 * LOG2E),
            jnp.tile(df_k_norm_g[layer].astype(F32).reshape(-1), nheads),
        ])
        u_tm, qkv = _in_proj(x2, norm_mix_g[layer].reshape(1, dm), w_in[layer].astype(BF16), qkg,
                             ind64, ind32, bsz, seq)

        wb, lre, lim, wc = _s5_params(ssm_lam_re[layer], ssm_lam_im[layer], ssm_log_dt[layer],
                                      ssm_b_re[layer], ssm_b_im[layer], ssm_c_re[layer], ssm_c_im[layer], bsz)
        y_ssm = _s5(u_tm.reshape(seq * bsz, GROUP_W), wb, lre, lim, wc, ssm_d[layer].reshape(1, GROUP_W),
                    ssm_w_glu[layer].astype(BF16), ssm_b_glu[layer].reshape(1, GROUP_W), bsz)
        y_ssm = y_ssm.reshape(seq, bsz * GROUP_W)

        tq, tk = TQ_ATT, TK_ATT
        const3 = lambda b, i: (0, 0, 0)
        y_sb = _attention_call(
            functools.partial(_sb_kernel, tq=tq, tk=tk, nheads=nheads),
            qkv, 0, _values_t(qkv[:, 2 * GROUP_W:3 * GROUP_W], bsz, seq, nheads, False),
            (later,), [pl.BlockSpec((tk, 2 * tk), lambda b, i: (0, 0))],
            [pltpu.VMEM((nheads * tq, GROUP_W), BF16), pltpu.VMEM((SUBLANES, tq), F32),
             pltpu.VMEM((GROUP_W, tq), F32),
             pltpu.VMEM((tk, nheads * tq), F32), pltpu.VMEM((tk, nheads * tq), F32)],
            bsz, seq, "sb_attention")

        bias, nback = _ch_bias_tiles(ch_rel_bias[layer], tq, tk)
        y_ch = _attention_call(
            functools.partial(_ch_kernel, tq=tq, tk=tk, nheads=nheads, nback=nback),
            qkv, 3, _values_t(qkv[:, 5 * GROUP_W:6 * GROUP_W], bsz, seq, nheads, True),
            (bias,), [pl.BlockSpec(bias.shape, lambda b, i: (0, 0, 0, 0))],
            [pltpu.VMEM((nheads * tq, GROUP_W), BF16), pltpu.VMEM((SUBLANES, tq), F32),
             pltpu.VMEM((nheads * VA_ROWS, tq), F32)],
            bsz, seq, "ch_attention")

        lambda_init = 0.8 - 0.6 * math.exp(-0.3 * layer)
        lam_p = df_lambda[layer].astype(F32)
        lam = jnp.exp(jnp.sum(lam_p[0] * lam_p[1])) - jnp.exp(jnp.sum(lam_p[2] * lam_p[3])) + lambda_init
        y_df = _attention_call(
            functools.partial(_df_kernel, tq=tq, tk=tk, nheads=nheads, slopes2=slopes2),
            qkv, 6, _values_t(qkv[:, 8 * GROUP_W:9 * GROUP_W], bsz, seq, nheads, True),
            (df_kb, df_db), [pl.BlockSpec(df_kb.shape, const3), pl.BlockSpec(df_db.shape, const3)],
            [pltpu.VMEM((2 * nheads * tq, GROUP_W), BF16), pltpu.VMEM((2 * nheads, tq), F32),
             pltpu.VMEM((2 * nheads * VA_ROWS, tq), F32),
             pltpu.VMEM((tk, 2 * nheads * tq), F32), pltpu.VMEM((tk, 2 * nheads * tq), F32)],
            bsz, seq, "df_attention", smem=(lam.reshape(1),))
        y_sb, y_ch, y_df = (_tokens_major(y, n) for y in (y_sb, y_ch, y_df))

        head_scale = jnp.concatenate([jnp.ones((dm - GROUP_W,), F32),
                                      jnp.full((GROUP_W,), 1.0 - lambda_init, F32)])
        g_out = (out_norm_g[layer].astype(F32) * head_scale).reshape(1, dm)
        x2 = _out_proj(y_ssm, y_sb, y_ch, y_df, g_out, ind64, w_out[layer].astype(BF16), x2, bsz, seq)

        idx = layer // 2
        g_ffn = norm_ffn_g[layer].reshape(1, dm)
        if layer % 2 == 0:
            x2 = _ffn(x2, g_ffn, ffn_w1[idx].astype(BF16), ffn_w3[idx].astype(BF16), ffn_w2[idx].astype(BF16))
        else:
            x2 = _moe(x2, g_ffn, moe_router[idx], moe_w1[idx].astype(BF16), moe_w3[idx].astype(BF16),
                      moe_w2[idx].astype(BF16), tri)
    return x2.reshape(bsz, seq, dm)
```

```python
import functools
import math

import jax
import jax.numpy as jnp
import numpy as np
from jax import lax
from jax.experimental import pallas as pl
from jax.experimental.pallas import tpu as pltpu

F32 = jnp.float32
BF16 = jnp.bfloat16

EPS = 1e-6
NEG_INF = -1e30
HEAD_DIM = 64
GROUP_W = 256
CHUNK = 64
SSM_GROUP = 16
SSM_STATE = 64
DF_QK_DIM = 32
CH_LEFT_CHUNKS = 8
REL_CLIP = 128
N_EXPERTS = 8
LOG2E = 1.4426950408889634

VMEM_LIMIT_BYTES = 56 * 1024 * 1024
SUBLANES = 8
LANES = 128

TM_PROJ = 512
TQ_ATT = 256
TK_ATT = 256
SSM_STEPS = 64
TM_EXPERT = 512
TF_EXPERT = 1792


def _cparams(sem):
    return pltpu.CompilerParams(dimension_semantics=sem, vmem_limit_bytes=VMEM_LIMIT_BYTES)


def _dot(a, b):
    return jnp.dot(a, b, preferred_element_type=F32)


def _dot_nt(a, b):
    return lax.dot_general(a, b, (((1,), (1,)), ((), ())), preferred_element_type=F32)


def _split_dot(v, m):
    hi = v.astype(BF16)
    lo = (v - hi.astype(F32)).astype(BF16)
    return _dot(hi, m) + _dot(lo, m)


def _group_ms(v, ind, inv_size):
    return _split_dot(v * v, ind) * inv_size


def _block_indicator(width, group):
    r = jnp.arange(width)[:, None] // group
    c = jnp.arange(width)[None, :] // group
    return (r == c).astype(BF16)


def _in_proj_kernel(x_ref, g_ref, w_ref, qkg_ref, ind64_ref, ind32_ref, u_ref, qkv_ref):
    x = x_ref[...]
    ms = jnp.mean(x * x, axis=-1, keepdims=True)
    h = (x * lax.rsqrt(ms + EPS) * g_ref[...]).astype(BF16)

    def cols(c):
        return _dot(h, w_ref[:, c * GROUP_W:(c + 1) * GROUP_W])

    def put(c, val):
        qkv_ref[:, (c - 1) * GROUP_W:c * GROUP_W] = val.astype(BF16)

    u_ref[...] = cols(0)
    put(1, cols(1) * (HEAD_DIM ** -0.5 * LOG2E))
    put(2, cols(2))
    put(3, cols(3))
    for c, gi in ((4, 0), (5, 1)):
        a = cols(c)
        ms_h = _group_ms(a, ind64_ref[...], 1.0 / HEAD_DIM)
        put(c, a * lax.rsqrt(ms_h + EPS) * qkg_ref[gi:gi + 1, :])
    put(6, cols(6))
    for c, gi in ((7, 2), (8, 3)):
        a = cols(c)
        ms_h = _group_ms(a, ind32_ref[...], 1.0 / DF_QK_DIM)
        put(c, a * lax.rsqrt(ms_h + EPS) * qkg_ref[gi:gi + 1, :])
    put(9, cols(9))


def _in_proj(x2, g, w, qkg, ind64, ind32, bsz, seq):
    n, dm = x2.shape
    tm = TM_PROJ
    ns = seq // tm
    ncols = w.shape[1]
    const = lambda i: (0, 0)
    return pl.pallas_call(
        _in_proj_kernel,
        out_shape=(jax.ShapeDtypeStruct((seq, bsz * GROUP_W), F32),
                   jax.ShapeDtypeStruct((n, ncols - GROUP_W), BF16)),
        grid=(n // tm,),
        in_specs=[
            pl.BlockSpec((tm, dm), lambda i: (i, 0)),
            pl.BlockSpec((1, dm), const),
            pl.BlockSpec((dm, ncols), const),
            pl.BlockSpec((4, GROUP_W), const),
            pl.BlockSpec((GROUP_W, GROUP_W), const),
            pl.BlockSpec((GROUP_W, GROUP_W), const),
        ],
        out_specs=(
            pl.BlockSpec((tm, GROUP_W), lambda i: (i % ns, i // ns)),
            pl.BlockSpec((tm, ncols - GROUP_W), lambda i: (i, 0)),
        ),
        compiler_params=_cparams(("arbitrary",)),
        name="in_proj",
    )(x2, g, w, qkg, ind64, ind32)


def _s5_kernel(u_ref, wb_ref, lre_ref, lim_ref, wc_ref, d_ref, wg_ref, bg_ref, y_ref, bu_ref, st_ref,
               *, steps, nstate):
    @pl.when(pl.program_id(0) == 0)
    def _():
        st_ref[...] = jnp.zeros_like(st_ref)

    u = u_ref[...]
    bu_ref[...] = _dot(u.astype(BF16), wb_ref[...])
    lre = lre_ref[...]
    lim = lim_ref[...]

    def step(t, carry):
        sre, sim = carry
        r = pl.multiple_of(t * SUBLANES, SUBLANES)
        bre = bu_ref[pl.ds(r, SUBLANES), 0:nstate]
        bim = bu_ref[pl.ds(r, SUBLANES), nstate:2 * nstate]
        nre = lre * sre - lim * sim + bre
        nim = lre * sim + lim * sre + bim
        bu_ref[pl.ds(r, SUBLANES), 0:nstate] = nre
        bu_ref[pl.ds(r, SUBLANES), nstate:2 * nstate] = nim
        return nre, nim

    sre, sim = lax.fori_loop(0, steps, step, (st_ref[:, 0:nstate], st_ref[:, nstate:2 * nstate]))
    st_ref[:, 0:nstate] = sre
    st_ref[:, nstate:2 * nstate] = sim

    y = _dot(bu_ref[...].astype(BF16), wc_ref[...]) + d_ref[...] * u
    y = jax.nn.gelu(y)
    gate = jax.nn.sigmoid(_dot(y.astype(BF16), wg_ref[...]) + bg_ref[...])
    y_ref[...] = (y * gate).astype(y_ref.dtype)


def _s5(u_tm, wb, lre, lim, wc, d, wg, bg, bsz):
    rows, w = u_tm.shape
    assert bsz == SUBLANES
    blk = SSM_STEPS * bsz
    nstate2 = wb.shape[1]
    const = lambda c: (0, 0)
    return pl.pallas_call(
        functools.partial(_s5_kernel, steps=SSM_STEPS, nstate=nstate2 // 2),
        out_shape=jax.ShapeDtypeStruct((rows, w), BF16),
        grid=(rows // blk,),
        in_specs=[
            pl.BlockSpec((blk, w), lambda c: (c, 0)),
            pl.BlockSpec((w, nstate2), const),
            pl.BlockSpec((bsz, nstate2 // 2), const),
            pl.BlockSpec((bsz, nstate2 // 2), const),
            pl.BlockSpec((nstate2, w), const),
            pl.BlockSpec((1, w), const),
            pl.BlockSpec((w, w), const),
            pl.BlockSpec((1, w), const),
        ],
        out_specs=pl.BlockSpec((blk, w), lambda c: (c, 0)),
        scratch_shapes=[pltpu.VMEM((blk, nstate2), F32), pltpu.VMEM((bsz, nstate2), F32)],
        compiler_params=_cparams(("arbitrary",)),
        name="s5_mixer",
    )(u_tm, wb, lre, lim, wc, d, wg, bg)


def _s5_params(lam_re, lam_im, log_dt, b_re, b_im, c_re, c_im, bsz):
    g, n = lam_re.shape
    dt = jnp.exp(log_dt.astype(F32))[:, None]
    xr = lam_re * dt
    th = lam_im * dt
    er = jnp.exp(xr)
    lbr = er * jnp.cos(th)
    lbi = er * jnp.sin(th)
    ar = jnp.expm1(xr) * jnp.cos(th) - 2.0 * jnp.sin(0.5 * th) ** 2
    ai = lbi
    den = lam_re * lam_re + lam_im * lam_im
    fr = (ar * lam_re + ai * lam_im) / den
    fi = (ai * lam_re - ar * lam_im) / den
    bbr = fr[..., None] * b_re - fi[..., None] * b_im
    bbi = fr[..., None] * b_im + fi[..., None] * b_re
    eye = jnp.eye(g, dtype=F32)
    p = b_re.shape[-1]
    wb_re = jnp.einsum("gnp,gh->gphn", bbr, eye).reshape(g * p, g * n)
    wb_im = jnp.einsum("gnp,gh->gphn", bbi, eye).reshape(g * p, g * n)
    wb = jnp.concatenate([wb_re, wb_im], axis=1).astype(BF16)
    wc_re = jnp.einsum("gpn,gh->gnhp", c_re, eye).reshape(g * n, g * p)
    wc_im = jnp.einsum("gpn,gh->gnhp", c_im, eye).reshape(g * n, g * p)
    wc = jnp.concatenate([wc_re, -wc_im], axis=0).astype(BF16)
    lre = jnp.broadcast_to(lbr.reshape(1, g * n), (bsz, g * n))
    lim = jnp.broadcast_to(lbi.reshape(1, g * n), (bsz, g * n))
    return wb, lre, lim, wc


VA_ROWS = 2 * HEAD_DIM


def _iota2(shape, dim):
    return lax.broadcasted_iota(jnp.int32, shape, dim)


def _stack_masked_q(q_ref, qs_ref, tq, ngroups, width):
    lane = _iota2((1, GROUP_W), 1)
    q = q_ref[...]
    for g in range(ngroups):
        keep = (lane >= g * width) & (lane < (g + 1) * width)
        qs_ref[g * tq:(g + 1) * tq, :] = jnp.where(keep, q, jnp.zeros_like(q))


def _key_scores(k_ref, qs_ref, j, tk):
    row0 = pl.multiple_of(jnp.maximum(j, 0) * tk, tk)
    return _dot_nt(k_ref[pl.ds(row0, tk), :], qs_ref[...])


def _score_prefetch(k_ref, qs_ref, j, dst_ref, tq, tk):
    row0 = pl.multiple_of(jnp.maximum(j, 0) * tk, tk)

    def emit(g):
        cols = slice(g * tq, (g + 1) * tq)
        dst_ref[:, cols] = _dot_nt(k_ref[pl.ds(row0, tk), :], qs_ref[cols, :])

    return emit


def _paired_key_tiles(i, k_ref, qs_ref, tile_fn, za_ref, zb_ref, tq, tk):
    def body(n, c):
        ja = i - 1 - 2 * n
        tile_fn(ja, za_ref, True, _score_prefetch(k_ref, qs_ref, ja - 1, zb_ref, tq, tk))
        tile_fn(ja - 1, zb_ref, ja >= 1, _score_prefetch(k_ref, qs_ref, ja - 2, za_ref, tq, tk))
        return c

    lax.fori_loop(0, (i + 1) // 2, body, 0)


def _sb_kernel(q_ref, k_ref, vt_ref, later_ref, o_ref, qs_ref, carry_ref, acc_ref, za_ref, zb_ref,
               *, tq, tk, nheads):
    i = pl.program_id(1)
    _stack_masked_q(q_ref, qs_ref, tq, nheads, HEAD_DIM)

    def tile(j, zt, valid, prefetch, diagonal=False):
        if diagonal:
            before = _iota2((tk, tq), 0) < _iota2((tk, tq), 1)
        else:
            penalty = jnp.where(valid, 0.0, -NEG_INF)
            keep = jnp.where(valid, 1.0, 0.0)
        jv = jnp.maximum(j, 0)

        def scan_stage(h):
            z = zt[:, h * tq:(h + 1) * tq]
            neg_abs = lax.bitcast_convert_type(
                lax.bitcast_convert_type(z, jnp.uint32) | jnp.uint32(0x80000000), F32)
            sp = jnp.maximum(z, 0.0) + jnp.log2(1.0 + jnp.exp2(neg_abs))
            nl1m = jnp.where(before, sp, 0.0) if diagonal else sp
            hi = nl1m.astype(BF16)
            lo = (nl1m - hi.astype(F32)).astype(BF16)
            inside = _dot(later_ref[...], jnp.concatenate([hi, lo], axis=0))
            return z - sp, inside, nl1m[0:1, :]

        def value_stage(h, logit, inside, first_row):
            hs = slice(h * HEAD_DIM, (h + 1) * HEAD_DIM)
            between = inside if diagonal else inside + (carry_ref[h:h + 1, :] + penalty)
            w = jnp.exp2(logit - between)
            if diagonal:
                w = jnp.where(before, w, 0.0)
            pv = _dot(vt_ref[jv, hs, :], w.astype(BF16))
            total = inside[0:1, :] + first_row
            if diagonal:
                acc_ref[hs, :] = pv
                carry_ref[h:h + 1, :] = total
            else:
                acc_ref[hs, :] = acc_ref[hs, :] + pv
                carry_ref[h:h + 1, :] = carry_ref[h:h + 1, :] + keep * total

        prefetch(0)
        pending = scan_stage(0)
        for h in range(1, nheads):
            prefetch(h)
            upcoming = scan_stage(h)
            value_stage(h - 1, *pending)
            pending = upcoming
        value_stage(nheads - 1, *pending)

    tile(i, _key_scores(k_ref, qs_ref, i, tk), True, _score_prefetch(k_ref, qs_ref, i - 1, za_ref, tq, tk),
         diagonal=True)
    _paired_key_tiles(i, k_ref, qs_ref, tile, za_ref, zb_ref, tq, tk)
    o_ref[...] = acc_ref[...].astype(o_ref.dtype)


def _softmax_tile(st, col0, bias, off, va, g, first, m_ref, acc_ref):
    rows = slice(g * VA_ROWS, (g + 1) * VA_ROWS)
    parts, alphas = [], []
    for c0 in range(0, bias.shape[1], LANES):
        cs = slice(c0, c0 + LANES)
        sc = slice(col0 + c0, col0 + c0 + LANES)
        blk_max = jnp.max(st[:, sc] + bias[:, cs], axis=0, keepdims=True)
        if first:
            m_new = blk_max
            shift = m_new
        else:
            m_old = m_ref[g:g + 1, cs]
            m_new = jnp.maximum(m_old, blk_max + off)
            shift = m_new - off
            alphas.append(jnp.exp2(m_old - m_new))
        parts.append(jnp.exp2((st[:, sc] - shift) + bias[:, cs]).astype(BF16))
        m_ref[g:g + 1, cs] = m_new
    pv = _dot(va, jnp.concatenate(parts, axis=1))
    if first:
        acc_ref[rows, :] = pv
    else:
        acc_ref[rows, :] = jnp.concatenate(alphas, axis=1) * acc_ref[rows, :] + pv


def _normalised(acc_ref, g):
    base = g * VA_ROWS
    return acc_ref[base:base + HEAD_DIM, :] / acc_ref[base + HEAD_DIM:base + HEAD_DIM + 1, :]


def _ch_kernel(q_ref, k_ref, va_ref, bias_ref, o_ref, qs_ref, m_ref, acc_ref, z_ref, *, tq, tk, nheads, nback):
    i = pl.program_id(1)
    _stack_masked_q(q_ref, qs_ref, tq, nheads, HEAD_DIM)

    def tile(n, first):
        j = i - n
        z_ref[...] = _key_scores(k_ref, qs_ref, j, tk)
        for h in range(nheads):
            _softmax_tile(z_ref, h * tq, bias_ref.at[h, n], 0.0, va_ref[j, h * VA_ROWS:(h + 1) * VA_ROWS, :],
                          h, first, m_ref, acc_ref)

    tile(0, True)

    def body(n, c):
        tile(n, False)
        return c

    lax.fori_loop(1, jnp.minimum(i, nback) + 1, body, 0)
    for h in range(nheads):
        o_ref[h * HEAD_DIM:(h + 1) * HEAD_DIM, :] = _normalised(acc_ref, h).astype(o_ref.dtype)


def _df_kernel(lam_ref, q_ref, k_ref, va_ref, kb_ref, db_ref, o_ref, qs_ref, m_ref, acc_ref, za_ref, zb_ref,
               *, tq, tk, nheads, slopes2):
    i = pl.program_id(1)
    _stack_masked_q(q_ref, qs_ref, tq, 2 * nheads, DF_QK_DIM)

    def tile(j, st, valid, prefetch, diagonal=False):
        jv = jnp.maximum(j, 0)
        for g in range(2 * nheads):
            prefetch(g)
            h = g // 2
            va = va_ref[jv, h * VA_ROWS:(h + 1) * VA_ROWS, :]
            if diagonal:
                _softmax_tile(st, g * tq, db_ref.at[h], 0.0, va, g, True, m_ref, acc_ref)
            else:
                off = jnp.where(valid, slopes2[h] * ((j - i) * tk).astype(F32), NEG_INF)
                _softmax_tile(st, g * tq, kb_ref.at[h], off, va, g, False, m_ref, acc_ref)

    zb_ref[...] = _key_scores(k_ref, qs_ref, i, tk)
    tile(i, zb_ref, True, _score_prefetch(k_ref, qs_ref, i - 1, za_ref, tq, tk), diagonal=True)
    _paired_key_tiles(i, k_ref, qs_ref, tile, za_ref, zb_ref, tq, tk)
    lam = lam_ref[0]
    for h in range(nheads):
        out = _normalised(acc_ref, 2 * h) - lam * _normalised(acc_ref, 2 * h + 1)
        o_ref[h * HEAD_DIM:(h + 1) * HEAD_DIM, :] = out.astype(o_ref.dtype)


def _attention_call(kernel, qkv, col0, vt, extra, extra_specs, scratch, bsz, seq, name, smem=()):
    tq, tk = TQ_ATT, TK_ATT
    assert tq == tk
    nq = seq // tq
    vrows = vt.shape[2]
    return pl.pallas_call(
        kernel,
        out_shape=jax.ShapeDtypeStruct((bsz, GROUP_W, seq), BF16),
        grid=(bsz, nq),
        in_specs=[pl.BlockSpec(memory_space=pltpu.SMEM) for _ in smem] + [
            pl.BlockSpec((tq, GROUP_W), lambda b, i: (b * nq + i, col0)),
            pl.BlockSpec((seq, GROUP_W), lambda b, i: (b, col0 + 1)),
            pl.BlockSpec((None, seq // tk, vrows, tk), lambda b, i: (b, 0, 0, 0)),
        ] + extra_specs,
        out_specs=pl.BlockSpec((None, GROUP_W, tq), lambda b, i: (b, 0, i)),
        scratch_shapes=scratch,
        compiler_params=_cparams(("arbitrary", "arbitrary")),
        name=name,
    )(*smem, qkv, qkv, vt, *extra)


def _values_t(v, bsz, seq, nheads, ones_row):
    if ones_row:
        v4 = v.reshape(bsz, seq, nheads, HEAD_DIM)
        pad = jnp.zeros((bsz, seq, nheads, VA_ROWS - HEAD_DIM - 1), v.dtype)
        v = jnp.concatenate([v4, jnp.ones((bsz, seq, nheads, 1), v.dtype), pad], axis=-1)
    rows = v.size // (bsz * seq)
    return v.reshape(bsz, seq // TK_ATT, TK_ATT, rows).transpose(0, 1, 3, 2)


def _tokens_major(o, n):
    return o.transpose(0, 2, 1).reshape(n, GROUP_W)


def _ch_bias_tiles(rel_bias, tq, tk):
    assert tq == tk
    nback = CH_LEFT_CHUNKS * CHUNK // tk
    nh = rel_bias.shape[0]
    n = np.arange(nback + 1)[:, None, None]
    s = np.arange(tk)[None, :, None]
    t = np.arange(tq)[None, None, :]
    delta = (n * tk + t) // CHUNK - s // CHUNK
    valid = (delta >= 0) & (delta <= CH_LEFT_CHUNKS)
    table = rel_bias.astype(F32) * LOG2E
    lo, hi = -(tk - 1), nback * tk + tq - 1
    ext = jnp.concatenate([jnp.repeat(table[:, :1], -REL_CLIP - lo, axis=1), table,
                           jnp.repeat(table[:, -1:], hi - REL_CLIP, axis=1)], axis=1)
    period = tk + tq
    tiles = []
    for m in range(nback + 1):
        w = ext[:, m * tk:m * tk + period - 1]
        w = jnp.concatenate([w, jnp.zeros((nh, 1), F32)], axis=1)
        skew = jnp.tile(w, (1, tk))[:, :tk * (period - 1)].reshape(nh, tk, period - 1)
        tiles.append(skew[:, :, tk - 1:tk - 1 + tq])
    bias = jnp.stack(tiles, axis=1)
    return jnp.where(jnp.asarray(valid)[None], bias, NEG_INF), nback


def _df_bias_tiles(slopes2, tq, tk):
    s = jnp.arange(tk, dtype=jnp.int32)[:, None]
    t = jnp.arange(tq, dtype=jnp.int32)[None, :]
    sl = jnp.asarray(slopes2, F32)[:, None, None]
    kb = sl * jnp.broadcast_to(s, (tk, tq)).astype(F32)
    allowed = (s // CHUNK) <= (t // CHUNK)
    db = jnp.where(allowed[None], sl * (t - jnp.abs(t - s)).astype(F32), NEG_INF)
    return kb, db


def _out_proj_kernel(ys_ref, ysb_ref, ych_ref, ydf_ref, g_ref, ind_ref, w_ref, x_ref, o_ref):
    acc = x_ref[...]
    for gi, y_ref in enumerate((ys_ref, ysb_ref, ych_ref, ydf_ref)):
        cs = slice(gi * GROUP_W, (gi + 1) * GROUP_W)
        y = y_ref[...].astype(F32)
        ms = _group_ms(y, ind_ref[...], 1.0 / HEAD_DIM)
        yn = (y * lax.rsqrt(ms + EPS) * g_ref[:, cs]).astype(BF16)
        acc = acc + _dot(yn, w_ref[cs, :])
    o_ref[...] = acc


def _out_proj(y_ssm_tm, y_sb, y_ch, y_df, g, ind64, w, x2, bsz, seq):
    n, dm = x2.shape
    tm = TM_PROJ
    ns = seq // tm
    const = lambda i: (0, 0)
    tile = pl.BlockSpec((tm, GROUP_W), lambda i: (i, 0))
    return pl.pallas_call(
        _out_proj_kernel,
        out_shape=jax.ShapeDtypeStruct((n, dm), F32),
        grid=(n // tm,),
        in_specs=[
            pl.BlockSpec((tm, GROUP_W), lambda i: (i % ns, i // ns)),
            tile, tile, tile,
            pl.BlockSpec((1, dm), const),
            pl.BlockSpec((GROUP_W, GROUP_W), const),
            pl.BlockSpec((dm, dm), const),
            pl.BlockSpec((tm, dm), lambda i: (i, 0)),
        ],
        out_specs=pl.BlockSpec((tm, dm), lambda i: (i, 0)),
        compiler_params=_cparams(("arbitrary",)),
        name="out_proj",
    )(y_ssm_tm, y_sb, y_ch, y_df, g, ind64, w, x2)


def _ffn_kernel(x_ref, g_ref, w1_ref, w3_ref, w2_ref, o_ref, hid_ref, *, tf):
    x = x_ref[...]
    ms = jnp.mean(x * x, axis=-1, keepdims=True)
    h = (x * lax.rsqrt(ms + EPS) * g_ref[...]).astype(BF16)
    dff = w1_ref.shape[1]
    for c in range(dff // tf):
        cs = slice(c * tf, (c + 1) * tf)
        a = _dot(h, w1_ref[:, cs])
        b = _dot(h, w3_ref[:, cs])
        hid_ref[:, cs] = (jax.nn.silu(a) * b).astype(BF16)
    o_ref[...] = x + _dot(hid_ref[...], w2_ref[...])


def _ffn(x2, g, w1, w3, w2):
    n, dm = x2.shape
    dff = w1.shape[1]
    tm = TM_PROJ
    const = lambda i: (0, 0)
    return pl.pallas_call(
        functools.partial(_ffn_kernel, tf=GROUP_W),
        out_shape=jax.ShapeDtypeStruct((n, dm), F32),
        grid=(n // tm,),
        in_specs=[
            pl.BlockSpec((tm, dm), lambda i: (i, 0)),
            pl.BlockSpec((1, dm), const),
            pl.BlockSpec((dm, dff), const),
            pl.BlockSpec((dm, dff), const),
            pl.BlockSpec((dff, dm), const),
        ],
        out_specs=pl.BlockSpec((tm, dm), lambda i: (i, 0)),
        scratch_shapes=[pltpu.VMEM((tm, dff), BF16)],
        compiler_params=_cparams(("arbitrary",)),
        name="ffn_dense",
    )(x2, g, w1, w3, w2)


def _router_kernel(x_ref, g_ref, wr_ref, tri_ref, h_ref, route_ref, cnt_ref, carry_ref):
    @pl.when(pl.program_id(0) == 0)
    def _():
        carry_ref[...] = jnp.zeros_like(carry_ref)

    x = x_ref[...]
    ms = jnp.mean(x * x, axis=-1, keepdims=True)
    h = x * lax.rsqrt(ms + EPS) * g_ref[...]
    h_hi = h.astype(BF16)
    half = h.shape[1] // 2
    bits = lax.bitcast_convert_type(h_hi.astype(F32), jnp.uint32)
    h_ref[...] = bits[:, :half] | lax.shift_right_logical(bits[:, half:], jnp.uint32(16))
    h_lo = (h - h_hi.astype(F32)).astype(BF16)
    logits = _dot(h_hi, wr_ref[0]) + _dot(h_lo, wr_ref[0]) + _dot(h_hi, wr_ref[1])

    lane = _iota2(logits.shape, 1).astype(F32)
    lg = jnp.where(lane < N_EXPERTS, logits, -jnp.inf)
    m1 = jnp.max(lg, axis=-1, keepdims=True)
    e1 = jnp.min(jnp.where(lg == m1, lane, float(LANES)), axis=-1, keepdims=True)
    lg2 = jnp.where(lane == e1, -jnp.inf, lg)
    m2 = jnp.max(lg2, axis=-1, keepdims=True)
    e2 = jnp.min(jnp.where(lg2 == m2, lane, float(LANES)), axis=-1, keepdims=True)
    t = jnp.exp(m2 - m1)
    g1 = 1.0 / (1.0 + t)
    g2 = t / (1.0 + t)

    hot1 = lane == e1
    hot2 = lane == e2
    sel = jnp.where(hot1 | hot2, 1.0, 0.0).astype(BF16)
    prior = _dot(tri_ref[...], sel) + carry_ref[...]
    r1 = jnp.sum(jnp.where(hot1, prior, 0.0), axis=-1, keepdims=True)
    r2 = jnp.sum(jnp.where(hot2, prior, 0.0), axis=-1, keepdims=True)
    carry_ref[...] = carry_ref[...] + jnp.sum(sel.astype(F32), axis=0, keepdims=True)
    cnt_ref[...] = carry_ref[...]

    out = jnp.where(lane == 0, e1, 0.0)
    out = jnp.where(lane == 1, e2, out)
    out = jnp.where(lane == 2, g1, out)
    out = jnp.where(lane == 3, g2, out)
    out = jnp.where(lane == 4, r1, out)
    out = jnp.where(lane == 5, r2, out)
    route_ref[...] = out


def _router(x2, g, wr2, tri):
    n, dm = x2.shape
    tm = TM_PROJ
    const = lambda i: (0, 0)
    return pl.pallas_call(
        _router_kernel,
        out_shape=(jax.ShapeDtypeStruct((n, dm // 2), jnp.uint32),
                   jax.ShapeDtypeStruct((n, LANES), F32),
                   jax.ShapeDtypeStruct((1, LANES), F32)),
        grid=(n // tm,),
        in_specs=[
            pl.BlockSpec((tm, dm), lambda i: (i, 0)),
            pl.BlockSpec((1, dm), const),
            pl.BlockSpec((2, dm, LANES), lambda i: (0, 0, 0)),
            pl.BlockSpec((tm, tm), const),
        ],
        out_specs=(
            pl.BlockSpec((tm, dm // 2), lambda i: (i, 0)),
            pl.BlockSpec((tm, LANES), lambda i: (i, 0)),
            pl.BlockSpec((1, LANES), const),
        ),
        scratch_shapes=[pltpu.VMEM((1, LANES), F32)],
        compiler_params=_cparams(("arbitrary",)),
        name="moe_router",
    )(x2, g, wr2, tri)


def _row_copy(src_ref, src_row, dst_ref, dst_row, sem):
    return pltpu.make_async_copy(src_ref.at[pl.ds(src_row, 1)], dst_ref.at[pl.ds(dst_row, 1)], sem)


def _dispatch_kernel(d1_ref, d2_ref, cnt_ref, h_ref, xs_ref, zero_ref, sem, tail_sem, *, tm, cap, nexp):
    def issue(r, c):
        _row_copy(h_ref, r, xs_ref, d1_ref[0, 0, r], sem).start()
        _row_copy(h_ref, r, xs_ref, d2_ref[0, 0, r], sem).start()
        return c

    lax.fori_loop(0, tm, issue, 0, unroll=8)

    def aligned_end(e):
        return ((cnt_ref[e] + (SUBLANES - 1)) // SUBLANES) * SUBLANES

    def tail_copy(e):
        row0 = pl.multiple_of(e * cap + aligned_end(e), SUBLANES)
        return pltpu.make_async_copy(zero_ref, xs_ref.at[pl.ds(row0, tm)], tail_sem)

    def tail_rows(e, fn):
        for k in range(SUBLANES - 1):
            @pl.when(cnt_ref[e] + k < aligned_end(e))
            def _():
                fn(_row_copy(zero_ref, 0, xs_ref, e * cap + cnt_ref[e] + k, tail_sem))

    @pl.when(pl.program_id(0) == pl.num_programs(0) - 1)
    def _():
        zero_ref[...] = jnp.zeros_like(zero_ref)
        for e in range(nexp):
            tail_copy(e).start()
            tail_rows(e, lambda cp: cp.start())
        for e in range(nexp):
            tail_copy(e).wait()
            tail_rows(e, lambda cp: cp.wait())

    for _ in range(2):
        pltpu.make_async_copy(h_ref, xs_ref.at[pl.ds(0, tm)], sem).wait()


def _dispatch(d1, d2, cnt, h, cap):
    n, wd = h.shape
    tm = TM_PROJ
    nt = n // tm
    idx_spec = pl.BlockSpec((1, 1, tm), lambda i: (i, 0, 0), memory_space=pltpu.SMEM)
    return pl.pallas_call(
        functools.partial(_dispatch_kernel, tm=tm, cap=cap, nexp=N_EXPERTS),
        out_shape=jax.ShapeDtypeStruct((N_EXPERTS * cap, wd), h.dtype),
        grid=(nt,),
        in_specs=[idx_spec, idx_spec, pl.BlockSpec(memory_space=pltpu.SMEM),
                  pl.BlockSpec((tm, wd), lambda i: (i, 0))],
        out_specs=pl.BlockSpec(memory_space=pl.ANY),
        scratch_shapes=[pltpu.VMEM((tm, wd), h.dtype), pltpu.SemaphoreType.DMA, pltpu.SemaphoreType.DMA],
        compiler_params=pltpu.CompilerParams(dimension_semantics=("arbitrary",),
                                             vmem_limit_bytes=VMEM_LIMIT_BYTES, disable_bounds_checks=True),
        name="moe_dispatch",
    )(d1.reshape(nt, 1, tm), d2.reshape(nt, 1, tm), cnt, h)


def _combine_kernel(d1_ref, d2_ref, x_ref, route_ref, ys_ref, o_ref, y1_ref, y2_ref, sem, *, tm):
    def issue(r, c):
        _row_copy(ys_ref, d1_ref[0, 0, r], y1_ref, r, sem).start()
        _row_copy(ys_ref, d2_ref[0, 0, r], y2_ref, r, sem).start()
        return c

    lax.fori_loop(0, tm, issue, 0, unroll=8)
    for buf in (y1_ref, y2_ref):
        pltpu.make_async_copy(ys_ref.at[pl.ds(0, tm)], buf, sem).wait()
    o_ref[...] = x_ref[...] + route_ref[:, 2:3] * y1_ref[...] + route_ref[:, 3:4] * y2_ref[...]


def _combine(d1, d2, x2, route, ys):
    n, dm = x2.shape
    tm = TM_PROJ
    nt = n // tm
    idx_spec = pl.BlockSpec((1, 1, tm), lambda i: (i, 0, 0), memory_space=pltpu.SMEM)
    return pl.pallas_call(
        functools.partial(_combine_kernel, tm=tm),
        out_shape=jax.ShapeDtypeStruct((n, dm), F32),
        grid=(nt,),
        in_specs=[idx_spec, idx_spec, pl.BlockSpec((tm, dm), lambda i: (i, 0)),
                  pl.BlockSpec((tm, LANES), lambda i: (i, 0)), pl.BlockSpec(memory_space=pl.ANY)],
        out_specs=pl.BlockSpec((tm, dm), lambda i: (i, 0)),
        scratch_shapes=[pltpu.VMEM((tm, dm), F32), pltpu.VMEM((tm, dm), F32), pltpu.SemaphoreType.DMA],
        compiler_params=pltpu.CompilerParams(dimension_semantics=("arbitrary",),
                                             vmem_limit_bytes=VMEM_LIMIT_BYTES, disable_bounds_checks=True),
        name="moe_combine",
    )(d1.reshape(nt, 1, tm), d2.reshape(nt, 1, tm), x2, route, ys)


def _experts_kernel(te_ref, blk_ref, nused_ref, xs_ref, w1_ref, w3_ref, w2_ref, o_ref, x_ref, hid_ref, acc_ref):
    t = pl.program_id(0)
    c = pl.program_id(1)
    nc = pl.num_programs(1)

    @pl.when(jnp.logical_and(t < nused_ref[0], c == 0))
    def _():
        w = xs_ref[...]
        half = w.shape[1]
        x_ref[:, :half] = lax.bitcast_convert_type(w & jnp.uint32(0xFFFF0000), F32).astype(BF16)
        x_ref[:, half:] = lax.bitcast_convert_type(lax.shift_left(w, jnp.uint32(16)), F32).astype(BF16)

    @pl.when(t < nused_ref[0])
    def _():
        x = x_ref[...]
        tf = w1_ref.shape[2]
        for s0 in range(0, tf, GROUP_W):
            cs = slice(s0, s0 + GROUP_W)
            a = _dot(x, w1_ref[0, :, cs])
            b = _dot(x, w3_ref[0, :, cs])
            hid_ref[:, cs] = (jax.nn.silu(a) * b).astype(BF16)
        part = _dot(hid_ref[...], w2_ref[0])

        @pl.when(c == 0)
        def _():
            acc_ref[...] = part

        @pl.when(c > 0)
        def _():
            acc_ref[...] = acc_ref[...] + part

        @pl.when(c == nc - 1)
        def _():
            o_ref[...] = acc_ref[...]

    @pl.when(jnp.logical_and(t >= nused_ref[0], c == nc - 1))
    def _():
        o_ref[...] = jnp.zeros_like(o_ref)


def _experts(tile_expert, tile_block, nused, xs, w1, w3, w2, nt):
    rows, wd = xs.shape
    dm = 2 * wd
    dff = w1.shape[2]
    tm, tf = TM_EXPERT, TF_EXPERT
    nc = dff // tf
    spare = rows // tm

    def cc(t, c, nu):
        return jnp.where(t < nu[0], c, nc - 1)

    grid_spec = pltpu.PrefetchScalarGridSpec(
        num_scalar_prefetch=3,
        grid=(nt, nc),
        in_specs=[
            pl.BlockSpec((tm, wd), lambda t, c, te, tb, nu: (tb[t], 0)),
            pl.BlockSpec((1, dm, tf), lambda t, c, te, tb, nu: (te[t], 0, cc(t, c, nu))),
            pl.BlockSpec((1, dm, tf), lambda t, c, te, tb, nu: (te[t], 0, cc(t, c, nu))),
            pl.BlockSpec((1, tf, dm), lambda t, c, te, tb, nu: (te[t], cc(t, c, nu), 0)),
        ],
        out_specs=pl.BlockSpec((tm, dm), lambda t, c, te, tb, nu: (jnp.where(t < nu[0], tb[t], spare), 0)),
        scratch_shapes=[pltpu.VMEM((tm, dm), BF16), pltpu.VMEM((tm, tf), BF16), pltpu.VMEM((tm, dm), F32)],
    )
    return pl.pallas_call(
        _experts_kernel,
        out_shape=jax.ShapeDtypeStruct((rows + tm, dm), F32),
        grid_spec=grid_spec,
        compiler_params=_cparams(("arbitrary", "arbitrary")),
        name="moe_experts",
    )(tile_expert, tile_block, nused, xs, w1, w3, w2)


def _moe(x2, g, w_router, w1, w3, w2, tri):
    n, dm = x2.shape
    wr = jnp.zeros((dm, LANES), F32).at[:, :N_EXPERTS].set(w_router.astype(F32))
    wr_hi = wr.astype(BF16)
    wr_lo = (wr - wr_hi.astype(F32)).astype(BF16)
    h, route, counts = _router(x2, g, jnp.stack([wr_hi, wr_lo]), tri)

    tm = TM_EXPERT
    assert n % tm == 0 and TM_PROJ == tm
    cap = n + tm
    d1 = route[:, 0].astype(jnp.int32) * cap + route[:, 4].astype(jnp.int32)
    d2 = route[:, 1].astype(jnp.int32) * cap + route[:, 5].astype(jnp.int32)
    cnt = counts[0, :N_EXPERTS].astype(jnp.int32)

    nt = 2 * n // tm + N_EXPERTS
    ntile = (cnt + tm - 1) // tm
    ends = jnp.cumsum(ntile)
    nused = ends[-1]
    t = jnp.minimum(jnp.arange(nt, dtype=jnp.int32), nused - 1)
    tile_expert = jnp.sum(t[:, None] >= ends[None, :], axis=1).astype(jnp.int32)
    tile_block = tile_expert * (cap // tm) + t - (ends - ntile)[tile_expert]

    xs = _dispatch(d1, d2, cnt, h, cap)
    ys = _experts(tile_expert, tile_block.astype(jnp.int32), nused.reshape(1).astype(jnp.int32), xs, w1, w3, w2, nt)
    return _combine(d1, d2, x2, route, ys)


def kernel(x, norm_mix_g, w_in, ssm_lam_re, ssm_lam_im, ssm_log_dt, ssm_b_re, ssm_b_im, ssm_c_re, ssm_c_im,
           ssm_d, ssm_w_glu, ssm_b_glu, ch_q_norm_g, ch_k_norm_g, ch_rel_bias, df_q_norm_g, df_k_norm_g,
           df_lambda, out_norm_g, w_out, norm_ffn_g, ffn_w1, ffn_w3, ffn_w2, moe_router, moe_w1, moe_w3, moe_w2):
    bsz, seq, dm = x.shape
    depth = w_in.shape[0]
    n = bsz * seq
    nheads = GROUP_W // HEAD_DIM
    ind64 = _block_indicator(GROUP_W, HEAD_DIM)
    ind32 = _block_indicator(GROUP_W, DF_QK_DIM)
    tri = (jnp.arange(TM_PROJ)[:, None] > jnp.arange(TM_PROJ)[None, :]).astype(BF16)
    slopes = [2.0 ** (-8.0 * (h + 1) / nheads) for h in range(nheads)]
    slopes2 = tuple(s * LOG2E for s in slopes)
    df_kb, df_db = _df_bias_tiles(slopes2, TQ_ATT, TK_ATT)
    later = (jnp.arange(TK_ATT)[:, None] < jnp.arange(TK_ATT)[None, :]).astype(BF16)
    later = jnp.concatenate([later, later], axis=1)

    x2 = x.reshape(n, dm)
    for layer in range(depth):
        qkg = jnp.stack([
            jnp.tile(ch_q_norm_g[layer].astype(F32), nheads) * (HEAD_DIM ** -0.5 * LOG2E),
            jnp.tile(ch_k_norm_g[layer].astype(F32), nheads),
            jnp.tile(df_q_norm_g[layer].astype(F32).reshape(-1), nheads) * (DF_QK_DIM ** -0.5 * LOG2E),
            jnp.tile(df_k_norm_g[layer].astype(F32).reshape(-1), nheads),
        ])
        u_tm, qkv = _in_proj(x2, norm_mix_g[layer].reshape(1, dm), w_in[layer].astype(BF16), qkg,
                             ind64, ind32, bsz, seq)

        wb, lre, lim, wc = _s5_params(ssm_lam_re[layer], ssm_lam_im[layer], ssm_log_dt[layer],
                                      ssm_b_re[layer], ssm_b_im[layer], ssm_c_re[layer], ssm_c_im[layer], bsz)
        y_ssm = _s5(u_tm.reshape(seq * bsz, GROUP_W), wb, lre, lim, wc, ssm_d[layer].reshape(1, GROUP_W),
                    ssm_w_glu[layer].astype(BF16), ssm_b_glu[layer].reshape(1, GROUP_W), bsz)
        y_ssm = y_ssm.reshape(seq, bsz * GROUP_W)

        tq, tk = TQ_ATT, TK_ATT
        const3 = lambda b, i: (0, 0, 0)
        y_sb = _attention_call(
            functools.partial(_sb_kernel, tq=tq, tk=tk, nheads=nheads),
            qkv, 0, _values_t(qkv[:, 2 * GROUP_W:3 * GROUP_W], bsz, seq, nheads, False),
            (later,), [pl.BlockSpec((tk, 2 * tk), lambda b, i: (0, 0))],
            [pltpu.VMEM((nheads * tq, GROUP_W), BF16), pltpu.VMEM((SUBLANES, tq), F32),
             pltpu.VMEM((GROUP_W, tq), F32),
             pltpu.VMEM((tk, nheads * tq), F32), pltpu.VMEM((tk, nheads * tq), F32)],
            bsz, seq, "sb_attention")

        bias, nback = _ch_bias_tiles(ch_rel_bias[layer], tq, tk)
        y_ch = _attention_call(
            functools.partial(_ch_kernel, tq=tq, tk=tk, nheads=nheads, nback=nback),
            qkv, 3, _values_t(qkv[:, 5 * GROUP_W:6 * GROUP_W], bsz, seq, nheads, True),
            (bias,), [pl.BlockSpec(bias.shape, lambda b, i: (0, 0, 0, 0))],
            [pltpu.VMEM((nheads * tq, GROUP_W), BF16), pltpu.VMEM((SUBLANES, tq), F32),
             pltpu.VMEM((nheads * VA_ROWS, tq), F32), pltpu.VMEM((tk, nheads * tq), F32)],
            bsz, seq, "ch_attention")

        lambda_init = 0.8 - 0.6 * math.exp(-0.3 * layer)
        lam_p = df_lambda[layer].astype(F32)
        lam = jnp.exp(jnp.sum(lam_p[0] * lam_p[1])) - jnp.exp(jnp.sum(lam_p[2] * lam_p[3])) + lambda_init
        y_df = _attention_call(
            functools.partial(_df_kernel, tq=tq, tk=tk, nheads=nheads, slopes2=slopes2),
            qkv, 6, _values_t(qkv[:, 8 * GROUP_W:9 * GROUP_W], bsz, seq, nheads, True),
            (df_kb, df_db), [pl.BlockSpec(df_kb.shape, const3), pl.BlockSpec(df_db.shape, const3)],
            [pltpu.VMEM((2 * nheads * tq, GROUP_W), BF16), pltpu.VMEM((2 * nheads, tq), F32),
             pltpu.VMEM((2 * nheads * VA_ROWS, tq), F32),
             pltpu.VMEM((tk, 2 * nheads * tq), F32), pltpu.VMEM((tk, 2 * nheads * tq), F32)],
            bsz, seq, "df_attention", smem=(lam.reshape(1),))
        y_sb, y_ch, y_df = (_tokens_major(y, n) for y in (y_sb, y_ch, y_df))

        head_scale = jnp.concatenate([jnp.ones((dm - GROUP_W,), F32),
                                      jnp.full((GROUP_W,), 1.0 - lambda_init, F32)])
        g_out = (out_norm_g[layer].astype(F32) * head_scale).reshape(1, dm)
        x2 = _out_proj(y_ssm, y_sb, y_ch, y_df, g_out, ind64, w_out[layer].astype(BF16), x2, bsz, seq)

        idx = layer // 2
        g_ffn = norm_ffn_g[layer].reshape(1, dm)
        if layer % 2 == 0:
            x2 = _ffn(x2, g_ffn, ffn_w1[idx].astype(BF16), ffn_w3[idx].astype(BF16), ffn_w2[idx].astype(BF16))
        else:
            x2 = _moe(x2, g_ffn, moe_router[idx], moe_w1[idx].astype(BF16), moe_w3[idx].astype(BF16),
                      moe_w2[idx].astype(BF16), tri)
    return x2.reshape(bsz, seq, dm)
```

```python
import functools
import math

import jax
import jax.numpy as jnp
import numpy as np
from jax import lax
from jax.experimental import pallas as pl
from jax.experimental.pallas import tpu as pltpu

F32 = jnp.float32
BF16 = jnp.bfloat16

EPS = 1e-6
NEG_INF = -1e30
HEAD_DIM = 64
GROUP_W = 256
CHUNK = 64
SSM_GROUP = 16
SSM_STATE = 64
DF_QK_DIM = 32
CH_LEFT_CHUNKS = 8
REL_CLIP = 128
N_EXPERTS = 8
LOG2E = 1.4426950408889634

VMEM_LIMIT_BYTES = 56 * 1024 * 1024
SUBLANES = 8
LANES = 128

TM_PROJ = 512
TQ_ATT = 256
TK_ATT = 256
SSM_STEPS = 64
TM_EXPERT = 512
TF_EXPERT = 1792


def _cparams(sem):
    return pltpu.CompilerParams(dimension_semantics=sem, vmem_limit_bytes=VMEM_LIMIT_BYTES)


def _dot(a, b):
    return jnp.dot(a, b, preferred_element_type=F32)


def _dot_nt(a, b):
    return lax.dot_general(a, b, (((1,), (1,)), ((), ())), preferred_element_type=F32)


def _split_dot(v, m):
    hi = v.astype(BF16)
    lo = (v - hi.astype(F32)).astype(BF16)
    return _dot(hi, m) + _dot(lo, m)


def _group_ms(v, ind, inv_size):
    return _split_dot(v * v, ind) * inv_size


def _block_indicator(width, group):
    r = jnp.arange(width)[:, None] // group
    c = jnp.arange(width)[None, :] // group
    return (r == c).astype(BF16)


def _in_proj_kernel(x_ref, g_ref, w_ref, qkg_ref, ind64_ref, ind32_ref, u_ref, qkv_ref):
    x = x_ref[...]
    ms = jnp.mean(x * x, axis=-1, keepdims=True)
    h = (x * lax.rsqrt(ms + EPS) * g_ref[...]).astype(BF16)

    def cols(c):
        return _dot(h, w_ref[:, c * GROUP_W:(c + 1) * GROUP_W])

    def put(c, val):
        qkv_ref[:, (c - 1) * GROUP_W:c * GROUP_W] = val.astype(BF16)

    u_ref[...] = cols(0)
    put(1, cols(1) * (HEAD_DIM ** -0.5 * LOG2E))
    put(2, cols(2))
    put(3, cols(3))
    for c, gi in ((4, 0), (5, 1)):
        a = cols(c)
        ms_h = _group_ms(a, ind64_ref[...], 1.0 / HEAD_DIM)
        put(c, a * lax.rsqrt(ms_h + EPS) * qkg_ref[gi:gi + 1, :])
    put(6, cols(6))
    for c, gi in ((7, 2), (8, 3)):
        a = cols(c)
        ms_h = _group_ms(a, ind32_ref[...], 1.0 / DF_QK_DIM)
        put(c, a * lax.rsqrt(ms_h + EPS) * qkg_ref[gi:gi + 1, :])
    put(9, cols(9))


def _in_proj(x2, g, w, qkg, ind64, ind32, bsz, seq):
    n, dm = x2.shape
    tm = TM_PROJ
    ns = seq // tm
    ncols = w.shape[1]
    const = lambda i: (0, 0)
    return pl.pallas_call(
        _in_proj_kernel,
        out_shape=(jax.ShapeDtypeStruct((seq, bsz * GROUP_W), F32),
                   jax.ShapeDtypeStruct((n, ncols - GROUP_W), BF16)),
        grid=(n // tm,),
        in_specs=[
            pl.BlockSpec((tm, dm), lambda i: (i, 0)),
            pl.BlockSpec((1, dm), const),
            pl.BlockSpec((dm, ncols), const),
            pl.BlockSpec((4, GROUP_W), const),
            pl.BlockSpec((GROUP_W, GROUP_W), const),
            pl.BlockSpec((GROUP_W, GROUP_W), const),
        ],
        out_specs=(
            pl.BlockSpec((tm, GROUP_W), lambda i: (i % ns, i // ns)),
            pl.BlockSpec((tm, ncols - GROUP_W), lambda i: (i, 0)),
        ),
        compiler_params=_cparams(("arbitrary",)),
        name="in_proj",
    )(x2, g, w, qkg, ind64, ind32)


def _s5_kernel(u_ref, wb_ref, lre_ref, lim_ref, wc_ref, d_ref, wg_ref, bg_ref, y_ref, bu_ref, st_ref,
               *, steps, nstate):
    @pl.when(pl.program_id(0) == 0)
    def _():
        st_ref[...] = jnp.zeros_like(st_ref)

    u = u_ref[...]
    bu_ref[...] = _dot(u.astype(BF16), wb_ref[...])
    lre = lre_ref[...]
    lim = lim_ref[...]

    def step(t, carry):
        sre, sim = carry
        r = pl.multiple_of(t * SUBLANES, SUBLANES)
        bre = bu_ref[pl.ds(r, SUBLANES), 0:nstate]
        bim = bu_ref[pl.ds(r, SUBLANES), nstate:2 * nstate]
        nre = lre * sre - lim * sim + bre
        nim = lre * sim + lim * sre + bim
        bu_ref[pl.ds(r, SUBLANES), 0:nstate] = nre
        bu_ref[pl.ds(r, SUBLANES), nstate:2 * nstate] = nim
        return nre, nim

    sre, sim = lax.fori_loop(0, steps, step, (st_ref[:, 0:nstate], st_ref[:, nstate:2 * nstate]))
    st_ref[:, 0:nstate] = sre
    st_ref[:, nstate:2 * nstate] = sim

    y = _dot(bu_ref[...].astype(BF16), wc_ref[...]) + d_ref[...] * u
    y = jax.nn.gelu(y)
    gate = jax.nn.sigmoid(_dot(y.astype(BF16), wg_ref[...]) + bg_ref[...])
    y_ref[...] = (y * gate).astype(y_ref.dtype)


def _s5(u_tm, wb, lre, lim, wc, d, wg, bg, bsz):
    rows, w = u_tm.shape
    assert bsz == SUBLANES
    blk = SSM_STEPS * bsz
    nstate2 = wb.shape[1]
    const = lambda c: (0, 0)
    return pl.pallas_call(
        functools.partial(_s5_kernel, steps=SSM_STEPS, nstate=nstate2 // 2),
        out_shape=jax.ShapeDtypeStruct((rows, w), BF16),
        grid=(rows // blk,),
        in_specs=[
            pl.BlockSpec((blk, w), lambda c: (c, 0)),
            pl.BlockSpec((w, nstate2), const),
            pl.BlockSpec((bsz, nstate2 // 2), const),
            pl.BlockSpec((bsz, nstate2 // 2), const),
            pl.BlockSpec((nstate2, w), const),
            pl.BlockSpec((1, w), const),
            pl.BlockSpec((w, w), const),
            pl.BlockSpec((1, w), const),
        ],
        out_specs=pl.BlockSpec((blk, w), lambda c: (c, 0)),
        scratch_shapes=[pltpu.VMEM((blk, nstate2), F32), pltpu.VMEM((bsz, nstate2), F32)],
        compiler_params=_cparams(("arbitrary",)),
        name="s5_mixer",
    )(u_tm, wb, lre, lim, wc, d, wg, bg)


def _s5_params(lam_re, lam_im, log_dt, b_re, b_im, c_re, c_im, bsz):
    g, n = lam_re.shape
    dt = jnp.exp(log_dt.astype(F32))[:, None]
    xr = lam_re * dt
    th = lam_im * dt
    er = jnp.exp(xr)
    lbr = er * jnp.cos(th)
    lbi = er * jnp.sin(th)
    ar = jnp.expm1(xr) * jnp.cos(th) - 2.0 * jnp.sin(0.5 * th) ** 2
    ai = lbi
    den = lam_re * lam_re + lam_im * lam_im
    fr = (ar * lam_re + ai * lam_im) / den
    fi = (ai * lam_re - ar * lam_im) / den
    bbr = fr[..., None] * b_re - fi[..., None] * b_im
    bbi = fr[..., None] * b_im + fi[..., None] * b_re
    eye = jnp.eye(g, dtype=F32)
    p = b_re.shape[-1]
    wb_re = jnp.einsum("gnp,gh->gphn", bbr, eye).reshape(g * p, g * n)
    wb_im = jnp.einsum("gnp,gh->gphn", bbi, eye).reshape(g * p, g * n)
    wb = jnp.concatenate([wb_re, wb_im], axis=1).astype(BF16)
    wc_re = jnp.einsum("gpn,gh->gnhp", c_re, eye).reshape(g * n, g * p)
    wc_im = jnp.einsum("gpn,gh->gnhp", c_im, eye).reshape(g * n, g * p)
    wc = jnp.concatenate([wc_re, -wc_im], axis=0).astype(BF16)
    lre = jnp.broadcast_to(lbr.reshape(1, g * n), (bsz, g * n))
    lim = jnp.broadcast_to(lbi.reshape(1, g * n), (bsz, g * n))
    return wb, lre, lim, wc


VA_ROWS = 2 * HEAD_DIM


def _iota2(shape, dim):
    return lax.broadcasted_iota(jnp.int32, shape, dim)


def _stack_masked_q(q_ref, qs_ref, tq, ngroups, width):
    lane = _iota2((1, GROUP_W), 1)
    q = q_ref[...]
    for g in range(ngroups):
        keep = (lane >= g * width) & (lane < (g + 1) * width)
        qs_ref[g * tq:(g + 1) * tq, :] = jnp.where(keep, q, jnp.zeros_like(q))


def _key_scores(k_ref, qs_ref, j, tk):
    row0 = pl.multiple_of(jnp.maximum(j, 0) * tk, tk)
    return _dot_nt(k_ref[pl.ds(row0, tk), :], qs_ref[...])


def _score_prefetch(k_ref, qs_ref, j, dst_ref, tq, tk, bias_of=None, max_ref=None):
    row0 = pl.multiple_of(jnp.maximum(j, 0) * tk, tk)

    def emit(g):
        cols = slice(g * tq, (g + 1) * tq)
        s = _dot_nt(k_ref[pl.ds(row0, tk), :], qs_ref[cols, :])
        if bias_of is not None:
            s = s + bias_of(g)[...]
        dst_ref[:, cols] = s
        if max_ref is not None:
            max_ref[g:g + 1, :] = jnp.max(s, axis=0, keepdims=True)

    return emit


def _paired_key_tiles(i, k_ref, qs_ref, tile_fn, buf_a, buf_b, tq, tk, bias_of=None):
    def body(n, c):
        ja = i - 1 - 2 * n
        tile_fn(ja, buf_a, True, _score_prefetch(k_ref, qs_ref, ja - 1, buf_b[0], tq, tk, bias_of, buf_b[1]))
        tile_fn(ja - 1, buf_b, ja >= 1, _score_prefetch(k_ref, qs_ref, ja - 2, buf_a[0], tq, tk, bias_of, buf_a[1]))
        return c

    lax.fori_loop(0, (i + 1) // 2, body, 0)


def _sb_kernel(q_ref, k_ref, vt_ref, later_ref, o_ref, qs_ref, carry_ref, acc_ref, za_ref, zb_ref,
               *, tq, tk, nheads):
    i = pl.program_id(1)
    _stack_masked_q(q_ref, qs_ref, tq, nheads, HEAD_DIM)

    def tile(j, buf, valid, prefetch, diagonal=False):
        zt = buf[0]
        if diagonal:
            before = _iota2((tk, tq), 0) < _iota2((tk, tq), 1)
        else:
            penalty = jnp.where(valid, 0.0, -NEG_INF)
            keep = jnp.where(valid, 1.0, 0.0)
        jv = jnp.maximum(j, 0)

        def scan_stage(h):
            z = zt[:, h * tq:(h + 1) * tq]
            neg_abs = lax.bitcast_convert_type(
                lax.bitcast_convert_type(z, jnp.uint32) | jnp.uint32(0x80000000), F32)
            sp = jnp.maximum(z, 0.0) + jnp.log2(1.0 + jnp.exp2(neg_abs))
            nl1m = jnp.where(before, sp, 0.0) if diagonal else sp
            inside = _dot(later_ref[...], nl1m.astype(BF16))
            return z - sp, inside, nl1m[0:1, :]

        def value_stage(h, logit, inside, first_row):
            hs = slice(h * HEAD_DIM, (h + 1) * HEAD_DIM)
            between = inside if diagonal else inside + (carry_ref[h:h + 1, :] + penalty)
            w = jnp.exp2(logit - between)
            if diagonal:
                w = jnp.where(before, w, 0.0)
            pv = _dot(vt_ref[jv, hs, :], w.astype(BF16))
            total = inside[0:1, :] + first_row
            if diagonal:
                acc_ref[hs, :] = pv
                carry_ref[h:h + 1, :] = total
            else:
                acc_ref[hs, :] = acc_ref[hs, :] + pv
                carry_ref[h:h + 1, :] = carry_ref[h:h + 1, :] + keep * total

        prefetch(0)
        pending = scan_stage(0)
        for h in range(1, nheads):
            prefetch(h)
            upcoming = scan_stage(h)
            value_stage(h - 1, *pending)
            pending = upcoming
        value_stage(nheads - 1, *pending)

    tile(i, (_key_scores(k_ref, qs_ref, i, tk), None), True,
         _score_prefetch(k_ref, qs_ref, i - 1, za_ref, tq, tk), diagonal=True)
    _paired_key_tiles(i, k_ref, qs_ref, tile, (za_ref, None), (zb_ref, None), tq, tk)
    o_ref[...] = acc_ref[...].astype(o_ref.dtype)


def _softmax_tile(buf, tq, off, va, g, first, m_ref, acc_ref):
    st_ref, max_ref = buf
    rows = slice(g * VA_ROWS, (g + 1) * VA_ROWS)
    parts, alphas = [], []
    for c0 in range(0, tq, LANES):
        cs = slice(c0, c0 + LANES)
        sc = slice(g * tq + c0, g * tq + c0 + LANES)
        blk_max = max_ref[g:g + 1, cs]
        if first:
            m_new = blk_max
            shift = m_new
        else:
            m_old = m_ref[g:g + 1, cs]
            m_new = jnp.maximum(m_old, blk_max + off)
            shift = m_new - off
            alphas.append(jnp.exp2(m_old - m_new))
        parts.append(jnp.exp2(st_ref[:, sc] - shift).astype(BF16))
        m_ref[g:g + 1, cs] = m_new
    pv = _dot(va, jnp.concatenate(parts, axis=1))
    if first:
        acc_ref[rows, :] = pv
    else:
        acc_ref[rows, :] = jnp.concatenate(alphas, axis=1) * acc_ref[rows, :] + pv


def _normalised(acc_ref, g):
    base = g * VA_ROWS
    return acc_ref[base:base + HEAD_DIM, :] / acc_ref[base + HEAD_DIM:base + HEAD_DIM + 1, :]


def _ch_kernel(q_ref, k_ref, va_ref, bias_ref, o_ref, qs_ref, m_ref, acc_ref, z_ref, zmax_ref,
               *, tq, tk, nheads, nback):
    i = pl.program_id(1)
    _stack_masked_q(q_ref, qs_ref, tq, nheads, HEAD_DIM)
    buf = (z_ref, zmax_ref)

    def tile(n, first):
        j = i - n
        emit = _score_prefetch(k_ref, qs_ref, j, z_ref, tq, tk, lambda h: bias_ref.at[h, n], zmax_ref)
        for h in range(nheads):
            emit(h)
        for h in range(nheads):
            _softmax_tile(buf, tq, 0.0, va_ref[j, h * VA_ROWS:(h + 1) * VA_ROWS, :], h, first, m_ref, acc_ref)

    tile(0, True)

    def body(n, c):
        tile(n, False)
        return c

    lax.fori_loop(1, jnp.minimum(i, nback) + 1, body, 0)
    for h in range(nheads):
        o_ref[h * HEAD_DIM:(h + 1) * HEAD_DIM, :] = _normalised(acc_ref, h).astype(o_ref.dtype)


def _df_kernel(lam_ref, q_ref, k_ref, va_ref, kb_ref, db_ref, o_ref, qs_ref, m_ref, acc_ref,
               za_ref, zb_ref, maxa_ref, maxb_ref, *, tq, tk, nheads, slopes2):
    i = pl.program_id(1)
    _stack_masked_q(q_ref, qs_ref, tq, 2 * nheads, DF_QK_DIM)
    buf_a, buf_b = (za_ref, maxa_ref), (zb_ref, maxb_ref)

    def past_bias(g):
        return kb_ref.at[g // 2]

    def tile(j, buf, valid, prefetch, diagonal=False):
        jv = jnp.maximum(j, 0)
        for g in range(2 * nheads):
            prefetch(g)
            h = g // 2
            va = va_ref[jv, h * VA_ROWS:(h + 1) * VA_ROWS, :]
            if diagonal:
                _softmax_tile(buf, tq, 0.0, va, g, True, m_ref, acc_ref)
            else:
                off = jnp.where(valid, slopes2[h] * ((j - i) * tk).astype(F32), NEG_INF)
                _softmax_tile(buf, tq, off, va, g, False, m_ref, acc_ref)

    diag = _score_prefetch(k_ref, qs_ref, i, zb_ref, tq, tk, lambda g: db_ref.at[g // 2], maxb_ref)
    for g in range(2 * nheads):
        diag(g)
    tile(i, buf_b, True, _score_prefetch(k_ref, qs_ref, i - 1, za_ref, tq, tk, past_bias, maxa_ref), diagonal=True)
    _paired_key_tiles(i, k_ref, qs_ref, tile, buf_a, buf_b, tq, tk, past_bias)
    lam = lam_ref[0]
    for h in range(nheads):
        out = _normalised(acc_ref, 2 * h) - lam * _normalised(acc_ref, 2 * h + 1)
        o_ref[h * HEAD_DIM:(h + 1) * HEAD_DIM, :] = out.astype(o_ref.dtype)


def _attention_call(kernel, qkv, col0, vt, extra, extra_specs, scratch, bsz, seq, name, smem=()):
    tq, tk = TQ_ATT, TK_ATT
    assert tq == tk
    nq = seq // tq
    vrows = vt.shape[2]
    return pl.pallas_call(
        kernel,
        out_shape=jax.ShapeDtypeStruct((bsz, GROUP_W, seq), BF16),
        grid=(bsz, nq),
        in_specs=[pl.BlockSpec(memory_space=pltpu.SMEM) for _ in smem] + [
            pl.BlockSpec((tq, GROUP_W), lambda b, i: (b * nq + i, col0)),
            pl.BlockSpec((seq, GROUP_W), lambda b, i: (b, col0 + 1)),
            pl.BlockSpec((None, seq // tk, vrows, tk), lambda b, i: (b, 0, 0, 0)),
        ] + extra_specs,
        out_specs=pl.BlockSpec((None, GROUP_W, tq), lambda b, i: (b, 0, i)),
        scratch_shapes=scratch,
        compiler_params=_cparams(("arbitrary", "arbitrary")),
        name=name,
    )(*smem, qkv, qkv, vt, *extra)


def _values_t(v, bsz, seq, nheads, ones_row):
    if ones_row:
        v4 = v.reshape(bsz, seq, nheads, HEAD_DIM)
        pad = jnp.zeros((bsz, seq, nheads, VA_ROWS - HEAD_DIM - 1), v.dtype)
        v = jnp.concatenate([v4, jnp.ones((bsz, seq, nheads, 1), v.dtype), pad], axis=-1)
    rows = v.size // (bsz * seq)
    return v.reshape(bsz, seq // TK_ATT, TK_ATT, rows).transpose(0, 1, 3, 2)


def _tokens_major(o, n):
    return o.transpose(0, 2, 1).reshape(n, GROUP_W)


def _ch_bias_tiles(rel_bias, tq, tk):
    assert tq == tk
    nback = CH_LEFT_CHUNKS * CHUNK // tk
    nh = rel_bias.shape[0]
    n = np.arange(nback + 1)[:, None, None]
    s = np.arange(tk)[None, :, None]
    t = np.arange(tq)[None, None, :]
    delta = (n * tk + t) // CHUNK - s // CHUNK
    valid = (delta >= 0) & (delta <= CH_LEFT_CHUNKS)
    table = rel_bias.astype(F32) * LOG2E
    lo, hi = -(tk - 1), nback * tk + tq - 1
    ext = jnp.concatenate([jnp.repeat(table[:, :1], -REL_CLIP - lo, axis=1), table,
                           jnp.repeat(table[:, -1:], hi - REL_CLIP, axis=1)], axis=1)
    period = tk + tq
    tiles = []
    for m in range(nback + 1):
        w = ext[:, m * tk:m * tk + period - 1]
        w = jnp.concatenate([w, jnp.zeros((nh, 1), F32)], axis=1)
        skew = jnp.tile(w, (1, tk))[:, :tk * (period - 1)].reshape(nh, tk, period - 1)
        tiles.append(skew[:, :, tk - 1:tk - 1 + tq])
    bias = jnp.stack(tiles, axis=1)
    return jnp.where(jnp.asarray(valid)[None], bias, NEG_INF), nback


def _df_bias_tiles(slopes2, tq, tk):
    s = jnp.arange(tk, dtype=jnp.int32)[:, None]
    t = jnp.arange(tq, dtype=jnp.int32)[None, :]
    sl = jnp.asarray(slopes2, F32)[:, None, None]
    kb = sl * jnp.broadcast_to(s, (tk, tq)).astype(F32)
    allowed = (s // CHUNK) <= (t // CHUNK)
    db = jnp.where(allowed[None], sl * (t - jnp.abs(t - s)).astype(F32), NEG_INF)
    return kb, db


def _out_proj_kernel(ys_ref, ysb_ref, ych_ref, ydf_ref, g_ref, ind_ref, w_ref, x_ref, o_ref):
    acc = x_ref[...]
    for gi, y_ref in enumerate((ys_ref, ysb_ref, ych_ref, ydf_ref)):
        cs = slice(gi * GROUP_W, (gi + 1) * GROUP_W)
        y = y_ref[...].astype(F32)
        ms = _group_ms(y, ind_ref[...], 1.0 / HEAD_DIM)
        yn = (y * lax.rsqrt(ms + EPS) * g_ref[:, cs]).astype(BF16)
        acc = acc + _dot(yn, w_ref[cs, :])
    o_ref[...] = acc


def _out_proj(y_ssm_tm, y_sb, y_ch, y_df, g, ind64, w, x2, bsz, seq):
    n, dm = x2.shape
    tm = TM_PROJ
    ns = seq // tm
    const = lambda i: (0, 0)
    tile = pl.BlockSpec((tm, GROUP_W), lambda i: (i, 0))
    return pl.pallas_call(
        _out_proj_kernel,
        out_shape=jax.ShapeDtypeStruct((n, dm), F32),
        grid=(n // tm,),
        in_specs=[
            pl.BlockSpec((tm, GROUP_W), lambda i: (i % ns, i // ns)),
            tile, tile, tile,
            pl.BlockSpec((1, dm), const),
            pl.BlockSpec((GROUP_W, GROUP_W), const),
            pl.BlockSpec((dm, dm), const),
            pl.BlockSpec((tm, dm), lambda i: (i, 0)),
        ],
        out_specs=pl.BlockSpec((tm, dm), lambda i: (i, 0)),
        compiler_params=_cparams(("arbitrary",)),
        name="out_proj",
    )(y_ssm_tm, y_sb, y_ch, y_df, g, ind64, w, x2)


def _ffn_kernel(x_ref, g_ref, w1_ref, w3_ref, w2_ref, o_ref, hid_ref, *, tf):
    x = x_ref[...]
    ms = jnp.mean(x * x, axis=-1, keepdims=True)
    h = (x * lax.rsqrt(ms + EPS) * g_ref[...]).astype(BF16)
    dff = w1_ref.shape[1]
    for c in range(dff // tf):
        cs = slice(c * tf, (c + 1) * tf)
        a = _dot(h, w1_ref[:, cs])
        b = _dot(h, w3_ref[:, cs])
        hid_ref[:, cs] = (jax.nn.silu(a) * b).astype(BF16)
    o_ref[...] = x + _dot(hid_ref[...], w2_ref[...])


def _ffn(x2, g, w1, w3, w2):
    n, dm = x2.shape
    dff = w1.shape[1]
    tm = TM_PROJ
    const = lambda i: (0, 0)
    return pl.pallas_call(
        functools.partial(_ffn_kernel, tf=GROUP_W),
        out_shape=jax.ShapeDtypeStruct((n, dm), F32),
        grid=(n // tm,),
        in_specs=[
            pl.BlockSpec((tm, dm), lambda i: (i, 0)),
            pl.BlockSpec((1, dm), const),
            pl.BlockSpec((dm, dff), const),
            pl.BlockSpec((dm, dff), const),
            pl.BlockSpec((dff, dm), const),
        ],
        out_specs=pl.BlockSpec((tm, dm), lambda i: (i, 0)),
        scratch_shapes=[pltpu.VMEM((tm, dff), BF16)],
        compiler_params=_cparams(("arbitrary",)),
        name="ffn_dense",
    )(x2, g, w1, w3, w2)


def _router_kernel(x_ref, g_ref, wr_ref, tri_ref, h_ref, route_ref, cnt_ref, carry_ref):
    @pl.when(pl.program_id(0) == 0)
    def _():
        carry_ref[...] = jnp.zeros_like(carry_ref)

    x = x_ref[...]
    ms = jnp.mean(x * x, axis=-1, keepdims=True)
    h = x * lax.rsqrt(ms + EPS) * g_ref[...]
    h_hi = h.astype(BF16)
    half = h.shape[1] // 2
    bits = lax.bitcast_convert_type(h_hi.astype(F32), jnp.uint32)
    h_ref[...] = bits[:, :half] | lax.shift_right_logical(bits[:, half:], jnp.uint32(16))
    h_lo = (h - h_hi.astype(F32)).astype(BF16)
    logits = _dot(h_hi, wr_ref[0]) + _dot(h_lo, wr_ref[0]) + _dot(h_hi, wr_ref[1])

    lane = _iota2(logits.shape, 1).astype(F32)
    lg = jnp.where(lane < N_EXPERTS, logits, -jnp.inf)
    m1 = jnp.max(lg, axis=-1, keepdims=True)
    e1 = jnp.min(jnp.where(lg == m1, lane, float(LANES)), axis=-1, keepdims=True)
    lg2 = jnp.where(lane == e1, -jnp.inf, lg)
    m2 = jnp.max(lg2, axis=-1, keepdims=True)
    e2 = jnp.min(jnp.where(lg2 == m2, lane, float(LANES)), axis=-1, keepdims=True)
    t = jnp.exp(m2 - m1)
    g1 = 1.0 / (1.0 + t)
    g2 = t / (1.0 + t)

    hot1 = lane == e1
    hot2 = lane == e2
    sel = jnp.where(hot1 | hot2, 1.0, 0.0).astype(BF16)
    prior = _dot(tri_ref[...], sel) + carry_ref[...]
    r1 = jnp.sum(jnp.where(hot1, prior, 0.0), axis=-1, keepdims=True)
    r2 = jnp.sum(jnp.where(hot2, prior, 0.0), axis=-1, keepdims=True)
    carry_ref[...] = carry_ref[...] + jnp.sum(sel.astype(F32), axis=0, keepdims=True)
    cnt_ref[...] = carry_ref[...]

    out = jnp.where(lane == 0, e1, 0.0)
    out = jnp.where(lane == 1, e2, out)
    out = jnp.where(lane == 2, g1, out)
    out = jnp.where(lane == 3, g2, out)
    out = jnp.where(lane == 4, r1, out)
    out = jnp.where(lane == 5, r2, out)
    route_ref[...] = out


def _router(x2, g, wr2, tri):
    n, dm = x2.shape
    tm = TM_PROJ
    const = lambda i: (0, 0)
    return pl.pallas_call(
        _router_kernel,
        out_shape=(jax.ShapeDtypeStruct((n, dm // 2), jnp.uint32),
                   jax.ShapeDtypeStruct((n, LANES), F32),
                   jax.ShapeDtypeStruct((1, LANES), F32)),
        grid=(n // tm,),
        in_specs=[
            pl.BlockSpec((tm, dm), lambda i: (i, 0)),
            pl.BlockSpec((1, dm), const),
            pl.BlockSpec((2, dm, LANES), lambda i: (0, 0, 0)),
            pl.BlockSpec((tm, tm), const),
        ],
        out_specs=(
            pl.BlockSpec((tm, dm // 2), lambda i: (i, 0)),
            pl.BlockSpec((tm, LANES), lambda i: (i, 0)),
            pl.BlockSpec((1, LANES), const),
        ),
        scratch_shapes=[pltpu.VMEM((1, LANES), F32)],
        compiler_params=_cparams(("arbitrary",)),
        name="moe_router",
    )(x2, g, wr2, tri)


def _row_copy(src_ref, src_row, dst_ref, dst_row, sem):
    return pltpu.make_async_copy(src_ref.at[pl.ds(src_row, 1)], dst_ref.at[pl.ds(dst_row, 1)], sem)


def _dispatch_kernel(d1_ref, d2_ref, cnt_ref, h_ref, xs_ref, zero_ref, sem, tail_sem, *, tm, cap, nexp):
    def issue(r, c):
        _row_copy(h_ref, r, xs_ref, d1_ref[0, 0, r], sem).start()
        _row_copy(h_ref, r, xs_ref, d2_ref[0, 0, r], sem).start()
        return c

    lax.fori_loop(0, tm, issue, 0, unroll=8)

    def aligned_end(e):
        return ((cnt_ref[e] + (SUBLANES - 1)) // SUBLANES) * SUBLANES

    def tail_copy(e):
        row0 = pl.multiple_of(e * cap + aligned_end(e), SUBLANES)
        return pltpu.make_async_copy(zero_ref, xs_ref.at[pl.ds(row0, tm)], tail_sem)

    def tail_rows(e, fn):
        for k in range(SUBLANES - 1):
            @pl.when(cnt_ref[e] + k < aligned_end(e))
            def _():
                fn(_row_copy(zero_ref, 0, xs_ref, e * cap + cnt_ref[e] + k, tail_sem))

    @pl.when(pl.program_id(0) == pl.num_programs(0) - 1)
    def _():
        zero_ref[...] = jnp.zeros_like(zero_ref)
        for e in range(nexp):
            tail_copy(e).start()
            tail_rows(e, lambda cp: cp.start())
        for e in range(nexp):
            tail_copy(e).wait()
            tail_rows(e, lambda cp: cp.wait())

    for _ in range(2):
        pltpu.make_async_copy(h_ref, xs_ref.at[pl.ds(0, tm)], sem).wait()


def _dispatch(d1, d2, cnt, h, cap):
    n, wd = h.shape
    tm = TM_PROJ
    nt = n // tm
    idx_spec = pl.BlockSpec((1, 1, tm), lambda i: (i, 0, 0), memory_space=pltpu.SMEM)
    return pl.pallas_call(
        functools.partial(_dispatch_kernel, tm=tm, cap=cap, nexp=N_EXPERTS),
        out_shape=jax.ShapeDtypeStruct((N_EXPERTS * cap, wd), h.dtype),
        grid=(nt,),
        in_specs=[idx_spec, idx_spec, pl.BlockSpec(memory_space=pltpu.SMEM),
                  pl.BlockSpec((tm, wd), lambda i: (i, 0))],
        out_specs=pl.BlockSpec(memory_space=pl.ANY),
        scratch_shapes=[pltpu.VMEM((tm, wd), h.dtype), pltpu.SemaphoreType.DMA, pltpu.SemaphoreType.DMA],
        compiler_params=pltpu.CompilerParams(dimension_semantics=("arbitrary",),
                                             vmem_limit_bytes=VMEM_LIMIT_BYTES, disable_bounds_checks=True),
        name="moe_dispatch",
    )(d1.reshape(nt, 1, tm), d2.reshape(nt, 1, tm), cnt, h)


def _combine_kernel(d1_ref, d2_ref, x_ref, route_ref, ys_ref, o_ref, y1_ref, y2_ref, sem, *, tm):
    def issue(r, c):
        _row_copy(ys_ref, d1_ref[0, 0, r], y1_ref, r, sem).start()
        _row_copy(ys_ref, d2_ref[0, 0, r], y2_ref, r, sem).start()
        return c

    lax.fori_loop(0, tm, issue, 0, unroll=8)
    for buf in (y1_ref, y2_ref):
        pltpu.make_async_copy(ys_ref.at[pl.ds(0, tm)], buf, sem).wait()
    o_ref[...] = x_ref[...] + route_ref[:, 2:3] * y1_ref[...] + route_ref[:, 3:4] * y2_ref[...]


def _combine(d1, d2, x2, route, ys):
    n, dm = x2.shape
    tm = TM_PROJ
    nt = n // tm
    idx_spec = pl.BlockSpec((1, 1, tm), lambda i: (i, 0, 0), memory_space=pltpu.SMEM)
    return pl.pallas_call(
        functools.partial(_combine_kernel, tm=tm),
        out_shape=jax.ShapeDtypeStruct((n, dm), F32),
        grid=(nt,),
        in_specs=[idx_spec, idx_spec, pl.BlockSpec((tm, dm), lambda i: (i, 0)),
                  pl.BlockSpec((tm, LANES), lambda i: (i, 0)), pl.BlockSpec(memory_space=pl.ANY)],
        out_specs=pl.BlockSpec((tm, dm), lambda i: (i, 0)),
        scratch_shapes=[pltpu.VMEM((tm, dm), F32), pltpu.VMEM((tm, dm), F32), pltpu.SemaphoreType.DMA],
        compiler_params=pltpu.CompilerParams(dimension_semantics=("arbitrary",),
                                             vmem_limit_bytes=VMEM_LIMIT_BYTES, disable_bounds_checks=True),
        name="moe_combine",
    )(d1.reshape(nt, 1, tm), d2.reshape(nt, 1, tm), x2, route, ys)


def _experts_kernel(te_ref, blk_ref, nused_ref, xs_ref, w1_ref, w3_ref, w2_ref, o_ref, x_ref, hid_ref, acc_ref):
    t = pl.program_id(0)
    c = pl.program_id(1)
    nc = pl.num_programs(1)

    @pl.when(jnp.logical_and(t < nused_ref[0], c == 0))
    def _():
        w = xs_ref[...]
        half = w.shape[1]
        x_ref[:, :half] = lax.bitcast_convert_type(w & jnp.uint32(0xFFFF0000), F32).astype(BF16)
        x_ref[:, half:] = lax.bitcast_convert_type(lax.shift_left(w, jnp.uint32(16)), F32).astype(BF16)

    @pl.when(t < nused_ref[0])
    def _():
        x = x_ref[...]
        tf = w1_ref.shape[2]
        for s0 in range(0, tf, GROUP_W):
            cs = slice(s0, s0 + GROUP_W)
            a = _dot(x, w1_ref[0, :, cs])
            b = _dot(x, w3_ref[0, :, cs])
            hid_ref[:, cs] = (jax.nn.silu(a) * b).astype(BF16)
        part = _dot(hid_ref[...], w2_ref[0])

        @pl.when(c == 0)
        def _():
            acc_ref[...] = part

        @pl.when(c > 0)
        def _():
            acc_ref[...] = acc_ref[...] + part

        @pl.when(c == nc - 1)
        def _():
            o_ref[...] = acc_ref[...]

    @pl.when(jnp.logical_and(t >= nused_ref[0], c == nc - 1))
    def _():
        o_ref[...] = jnp.zeros_like(o_ref)


def _experts(tile_expert, tile_block, nused, xs, w1, w3, w2, nt):
    rows, wd = xs.shape
    dm = 2 * wd
    dff = w1.shape[2]
    tm, tf = TM_EXPERT, TF_EXPERT
    nc = dff // tf
    spare = rows // tm

    def cc(t, c, nu):
        return jnp.where(t < nu[0], c, nc - 1)

    grid_spec = pltpu.PrefetchScalarGridSpec(
        num_scalar_prefetch=3,
        grid=(nt, nc),
        in_specs=[
            pl.BlockSpec((tm, wd), lambda t, c, te, tb, nu: (tb[t], 0)),
            pl.BlockSpec((1, dm, tf), lambda t, c, te, tb, nu: (te[t], 0, cc(t, c, nu))),
            pl.BlockSpec((1, dm, tf), lambda t, c, te, tb, nu: (te[t], 0, cc(t, c, nu))),
            pl.BlockSpec((1, tf, dm), lambda t, c, te, tb, nu: (te[t], cc(t, c, nu), 0)),
        ],
        out_specs=pl.BlockSpec((tm, dm), lambda t, c, te, tb, nu: (jnp.where(t < nu[0], tb[t], spare), 0)),
        scratch_shapes=[pltpu.VMEM((tm, dm), BF16), pltpu.VMEM((tm, tf), BF16), pltpu.VMEM((tm, dm), F32)],
    )
    return pl.pallas_call(
        _experts_kernel,
        out_shape=jax.ShapeDtypeStruct((rows + tm, dm), F32),
        grid_spec=grid_spec,
        compiler_params=_cparams(("arbitrary", "arbitrary")),
        name="moe_experts",
    )(tile_expert, tile_block, nused, xs, w1, w3, w2)


def _moe(x2, g, w_router, w1, w3, w2, tri):
    n, dm = x2.shape
    wr = jnp.zeros((dm, LANES), F32).at[:, :N_EXPERTS].set(w_router.astype(F32))
    wr_hi = wr.astype(BF16)
    wr_lo = (wr - wr_hi.astype(F32)).astype(BF16)
    h, route, counts = _router(x2, g, jnp.stack([wr_hi, wr_lo]), tri)

    tm = TM_EXPERT
    assert n % tm == 0 and TM_PROJ == tm
    cap = n + tm
    d1 = route[:, 0].astype(jnp.int32) * cap + route[:, 4].astype(jnp.int32)
    d2 = route[:, 1].astype(jnp.int32) * cap + route[:, 5].astype(jnp.int32)
    cnt = counts[0, :N_EXPERTS].astype(jnp.int32)

    nt = 2 * n // tm + N_EXPERTS
    ntile = (cnt + tm - 1) // tm
    ends = jnp.cumsum(ntile)
    nused = ends[-1]
    t = jnp.minimum(jnp.arange(nt, dtype=jnp.int32), nused - 1)
    tile_expert = jnp.sum(t[:, None] >= ends[None, :], axis=1).astype(jnp.int32)
    tile_block = tile_expert * (cap // tm) + t - (ends - ntile)[tile_expert]

    xs = _dispatch(d1, d2, cnt, h, cap)
    ys = _experts(tile_expert, tile_block.astype(jnp.int32), nused.reshape(1).astype(jnp.int32), xs, w1, w3, w2, nt)
    return _combine(d1, d2, x2, route, ys)


def kernel(x, norm_mix_g, w_in, ssm_lam_re, ssm_lam_im, ssm_log_dt, ssm_b_re, ssm_b_im, ssm_c_re, ssm_c_im,
           ssm_d, ssm_w_glu, ssm_b_glu, ch_q_norm_g, ch_k_norm_g, ch_rel_bias, df_q_norm_g, df_k_norm_g,
           df_lambda, out_norm_g, w_out, norm_ffn_g, ffn_w1, ffn_w3, ffn_w2, moe_router, moe_w1, moe_w3, moe_w2):
    bsz, seq, dm = x.shape
    depth = w_in.shape[0]
    n = bsz * seq
    nheads = GROUP_W // HEAD_DIM
    ind64 = _block_indicator(GROUP_W, HEAD_DIM)
    ind32 = _block_indicator(GROUP_W, DF_QK_DIM)
    tri = (jnp.arange(TM_PROJ)[:, None] > jnp.arange(TM_PROJ)[None, :]).astype(BF16)
    slopes = [2.0 ** (-8.0 * (h + 1) / nheads) for h in range(nheads)]
    slopes2 = tuple(s * LOG2E for s in slopes)
    df_kb, df_db = _df_bias_tiles(slopes2, TQ_ATT, TK_ATT)
    later = (jnp.arange(TK_ATT)[:, None] < jnp.arange(TK_ATT)[None, :]).astype(BF16)

    x2 = x.reshape(n, dm)
    for layer in range(depth):
        qkg = jnp.stack([
            jnp.tile(ch_q_norm_g[layer].astype(F32), nheads) * (HEAD_DIM ** -0.5 * LOG2E),
            jnp.tile(ch_k_norm_g[layer].astype(F32), nheads),
            jnp.tile(df_q_norm_g[layer].astype(F32).reshape(-1), nheads) * (DF_QK_DIM ** -0.5 * LOG2E),
            jnp.tile(df_k_norm_g[layer].astype(F32).reshape(-1), nheads),
        ])
        u_tm, qkv = _in_proj(x2, norm_mix_g[layer].reshape(1, dm), w_in[layer].astype(BF16), qkg,
                             ind64, ind32, bsz, seq)

        wb, lre, lim, wc = _s5_params(ssm_lam_re[layer], ssm_lam_im[layer], ssm_log_dt[layer],
                                      ssm_b_re[layer], ssm_b_im[layer], ssm_c_re[layer], ssm_c_im[layer], bsz)
        y_ssm = _s5(u_tm.reshape(seq * bsz, GROUP_W), wb, lre, lim, wc, ssm_d[layer].reshape(1, GROUP_W),
                    ssm_w_glu[layer].astype(BF16), ssm_b_glu[layer].reshape(1, GROUP_W), bsz)
        y_ssm = y_ssm.reshape(seq, bsz * GROUP_W)

        tq, tk = TQ_ATT, TK_ATT
        const3 = lambda b, i: (0, 0, 0)
        y_sb = _attention_call(
            functools.partial(_sb_kernel, tq=tq, tk=tk, nheads=nheads),
            qkv, 0, _values_t(qkv[:, 2 * GROUP_W:3 * GROUP_W], bsz, seq, nheads, False),
            (later,), [pl.BlockSpec((tk, tk), lambda b, i: (0, 0))],
            [pltpu.VMEM((nheads * tq, GROUP_W), BF16), pltpu.VMEM((SUBLANES, tq), F32),
             pltpu.VMEM((GROUP_W, tq), F32),
             pltpu.VMEM((tk, nheads * tq), F32), pltpu.VMEM((tk, nheads * tq), F32)],
            bsz, seq, "sb_attention")

        bias, nback = _ch_bias_tiles(ch_rel_bias[layer], tq, tk)
        y_ch = _attention_call(
            functools.partial(_ch_kernel, tq=tq, tk=tk, nheads=nheads, nback=nback),
            qkv, 3, _values_t(qkv[:, 5 * GROUP_W:6 * GROUP_W], bsz, seq, nheads, True),
            (bias,), [pl.BlockSpec(bias.shape, lambda b, i: (0, 0, 0, 0))],
            [pltpu.VMEM((nheads * tq, GROUP_W), BF16), pltpu.VMEM((SUBLANES, tq), F32),
             pltpu.VMEM((nheads * VA_ROWS, tq), F32), pltpu.VMEM((tk, nheads * tq), F32),
             pltpu.VMEM((SUBLANES, tq), F32)],
            bsz, seq, "ch_attention")

        lambda_init = 0.8 - 0.6 * math.exp(-0.3 * layer)
        lam_p = df_lambda[layer].astype(F32)
        lam = jnp.exp(jnp.sum(lam_p[0] * lam_p[1])) - jnp.exp(jnp.sum(lam_p[2] * lam_p[3])) + lambda_init
        y_df = _attention_call(
            functools.partial(_df_kernel, tq=tq, tk=tk, nheads=nheads, slopes2=slopes2),
            qkv, 6, _values_t(qkv[:, 8 * GROUP_W:9 * GROUP_W], bsz, seq, nheads, True),
            (df_kb, df_db), [pl.BlockSpec(df_kb.shape, const3), pl.BlockSpec(df_db.shape, const3)],
            [pltpu.VMEM((2 * nheads * tq, GROUP_W), BF16), pltpu.VMEM((2 * nheads, tq), F32),
             pltpu.VMEM((2 * nheads * VA_ROWS, tq), F32),
             pltpu.VMEM((tk, 2 * nheads * tq), F32), pltpu.VMEM((tk, 2 * nheads * tq), F32),
             pltpu.VMEM((2 * nheads, tq), F32), pltpu.VMEM((2 * nheads, tq), F32)],
            bsz, seq, "df_attention", smem=(lam.reshape(1),))
        y_sb, y_ch, y_df = (_tokens_major(y, n) for y in (y_sb, y_ch, y_df))

        head_scale = jnp.concatenate([jnp.ones((dm - GROUP_W,), F32),
                                      jnp.full((GROUP_W,), 1.0 - lambda_init, F32)])
        g_out = (out_norm_g[layer].astype(F32) * head_scale).reshape(1, dm)
        x2 = _out_proj(y_ssm, y_sb, y_ch, y_df, g_out, ind64, w_out[layer].astype(BF16), x2, bsz, seq)

        idx = layer // 2
        g_ffn = norm_ffn_g[layer].reshape(1, dm)
        if layer % 2 == 0:
            x2 = _ffn(x2, g_ffn, ffn_w1[idx].astype(BF16), ffn_w3[idx].astype(BF16), ffn_w2[idx].astype(BF16))
        else:
            x2 = _moe(x2, g_ffn, moe_router[idx], moe_w1[idx].astype(BF16), moe_w3[idx].astype(BF16),
                      moe_w2[idx].astype(BF16), tri)
    return x2.reshape(bsz, seq, dm)
```

```python
import functools
import math

import jax
import jax.numpy as jnp
import numpy as np
from jax import lax
from jax.experimental import pallas as pl
from jax.experimental.pallas import tpu as pltpu

F32 = jnp.float32
BF16 = jnp.bfloat16

EPS = 1e-6
NEG_INF = -1e30
HEAD_DIM = 64
GROUP_W = 256
CHUNK = 64
SSM_GROUP = 16
SSM_STATE = 64
DF_QK_DIM = 32
CH_LEFT_CHUNKS = 8
REL_CLIP = 128
N_EXPERTS = 8
LOG2E = 1.4426950408889634

VMEM_LIMIT_BYTES = 56 * 1024 * 1024
SUBLANES = 8
LANES = 128

TM_PROJ = 512
TQ_ATT = 256
TK_ATT = 256
SSM_STEPS = 64
TM_EXPERT = 512
TF_EXPERT = 1792


def _cparams(sem):
    return pltpu.CompilerParams(dimension_semantics=sem, vmem_limit_bytes=VMEM_LIMIT_BYTES)


def _dot(a, b):
    return jnp.dot(a, b, preferred_element_type=F32)


def _dot_nt(a, b):
    return lax.dot_general(a, b, (((1,), (1,)), ((), ())), preferred_element_type=F32)


def _split_dot(v, m):
    hi = v.astype(BF16)
    lo = (v - hi.astype(F32)).astype(BF16)
    return _dot(hi, m) + _dot(lo, m)


def _group_ms(v, ind, inv_size):
    return _split_dot(v * v, ind) * inv_size


def _block_indicator(width, group):
    r = jnp.arange(width)[:, None] // group
    c = jnp.arange(width)[None, :] // group
    return (r == c).astype(BF16)


def _in_proj_kernel(x_ref, g_ref, w_ref, qkg_ref, ind64_ref, ind32_ref, u_ref, qk_ref, vsb_ref, vch_ref, vdf_ref,
                    *, tk, nheads):
    x = x_ref[...]
    ms = jnp.mean(x * x, axis=-1, keepdims=True)
    h = (x * lax.rsqrt(ms + EPS) * g_ref[...]).astype(BF16)

    def cols(c):
        return _dot(h, w_ref[:, c * GROUP_W:(c + 1) * GROUP_W])

    def put(slot, val):
        qk_ref[:, slot * GROUP_W:(slot + 1) * GROUP_W] = val.astype(BF16)

    def put_values(v_ref, v, ones_row):
        for kt in range(v.shape[0] // tk):
            vt = v[kt * tk:(kt + 1) * tk, :].T
            if not ones_row:
                v_ref[kt] = vt.astype(BF16)
                continue
            pad = jnp.where(_iota2((VA_ROWS - HEAD_DIM, tk), 0) == 0, 1.0, 0.0).astype(BF16)
            for hd in range(nheads):
                v_ref[kt, hd * VA_ROWS:hd * VA_ROWS + HEAD_DIM, :] = vt[hd * HEAD_DIM:(hd + 1) * HEAD_DIM, :].astype(BF16)
                v_ref[kt, hd * VA_ROWS + HEAD_DIM:(hd + 1) * VA_ROWS, :] = pad

    u_ref[...] = cols(0)
    put(0, cols(1) * (HEAD_DIM ** -0.5 * LOG2E))
    put(1, cols(2))
    put_values(vsb_ref, cols(3), False)
    for c, gi in ((4, 0), (5, 1)):
        a = cols(c)
        ms_h = _group_ms(a, ind64_ref[...], 1.0 / HEAD_DIM)
        put(c - 2, a * lax.rsqrt(ms_h + EPS) * qkg_ref[gi:gi + 1, :])
    put_values(vch_ref, cols(6), True)
    for c, gi in ((7, 2), (8, 3)):
        a = cols(c)
        ms_h = _group_ms(a, ind32_ref[...], 1.0 / DF_QK_DIM)
        put(c - 3, a * lax.rsqrt(ms_h + EPS) * qkg_ref[gi:gi + 1, :])
    put_values(vdf_ref, cols(9), True)


def _in_proj(x2, g, w, qkg, ind64, ind32, bsz, seq):
    n, dm = x2.shape
    tm, tk = TM_PROJ, TK_ATT
    ns = seq // tm
    nheads = GROUP_W // HEAD_DIM
    const = lambda i: (0, 0)

    def values(rows):
        return (jax.ShapeDtypeStruct((bsz, seq // tk, rows, tk), BF16),
                pl.BlockSpec((None, tm // tk, rows, tk), lambda i: (i // ns, i % ns, 0, 0)))

    v_shapes, v_specs = zip(values(GROUP_W), values(nheads * VA_ROWS), values(nheads * VA_ROWS))
    return pl.pallas_call(
        functools.partial(_in_proj_kernel, tk=tk, nheads=nheads),
        out_shape=(jax.ShapeDtypeStruct((seq, bsz * GROUP_W), F32),
                   jax.ShapeDtypeStruct((n, 6 * GROUP_W), BF16)) + v_shapes,
        grid=(n // tm,),
        in_specs=[
            pl.BlockSpec((tm, dm), lambda i: (i, 0)),
            pl.BlockSpec((1, dm), const),
            pl.BlockSpec(w.shape, const),
            pl.BlockSpec((4, GROUP_W), const),
            pl.BlockSpec((GROUP_W, GROUP_W), const),
            pl.BlockSpec((GROUP_W, GROUP_W), const),
        ],
        out_specs=(
            pl.BlockSpec((tm, GROUP_W), lambda i: (i % ns, i // ns)),
            pl.BlockSpec((tm, 6 * GROUP_W), lambda i: (i, 0)),
        ) + v_specs,
        compiler_params=_cparams(("arbitrary",)),
        name="in_proj",
    )(x2, g, w, qkg, ind64, ind32)


def _s5_kernel(u_ref, wb_ref, lre_ref, lim_ref, wc_ref, d_ref, wg_ref, bg_ref, y_ref, bu_ref, st_ref,
               *, steps, nstate):
    @pl.when(pl.program_id(0) == 0)
    def _():
        st_ref[...] = jnp.zeros_like(st_ref)

    u = u_ref[...]
    bu_ref[...] = _dot(u.astype(BF16), wb_ref[...])
    lre = lre_ref[...]
    lim = lim_ref[...]

    def step(t, carry):
        sre, sim = carry
        r = pl.multiple_of(t * SUBLANES, SUBLANES)
        bre = bu_ref[pl.ds(r, SUBLANES), 0:nstate]
        bim = bu_ref[pl.ds(r, SUBLANES), nstate:2 * nstate]
        nre = lre * sre - lim * sim + bre
        nim = lre * sim + lim * sre + bim
        bu_ref[pl.ds(r, SUBLANES), 0:nstate] = nre
        bu_ref[pl.ds(r, SUBLANES), nstate:2 * nstate] = nim
        return nre, nim

    sre, sim = lax.fori_loop(0, steps, step, (st_ref[:, 0:nstate], st_ref[:, nstate:2 * nstate]))
    st_ref[:, 0:nstate] = sre
    st_ref[:, nstate:2 * nstate] = sim

    y = _dot(bu_ref[...].astype(BF16), wc_ref[...]) + d_ref[...] * u
    y = jax.nn.gelu(y)
    gate = jax.nn.sigmoid(_dot(y.astype(BF16), wg_ref[...]) + bg_ref[...])
    y_ref[...] = (y * gate).astype(y_ref.dtype)


def _s5(u_tm, wb, lre, lim, wc, d, wg, bg, bsz):
    rows, w = u_tm.shape
    assert bsz == SUBLANES
    blk = SSM_STEPS * bsz
    nstate2 = wb.shape[1]
    const = lambda c: (0, 0)
    return pl.pallas_call(
        functools.partial(_s5_kernel, steps=SSM_STEPS, nstate=nstate2 // 2),
        out_shape=jax.ShapeDtypeStruct((rows, w), BF16),
        grid=(rows // blk,),
        in_specs=[
            pl.BlockSpec((blk, w), lambda c: (c, 0)),
            pl.BlockSpec((w, nstate2), const),
            pl.BlockSpec((bsz, nstate2 // 2), const),
            pl.BlockSpec((bsz, nstate2 // 2), const),
            pl.BlockSpec((nstate2, w), const),
            pl.BlockSpec((1, w), const),
            pl.BlockSpec((w, w), const),
            pl.BlockSpec((1, w), const),
        ],
        out_specs=pl.BlockSpec((blk, w), lambda c: (c, 0)),
        scratch_shapes=[pltpu.VMEM((blk, nstate2), F32), pltpu.VMEM((bsz, nstate2), F32)],
        compiler_params=_cparams(("arbitrary",)),
        name="s5_mixer",
    )(u_tm, wb, lre, lim, wc, d, wg, bg)


def _s5_params(lam_re, lam_im, log_dt, b_re, b_im, c_re, c_im, bsz):
    g, n = lam_re.shape
    dt = jnp.exp(log_dt.astype(F32))[:, None]
    xr = lam_re * dt
    th = lam_im * dt
    er = jnp.exp(xr)
    lbr = er * jnp.cos(th)
    lbi = er * jnp.sin(th)
    ar = jnp.expm1(xr) * jnp.cos(th) - 2.0 * jnp.sin(0.5 * th) ** 2
    ai = lbi
    den = lam_re * lam_re + lam_im * lam_im
    fr = (ar * lam_re + ai * lam_im) / den
    fi = (ai * lam_re - ar * lam_im) / den
    bbr = fr[..., None] * b_re - fi[..., None] * b_im
    bbi = fr[..., None] * b_im + fi[..., None] * b_re
    eye = jnp.eye(g, dtype=F32)
    p = b_re.shape[-1]
    wb_re = jnp.einsum("gnp,gh->gphn", bbr, eye).reshape(g * p, g * n)
    wb_im = jnp.einsum("gnp,gh->gphn", bbi, eye).reshape(g * p, g * n)
    wb = jnp.concatenate([wb_re, wb_im], axis=1).astype(BF16)
    wc_re = jnp.einsum("gpn,gh->gnhp", c_re, eye).reshape(g * n, g * p)
    wc_im = jnp.einsum("gpn,gh->gnhp", c_im, eye).reshape(g * n, g * p)
    wc = jnp.concatenate([wc_re, -wc_im], axis=0).astype(BF16)
    lre = jnp.broadcast_to(lbr.reshape(1, g * n), (bsz, g * n))
    lim = jnp.broadcast_to(lbi.reshape(1, g * n), (bsz, g * n))
    return wb, lre, lim, wc


VA_ROWS = 2 * HEAD_DIM


def _iota2(shape, dim):
    return lax.broadcasted_iota(jnp.int32, shape, dim)


def _stack_masked_q(q_ref, qs_ref, tq, ngroups, width):
    lane = _iota2((1, GROUP_W), 1)
    q = q_ref[...]
    for g in range(ngroups):
        keep = (lane >= g * width) & (lane < (g + 1) * width)
        qs_ref[g * tq:(g + 1) * tq, :] = jnp.where(keep, q, jnp.zeros_like(q))


def _key_scores(k_ref, qs_ref, j, tk):
    row0 = pl.multiple_of(jnp.maximum(j, 0) * tk, tk)
    return _dot_nt(k_ref[pl.ds(row0, tk), :], qs_ref[...])


def _score_prefetch(k_ref, qs_ref, j, dst_ref, tq, tk, bias_of=None, max_ref=None):
    row0 = pl.multiple_of(jnp.maximum(j, 0) * tk, tk)

    def emit(g):
        cols = slice(g * tq, (g + 1) * tq)
        s = _dot_nt(k_ref[pl.ds(row0, tk), :], qs_ref[cols, :])
        if bias_of is not None:
            s = s + bias_of(g)[...]
        dst_ref[:, cols] = s
        if max_ref is not None:
            max_ref[g:g + 1, :] = jnp.max(s, axis=0, keepdims=True)

    return emit


def _paired_key_tiles(i, k_ref, qs_ref, tile_fn, buf_a, buf_b, tq, tk, bias_of=None):
    def body(n, c):
        ja = i - 1 - 2 * n
        tile_fn(ja, buf_a, True, _score_prefetch(k_ref, qs_ref, ja - 1, buf_b[0], tq, tk, bias_of, buf_b[1]))
        tile_fn(ja - 1, buf_b, ja >= 1, _score_prefetch(k_ref, qs_ref, ja - 2, buf_a[0], tq, tk, bias_of, buf_a[1]))
        return c

    lax.fori_loop(0, (i + 1) // 2, body, 0)


def _sb_kernel(q_ref, k_ref, vt_ref, later_ref, o_ref, qs_ref, carry_ref, acc_ref, za_ref, zb_ref,
               *, tq, tk, nheads):
    i = pl.program_id(1)
    _stack_masked_q(q_ref, qs_ref, tq, nheads, HEAD_DIM)

    def tile(j, buf, valid, prefetch, diagonal=False):
        zt = buf[0]
        if diagonal:
            before = _iota2((tk, tq), 0) < _iota2((tk, tq), 1)
        else:
            penalty = jnp.where(valid, 0.0, -NEG_INF)
            keep = jnp.where(valid, 1.0, 0.0)
        jv = jnp.maximum(j, 0)

        def scan_stage(h):
            z = zt[:, h * tq:(h + 1) * tq]
            neg_abs = lax.bitcast_convert_type(
                lax.bitcast_convert_type(z, jnp.uint32) | jnp.uint32(0x80000000), F32)
            sp = jnp.maximum(z, 0.0) + jnp.log2(1.0 + jnp.exp2(neg_abs))
            nl1m = jnp.where(before, sp, 0.0) if diagonal else sp
            inside = _dot(later_ref[...], nl1m.astype(BF16))
            return z - sp, inside, nl1m[0:1, :]

        def value_stage(h, logit, inside, first_row):
            hs = slice(h * HEAD_DIM, (h + 1) * HEAD_DIM)
            between = inside if diagonal else inside + (carry_ref[h:h + 1, :] + penalty)
            w = jnp.exp2(logit - between)
            if diagonal:
                w = jnp.where(before, w, 0.0)
            pv = _dot(vt_ref[jv, hs, :], w.astype(BF16))
            total = inside[0:1, :] + first_row
            if diagonal:
                acc_ref[hs, :] = pv
                carry_ref[h:h + 1, :] = total
            else:
                acc_ref[hs, :] = acc_ref[hs, :] + pv
                carry_ref[h:h + 1, :] = carry_ref[h:h + 1, :] + keep * total

        prefetch(0)
        pending = scan_stage(0)
        for h in range(1, nheads):
            prefetch(h)
            upcoming = scan_stage(h)
            value_stage(h - 1, *pending)
            pending = upcoming
        value_stage(nheads - 1, *pending)

    tile(i, (_key_scores(k_ref, qs_ref, i, tk), None), True,
         _score_prefetch(k_ref, qs_ref, i - 1, za_ref, tq, tk), diagonal=True)
    _paired_key_tiles(i, k_ref, qs_ref, tile, (za_ref, None), (zb_ref, None), tq, tk)
    o_ref[...] = acc_ref[...].T.astype(o_ref.dtype)


def _softmax_tile(buf, tq, off, va, g, first, m_ref, acc_ref):
    st_ref, max_ref = buf
    rows = slice(g * VA_ROWS, (g + 1) * VA_ROWS)
    parts, alphas = [], []
    for c0 in range(0, tq, LANES):
        cs = slice(c0, c0 + LANES)
        sc = slice(g * tq + c0, g * tq + c0 + LANES)
        blk_max = max_ref[g:g + 1, cs]
        if first:
            m_new = blk_max
            shift = m_new
        else:
            m_old = m_ref[g:g + 1, cs]
            m_new = jnp.maximum(m_old, blk_max + off)
            shift = m_new - off
            alphas.append(jnp.exp2(m_old - m_new))
        parts.append(jnp.exp2(st_ref[:, sc] - shift).astype(BF16))
        m_ref[g:g + 1, cs] = m_new
    pv = _dot(va, jnp.concatenate(parts, axis=1))
    if first:
        acc_ref[rows, :] = pv
    else:
        acc_ref[rows, :] = jnp.concatenate(alphas, axis=1) * acc_ref[rows, :] + pv


def _normalised(acc_ref, g):
    base = g * VA_ROWS
    return acc_ref[base:base + HEAD_DIM, :] / acc_ref[base + HEAD_DIM:base + HEAD_DIM + 1, :]


def _ch_kernel(q_ref, k_ref, va_ref, bias_ref, o_ref, qs_ref, m_ref, acc_ref, z_ref, zmax_ref,
               *, tq, tk, nheads, nback):
    i = pl.program_id(1)
    _stack_masked_q(q_ref, qs_ref, tq, nheads, HEAD_DIM)
    buf = (z_ref, zmax_ref)

    def tile(n, first):
        j = i - n
        emit = _score_prefetch(k_ref, qs_ref, j, z_ref, tq, tk, lambda h: bias_ref.at[h, n], zmax_ref)
        for h in range(nheads):
            emit(h)
        for h in range(nheads):
            _softmax_tile(buf, tq, 0.0, va_ref[j, h * VA_ROWS:(h + 1) * VA_ROWS, :], h, first, m_ref, acc_ref)

    tile(0, True)

    def body(n, c):
        tile(n, False)
        return c

    lax.fori_loop(1, jnp.minimum(i, nback) + 1, body, 0)
    out = jnp.concatenate([_normalised(acc_ref, h) for h in range(nheads)], axis=0)
    o_ref[...] = out.T.astype(o_ref.dtype)


def _df_kernel(lam_ref, q_ref, k_ref, va_ref, kb_ref, db_ref, o_ref, qs_ref, m_ref, acc_ref,
               za_ref, zb_ref, maxa_ref, maxb_ref, *, tq, tk, nheads, slopes2):
    i = pl.program_id(1)
    _stack_masked_q(q_ref, qs_ref, tq, 2 * nheads, DF_QK_DIM)
    buf_a, buf_b = (za_ref, maxa_ref), (zb_ref, maxb_ref)

    def past_bias(g):
        return kb_ref.at[g // 2]

    def tile(j, buf, valid, prefetch, diagonal=False):
        jv = jnp.maximum(j, 0)
        for g in range(2 * nheads):
            prefetch(g)
            h = g // 2
            va = va_ref[jv, h * VA_ROWS:(h + 1) * VA_ROWS, :]
            if diagonal:
                _softmax_tile(buf, tq, 0.0, va, g, True, m_ref, acc_ref)
            else:
                off = jnp.where(valid, slopes2[h] * ((j - i) * tk).astype(F32), NEG_INF)
                _softmax_tile(buf, tq, off, va, g, False, m_ref, acc_ref)

    diag = _score_prefetch(k_ref, qs_ref, i, zb_ref, tq, tk, lambda g: db_ref.at[g // 2], maxb_ref)
    for g in range(2 * nheads):
        diag(g)
    tile(i, buf_b, True, _score_prefetch(k_ref, qs_ref, i - 1, za_ref, tq, tk, past_bias, maxa_ref), diagonal=True)
    _paired_key_tiles(i, k_ref, qs_ref, tile, buf_a, buf_b, tq, tk, past_bias)
    lam = lam_ref[0]
    out = jnp.concatenate([_normalised(acc_ref, 2 * h) - lam * _normalised(acc_ref, 2 * h + 1)
                           for h in range(nheads)], axis=0)
    o_ref[...] = out.T.astype(o_ref.dtype)


def _attention_call(kernel, qk, col0, vt, extra, extra_specs, scratch, bsz, seq, name, smem=()):
    tq, tk = TQ_ATT, TK_ATT
    assert tq == tk
    nq = seq // tq
    vrows = vt.shape[2]
    return pl.pallas_call(
        kernel,
        out_shape=jax.ShapeDtypeStruct((bsz * seq, GROUP_W), BF16),
        grid=(bsz, nq),
        in_specs=[pl.BlockSpec(memory_space=pltpu.SMEM) for _ in smem] + [
            pl.BlockSpec((tq, GROUP_W), lambda b, i: (b * nq + i, col0)),
            pl.BlockSpec((seq, GROUP_W), lambda b, i: (b, col0 + 1)),
            pl.BlockSpec((None, seq // tk, vrows, tk), lambda b, i: (b, 0, 0, 0)),
        ] + extra_specs,
        out_specs=pl.BlockSpec((tq, GROUP_W), lambda b, i: (b * nq + i, 0)),
        scratch_shapes=scratch,
        compiler_params=_cparams(("arbitrary", "arbitrary")),
        name=name,
    )(*smem, qk, qk, vt, *extra)


def _ch_bias_tiles(rel_bias, tq, tk):
    assert tq == tk
    nback = CH_LEFT_CHUNKS * CHUNK // tk
    nh = rel_bias.shape[0]
    n = np.arange(nback + 1)[:, None, None]
    s = np.arange(tk)[None, :, None]
    t = np.arange(tq)[None, None, :]
    delta = (n * tk + t) // CHUNK - s // CHUNK
    valid = (delta >= 0) & (delta <= CH_LEFT_CHUNKS)
    table = rel_bias.astype(F32) * LOG2E
    lo, hi = -(tk - 1), nback * tk + tq - 1
    ext = jnp.concatenate([jnp.repeat(table[:, :1], -REL_CLIP - lo, axis=1), table,
                           jnp.repeat(table[:, -1:], hi - REL_CLIP, axis=1)], axis=1)
    period = tk + tq
    tiles = []
    for m in range(nback + 1):
        w = ext[:, m * tk:m * tk + period - 1]
        w = jnp.concatenate([w, jnp.zeros((nh, 1), F32)], axis=1)
        skew = jnp.tile(w, (1, tk))[:, :tk * (period - 1)].reshape(nh, tk, period - 1)
        tiles.append(skew[:, :, tk - 1:tk - 1 + tq])
    bias = jnp.stack(tiles, axis=1)
    return jnp.where(jnp.asarray(valid)[None], bias, NEG_INF), nback


def _df_bias_tiles(slopes2, tq, tk):
    s = jnp.arange(tk, dtype=jnp.int32)[:, None]
    t = jnp.arange(tq, dtype=jnp.int32)[None, :]
    sl = jnp.asarray(slopes2, F32)[:, None, None]
    kb = sl * jnp.broadcast_to(s, (tk, tq)).astype(F32)
    allowed = (s // CHUNK) <= (t // CHUNK)
    db = jnp.where(allowed[None], sl * (t - jnp.abs(t - s)).astype(F32), NEG_INF)
    return kb, db


def _out_proj_kernel(ys_ref, ysb_ref, ych_ref, ydf_ref, g_ref, ind_ref, w_ref, x_ref, o_ref):
    acc = x_ref[...]
    for gi, y_ref in enumerate((ys_ref, ysb_ref, ych_ref, ydf_ref)):
        cs = slice(gi * GROUP_W, (gi + 1) * GROUP_W)
        y = y_ref[...].astype(F32)
        ms = _group_ms(y, ind_ref[...], 1.0 / HEAD_DIM)
        yn = (y * lax.rsqrt(ms + EPS) * g_ref[:, cs]).astype(BF16)
        acc = acc + _dot(yn, w_ref[cs, :])
    o_ref[...] = acc


def _out_proj(y_ssm_tm, y_sb, y_ch, y_df, g, ind64, w, x2, bsz, seq):
    n, dm = x2.shape
    tm = TM_PROJ
    ns = seq // tm
    const = lambda i: (0, 0)
    tile = pl.BlockSpec((tm, GROUP_W), lambda i: (i, 0))
    return pl.pallas_call(
        _out_proj_kernel,
        out_shape=jax.ShapeDtypeStruct((n, dm), F32),
        grid=(n // tm,),
        in_specs=[
            pl.BlockSpec((tm, GROUP_W), lambda i: (i % ns, i // ns)),
            tile, tile, tile,
            pl.BlockSpec((1, dm), const),
            pl.BlockSpec((GROUP_W, GROUP_W), const),
            pl.BlockSpec((dm, dm), const),
            pl.BlockSpec((tm, dm), lambda i: (i, 0)),
        ],
        out_specs=pl.BlockSpec((tm, dm), lambda i: (i, 0)),
        compiler_params=_cparams(("arbitrary",)),
        name="out_proj",
    )(y_ssm_tm, y_sb, y_ch, y_df, g, ind64, w, x2)


def _ffn_kernel(x_ref, g_ref, w1_ref, w3_ref, w2_ref, o_ref, hid_ref, *, tf):
    x = x_ref[...]
    ms = jnp.mean(x * x, axis=-1, keepdims=True)
    h = (x * lax.rsqrt(ms + EPS) * g_ref[...]).astype(BF16)
    dff = w1_ref.shape[1]
    for c in range(dff // tf):
        cs = slice(c * tf, (c + 1) * tf)
        a = _dot(h, w1_ref[:, cs])
        b = _dot(h, w3_ref[:, cs])
        hid_ref[:, cs] = (jax.nn.silu(a) * b).astype(BF16)
    o_ref[...] = x + _dot(hid_ref[...], w2_ref[...])


def _ffn(x2, g, w1, w3, w2):
    n, dm = x2.shape
    dff = w1.shape[1]
    tm = TM_PROJ
    const = lambda i: (0, 0)
    return pl.pallas_call(
        functools.partial(_ffn_kernel, tf=GROUP_W),
        out_shape=jax.ShapeDtypeStruct((n, dm), F32),
        grid=(n // tm,),
        in_specs=[
            pl.BlockSpec((tm, dm), lambda i: (i, 0)),
            pl.BlockSpec((1, dm), const),
            pl.BlockSpec((dm, dff), const),
            pl.BlockSpec((dm, dff), const),
            pl.BlockSpec((dff, dm), const),
        ],
        out_specs=pl.BlockSpec((tm, dm), lambda i: (i, 0)),
        scratch_shapes=[pltpu.VMEM((tm, dff), BF16)],
        compiler_params=_cparams(("arbitrary",)),
        name="ffn_dense",
    )(x2, g, w1, w3, w2)


def _router_kernel(x_ref, g_ref, wr_ref, tri_ref, h_ref, route_ref, cnt_ref, carry_ref):
    @pl.when(pl.program_id(0) == 0)
    def _():
        carry_ref[...] = jnp.zeros_like(carry_ref)

    x = x_ref[...]
    ms = jnp.mean(x * x, axis=-1, keepdims=True)
    h = x * lax.rsqrt(ms + EPS) * g_ref[...]
    h_hi = h.astype(BF16)
    half = h.shape[1] // 2
    bits = lax.bitcast_convert_type(h_hi.astype(F32), jnp.uint32)
    h_ref[...] = bits[:, :half] | lax.shift_right_logical(bits[:, half:], jnp.uint32(16))
    h_lo = (h - h_hi.astype(F32)).astype(BF16)
    logits = _dot(h_hi, wr_ref[0]) + _dot(h_lo, wr_ref[0]) + _dot(h_hi, wr_ref[1])

    lane = _iota2(logits.shape, 1).astype(F32)
    lg = jnp.where(lane < N_EXPERTS, logits, -jnp.inf)
    m1 = jnp.max(lg, axis=-1, keepdims=True)
    e1 = jnp.min(jnp.where(lg == m1, lane, float(LANES)), axis=-1, keepdims=True)
    lg2 = jnp.where(lane == e1, -jnp.inf, lg)
    m2 = jnp.max(lg2, axis=-1, keepdims=True)
    e2 = jnp.min(jnp.where(lg2 == m2, lane, float(LANES)), axis=-1, keepdims=True)
    t = jnp.exp(m2 - m1)
    g1 = 1.0 / (1.0 + t)
    g2 = t / (1.0 + t)

    hot1 = lane == e1
    hot2 = lane == e2
    sel = jnp.where(hot1 | hot2, 1.0, 0.0).astype(BF16)
    prior = _dot(tri_ref[...], sel) + carry_ref[...]
    r1 = jnp.sum(jnp.where(hot1, prior, 0.0), axis=-1, keepdims=True)
    r2 = jnp.sum(jnp.where(hot2, prior, 0.0), axis=-1, keepdims=True)
    carry_ref[...] = carry_ref[...] + jnp.sum(sel.astype(F32), axis=0, keepdims=True)
    cnt_ref[...] = carry_ref[...]

    out = jnp.where(lane == 0, e1, 0.0)
    out = jnp.where(lane == 1, e2, out)
    out = jnp.where(lane == 2, g1, out)
    out = jnp.where(lane == 3, g2, out)
    out = jnp.where(lane == 4, r1, out)
    out = jnp.where(lane == 5, r2, out)
    route_ref[...] = out


def _router(x2, g, wr2, tri):
    n, dm = x2.shape
    tm = TM_PROJ
    const = lambda i: (0, 0)
    return pl.pallas_call(
        _router_kernel,
        out_shape=(jax.ShapeDtypeStruct((n, dm // 2), jnp.uint32),
                   jax.ShapeDtypeStruct((n, LANES), F32),
                   jax.ShapeDtypeStruct((1, LANES), F32)),
        grid=(n // tm,),
        in_specs=[
            pl.BlockSpec((tm, dm), lambda i: (i, 0)),
            pl.BlockSpec((1, dm), const),
            pl.BlockSpec((2, dm, LANES), lambda i: (0, 0, 0)),
            pl.BlockSpec((tm, tm), const),
        ],
        out_specs=(
            pl.BlockSpec((tm, dm // 2), lambda i: (i, 0)),
            pl.BlockSpec((tm, LANES), lambda i: (i, 0)),
            pl.BlockSpec((1, LANES), const),
        ),
        scratch_shapes=[pltpu.VMEM((1, LANES), F32)],
        compiler_params=_cparams(("arbitrary",)),
        name="moe_router",
    )(x2, g, wr2, tri)


def _row_copy(src_ref, src_row, dst_ref, dst_row, sem):
    return pltpu.make_async_copy(src_ref.at[pl.ds(src_row, 1)], dst_ref.at[pl.ds(dst_row, 1)], sem)


def _dispatch_kernel(d1_ref, d2_ref, cnt_ref, h_ref, xs_ref, zero_ref, sem, tail_sem, *, tm, cap, nexp):
    def issue(r, c):
        _row_copy(h_ref, r, xs_ref, d1_ref[0, 0, r], sem).start()
        _row_copy(h_ref, r, xs_ref, d2_ref[0, 0, r], sem).start(priority=1)
        return c

    lax.fori_loop(0, tm, issue, 0, unroll=8)

    def aligned_end(e):
        return ((cnt_ref[e] + (SUBLANES - 1)) // SUBLANES) * SUBLANES

    def tail_copy(e):
        row0 = pl.multiple_of(e * cap + aligned_end(e), SUBLANES)
        return pltpu.make_async_copy(zero_ref, xs_ref.at[pl.ds(row0, tm)], tail_sem)

    def tail_rows(e, fn):
        for k in range(SUBLANES - 1):
            @pl.when(cnt_ref[e] + k < aligned_end(e))
            def _():
                fn(_row_copy(zero_ref, 0, xs_ref, e * cap + cnt_ref[e] + k, tail_sem))

    @pl.when(pl.program_id(0) == pl.num_programs(0) - 1)
    def _():
        zero_ref[...] = jnp.zeros_like(zero_ref)
        for e in range(nexp):
            tail_copy(e).start()
            tail_rows(e, lambda cp: cp.start())
        for e in range(nexp):
            tail_copy(e).wait()
            tail_rows(e, lambda cp: cp.wait())

    for _ in range(2):
        pltpu.make_async_copy(h_ref, xs_ref.at[pl.ds(0, tm)], sem).wait()


def _dispatch(d1, d2, cnt, h, cap):
    n, wd = h.shape
    tm = TM_PROJ
    nt = n // tm
    idx_spec = pl.BlockSpec((1, 1, tm), lambda i: (i, 0, 0), memory_space=pltpu.SMEM)
    return pl.pallas_call(
        functools.partial(_dispatch_kernel, tm=tm, cap=cap, nexp=N_EXPERTS),
        out_shape=jax.ShapeDtypeStruct((N_EXPERTS * cap, wd), h.dtype),
        grid=(nt,),
        in_specs=[idx_spec, idx_spec, pl.BlockSpec(memory_space=pltpu.SMEM),
                  pl.BlockSpec((tm, wd), lambda i: (i, 0))],
        out_specs=pl.BlockSpec(memory_space=pl.ANY),
        scratch_shapes=[pltpu.VMEM((tm, wd), h.dtype), pltpu.SemaphoreType.DMA, pltpu.SemaphoreType.DMA],
        compiler_params=pltpu.CompilerParams(dimension_semantics=("arbitrary",),
                                             vmem_limit_bytes=VMEM_LIMIT_BYTES, disable_bounds_checks=True),
        name="moe_dispatch",
    )(d1.reshape(nt, 1, tm), d2.reshape(nt, 1, tm), cnt, h)


def _combine_kernel(d1_ref, d2_ref, x_ref, route_ref, ys_ref, o_ref, y1_ref, y2_ref, sem, *, tm):
    def issue(r, c):
        _row_copy(ys_ref, d1_ref[0, 0, r], y1_ref, r, sem).start()
        _row_copy(ys_ref, d2_ref[0, 0, r], y2_ref, r, sem).start(priority=1)
        return c

    lax.fori_loop(0, tm, issue, 0, unroll=8)
    for buf in (y1_ref, y2_ref):
        pltpu.make_async_copy(ys_ref.at[pl.ds(0, tm)], buf, sem).wait()
    o_ref[...] = x_ref[...] + route_ref[:, 2:3] * y1_ref[...] + route_ref[:, 3:4] * y2_ref[...]


def _combine(d1, d2, x2, route, ys):
    n, dm = x2.shape
    tm = TM_PROJ
    nt = n // tm
    idx_spec = pl.BlockSpec((1, 1, tm), lambda i: (i, 0, 0), memory_space=pltpu.SMEM)
    return pl.pallas_call(
        functools.partial(_combine_kernel, tm=tm),
        out_shape=jax.ShapeDtypeStruct((n, dm), F32),
        grid=(nt,),
        in_specs=[idx_spec, idx_spec, pl.BlockSpec((tm, dm), lambda i: (i, 0)),
                  pl.BlockSpec((tm, LANES), lambda i: (i, 0)), pl.BlockSpec(memory_space=pl.ANY)],
        out_specs=pl.BlockSpec((tm, dm), lambda i: (i, 0)),
        scratch_shapes=[pltpu.VMEM((tm, dm), F32), pltpu.VMEM((tm, dm), F32), pltpu.SemaphoreType.DMA],
        compiler_params=pltpu.CompilerParams(dimension_semantics=("arbitrary",),
                                             vmem_limit_bytes=VMEM_LIMIT_BYTES, disable_bounds_checks=True),
        name="moe_combine",
    )(d1.reshape(nt, 1, tm), d2.reshape(nt, 1, tm), x2, route, ys)


def _experts_kernel(te_ref, blk_ref, nused_ref, xs_ref, w1_ref, w3_ref, w2_ref, o_ref, x_ref, hid_ref, acc_ref):
    t = pl.program_id(0)
    c = pl.program_id(1)
    nc = pl.num_programs(1)

    @pl.when(jnp.logical_and(t < nused_ref[0], c == 0))
    def _():
        w = xs_ref[...]
        half = w.shape[1]
        x_ref[:, :half] = lax.bitcast_convert_type(w & jnp.uint32(0xFFFF0000), F32).astype(BF16)
        x_ref[:, half:] = lax.bitcast_convert_type(lax.shift_left(w, jnp.uint32(16)), F32).astype(BF16)

    @pl.when(t < nused_ref[0])
    def _():
        x = x_ref[...]
        tf = w1_ref.shape[2]
        for s0 in range(0, tf, GROUP_W):
            cs = slice(s0, s0 + GROUP_W)
            a = _dot(x, w1_ref[0, :, cs])
            b = _dot(x, w3_ref[0, :, cs])
            hid_ref[:, cs] = (jax.nn.silu(a) * b).astype(BF16)
        part = _dot(hid_ref[...], w2_ref[0])

        @pl.when(c == 0)
        def _():
            acc_ref[...] = part

        @pl.when(c > 0)
        def _():
            acc_ref[...] = acc_ref[...] + part

        @pl.when(c == nc - 1)
        def _():
            o_ref[...] = acc_ref[...]

    @pl.when(jnp.logical_and(t >= nused_ref[0], c == nc - 1))
    def _():
        o_ref[...] = jnp.zeros_like(o_ref)


def _experts(tile_expert, tile_block, nused, xs, w1, w3, w2, nt):
    rows, wd = xs.shape
    dm = 2 * wd
    dff = w1.shape[2]
    tm, tf = TM_EXPERT, TF_EXPERT
    nc = dff // tf
    spare = rows // tm

    def cc(t, c, nu):
        return jnp.where(t < nu[0], c, nc - 1)

    grid_spec = pltpu.PrefetchScalarGridSpec(
        num_scalar_prefetch=3,
        grid=(nt, nc),
        in_specs=[
            pl.BlockSpec((tm, wd), lambda t, c, te, tb, nu: (tb[t], 0)),
            pl.BlockSpec((1, dm, tf), lambda t, c, te, tb, nu: (te[t], 0, cc(t, c, nu))),
            pl.BlockSpec((1, dm, tf), lambda t, c, te, tb, nu: (te[t], 0, cc(t, c, nu))),
            pl.BlockSpec((1, tf, dm), lambda t, c, te, tb, nu: (te[t], cc(t, c, nu), 0)),
        ],
        out_specs=pl.BlockSpec((tm, dm), lambda t, c, te, tb, nu: (jnp.where(t < nu[0], tb[t], spare), 0)),
        scratch_shapes=[pltpu.VMEM((tm, dm), BF16), pltpu.VMEM((tm, tf), BF16), pltpu.VMEM((tm, dm), F32)],
    )
    return pl.pallas_call(
        _experts_kernel,
        out_shape=jax.ShapeDtypeStruct((rows + tm, dm), F32),
        grid_spec=grid_spec,
        compiler_params=_cparams(("arbitrary", "arbitrary")),
        name="moe_experts",
    )(tile_expert, tile_block, nused, xs, w1, w3, w2)


def _moe(x2, g, w_router, w1, w3, w2, tri):
    n, dm = x2.shape
    wr = jnp.zeros((dm, LANES), F32).at[:, :N_EXPERTS].set(w_router.astype(F32))
    wr_hi = wr.astype(BF16)
    wr_lo = (wr - wr_hi.astype(F32)).astype(BF16)
    h, route, counts = _router(x2, g, jnp.stack([wr_hi, wr_lo]), tri)

    tm = TM_EXPERT
    assert n % tm == 0 and TM_PROJ == tm
    cap = n + tm
    d1 = route[:, 0].astype(jnp.int32) * cap + route[:, 4].astype(jnp.int32)
    d2 = route[:, 1].astype(jnp.int32) * cap + route[:, 5].astype(jnp.int32)
    cnt = counts[0, :N_EXPERTS].astype(jnp.int32)

    nt = 2 * n // tm + N_EXPERTS
    ntile = (cnt + tm - 1) // tm
    ends = jnp.cumsum(ntile)
    nused = ends[-1]
    t = jnp.minimum(jnp.arange(nt, dtype=jnp.int32), nused - 1)
    tile_expert = jnp.sum(t[:, None] >= ends[None, :], axis=1).astype(jnp.int32)
    tile_block = tile_expert * (cap // tm) + t - (ends - ntile)[tile_expert]

    xs = _dispatch(d1, d2, cnt, h, cap)
    ys = _experts(tile_expert, tile_block.astype(jnp.int32), nused.reshape(1).astype(jnp.int32), xs, w1, w3, w2, nt)
    return _combine(d1, d2, x2, route, ys)


def kernel(x, norm_mix_g, w_in, ssm_lam_re, ssm_lam_im, ssm_log_dt, ssm_b_re, ssm_b_im, ssm_c_re, ssm_c_im,
           ssm_d, ssm_w_glu, ssm_b_glu, ch_q_norm_g, ch_k_norm_g, ch_rel_bias, df_q_norm_g, df_k_norm_g,
           df_lambda, out_norm_g, w_out, norm_ffn_g, ffn_w1, ffn_w3, ffn_w2, moe_router, moe_w1, moe_w3, moe_w2):
    bsz, seq, dm = x.shape
    depth = w_in.shape[0]
    n = bsz * seq
    nheads = GROUP_W // HEAD_DIM
    ind64 = _block_indicator(GROUP_W, HEAD_DIM)
    ind32 = _block_indicator(GROUP_W, DF_QK_DIM)
    tri = (jnp.arange(TM_PROJ)[:, None] > jnp.arange(TM_PROJ)[None, :]).astype(BF16)
    slopes = [2.0 ** (-8.0 * (h + 1) / nheads) for h in range(nheads)]
    slopes2 = tuple(s * LOG2E for s in slopes)
    df_kb, df_db = _df_bias_tiles(slopes2, TQ_ATT, TK_ATT)
    later = (jnp.arange(TK_ATT)[:, None] < jnp.arange(TK_ATT)[None, :]).astype(BF16)

    x2 = x.reshape(n, dm)
    for layer in range(depth):
        qkg = jnp.stack([
            jnp.tile(ch_q_norm_g[layer].astype(F32), nheads) * (HEAD_DIM ** -0.5 * LOG2E),
            jnp.tile(ch_k_norm_g[layer].astype(F32), nheads),
            jnp.tile(df_q_norm_g[layer].astype(F32).reshape(-1), nheads) * (DF_QK_DIM ** -0.5 * LOG2E),
            jnp.tile(df_k_norm_g[layer].astype(F32).reshape(-1), nheads),
        ])
        u_tm, qk, v_sb, v_ch, v_df = _in_proj(x2, norm_mix_g[layer].reshape(1, dm), w_in[layer].astype(BF16), qkg,
                             ind64, ind32, bsz, seq)

        wb, lre, lim, wc = _s5_params(ssm_lam_re[layer], ssm_lam_im[layer], ssm_log_dt[layer],
                                      ssm_b_re[layer], ssm_b_im[layer], ssm_c_re[layer], ssm_c_im[layer], bsz)
        y_ssm = _s5(u_tm.reshape(seq * bsz, GROUP_W), wb, lre, lim, wc, ssm_d[layer].reshape(1, GROUP_W),
                    ssm_w_glu[layer].astype(BF16), ssm_b_glu[layer].reshape(1, GROUP_W), bsz)
        y_ssm = y_ssm.reshape(seq, bsz * GROUP_W)

        tq, tk = TQ_ATT, TK_ATT
        const3 = lambda b, i: (0, 0, 0)
        y_sb = _attention_call(
            functools.partial(_sb_kernel, tq=tq, tk=tk, nheads=nheads),
            qk, 0, v_sb,
            (later,), [pl.BlockSpec((tk, tk), lambda b, i: (0, 0))],
            [pltpu.VMEM((nheads * tq, GROUP_W), BF16), pltpu.VMEM((SUBLANES, tq), F32),
             pltpu.VMEM((GROUP_W, tq), F32),
             pltpu.VMEM((tk, nheads * tq), F32), pltpu.VMEM((tk, nheads * tq), F32)],
            bsz, seq, "sb_attention")

        bias, nback = _ch_bias_tiles(ch_rel_bias[layer], tq, tk)
        y_ch = _attention_call(
            functools.partial(_ch_kernel, tq=tq, tk=tk, nheads=nheads, nback=nback),
            qk, 2, v_ch,
            (bias,), [pl.BlockSpec(bias.shape, lambda b, i: (0, 0, 0, 0))],
            [pltpu.VMEM((nheads * tq, GROUP_W), BF16), pltpu.VMEM((SUBLANES, tq), F32),
             pltpu.VMEM((nheads * VA_ROWS, tq), F32), pltpu.VMEM((tk, nheads * tq), F32),
             pltpu.VMEM((SUBLANES, tq), F32)],
            bsz, seq, "ch_attention")

        lambda_init = 0.8 - 0.6 * math.exp(-0.3 * layer)
        lam_p = df_lambda[layer].astype(F32)
        lam = jnp.exp(jnp.sum(lam_p[0] * lam_p[1])) - jnp.exp(jnp.sum(lam_p[2] * lam_p[3])) + lambda_init
        y_df = _attention_call(
            functools.partial(_df_kernel, tq=tq, tk=tk, nheads=nheads, slopes2=slopes2),
            qk, 4, v_df,
            (df_kb, df_db), [pl.BlockSpec(df_kb.shape, const3), pl.BlockSpec(df_db.shape, const3)],
            [pltpu.VMEM((2 * nheads * tq, GROUP_W), BF16), pltpu.VMEM((2 * nheads, tq), F32),
             pltpu.VMEM((2 * nheads * VA_ROWS, tq), F32),
             pltpu.VMEM((tk, 2 * nheads * tq), F32), pltpu.VMEM((tk, 2 * nheads * tq), F32),
             pltpu.VMEM((2 * nheads, tq), F32), pltpu.VMEM((2 * nheads, tq), F32)],
            bsz, seq, "df_attention", smem=(lam.reshape(1),))

        head_scale = jnp.concatenate([jnp.ones((dm - GROUP_W,), F32),
                                      jnp.full((GROUP_W,), 1.0 - lambda_init, F32)])
        g_out = (out_norm_g[layer].astype(F32) * head_scale).reshape(1, dm)
        x2 = _out_proj(y_ssm, y_sb, y_ch, y_df, g_out, ind64, w_out[layer].astype(BF16), x2, bsz, seq)

        idx = layer // 2
        g_ffn = norm_ffn_g[layer].reshape(1, dm)
        if layer % 2 == 0:
            x2 = _ffn(x2, g_ffn, ffn_w1[idx].astype(BF16), ffn_w3[idx].astype(BF16), ffn_w2[idx].astype(BF16))
        else:
            x2 = _moe(x2, g_ffn, moe_router[idx], moe_w1[idx].astype(BF16), moe_w3[idx].astype(BF16),
                      moe_w2[idx].astype(BF16), tri)
    return x2.reshape(bsz, seq, dm)
```

```python
import functools
import math

import jax
import jax.numpy as jnp
import numpy as np
from jax import lax
from jax.experimental import pallas as pl
from jax.experimental.pallas import tpu as pltpu

F32 = jnp.float32
BF16 = jnp.bfloat16

EPS = 1e-6
NEG_INF = -1e30
HEAD_DIM = 64
GROUP_W = 256
CHUNK = 64
SSM_GROUP = 16
SSM_STATE = 64
DF_QK_DIM = 32
CH_LEFT_CHUNKS = 8
REL_CLIP = 128
N_EXPERTS = 8
LOG2E = 1.4426950408889634

VMEM_LIMIT_BYTES = 56 * 1024 * 1024
SUBLANES = 8
LANES = 128

TM_PROJ = 512
TQ_ATT = 256
TK_ATT = 256
SSM_STEPS = 64
TM_EXPERT = 512
TF_EXPERT = 1792


def _cparams(sem):
    return pltpu.CompilerParams(dimension_semantics=sem, vmem_limit_bytes=VMEM_LIMIT_BYTES)


def _dot(a, b):
    return jnp.dot(a, b, preferred_element_type=F32)


def _dot_nt(a, b):
    return lax.dot_general(a, b, (((1,), (1,)), ((), ())), preferred_element_type=F32)


def _split_dot(v, m):
    hi = v.astype(BF16)
    lo = (v - hi.astype(F32)).astype(BF16)
    return _dot(hi, m) + _dot(lo, m)


def _group_ms(v, ind, inv_size):
    return _split_dot(v * v, ind) * inv_size


def _block_indicator(width, group):
    r = jnp.arange(width)[:, None] // group
    c = jnp.arange(width)[None, :] // group
    return (r == c).astype(BF16)


def _in_proj_kernel(x_ref, g_ref, w_ref, qkg_ref, ind64_ref, ind32_ref, u_ref, qk_ref, vsb_ref, vch_ref, vdf_ref,
                    *, tk, nheads):
    x = x_ref[...]
    ms = jnp.mean(x * x, axis=-1, keepdims=True)
    h = (x * lax.rsqrt(ms + EPS) * g_ref[...]).astype(BF16)

    def cols(c):
        return _dot(h, w_ref[:, c * GROUP_W:(c + 1) * GROUP_W])

    def put(slot, val):
        qk_ref[:, slot * GROUP_W:(slot + 1) * GROUP_W] = val.astype(BF16)

    def put_values(v_ref, v, ones_row):
        for kt in range(v.shape[0] // tk):
            vt = v[kt * tk:(kt + 1) * tk, :].T
            if not ones_row:
                v_ref[kt] = vt.astype(BF16)
                continue
            pad = jnp.where(_iota2((VA_ROWS - HEAD_DIM, tk), 0) == 0, 1.0, 0.0).astype(BF16)
            for hd in range(nheads):
                v_ref[kt, hd * VA_ROWS:hd * VA_ROWS + HEAD_DIM, :] = vt[hd * HEAD_DIM:(hd + 1) * HEAD_DIM, :].astype(BF16)
                v_ref[kt, hd * VA_ROWS + HEAD_DIM:(hd + 1) * VA_ROWS, :] = pad

    u_ref[...] = cols(0)
    put(0, cols(1) * (HEAD_DIM ** -0.5 * LOG2E))
    put(1, cols(2))
    put_values(vsb_ref, cols(3), False)
    for c, gi in ((4, 0), (5, 1)):
        a = cols(c)
        ms_h = _group_ms(a, ind64_ref[...], 1.0 / HEAD_DIM)
        put(c - 2, a * lax.rsqrt(ms_h + EPS) * qkg_ref[gi:gi + 1, :])
    put_values(vch_ref, cols(6), True)
    for c, gi in ((7, 2), (8, 3)):
        a = cols(c)
        ms_h = _group_ms(a, ind32_ref[...], 1.0 / DF_QK_DIM)
        put(c - 3, a * lax.rsqrt(ms_h + EPS) * qkg_ref[gi:gi + 1, :])
    put_values(vdf_ref, cols(9), True)


def _in_proj(x2, g, w, qkg, ind64, ind32, bsz, seq):
    n, dm = x2.shape
    tm, tk = TM_PROJ, TK_ATT
    ns = seq // tm
    nheads = GROUP_W // HEAD_DIM
    const = lambda i: (0, 0)

    def values(rows):
        return (jax.ShapeDtypeStruct((bsz, seq // tk, rows, tk), BF16),
                pl.BlockSpec((None, tm // tk, rows, tk), lambda i: (i // ns, i % ns, 0, 0)))

    v_shapes, v_specs = zip(values(GROUP_W), values(nheads * VA_ROWS), values(nheads * VA_ROWS))
    return pl.pallas_call(
        functools.partial(_in_proj_kernel, tk=tk, nheads=nheads),
        out_shape=(jax.ShapeDtypeStruct((seq, bsz * GROUP_W), F32),
                   jax.ShapeDtypeStruct((n, 6 * GROUP_W), BF16)) + v_shapes,
        grid=(n // tm,),
        in_specs=[
            pl.BlockSpec((tm, dm), lambda i: (i, 0)),
            pl.BlockSpec((1, dm), const),
            pl.BlockSpec(w.shape, const),
            pl.BlockSpec((4, GROUP_W), const),
            pl.BlockSpec((GROUP_W, GROUP_W), const),
            pl.BlockSpec((GROUP_W, GROUP_W), const),
        ],
        out_specs=(
            pl.BlockSpec((tm, GROUP_W), lambda i: (i % ns, i // ns)),
            pl.BlockSpec((tm, 6 * GROUP_W), lambda i: (i, 0)),
        ) + v_specs,
        compiler_params=_cparams(("arbitrary",)),
        name="in_proj",
    )(x2, g, w, qkg, ind64, ind32)


def _s5_kernel(u_ref, wb_ref, lre_ref, lim_ref, wc_ref, d_ref, wg_ref, bg_ref, y_ref, bu_ref, st_ref,
               *, steps, nstate):
    @pl.when(pl.program_id(0) == 0)
    def _():
        st_ref[...] = jnp.zeros_like(st_ref)

    u = u_ref[...]
    bu_ref[...] = _dot(u.astype(BF16), wb_ref[...])
    lre = lre_ref[...]
    lim = lim_ref[...]

    def step(t, carry):
        sre, sim = carry
        r = pl.multiple_of(t * SUBLANES, SUBLANES)
        bre = bu_ref[pl.ds(r, SUBLANES), 0:nstate]
        bim = bu_ref[pl.ds(r, SUBLANES), nstate:2 * nstate]
        nre = lre * sre - lim * sim + bre
        nim = lre * sim + lim * sre + bim
        bu_ref[pl.ds(r, SUBLANES), 0:nstate] = nre
        bu_ref[pl.ds(r, SUBLANES), nstate:2 * nstate] = nim
        return nre, nim

    sre, sim = lax.fori_loop(0, steps, step, (st_ref[:, 0:nstate], st_ref[:, nstate:2 * nstate]))
    st_ref[:, 0:nstate] = sre
    st_ref[:, nstate:2 * nstate] = sim

    y = _dot(bu_ref[...].astype(BF16), wc_ref[...]) + d_ref[...] * u
    y = jax.nn.gelu(y)
    gate = jax.nn.sigmoid(_dot(y.astype(BF16), wg_ref[...]) + bg_ref[...])
    y_ref[...] = (y * gate).astype(y_ref.dtype)


def _s5(u_tm, wb, lre, lim, wc, d, wg, bg, bsz):
    rows, w = u_tm.shape
    assert bsz == SUBLANES
    blk = SSM_STEPS * bsz
    nstate2 = wb.shape[1]
    const = lambda c: (0, 0)
    return pl.pallas_call(
        functools.partial(_s5_kernel, steps=SSM_STEPS, nstate=nstate2 // 2),
        out_shape=jax.ShapeDtypeStruct((rows, w), BF16),
        grid=(rows // blk,),
        in_specs=[
            pl.BlockSpec((blk, w), lambda c: (c, 0)),
            pl.BlockSpec((w, nstate2), const),
            pl.BlockSpec((bsz, nstate2 // 2), const),
            pl.BlockSpec((bsz, nstate2 // 2), const),
            pl.BlockSpec((nstate2, w), const),
            pl.BlockSpec((1, w), const),
            pl.BlockSpec((w, w), const),
            pl.BlockSpec((1, w), const),
        ],
        out_specs=pl.BlockSpec((blk, w), lambda c: (c, 0)),
        scratch_shapes=[pltpu.VMEM((blk, nstate2), F32), pltpu.VMEM((bsz, nstate2), F32)],
        compiler_params=_cparams(("arbitrary",)),
        name="s5_mixer",
    )(u_tm, wb, lre, lim, wc, d, wg, bg)


def _s5_params(lam_re, lam_im, log_dt, b_re, b_im, c_re, c_im, bsz):
    g, n = lam_re.shape
    dt = jnp.exp(log_dt.astype(F32))[:, None]
    xr = lam_re * dt
    th = lam_im * dt
    er = jnp.exp(xr)
    lbr = er * jnp.cos(th)
    lbi = er * jnp.sin(th)
    ar = jnp.expm1(xr) * jnp.cos(th) - 2.0 * jnp.sin(0.5 * th) ** 2
    ai = lbi
    den = lam_re * lam_re + lam_im * lam_im
    fr = (ar * lam_re + ai * lam_im) / den
    fi = (ai * lam_re - ar * lam_im) / den
    bbr = fr[..., None] * b_re - fi[..., None] * b_im
    bbi = fr[..., None] * b_im + fi[..., None] * b_re
    eye = jnp.eye(g, dtype=F32)
    p = b_re.shape[-1]
    wb_re = jnp.einsum("gnp,gh->gphn", bbr, eye).reshape(g * p, g * n)
    wb_im = jnp.einsum("gnp,gh->gphn", bbi, eye).reshape(g * p, g * n)
    wb = jnp.concatenate([wb_re, wb_im], axis=1).astype(BF16)
    wc_re = jnp.einsum("gpn,gh->gnhp", c_re, eye).reshape(g * n, g * p)
    wc_im = jnp.einsum("gpn,gh->gnhp", c_im, eye).reshape(g * n, g * p)
    wc = jnp.concatenate([wc_re, -wc_im], axis=0).astype(BF16)
    lre = jnp.broadcast_to(lbr.reshape(1, g * n), (bsz, g * n))
    lim = jnp.broadcast_to(lbi.reshape(1, g * n), (bsz, g * n))
    return wb, lre, lim, wc


VA_ROWS = 2 * HEAD_DIM


def _iota2(shape, dim):
    return lax.broadcasted_iota(jnp.int32, shape, dim)


def _stack_masked_q(q_ref, qs_ref, tq, ngroups, width):
    lane = _iota2((1, GROUP_W), 1)
    q = q_ref[...]
    for g in range(ngroups):
        keep = (lane >= g * width) & (lane < (g + 1) * width)
        qs_ref[g * tq:(g + 1) * tq, :] = jnp.where(keep, q, jnp.zeros_like(q))


def _key_scores(k_ref, qs_ref, j, tk):
    row0 = pl.multiple_of(jnp.maximum(j, 0) * tk, tk)
    return _dot_nt(k_ref[pl.ds(row0, tk), :], qs_ref[...])


def _score_prefetch(k_ref, qs_ref, j, dst_ref, tq, tk, bias_of=None, max_ref=None):
    row0 = pl.multiple_of(jnp.maximum(j, 0) * tk, tk)

    def emit(g):
        cols = slice(g * tq, (g + 1) * tq)
        s = _dot_nt(k_ref[pl.ds(row0, tk), :], qs_ref[cols, :])
        if bias_of is not None:
            s = s + bias_of(g)[...]
        dst_ref[:, cols] = s
        if max_ref is not None:
            max_ref[g:g + 1, :] = jnp.max(s, axis=0, keepdims=True)

    return emit


def _paired_key_tiles(i, k_ref, qs_ref, tile_fn, buf_a, buf_b, tq, tk, bias_of=None):
    def body(n, c):
        ja = i - 1 - 2 * n
        tile_fn(ja, buf_a, _score_prefetch(k_ref, qs_ref, ja - 1, buf_b[0], tq, tk, bias_of, buf_b[1]))
        tile_fn(ja - 1, buf_b, _score_prefetch(k_ref, qs_ref, ja - 2, buf_a[0], tq, tk, bias_of, buf_a[1]))
        return c

    lax.fori_loop(0, i // 2, body, 0)

    @pl.when(jnp.bitwise_and(i, 1) == 1)
    def _():
        tile_fn(0, buf_a, lambda g: None)


def _sb_kernel(q_ref, k_ref, vt_ref, later_ref, o_ref, qs_ref, carry_ref, acc_ref, za_ref, zb_ref,
               *, tq, tk, nheads):
    i = pl.program_id(1)
    _stack_masked_q(q_ref, qs_ref, tq, nheads, HEAD_DIM)

    def tile(j, buf, prefetch, diagonal=False):
        zt = buf[0]
        if diagonal:
            before = _iota2((tk, tq), 0) < _iota2((tk, tq), 1)

        def scan_stage(h):
            z = zt[:, h * tq:(h + 1) * tq]
            neg_abs = lax.bitcast_convert_type(
                lax.bitcast_convert_type(z, jnp.uint32) | jnp.uint32(0x80000000), F32)
            sp = jnp.maximum(z, 0.0) + jnp.log2(1.0 + jnp.exp2(neg_abs))
            nl1m = jnp.where(before, sp, 0.0) if diagonal else sp
            inside = _dot(later_ref[...], nl1m.astype(BF16))
            return z - sp, inside, nl1m[0:1, :]

        def value_stage(h, logit, inside, first_row):
            hs = slice(h * HEAD_DIM, (h + 1) * HEAD_DIM)
            between = inside if diagonal else inside + carry_ref[h:h + 1, :]
            w = jnp.exp2(logit - between)
            if diagonal:
                w = jnp.where(before, w, 0.0)
            pv = _dot(vt_ref[j, hs, :], w.astype(BF16))
            total = inside[0:1, :] + first_row
            if diagonal:
                acc_ref[hs, :] = pv
                carry_ref[h:h + 1, :] = total
            else:
                acc_ref[hs, :] = acc_ref[hs, :] + pv
                carry_ref[h:h + 1, :] = carry_ref[h:h + 1, :] + total

        prefetch(0)
        pending = scan_stage(0)
        for h in range(1, nheads):
            prefetch(h)
            upcoming = scan_stage(h)
            value_stage(h - 1, *pending)
            pending = upcoming
        value_stage(nheads - 1, *pending)

    tile(i, (_key_scores(k_ref, qs_ref, i, tk), None), _score_prefetch(k_ref, qs_ref, i - 1, za_ref, tq, tk),
         diagonal=True)
    _paired_key_tiles(i, k_ref, qs_ref, tile, (za_ref, None), (zb_ref, None), tq, tk)
    o_ref[...] = acc_ref[...].T.astype(o_ref.dtype)


def _softmax_tile(buf, tq, off, va, g, first, m_ref, acc_ref):
    st_ref, max_ref = buf
    rows = slice(g * VA_ROWS, (g + 1) * VA_ROWS)
    parts, alphas = [], []
    for c0 in range(0, tq, LANES):
        cs = slice(c0, c0 + LANES)
        sc = slice(g * tq + c0, g * tq + c0 + LANES)
        blk_max = max_ref[g:g + 1, cs]
        if first:
            m_new = blk_max
            shift = m_new
        else:
            m_old = m_ref[g:g + 1, cs]
            m_new = jnp.maximum(m_old, blk_max + off)
            shift = m_new - off
            alphas.append(jnp.exp2(m_old - m_new))
        parts.append(jnp.exp2(st_ref[:, sc] - shift).astype(BF16))
        m_ref[g:g + 1, cs] = m_new
    pv = _dot(va, jnp.concatenate(parts, axis=1))
    if first:
        acc_ref[rows, :] = pv
    else:
        acc_ref[rows, :] = jnp.concatenate(alphas, axis=1) * acc_ref[rows, :] + pv


def _normalised(acc_ref, g):
    base = g * VA_ROWS
    return acc_ref[base:base + HEAD_DIM, :] / acc_ref[base + HEAD_DIM:base + HEAD_DIM + 1, :]


def _ch_kernel(q_ref, k_ref, va_ref, bias_ref, o_ref, qs_ref, z_ref, *, tq, tk, nheads, nback):
    i = pl.program_id(1)
    _stack_masked_q(q_ref, qs_ref, tq, nheads, HEAD_DIM)

    def attend(ntiles):
        tile_max = [[] for _ in range(nheads)]
        for n in range(ntiles):
            row0 = pl.multiple_of((i - n) * tk, tk)
            for h in range(nheads):
                cols = slice(h * tq, (h + 1) * tq)
                s = _dot_nt(k_ref[pl.ds(row0, tk), :], qs_ref[cols, :]) + bias_ref[h, n]
                z_ref[n * tk:(n + 1) * tk, cols] = s
                tile_max[h].append(jnp.max(s, axis=0, keepdims=True))
        outs = []
        for h in range(nheads):
            m = functools.reduce(jnp.maximum, tile_max[h])
            acc = None
            for n in range(ntiles):
                p = jnp.exp2(z_ref[n * tk:(n + 1) * tk, h * tq:(h + 1) * tq] - m).astype(BF16)
                pv = _dot(va_ref[i - n, h * VA_ROWS:(h + 1) * VA_ROWS, :], p)
                acc = pv if acc is None else acc + pv
            outs.append(acc[:HEAD_DIM, :] / acc[HEAD_DIM:HEAD_DIM + 1, :])
        o_ref[...] = jnp.concatenate(outs, axis=0).T.astype(o_ref.dtype)

    for ntiles in range(1, nback + 2):
        pl.when(jnp.minimum(i, nback) + 1 == ntiles)(functools.partial(attend, ntiles))


def _df_kernel(lam_ref, q_ref, k_ref, va_ref, kb_ref, db_ref, o_ref, qs_ref, m_ref, acc_ref,
               za_ref, zb_ref, maxa_ref, maxb_ref, *, tq, tk, nheads, slopes2):
    i = pl.program_id(1)
    _stack_masked_q(q_ref, qs_ref, tq, 2 * nheads, DF_QK_DIM)
    buf_a, buf_b = (za_ref, maxa_ref), (zb_ref, maxb_ref)

    def past_bias(g):
        return kb_ref.at[g // 2]

    def tile(j, buf, prefetch, diagonal=False):
        for g in range(2 * nheads):
            prefetch(g)
            h = g // 2
            va = va_ref[j, h * VA_ROWS:(h + 1) * VA_ROWS, :]
            if diagonal:
                _softmax_tile(buf, tq, 0.0, va, g, True, m_ref, acc_ref)
            else:
                _softmax_tile(buf, tq, slopes2[h] * ((j - i) * tk).astype(F32), va, g, False, m_ref, acc_ref)

    diag = _score_prefetch(k_ref, qs_ref, i, zb_ref, tq, tk, lambda g: db_ref.at[g // 2], maxb_ref)
    for g in range(2 * nheads):
        diag(g)
    tile(i, buf_b, _score_prefetch(k_ref, qs_ref, i - 1, za_ref, tq, tk, past_bias, maxa_ref), diagonal=True)
    _paired_key_tiles(i, k_ref, qs_ref, tile, buf_a, buf_b, tq, tk, past_bias)
    lam = lam_ref[0]
    out = jnp.concatenate([_normalised(acc_ref, 2 * h) - lam * _normalised(acc_ref, 2 * h + 1)
                           for h in range(nheads)], axis=0)
    o_ref[...] = out.T.astype(o_ref.dtype)


def _attention_call(kernel, qk, col0, vt, extra, extra_specs, scratch, bsz, seq, name, smem=()):
    tq, tk = TQ_ATT, TK_ATT
    assert tq == tk
    nq = seq // tq
    vrows = vt.shape[2]
    return pl.pallas_call(
        kernel,
        out_shape=jax.ShapeDtypeStruct((bsz * seq, GROUP_W), BF16),
        grid=(bsz, nq),
        in_specs=[pl.BlockSpec(memory_space=pltpu.SMEM) for _ in smem] + [
            pl.BlockSpec((tq, GROUP_W), lambda b, i: (b * nq + i, col0)),
            pl.BlockSpec((seq, GROUP_W), lambda b, i: (b, col0 + 1)),
            pl.BlockSpec((None, seq // tk, vrows, tk), lambda b, i: (b, 0, 0, 0)),
        ] + extra_specs,
        out_specs=pl.BlockSpec((tq, GROUP_W), lambda b, i: (b * nq + i, 0)),
        scratch_shapes=scratch,
        compiler_params=_cparams(("arbitrary", "arbitrary")),
        name=name,
    )(*smem, qk, qk, vt, *extra)


def _ch_bias_tiles(rel_bias, tq, tk):
    assert tq == tk
    nback = CH_LEFT_CHUNKS * CHUNK // tk
    nh = rel_bias.shape[0]
    n = np.arange(nback + 1)[:, None, None]
    s = np.arange(tk)[None, :, None]
    t = np.arange(tq)[None, None, :]
    delta = (n * tk + t) // CHUNK - s // CHUNK
    valid = (delta >= 0) & (delta <= CH_LEFT_CHUNKS)
    table = rel_bias.astype(F32) * LOG2E
    lo, hi = -(tk - 1), nback * tk + tq - 1
    ext = jnp.concatenate([jnp.repeat(table[:, :1], -REL_CLIP - lo, axis=1), table,
                           jnp.repeat(table[:, -1:], hi - REL_CLIP, axis=1)], axis=1)
    period = tk + tq
    tiles = []
    for m in range(nback + 1):
        w = ext[:, m * tk:m * tk + period - 1]
        w = jnp.concatenate([w, jnp.zeros((nh, 1), F32)], axis=1)
        skew = jnp.tile(w, (1, tk))[:, :tk * (period - 1)].reshape(nh, tk, period - 1)
        tiles.append(skew[:, :, tk - 1:tk - 1 + tq])
    bias = jnp.stack(tiles, axis=1)
    return jnp.where(jnp.asarray(valid)[None], bias, NEG_INF), nback


def _df_bias_tiles(slopes2, tq, tk):
    s = jnp.arange(tk, dtype=jnp.int32)[:, None]
    t = jnp.arange(tq, dtype=jnp.int32)[None, :]
    sl = jnp.asarray(slopes2, F32)[:, None, None]
    kb = sl * jnp.broadcast_to(s, (tk, tq)).astype(F32)
    allowed = (s // CHUNK) <= (t // CHUNK)
    db = jnp.where(allowed[None], sl * (t - jnp.abs(t - s)).astype(F32), NEG_INF)
    return kb, db


def _out_proj_kernel(ys_ref, ysb_ref, ych_ref, ydf_ref, g_ref, ind_ref, w_ref, x_ref, o_ref):
    acc = x_ref[...]
    for gi, y_ref in enumerate((ys_ref, ysb_ref, ych_ref, ydf_ref)):
        cs = slice(gi * GROUP_W, (gi + 1) * GROUP_W)
        y = y_ref[...].astype(F32)
        ms = _group_ms(y, ind_ref[...], 1.0 / HEAD_DIM)
        yn = (y * lax.rsqrt(ms + EPS) * g_ref[:, cs]).astype(BF16)
        acc = acc + _dot(yn, w_ref[cs, :])
    o_ref[...] = acc


def _out_proj(y_ssm_tm, y_sb, y_ch, y_df, g, ind64, w, x2, bsz, seq):
    n, dm = x2.shape
    tm = TM_PROJ
    ns = seq // tm
    const = lambda i: (0, 0)
    tile = pl.BlockSpec((tm, GROUP_W), lambda i: (i, 0))
    return pl.pallas_call(
        _out_proj_kernel,
        out_shape=jax.ShapeDtypeStruct((n, dm), F32),
        grid=(n // tm,),
        in_specs=[
            pl.BlockSpec((tm, GROUP_W), lambda i: (i % ns, i // ns)),
            tile, tile, tile,
            pl.BlockSpec((1, dm), const),
            pl.BlockSpec((GROUP_W, GROUP_W), const),
            pl.BlockSpec((dm, dm), const),
            pl.BlockSpec((tm, dm), lambda i: (i, 0)),
        ],
        out_specs=pl.BlockSpec((tm, dm), lambda i: (i, 0)),
        compiler_params=_cparams(("arbitrary",)),
        name="out_proj",
    )(y_ssm_tm, y_sb, y_ch, y_df, g, ind64, w, x2)


def _ffn_kernel(x_ref, g_ref, w1_ref, w3_ref, w2_ref, o_ref, hid_ref, *, tf):
    x = x_ref[...]
    ms = jnp.mean(x * x, axis=-1, keepdims=True)
    h = (x * lax.rsqrt(ms + EPS) * g_ref[...]).astype(BF16)
    dff = w1_ref.shape[1]
    for c in range(dff // tf):
        cs = slice(c * tf, (c + 1) * tf)
        a = _dot(h, w1_ref[:, cs])
        b = _dot(h, w3_ref[:, cs])
        hid_ref[:, cs] = (jax.nn.silu(a) * b).astype(BF16)
    o_ref[...] = x + _dot(hid_ref[...], w2_ref[...])


def _ffn(x2, g, w1, w3, w2):
    n, dm = x2.shape
    dff = w1.shape[1]
    tm = TM_PROJ
    const = lambda i: (0, 0)
    return pl.pallas_call(
        functools.partial(_ffn_kernel, tf=GROUP_W),
        out_shape=jax.ShapeDtypeStruct((n, dm), F32),
        grid=(n // tm,),
        in_specs=[
            pl.BlockSpec((tm, dm), lambda i: (i, 0)),
            pl.BlockSpec((1, dm), const),
            pl.BlockSpec((dm, dff), const),
            pl.BlockSpec((dm, dff), const),
            pl.BlockSpec((dff, dm), const),
        ],
        out_specs=pl.BlockSpec((tm, dm), lambda i: (i, 0)),
        scratch_shapes=[pltpu.VMEM((tm, dff), BF16)],
        compiler_params=_cparams(("arbitrary",)),
        name="ffn_dense",
    )(x2, g, w1, w3, w2)


def _router_kernel(x_ref, g_ref, wr_ref, tri_ref, h_ref, route_ref, cnt_ref, carry_ref):
    @pl.when(pl.program_id(0) == 0)
    def _():
        carry_ref[...] = jnp.zeros_like(carry_ref)

    x = x_ref[...]
    ms = jnp.mean(x * x, axis=-1, keepdims=True)
    h = x * lax.rsqrt(ms + EPS) * g_ref[...]
    h_hi = h.astype(BF16)
    half = h.shape[1] // 2
    bits = lax.bitcast_convert_type(h_hi.astype(F32), jnp.uint32)
    h_ref[...] = bits[:, :half] | lax.shift_right_logical(bits[:, half:], jnp.uint32(16))
    h_lo = (h - h_hi.astype(F32)).astype(BF16)
    logits = _dot(h_hi, wr_ref[0]) + _dot(h_lo, wr_ref[0]) + _dot(h_hi, wr_ref[1])

    lane = _iota2(logits.shape, 1).astype(F32)
    lg = jnp.where(lane < N_EXPERTS, logits, -jnp.inf)
    m1 = jnp.max(lg, axis=-1, keepdims=True)
    e1 = jnp.min(jnp.where(lg == m1, lane, float(LANES)), axis=-1, keepdims=True)
    lg2 = jnp.where(lane == e1, -jnp.inf, lg)
    m2 = jnp.max(lg2, axis=-1, keepdims=True)
    e2 = jnp.min(jnp.where(lg2 == m2, lane, float(LANES)), axis=-1, keepdims=True)
    t = jnp.exp(m2 - m1)
    g1 = 1.0 / (1.0 + t)
    g2 = t / (1.0 + t)

    hot1 = lane == e1
    hot2 = lane == e2
    sel = jnp.where(hot1 | hot2, 1.0, 0.0).astype(BF16)
    prior = _dot(tri_ref[...], sel) + carry_ref[...]
    r1 = jnp.sum(jnp.where(hot1, prior, 0.0), axis=-1, keepdims=True)
    r2 = jnp.sum(jnp.where(hot2, prior, 0.0), axis=-1, keepdims=True)
    carry_ref[...] = carry_ref[...] + jnp.sum(sel.astype(F32), axis=0, keepdims=True)
    cnt_ref[...] = carry_ref[...]

    out = jnp.where(lane == 0, e1, 0.0)
    out = jnp.where(lane == 1, e2, out)
    out = jnp.where(lane == 2, g1, out)
    out = jnp.where(lane == 3, g2, out)
    out = jnp.where(lane == 4, r1, out)
    out = jnp.where(lane == 5, r2, out)
    route_ref[...] = out


def _router(x2, g, wr2, tri):
    n, dm = x2.shape
    tm = TM_PROJ
    const = lambda i: (0, 0)
    return pl.pallas_call(
        _router_kernel,
        out_shape=(jax.ShapeDtypeStruct((n, dm // 2), jnp.uint32),
                   jax.ShapeDtypeStruct((n, LANES), F32),
                   jax.ShapeDtypeStruct((1, LANES), F32)),
        grid=(n // tm,),
        in_specs=[
            pl.BlockSpec((tm, dm), lambda i: (i, 0)),
            pl.BlockSpec((1, dm), const),
            pl.BlockSpec((2, dm, LANES), lambda i: (0, 0, 0)),
            pl.BlockSpec((tm, tm), const),
        ],
        out_specs=(
            pl.BlockSpec((tm, dm // 2), lambda i: (i, 0)),
            pl.BlockSpec((tm, LANES), lambda i: (i, 0)),
            pl.BlockSpec((1, LANES), const),
        ),
        scratch_shapes=[pltpu.VMEM((1, LANES), F32)],
        compiler_params=_cparams(("arbitrary",)),
        name="moe_router",
    )(x2, g, wr2, tri)


def _row_copy(src_ref, src_row, dst_ref, dst_row, sem):
    return pltpu.make_async_copy(src_ref.at[pl.ds(src_row, 1)], dst_ref.at[pl.ds(dst_row, 1)], sem)


def _dispatch_kernel(d1_ref, d2_ref, cnt_ref, h_ref, xs_ref, zero_ref, sem, tail_sem, *, tm, cap, nexp):
    def issue(r, c):
        _row_copy(h_ref, r, xs_ref, d1_ref[0, 0, r], sem).start()
        _row_copy(h_ref, r, xs_ref, d2_ref[0, 0, r], sem).start(priority=1)
        return c

    lax.fori_loop(0, tm, issue, 0, unroll=8)

    def aligned_end(e):
        return ((cnt_ref[e] + (SUBLANES - 1)) // SUBLANES) * SUBLANES

    def tail_copy(e):
        row0 = pl.multiple_of(e * cap + aligned_end(e), SUBLANES)
        return pltpu.make_async_copy(zero_ref, xs_ref.at[pl.ds(row0, tm)], tail_sem)

    def tail_rows(e, fn):
        for k in range(SUBLANES - 1):
            @pl.when(cnt_ref[e] + k < aligned_end(e))
            def _():
                fn(_row_copy(zero_ref, 0, xs_ref, e * cap + cnt_ref[e] + k, tail_sem))

    @pl.when(pl.program_id(0) == pl.num_programs(0) - 1)
    def _():
        zero_ref[...] = jnp.zeros_like(zero_ref)
        for e in range(nexp):
            tail_copy(e).start()
            tail_rows(e, lambda cp: cp.start())
        for e in range(nexp):
            tail_copy(e).wait()
            tail_rows(e, lambda cp: cp.wait())

    for _ in range(2):
        pltpu.make_async_copy(h_ref, xs_ref.at[pl.ds(0, tm)], sem).wait()


def _dispatch(d1, d2, cnt, h, cap):
    n, wd = h.shape
    tm = TM_PROJ
    nt = n // tm
    idx_spec = pl.BlockSpec((1, 1, tm), lambda i: (i, 0, 0), memory_space=pltpu.SMEM)
    return pl.pallas_call(
        functools.partial(_dispatch_kernel, tm=tm, cap=cap, nexp=N_EXPERTS),
        out_shape=jax.ShapeDtypeStruct((N_EXPERTS * cap, wd), h.dtype),
        grid=(nt,),
        in_specs=[idx_spec, idx_spec, pl.BlockSpec(memory_space=pltpu.SMEM),
                  pl.BlockSpec((tm, wd), lambda i: (i, 0))],
        out_specs=pl.BlockSpec(memory_space=pl.ANY),
        scratch_shapes=[pltpu.VMEM((tm, wd), h.dtype), pltpu.SemaphoreType.DMA, pltpu.SemaphoreType.DMA],
        compiler_params=pltpu.CompilerParams(dimension_semantics=("arbitrary",),
                                             vmem_limit_bytes=VMEM_LIMIT_BYTES, disable_bounds_checks=True),
        name="moe_dispatch",
    )(d1.reshape(nt, 1, tm), d2.reshape(nt, 1, tm), cnt, h)


def _combine_kernel(d1_ref, d2_ref, x_ref, route_ref, ys_ref, o_ref, y1_ref, y2_ref, sem, *, tm):
    def issue(r, c):
        _row_copy(ys_ref, d1_ref[0, 0, r], y1_ref, r, sem).start()
        _row_copy(ys_ref, d2_ref[0, 0, r], y2_ref, r, sem).start(priority=1)
        return c

    lax.fori_loop(0, tm, issue, 0, unroll=8)
    for buf in (y1_ref, y2_ref):
        pltpu.make_async_copy(ys_ref.at[pl.ds(0, tm)], buf, sem).wait()
    o_ref[...] = x_ref[...] + route_ref[:, 2:3] * y1_ref[...] + route_ref[:, 3:4] * y2_ref[...]


def _combine(d1, d2, x2, route, ys):
    n, dm = x2.shape
    tm = TM_PROJ
    nt = n // tm
    idx_spec = pl.BlockSpec((1, 1, tm), lambda i: (i, 0, 0), memory_space=pltpu.SMEM)
    return pl.pallas_call(
        functools.partial(_combine_kernel, tm=tm),
        out_shape=jax.ShapeDtypeStruct((n, dm), F32),
        grid=(nt,),
        in_specs=[idx_spec, idx_spec, pl.BlockSpec((tm, dm), lambda i: (i, 0)),
                  pl.BlockSpec((tm, LANES), lambda i: (i, 0)), pl.BlockSpec(memory_space=pl.ANY)],
        out_specs=pl.BlockSpec((tm, dm), lambda i: (i, 0)),
        scratch_shapes=[pltpu.VMEM((tm, dm), F32), pltpu.VMEM((tm, dm), F32), pltpu.SemaphoreType.DMA],
        compiler_params=pltpu.CompilerParams(dimension_semantics=("arbitrary",),
                                             vmem_limit_bytes=VMEM_LIMIT_BYTES, disable_bounds_checks=True),
        name="moe_combine",
    )(d1.reshape(nt, 1, tm), d2.reshape(nt, 1, tm), x2, route, ys)


def _experts_kernel(te_ref, blk_ref, nused_ref, xs_ref, w1_ref, w3_ref, w2_ref, o_ref, x_ref, hid_ref, acc_ref):
    t = pl.program_id(0)
    c = pl.program_id(1)
    nc = pl.num_programs(1)

    @pl.when(jnp.logical_and(t < nused_ref[0], c == 0))
    def _():
        w = xs_ref[...]
        half = w.shape[1]
        x_ref[:, :half] = lax.bitcast_convert_type(w & jnp.uint32(0xFFFF0000), F32).astype(BF16)
        x_ref[:, half:] = lax.bitcast_convert_type(lax.shift_left(w, jnp.uint32(16)), F32).astype(BF16)

    @pl.when(t < nused_ref[0])
    def _():
        x = x_ref[...]
        tf = w1_ref.shape[2]
        for s0 in range(0, tf, GROUP_W):
            cs = slice(s0, s0 + GROUP_W)
            a = _dot(x, w1_ref[0, :, cs])
            b = _dot(x, w3_ref[0, :, cs])
            hid_ref[:, cs] = (jax.nn.silu(a) * b).astype(BF16)
        part = _dot(hid_ref[...], w2_ref[0])

        @pl.when(c == 0)
        def _():
            acc_ref[...] = part

        @pl.when(c > 0)
        def _():
            acc_ref[...] = acc_ref[...] + part

        @pl.when(c == nc - 1)
        def _():
            o_ref[...] = acc_ref[...]

    @pl.when(jnp.logical_and(t >= nused_ref[0], c == nc - 1))
    def _():
        o_ref[...] = jnp.zeros_like(o_ref)


def _experts(tile_expert, tile_block, nused, xs, w1, w3, w2, nt):
    rows, wd = xs.shape
    dm = 2 * wd
    dff = w1.shape[2]
    tm, tf = TM_EXPERT, TF_EXPERT
    nc = dff // tf
    spare = rows // tm

    def cc(t, c, nu):
        return jnp.where(t < nu[0], c, nc - 1)

    grid_spec = pltpu.PrefetchScalarGridSpec(
        num_scalar_prefetch=3,
        grid=(nt, nc),
        in_specs=[
            pl.BlockSpec((tm, wd), lambda t, c, te, tb, nu: (tb[t], 0)),
            pl.BlockSpec((1, dm, tf), lambda t, c, te, tb, nu: (te[t], 0, cc(t, c, nu))),
            pl.BlockSpec((1, dm, tf), lambda t, c, te, tb, nu: (te[t], 0, cc(t, c, nu))),
            pl.BlockSpec((1, tf, dm), lambda t, c, te, tb, nu: (te[t], cc(t, c, nu), 0)),
        ],
        out_specs=pl.BlockSpec((tm, dm), lambda t, c, te, tb, nu: (jnp.where(t < nu[0], tb[t], spare), 0)),
        scratch_shapes=[pltpu.VMEM((tm, dm), BF16), pltpu.VMEM((tm, tf), BF16), pltpu.VMEM((tm, dm), F32)],
    )
    return pl.pallas_call(
        _experts_kernel,
        out_shape=jax.ShapeDtypeStruct((rows + tm, dm), F32),
        grid_spec=grid_spec,
        compiler_params=_cparams(("arbitrary", "arbitrary")),
        name="moe_experts",
    )(tile_expert, tile_block, nused, xs, w1, w3, w2)


def _moe(x2, g, w_router, w1, w3, w2, tri):
    n, dm = x2.shape
    wr = jnp.zeros((dm, LANES), F32).at[:, :N_EXPERTS].set(w_router.astype(F32))
    wr_hi = wr.astype(BF16)
    wr_lo = (wr - wr_hi.astype(F32)).astype(BF16)
    h, route, counts = _router(x2, g, jnp.stack([wr_hi, wr_lo]), tri)

    tm = TM_EXPERT
    assert n % tm == 0 and TM_PROJ == tm
    cap = n + tm
    d1 = route[:, 0].astype(jnp.int32) * cap + route[:, 4].astype(jnp.int32)
    d2 = route[:, 1].astype(jnp.int32) * cap + route[:, 5].astype(jnp.int32)
    cnt = counts[0, :N_EXPERTS].astype(jnp.int32)

    nt = 2 * n // tm + N_EXPERTS
    ntile = (cnt + tm - 1) // tm
    ends = jnp.cumsum(ntile)
    nused = ends[-1]
    t = jnp.minimum(jnp.arange(nt, dtype=jnp.int32), nused - 1)
    tile_expert = jnp.sum(t[:, None] >= ends[None, :], axis=1).astype(jnp.int32)
    tile_block = tile_expert * (cap // tm) + t - (ends - ntile)[tile_expert]

    xs = _dispatch(d1, d2, cnt, h, cap)
    ys = _experts(tile_expert, tile_block.astype(jnp.int32), nused.reshape(1).astype(jnp.int32), xs, w1, w3, w2, nt)
    return _combine(d1, d2, x2, route, ys)


def kernel(x, norm_mix_g, w_in, ssm_lam_re, ssm_lam_im, ssm_log_dt, ssm_b_re, ssm_b_im, ssm_c_re, ssm_c_im,
           ssm_d, ssm_w_glu, ssm_b_glu, ch_q_norm_g, ch_k_norm_g, ch_rel_bias, df_q_norm_g, df_k_norm_g,
           df_lambda, out_norm_g, w_out, norm_ffn_g, ffn_w1, ffn_w3, ffn_w2, moe_router, moe_w1, moe_w3, moe_w2):
    bsz, seq, dm = x.shape
    depth = w_in.shape[0]
    n = bsz * seq
    nheads = GROUP_W // HEAD_DIM
    ind64 = _block_indicator(GROUP_W, HEAD_DIM)
    ind32 = _block_indicator(GROUP_W, DF_QK_DIM)
    tri = (jnp.arange(TM_PROJ)[:, None] > jnp.arange(TM_PROJ)[None, :]).astype(BF16)
    slopes = [2.0 ** (-8.0 * (h + 1) / nheads) for h in range(nheads)]
    slopes2 = tuple(s * LOG2E for s in slopes)
    df_kb, df_db = _df_bias_tiles(slopes2, TQ_ATT, TK_ATT)
    later = (jnp.arange(TK_ATT)[:, None] < jnp.arange(TK_ATT)[None, :]).astype(BF16)

    x2 = x.reshape(n, dm)
    for layer in range(depth):
        qkg = jnp.stack([
            jnp.tile(ch_q_norm_g[layer].astype(F32), nheads) * (HEAD_DIM ** -0.5 * LOG2E),
            jnp.tile(ch_k_norm_g[layer].astype(F32), nheads),
            jnp.tile(df_q_norm_g[layer].astype(F32).reshape(-1), nheads) * (DF_QK_DIM ** -0.5 * LOG2E),
            jnp.tile(df_k_norm_g[layer].astype(F32).reshape(-1), nheads),
        ])
        u_tm, qk, v_sb, v_ch, v_df = _in_proj(x2, norm_mix_g[layer].reshape(1, dm), w_in[layer].astype(BF16), qkg,
                             ind64, ind32, bsz, seq)

        wb, lre, lim, wc = _s5_params(ssm_lam_re[layer], ssm_lam_im[layer], ssm_log_dt[layer],
                                      ssm_b_re[layer], ssm_b_im[layer], ssm_c_re[layer], ssm_c_im[layer], bsz)
        y_ssm = _s5(u_tm.reshape(seq * bsz, GROUP_W), wb, lre, lim, wc, ssm_d[layer].reshape(1, GROUP_W),
                    ssm_w_glu[layer].astype(BF16), ssm_b_glu[layer].reshape(1, GROUP_W), bsz)
        y_ssm = y_ssm.reshape(seq, bsz * GROUP_W)

        tq, tk = TQ_ATT, TK_ATT
        const3 = lambda b, i: (0, 0, 0)
        y_sb = _attention_call(
            functools.partial(_sb_kernel, tq=tq, tk=tk, nheads=nheads),
            qk, 0, v_sb,
            (later,), [pl.BlockSpec((tk, tk), lambda b, i: (0, 0))],
            [pltpu.VMEM((nheads * tq, GROUP_W), BF16), pltpu.VMEM((SUBLANES, tq), F32),
             pltpu.VMEM((GROUP_W, tq), F32),
             pltpu.VMEM((tk, nheads * tq), F32), pltpu.VMEM((tk, nheads * tq), F32)],
            bsz, seq, "sb_attention")

        bias, nback = _ch_bias_tiles(ch_rel_bias[layer], tq, tk)
        y_ch = _attention_call(
            functools.partial(_ch_kernel, tq=tq, tk=tk, nheads=nheads, nback=nback),
            qk, 2, v_ch,
            (bias,), [pl.BlockSpec(bias.shape, lambda b, i: (0, 0, 0, 0))],
            [pltpu.VMEM((nheads * tq, GROUP_W), BF16), pltpu.VMEM(((nback + 1) * tk, nheads * tq), F32)],
            bsz, seq, "ch_attention")

        lambda_init = 0.8 - 0.6 * math.exp(-0.3 * layer)
        lam_p = df_lambda[layer].astype(F32)
        lam = jnp.exp(jnp.sum(lam_p[0] * lam_p[1])) - jnp.exp(jnp.sum(lam_p[2] * lam_p[3])) + lambda_init
        y_df = _attention_call(
            functools.partial(_df_kernel, tq=tq, tk=tk, nheads=nheads, slopes2=slopes2),
            qk, 4, v_df,
            (df_kb, df_db), [pl.BlockSpec(df_kb.shape, const3), pl.BlockSpec(df_db.shape, const3)],
            [pltpu.VMEM((2 * nheads * tq, GROUP_W), BF16), pltpu.VMEM((2 * nheads, tq), F32),
             pltpu.VMEM((2 * nheads * VA_ROWS, tq), F32),
             pltpu.VMEM((tk, 2 * nheads * tq), F32), pltpu.VMEM((tk, 2 * nheads * tq), F32),
             pltpu.VMEM((2 * nheads, tq), F32), pltpu.VMEM((2 * nheads, tq), F32)],
            bsz, seq, "df_attention", smem=(lam.reshape(1),))

        head_scale = jnp.concatenate([jnp.ones((dm - GROUP_W,), F32),
                                      jnp.full((GROUP_W,), 1.0 - lambda_init, F32)])
        g_out = (out_norm_g[layer].astype(F32) * head_scale).reshape(1, dm)
        x2 = _out_proj(y_ssm, y_sb, y_ch, y_df, g_out, ind64, w_out[layer].astype(BF16), x2, bsz, seq)

        idx = layer // 2
        g_ffn = norm_ffn_g[layer].reshape(1, dm)
        if layer % 2 == 0:
            x2 = _ffn(x2, g_ffn, ffn_w1[idx].astype(BF16), ffn_w3[idx].astype(BF16), ffn_w2[idx].astype(BF16))
        else:
            x2 = _moe(x2, g_ffn, moe_router[idx], moe_w1[idx].astype(BF16), moe_w3[idx].astype(BF16),
                      moe_w2[idx].astype(BF16), tri)
    return x2.reshape(bsz, seq, dm)
```

```python
import functools
import math

import jax
import jax.numpy as jnp
import numpy as np
from jax import lax
from jax.experimental import pallas as pl
from jax.experimental.pallas import tpu as pltpu

F32 = jnp.float32
BF16 = jnp.bfloat16

EPS = 1e-6
NEG_INF = -1e30
HEAD_DIM = 64
GROUP_W = 256
CHUNK = 64
SSM_GROUP = 16
SSM_STATE = 64
DF_QK_DIM = 32
CH_LEFT_CHUNKS = 8
REL_CLIP = 128
N_EXPERTS = 8
LOG2E = 1.4426950408889634

VMEM_LIMIT_BYTES = 56 * 1024 * 1024
SUBLANES = 8
LANES = 128

TM_PROJ = 512
TQ_ATT = 256
TK_ATT = 256
SSM_STEPS = 64
TM_EXPERT = 512
TF_EXPERT = 1792


def _cparams(sem):
    return pltpu.CompilerParams(dimension_semantics=sem, vmem_limit_bytes=VMEM_LIMIT_BYTES)


def _dot(a, b):
    return jnp.dot(a, b, preferred_element_type=F32)


def _dot_nt(a, b):
    return lax.dot_general(a, b, (((1,), (1,)), ((), ())), preferred_element_type=F32)


def _group_ms(v, ind, inv_size):
    return _dot((v * v).astype(BF16), ind) * inv_size


def _block_indicator(width, group):
    r = jnp.arange(width)[:, None] // group
    c = jnp.arange(width)[None, :] // group
    return (r == c).astype(BF16)


def _in_proj_kernel(x_ref, g_ref, w_ref, qkg_ref, ind64_ref, ind32_ref, u_ref, qk_ref, vsb_ref, vch_ref, vdf_ref,
                    *, tk, nheads):
    x = x_ref[...]
    ms = jnp.mean(x * x, axis=-1, keepdims=True)
    h = (x * lax.rsqrt(ms + EPS) * g_ref[...]).astype(BF16)

    def cols(c):
        return _dot(h, w_ref[:, c * GROUP_W:(c + 1) * GROUP_W])

    def put(slot, val):
        qk_ref[:, slot * GROUP_W:(slot + 1) * GROUP_W] = val.astype(BF16)

    def put_values(v_ref, v, ones_row):
        for kt in range(v.shape[0] // tk):
            vt = v[kt * tk:(kt + 1) * tk, :].T
            if not ones_row:
                v_ref[kt] = vt.astype(BF16)
                continue
            pad = jnp.where(_iota2((VA_ROWS - HEAD_DIM, tk), 0) == 0, 1.0, 0.0).astype(BF16)
            for hd in range(nheads):
                v_ref[kt, hd * VA_ROWS:hd * VA_ROWS + HEAD_DIM, :] = vt[hd * HEAD_DIM:(hd + 1) * HEAD_DIM, :].astype(BF16)
                v_ref[kt, hd * VA_ROWS + HEAD_DIM:(hd + 1) * VA_ROWS, :] = pad

    u_ref[...] = cols(0)
    put(0, cols(1) * (HEAD_DIM ** -0.5 * LOG2E))
    put(1, cols(2))
    put_values(vsb_ref, cols(3), False)
    for c, gi in ((4, 0), (5, 1)):
        a = cols(c)
        ms_h = _group_ms(a, ind64_ref[...], 1.0 / HEAD_DIM)
        put(c - 2, a * lax.rsqrt(ms_h + EPS) * qkg_ref[gi:gi + 1, :])
    put_values(vch_ref, cols(6), True)
    for c, gi in ((7, 2), (8, 3)):
        a = cols(c)
        ms_h = _group_ms(a, ind32_ref[...], 1.0 / DF_QK_DIM)
        put(c - 3, a * lax.rsqrt(ms_h + EPS) * qkg_ref[gi:gi + 1, :])
    put_values(vdf_ref, cols(9), True)


def _in_proj(x2, g, w, qkg, ind64, ind32, bsz, seq):
    n, dm = x2.shape
    tm, tk = TM_PROJ, TK_ATT
    ns = seq // tm
    nheads = GROUP_W // HEAD_DIM
    const = lambda i: (0, 0)

    def values(rows):
        return (jax.ShapeDtypeStruct((bsz, seq // tk, rows, tk), BF16),
                pl.BlockSpec((None, tm // tk, rows, tk), lambda i: (i // ns, i % ns, 0, 0)))

    v_shapes, v_specs = zip(values(GROUP_W), values(nheads * VA_ROWS), values(nheads * VA_ROWS))
    return pl.pallas_call(
        functools.partial(_in_proj_kernel, tk=tk, nheads=nheads),
        out_shape=(jax.ShapeDtypeStruct((seq, bsz * GROUP_W), F32),
                   jax.ShapeDtypeStruct((n, 6 * GROUP_W), BF16)) + v_shapes,
        grid=(n // tm,),
        in_specs=[
            pl.BlockSpec((tm, dm), lambda i: (i, 0)),
            pl.BlockSpec((1, dm), const),
            pl.BlockSpec(w.shape, const),
            pl.BlockSpec((4, GROUP_W), const),
            pl.BlockSpec((GROUP_W, GROUP_W), const),
            pl.BlockSpec((GROUP_W, GROUP_W), const),
        ],
        out_specs=(
            pl.BlockSpec((tm, GROUP_W), lambda i: (i % ns, i // ns)),
            pl.BlockSpec((tm, 6 * GROUP_W), lambda i: (i, 0)),
        ) + v_specs,
        compiler_params=_cparams(("arbitrary",)),
        name="in_proj",
    )(x2, g, w, qkg, ind64, ind32)


def _s5_kernel(u_ref, wb_ref, lre_ref, lim_ref, wc_ref, d_ref, wg_ref, bg_ref, y_ref, bu_ref, st_ref,
               *, steps, nstate):
    @pl.when(pl.program_id(0) == 0)
    def _():
        st_ref[...] = jnp.zeros_like(st_ref)

    u = u_ref[...]
    bu_ref[...] = _dot(u.astype(BF16), wb_ref[...])
    lre = lre_ref[...]
    lim = lim_ref[...]

    def step(t, carry):
        sre, sim = carry
        r = pl.multiple_of(t * SUBLANES, SUBLANES)
        bre = bu_ref[pl.ds(r, SUBLANES), 0:nstate]
        bim = bu_ref[pl.ds(r, SUBLANES), nstate:2 * nstate]
        nre = lre * sre - lim * sim + bre
        nim = lre * sim + lim * sre + bim
        bu_ref[pl.ds(r, SUBLANES), 0:nstate] = nre
        bu_ref[pl.ds(r, SUBLANES), nstate:2 * nstate] = nim
        return nre, nim

    sre, sim = lax.fori_loop(0, steps, step, (st_ref[:, 0:nstate], st_ref[:, nstate:2 * nstate]))
    st_ref[:, 0:nstate] = sre
    st_ref[:, nstate:2 * nstate] = sim

    y = _dot(bu_ref[...].astype(BF16), wc_ref[...]) + d_ref[...] * u
    y = jax.nn.gelu(y)
    gate = jax.nn.sigmoid(_dot(y.astype(BF16), wg_ref[...]) + bg_ref[...])
    y_ref[...] = (y * gate).astype(y_ref.dtype)


def _s5(u_tm, wb, lre, lim, wc, d, wg, bg, bsz):
    rows, w = u_tm.shape
    assert bsz == SUBLANES
    blk = SSM_STEPS * bsz
    nstate2 = wb.shape[1]
    const = lambda c: (0, 0)
    return pl.pallas_call(
        functools.partial(_s5_kernel, steps=SSM_STEPS, nstate=nstate2 // 2),
        out_shape=jax.ShapeDtypeStruct((rows, w), BF16),
        grid=(rows // blk,),
        in_specs=[
            pl.BlockSpec((blk, w), lambda c: (c, 0)),
            pl.BlockSpec((w, nstate2), const),
            pl.BlockSpec((bsz, nstate2 // 2), const),
            pl.BlockSpec((bsz, nstate2 // 2), const),
            pl.BlockSpec((nstate2, w), const),
            pl.BlockSpec((1, w), const),
            pl.BlockSpec((w, w), const),
            pl.BlockSpec((1, w), const),
        ],
        out_specs=pl.BlockSpec((blk, w), lambda c: (c, 0)),
        scratch_shapes=[pltpu.VMEM((blk, nstate2), F32), pltpu.VMEM((bsz, nstate2), F32)],
        compiler_params=_cparams(("arbitrary",)),
        name="s5_mixer",
    )(u_tm, wb, lre, lim, wc, d, wg, bg)


def _s5_params(lam_re, lam_im, log_dt, b_re, b_im, c_re, c_im, bsz):
    g, n = lam_re.shape
    dt = jnp.exp(log_dt.astype(F32))[:, None]
    xr = lam_re * dt
    th = lam_im * dt
    er = jnp.exp(xr)
    lbr = er * jnp.cos(th)
    lbi = er * jnp.sin(th)
    ar = jnp.expm1(xr) * jnp.cos(th) - 2.0 * jnp.sin(0.5 * th) ** 2
    ai = lbi
    den = lam_re * lam_re + lam_im * lam_im
    fr = (ar * lam_re + ai * lam_im) / den
    fi = (ai * lam_re - ar * lam_im) / den
    bbr = fr[..., None] * b_re - fi[..., None] * b_im
    bbi = fr[..., None] * b_im + fi[..., None] * b_re
    eye = jnp.eye(g, dtype=F32)
    p = b_re.shape[-1]
    wb_re = jnp.einsum("gnp,gh->gphn", bbr, eye).reshape(g * p, g * n)
    wb_im = jnp.einsum("gnp,gh->gphn", bbi, eye).reshape(g * p, g * n)
    wb = jnp.concatenate([wb_re, wb_im], axis=1).astype(BF16)
    wc_re = jnp.einsum("gpn,gh->gnhp", c_re, eye).reshape(g * n, g * p)
    wc_im = jnp.einsum("gpn,gh->gnhp", c_im, eye).reshape(g * n, g * p)
    wc = jnp.concatenate([wc_re, -wc_im], axis=0).astype(BF16)
    lre = jnp.broadcast_to(lbr.reshape(1, g * n), (bsz, g * n))
    lim = jnp.broadcast_to(lbi.reshape(1, g * n), (bsz, g * n))
    return wb, lre, lim, wc


VA_ROWS = 2 * HEAD_DIM


def _iota2(shape, dim):
    return lax.broadcasted_iota(jnp.int32, shape, dim)


def _stack_masked_q(q_ref, qs_ref, tq, ngroups, width):
    lane = _iota2((1, GROUP_W), 1)
    q = q_ref[...]
    for g in range(ngroups):
        keep = (lane >= g * width) & (lane < (g + 1) * width)
        qs_ref[g * tq:(g + 1) * tq, :] = jnp.where(keep, q, jnp.zeros_like(q))


def _key_scores(k_ref, qs_ref, j, tk):
    row0 = pl.multiple_of(jnp.maximum(j, 0) * tk, tk)
    return _dot_nt(k_ref[pl.ds(row0, tk), :], qs_ref[...])


def _score_prefetch(k_ref, qs_ref, j, dst_ref, tq, tk, bias_of=None, max_ref=None):
    row0 = pl.multiple_of(jnp.maximum(j, 0) * tk, tk)

    def emit(g):
        cols = slice(g * tq, (g + 1) * tq)
        s = _dot_nt(k_ref[pl.ds(row0, tk), :], qs_ref[cols, :])
        if bias_of is not None:
            s = s + bias_of(g)[...]
        dst_ref[:, cols] = s
        if max_ref is not None:
            max_ref[g:g + 1, :] = jnp.max(s, axis=0, keepdims=True)

    return emit


def _paired_key_tiles(i, k_ref, qs_ref, tile_fn, buf_a, buf_b, tq, tk, bias_of=None):
    def body(n, c):
        ja = i - 1 - 2 * n
        tile_fn(ja, buf_a, _score_prefetch(k_ref, qs_ref, ja - 1, buf_b[0], tq, tk, bias_of, buf_b[1]))
        tile_fn(ja - 1, buf_b, _score_prefetch(k_ref, qs_ref, ja - 2, buf_a[0], tq, tk, bias_of, buf_a[1]))
        return c

    lax.fori_loop(0, i // 2, body, 0)

    @pl.when(jnp.bitwise_and(i, 1) == 1)
    def _():
        tile_fn(0, buf_a, lambda g: None)


def _sb_kernel(q_ref, k_ref, vt_ref, later_ref, o_ref, qs_ref, carry_ref, acc_ref, za_ref, zb_ref,
               *, tq, tk, nheads):
    i = pl.program_id(1)
    _stack_masked_q(q_ref, qs_ref, tq, nheads, HEAD_DIM)

    def tile(j, buf, prefetch, diagonal=False):
        zt = buf[0]
        if diagonal:
            before = _iota2((tk, tq), 0) < _iota2((tk, tq), 1)

        def scan_stage(h):
            z = zt[:, h * tq:(h + 1) * tq]
            neg_abs = lax.bitcast_convert_type(
                lax.bitcast_convert_type(z, jnp.uint32) | jnp.uint32(0x80000000), F32)
            sp = jnp.maximum(z, 0.0) + jnp.log2(1.0 + jnp.exp2(neg_abs))
            nl1m = jnp.where(before, sp, 0.0) if diagonal else sp
            inside = _dot(later_ref[...], nl1m.astype(BF16))
            return z - sp, inside, nl1m[0:1, :]

        def value_stage(h, logit, inside, first_row):
            hs = slice(h * HEAD_DIM, (h + 1) * HEAD_DIM)
            between = inside if diagonal else inside + carry_ref[h:h + 1, :]
            w = jnp.exp2(logit - between)
            if diagonal:
                w = jnp.where(before, w, 0.0)
            pv = _dot(vt_ref[j, hs, :], w.astype(BF16))
            total = inside[0:1, :] + first_row
            if diagonal:
                acc_ref[hs, :] = pv
                carry_ref[h:h + 1, :] = total
            else:
                acc_ref[hs, :] = acc_ref[hs, :] + pv
                carry_ref[h:h + 1, :] = carry_ref[h:h + 1, :] + total

        prefetch(0)
        pending = scan_stage(0)
        for h in range(1, nheads):
            prefetch(h)
            upcoming = scan_stage(h)
            value_stage(h - 1, *pending)
            pending = upcoming
        value_stage(nheads - 1, *pending)

    tile(i, (_key_scores(k_ref, qs_ref, i, tk), None), _score_prefetch(k_ref, qs_ref, i - 1, za_ref, tq, tk),
         diagonal=True)
    _paired_key_tiles(i, k_ref, qs_ref, tile, (za_ref, None), (zb_ref, None), tq, tk)
    o_ref[...] = acc_ref[...].T.astype(o_ref.dtype)


def _softmax_tile(buf, tq, off, va, g, first, m_ref, acc_ref):
    st_ref, max_ref = buf
    rows = slice(g * VA_ROWS, (g + 1) * VA_ROWS)
    parts, alphas = [], []
    for c0 in range(0, tq, LANES):
        cs = slice(c0, c0 + LANES)
        sc = slice(g * tq + c0, g * tq + c0 + LANES)
        blk_max = max_ref[g:g + 1, cs]
        if first:
            m_new = blk_max
            shift = m_new
        else:
            m_old = m_ref[g:g + 1, cs]
            m_new = jnp.maximum(m_old, blk_max + off)
            shift = m_new - off
            alphas.append(jnp.exp2(m_old - m_new))
        parts.append(jnp.exp2(st_ref[:, sc] - shift).astype(BF16))
        m_ref[g:g + 1, cs] = m_new
    pv = _dot(va, jnp.concatenate(parts, axis=1))
    if first:
        acc_ref[rows, :] = pv
    else:
        acc_ref[rows, :] = jnp.concatenate(alphas, axis=1) * acc_ref[rows, :] + pv


def _normalised(acc_ref, g):
    base = g * VA_ROWS
    return acc_ref[base:base + HEAD_DIM, :] / acc_ref[base + HEAD_DIM:base + HEAD_DIM + 1, :]


def _ch_kernel(q_ref, k_ref, va_ref, bias_ref, o_ref, qs_ref, z_ref, *, tq, tk, nheads, nback):
    i = pl.program_id(1)
    _stack_masked_q(q_ref, qs_ref, tq, nheads, HEAD_DIM)

    def attend(ntiles):
        tile_max = [[] for _ in range(nheads)]
        for n in range(ntiles):
            row0 = pl.multiple_of((i - n) * tk, tk)
            for h in range(nheads):
                cols = slice(h * tq, (h + 1) * tq)
                s = _dot_nt(k_ref[pl.ds(row0, tk), :], qs_ref[cols, :]) + bias_ref[h, n]
                z_ref[n * tk:(n + 1) * tk, cols] = s
                tile_max[h].append(jnp.max(s, axis=0, keepdims=True))
        outs = []
        for h in range(nheads):
            m = functools.reduce(jnp.maximum, tile_max[h])
            acc = None
            for n in range(ntiles):
                p = jnp.exp2(z_ref[n * tk:(n + 1) * tk, h * tq:(h + 1) * tq] - m).astype(BF16)
                pv = _dot(va_ref[i - n, h * VA_ROWS:(h + 1) * VA_ROWS, :], p)
                acc = pv if acc is None else acc + pv
            outs.append(acc[:HEAD_DIM, :] / acc[HEAD_DIM:HEAD_DIM + 1, :])
        o_ref[...] = jnp.concatenate(outs, axis=0).T.astype(o_ref.dtype)

    for ntiles in range(1, nback + 2):
        pl.when(jnp.minimum(i, nback) + 1 == ntiles)(functools.partial(attend, ntiles))


def _df_kernel(lam_ref, q_ref, k_ref, va_ref, kb_ref, db_ref, o_ref, qs_ref, m_ref, acc_ref,
               za_ref, zb_ref, maxa_ref, maxb_ref, *, tq, tk, nheads, slopes2):
    i = pl.program_id(1)
    _stack_masked_q(q_ref, qs_ref, tq, 2 * nheads, DF_QK_DIM)
    buf_a, buf_b = (za_ref, maxa_ref), (zb_ref, maxb_ref)

    def past_bias(g):
        return kb_ref.at[g // 2]

    def tile(j, buf, prefetch, diagonal=False):
        for g in range(2 * nheads):
            prefetch(g)
            h = g // 2
            va = va_ref[j, h * VA_ROWS:(h + 1) * VA_ROWS, :]
            if diagonal:
                _softmax_tile(buf, tq, 0.0, va, g, True, m_ref, acc_ref)
            else:
                _softmax_tile(buf, tq, slopes2[h] * ((j - i) * tk).astype(F32), va, g, False, m_ref, acc_ref)

    diag = _score_prefetch(k_ref, qs_ref, i, zb_ref, tq, tk, lambda g: db_ref.at[g // 2], maxb_ref)
    for g in range(2 * nheads):
        diag(g)
    tile(i, buf_b, _score_prefetch(k_ref, qs_ref, i - 1, za_ref, tq, tk, past_bias, maxa_ref), diagonal=True)
    _paired_key_tiles(i, k_ref, qs_ref, tile, buf_a, buf_b, tq, tk, past_bias)
    lam = lam_ref[0]
    out = jnp.concatenate([_normalised(acc_ref, 2 * h) - lam * _normalised(acc_ref, 2 * h + 1)
                           for h in range(nheads)], axis=0)
    o_ref[...] = out.T.astype(o_ref.dtype)


def _attention_call(kernel, qk, col0, vt, extra, extra_specs, scratch, bsz, seq, name, smem=()):
    tq, tk = TQ_ATT, TK_ATT
    assert tq == tk
    nq = seq // tq
    vrows = vt.shape[2]
    return pl.pallas_call(
        kernel,
        out_shape=jax.ShapeDtypeStruct((bsz * seq, GROUP_W), BF16),
        grid=(bsz, nq),
        in_specs=[pl.BlockSpec(memory_space=pltpu.SMEM) for _ in smem] + [
            pl.BlockSpec((tq, GROUP_W), lambda b, i: (b * nq + i, col0)),
            pl.BlockSpec((seq, GROUP_W), lambda b, i: (b, col0 + 1)),
            pl.BlockSpec((None, seq // tk, vrows, tk), lambda b, i: (b, 0, 0, 0)),
        ] + extra_specs,
        out_specs=pl.BlockSpec((tq, GROUP_W), lambda b, i: (b * nq + i, 0)),
        scratch_shapes=scratch,
        compiler_params=_cparams(("arbitrary", "arbitrary")),
        name=name,
    )(*smem, qk, qk, vt, *extra)


def _ch_bias_tiles(rel_bias, tq, tk):
    assert tq == tk
    nback = CH_LEFT_CHUNKS * CHUNK // tk
    nh = rel_bias.shape[0]
    n = np.arange(nback + 1)[:, None, None]
    s = np.arange(tk)[None, :, None]
    t = np.arange(tq)[None, None, :]
    delta = (n * tk + t) // CHUNK - s // CHUNK
    valid = (delta >= 0) & (delta <= CH_LEFT_CHUNKS)
    table = rel_bias.astype(F32) * LOG2E
    lo, hi = -(tk - 1), nback * tk + tq - 1
    ext = jnp.concatenate([jnp.repeat(table[:, :1], -REL_CLIP - lo, axis=1), table,
                           jnp.repeat(table[:, -1:], hi - REL_CLIP, axis=1)], axis=1)
    period = tk + tq
    tiles = []
    for m in range(nback + 1):
        w = ext[:, m * tk:m * tk + period - 1]
        w = jnp.concatenate([w, jnp.zeros((nh, 1), F32)], axis=1)
        skew = jnp.tile(w, (1, tk))[:, :tk * (period - 1)].reshape(nh, tk, period - 1)
        tiles.append(skew[:, :, tk - 1:tk - 1 + tq])
    bias = jnp.stack(tiles, axis=1)
    return jnp.where(jnp.asarray(valid)[None], bias, NEG_INF), nback


def _df_bias_tiles(slopes2, tq, tk):
    s = jnp.arange(tk, dtype=jnp.int32)[:, None]
    t = jnp.arange(tq, dtype=jnp.int32)[None, :]
    sl = jnp.asarray(slopes2, F32)[:, None, None]
    kb = sl * jnp.broadcast_to(s, (tk, tq)).astype(F32)
    allowed = (s // CHUNK) <= (t // CHUNK)
    db = jnp.where(allowed[None], sl * (t - jnp.abs(t - s)).astype(F32), NEG_INF)
    return kb, db


def _out_proj_kernel(ys_ref, ysb_ref, ych_ref, ydf_ref, g_ref, ind_ref, w_ref, x_ref, o_ref):
    acc = x_ref[...]
    for gi, y_ref in enumerate((ys_ref, ysb_ref, ych_ref, ydf_ref)):
        cs = slice(gi * GROUP_W, (gi + 1) * GROUP_W)
        y = y_ref[...].astype(F32)
        ms = _group_ms(y, ind_ref[...], 1.0 / HEAD_DIM)
        yn = (y * lax.rsqrt(ms + EPS) * g_ref[:, cs]).astype(BF16)
        acc = acc + _dot(yn, w_ref[cs, :])
    o_ref[...] = acc


def _out_proj(y_ssm_tm, y_sb, y_ch, y_df, g, ind64, w, x2, bsz, seq):
    n, dm = x2.shape
    tm = TM_PROJ
    ns = seq // tm
    const = lambda i: (0, 0)
    tile = pl.BlockSpec((tm, GROUP_W), lambda i: (i, 0))
    return pl.pallas_call(
        _out_proj_kernel,
        out_shape=jax.ShapeDtypeStruct((n, dm), F32),
        grid=(n // tm,),
        in_specs=[
            pl.BlockSpec((tm, GROUP_W), lambda i: (i % ns, i // ns)),
            tile, tile, tile,
            pl.BlockSpec((1, dm), const),
            pl.BlockSpec((GROUP_W, GROUP_W), const),
            pl.BlockSpec((dm, dm), const),
            pl.BlockSpec((tm, dm), lambda i: (i, 0)),
        ],
        out_specs=pl.BlockSpec((tm, dm), lambda i: (i, 0)),
        compiler_params=_cparams(("arbitrary",)),
        name="out_proj",
    )(y_ssm_tm, y_sb, y_ch, y_df, g, ind64, w, x2)


def _ffn_kernel(x_ref, g_ref, w1_ref, w3_ref, w2_ref, o_ref, hid_ref, *, tf):
    x = x_ref[...]
    ms = jnp.mean(x * x, axis=-1, keepdims=True)
    h = (x * lax.rsqrt(ms + EPS) * g_ref[...]).astype(BF16)
    dff = w1_ref.shape[1]
    for c in range(dff // tf):
        cs = slice(c * tf, (c + 1) * tf)
        a = _dot(h, w1_ref[:, cs])
        b = _dot(h, w3_ref[:, cs])
        hid_ref[:, cs] = (jax.nn.silu(a) * b).astype(BF16)
    o_ref[...] = x + _dot(hid_ref[...], w2_ref[...])


def _ffn(x2, g, w1, w3, w2):
    n, dm = x2.shape
    dff = w1.shape[1]
    tm = TM_PROJ
    const = lambda i: (0, 0)
    return pl.pallas_call(
        functools.partial(_ffn_kernel, tf=GROUP_W),
        out_shape=jax.ShapeDtypeStruct((n, dm), F32),
        grid=(n // tm,),
        in_specs=[
            pl.BlockSpec((tm, dm), lambda i: (i, 0)),
            pl.BlockSpec((1, dm), const),
            pl.BlockSpec((dm, dff), const),
            pl.BlockSpec((dm, dff), const),
            pl.BlockSpec((dff, dm), const),
        ],
        out_specs=pl.BlockSpec((tm, dm), lambda i: (i, 0)),
        scratch_shapes=[pltpu.VMEM((tm, dff), BF16)],
        compiler_params=_cparams(("arbitrary",)),
        name="ffn_dense",
    )(x2, g, w1, w3, w2)


def _router_kernel(x_ref, g_ref, wr_ref, tri_ref, h_ref, route_ref, cnt_ref, carry_ref):
    @pl.when(pl.program_id(0) == 0)
    def _():
        carry_ref[...] = jnp.zeros_like(carry_ref)

    x = x_ref[...]
    ms = jnp.mean(x * x, axis=-1, keepdims=True)
    h = x * lax.rsqrt(ms + EPS) * g_ref[...]
    h_hi = h.astype(BF16)
    half = h.shape[1] // 2
    bits = lax.bitcast_convert_type(h_hi.astype(F32), jnp.uint32)
    h_ref[...] = bits[:, :half] | lax.shift_right_logical(bits[:, half:], jnp.uint32(16))
    h_lo = (h - h_hi.astype(F32)).astype(BF16)
    logits = _dot(h_hi, wr_ref[0]) + _dot(h_lo, wr_ref[0]) + _dot(h_hi, wr_ref[1])

    lane = _iota2(logits.shape, 1).astype(F32)
    lg = jnp.where(lane < N_EXPERTS, logits, -jnp.inf)
    m1 = jnp.max(lg, axis=-1, keepdims=True)
    e1 = jnp.min(jnp.where(lg == m1, lane, float(LANES)), axis=-1, keepdims=True)
    lg2 = jnp.where(lane == e1, -jnp.inf, lg)
    m2 = jnp.max(lg2, axis=-1, keepdims=True)
    e2 = jnp.min(jnp.where(lg2 == m2, lane, float(LANES)), axis=-1, keepdims=True)
    t = jnp.exp(m2 - m1)
    g1 = 1.0 / (1.0 + t)
    g2 = t / (1.0 + t)

    hot1 = lane == e1
    hot2 = lane == e2
    sel = jnp.where(hot1 | hot2, 1.0, 0.0).astype(BF16)
    prior = _dot(tri_ref[...], sel) + carry_ref[...]
    r1 = jnp.sum(jnp.where(hot1, prior, 0.0), axis=-1, keepdims=True)
    r2 = jnp.sum(jnp.where(hot2, prior, 0.0), axis=-1, keepdims=True)
    carry_ref[...] = carry_ref[...] + jnp.sum(sel.astype(F32), axis=0, keepdims=True)
    cnt_ref[...] = carry_ref[...]

    out = jnp.where(lane == 0, e1, 0.0)
    out = jnp.where(lane == 1, e2, out)
    out = jnp.where(lane == 2, g1, out)
    out = jnp.where(lane == 3, g2, out)
    out = jnp.where(lane == 4, r1, out)
    out = jnp.where(lane == 5, r2, out)
    route_ref[...] = out


def _router(x2, g, wr2, tri):
    n, dm = x2.shape
    tm = TM_PROJ
    const = lambda i: (0, 0)
    return pl.pallas_call(
        _router_kernel,
        out_shape=(jax.ShapeDtypeStruct((n, dm // 2), jnp.uint32),
                   jax.ShapeDtypeStruct((n, LANES), F32),
                   jax.ShapeDtypeStruct((1, LANES), F32)),
        grid=(n // tm,),
        in_specs=[
            pl.BlockSpec((tm, dm), lambda i: (i, 0)),
            pl.BlockSpec((1, dm), const),
            pl.BlockSpec((2, dm, LANES), lambda i: (0, 0, 0)),
            pl.BlockSpec((tm, tm), const),
        ],
        out_specs=(
            pl.BlockSpec((tm, dm // 2), lambda i: (i, 0)),
            pl.BlockSpec((tm, LANES), lambda i: (i, 0)),
            pl.BlockSpec((1, LANES), const),
        ),
        scratch_shapes=[pltpu.VMEM((1, LANES), F32)],
        compiler_params=_cparams(("arbitrary",)),
        name="moe_router",
    )(x2, g, wr2, tri)


def _row_copy(src_ref, src_row, dst_ref, dst_row, sem):
    return pltpu.make_async_copy(src_ref.at[pl.ds(src_row, 1)], dst_ref.at[pl.ds(dst_row, 1)], sem)


def _dispatch_kernel(d1_ref, d2_ref, cnt_ref, h_ref, xs_ref, zero_ref, sem, tail_sem, *, tm, cap, nexp):
    def issue(r, c):
        _row_copy(h_ref, r, xs_ref, d1_ref[0, 0, r], sem).start()
        _row_copy(h_ref, r, xs_ref, d2_ref[0, 0, r], sem).start(priority=1)
        return c

    lax.fori_loop(0, tm, issue, 0, unroll=8)

    def aligned_end(e):
        return ((cnt_ref[e] + (SUBLANES - 1)) // SUBLANES) * SUBLANES

    def tail_copy(e):
        row0 = pl.multiple_of(e * cap + aligned_end(e), SUBLANES)
        return pltpu.make_async_copy(zero_ref, xs_ref.at[pl.ds(row0, tm)], tail_sem)

    def tail_rows(e, fn):
        for k in range(SUBLANES - 1):
            @pl.when(cnt_ref[e] + k < aligned_end(e))
            def _():
                fn(_row_copy(zero_ref, 0, xs_ref, e * cap + cnt_ref[e] + k, tail_sem))

    @pl.when(pl.program_id(0) == pl.num_programs(0) - 1)
    def _():
        zero_ref[...] = jnp.zeros_like(zero_ref)
        for e in range(nexp):
            tail_copy(e).start()
            tail_rows(e, lambda cp: cp.start())
        for e in range(nexp):
            tail_copy(e).wait()
            tail_rows(e, lambda cp: cp.wait())

    for _ in range(2):
        pltpu.make_async_copy(h_ref, xs_ref.at[pl.ds(0, tm)], sem).wait()


def _dispatch(d1, d2, cnt, h, cap):
    n, wd = h.shape
    tm = TM_PROJ
    nt = n // tm
    idx_spec = pl.BlockSpec((1, 1, tm), lambda i: (i, 0, 0), memory_space=pltpu.SMEM)
    return pl.pallas_call(
        functools.partial(_dispatch_kernel, tm=tm, cap=cap, nexp=N_EXPERTS),
        out_shape=jax.ShapeDtypeStruct((N_EXPERTS * cap, wd), h.dtype),
        grid=(nt,),
        in_specs=[idx_spec, idx_spec, pl.BlockSpec(memory_space=pltpu.SMEM),
                  pl.BlockSpec((tm, wd), lambda i: (i, 0))],
        out_specs=pl.BlockSpec(memory_space=pl.ANY),
        scratch_shapes=[pltpu.VMEM((tm, wd), h.dtype), pltpu.SemaphoreType.DMA, pltpu.SemaphoreType.DMA],
        compiler_params=pltpu.CompilerParams(dimension_semantics=("arbitrary",),
                                             vmem_limit_bytes=VMEM_LIMIT_BYTES, disable_bounds_checks=True),
        name="moe_dispatch",
    )(d1.reshape(nt, 1, tm), d2.reshape(nt, 1, tm), cnt, h)


def _combine_kernel(d1_ref, d2_ref, n1_ref, n2_ref, x_ref, route_ref, ys_ref, o_ref, y1_ref, y2_ref, sems, *, tm):
    i = pl.program_id(0)
    slot = jnp.bitwise_and(i, 1)

    def fetch(a_ref, b_ref, s):
        def issue(r, c):
            _row_copy(ys_ref, a_ref[0, 0, r], y1_ref.at[s], r, sems.at[s]).start()
            _row_copy(ys_ref, b_ref[0, 0, r], y2_ref.at[s], r, sems.at[s]).start(priority=1)
            return c

        lax.fori_loop(0, tm, issue, 0, unroll=8)

    @pl.when(i == 0)
    def _():
        fetch(d1_ref, d2_ref, slot)

    @pl.when(i + 1 < pl.num_programs(0))
    def _():
        fetch(n1_ref, n2_ref, 1 - slot)

    for buf in (y1_ref, y2_ref):
        pltpu.make_async_copy(ys_ref.at[pl.ds(0, tm)], buf.at[slot], sems.at[slot]).wait()
    o_ref[...] = x_ref[...] + route_ref[:, 2:3] * y1_ref[slot] + route_ref[:, 3:4] * y2_ref[slot]


def _combine(d1, d2, x2, route, ys):
    n, dm = x2.shape
    tm = TM_PROJ
    nt = n // tm
    idx_spec = pl.BlockSpec((1, 1, tm), lambda i: (i, 0, 0), memory_space=pltpu.SMEM)
    next_spec = pl.BlockSpec((1, 1, tm), lambda i: (jnp.minimum(i + 1, nt - 1), 0, 0), memory_space=pltpu.SMEM)
    d1, d2 = d1.reshape(nt, 1, tm), d2.reshape(nt, 1, tm)
    return pl.pallas_call(
        functools.partial(_combine_kernel, tm=tm),
        out_shape=jax.ShapeDtypeStruct((n, dm), F32),
        grid=(nt,),
        in_specs=[idx_spec, idx_spec, next_spec, next_spec, pl.BlockSpec((tm, dm), lambda i: (i, 0)),
                  pl.BlockSpec((tm, LANES), lambda i: (i, 0)), pl.BlockSpec(memory_space=pl.ANY)],
        out_specs=pl.BlockSpec((tm, dm), lambda i: (i, 0)),
        scratch_shapes=[pltpu.VMEM((2, tm, dm), F32), pltpu.VMEM((2, tm, dm), F32), pltpu.SemaphoreType.DMA((2,))],
        compiler_params=pltpu.CompilerParams(dimension_semantics=("arbitrary",),
                                             vmem_limit_bytes=VMEM_LIMIT_BYTES, disable_bounds_checks=True),
        name="moe_combine",
    )(d1, d2, d1, d2, x2, route, ys)


def _experts_kernel(te_ref, blk_ref, nused_ref, xs_ref, w1_ref, w3_ref, w2_ref, o_ref, x_ref, hid_ref, acc_ref):
    t = pl.program_id(0)
    c = pl.program_id(1)
    nc = pl.num_programs(1)

    @pl.when(jnp.logical_and(t < nused_ref[0], c == 0))
    def _():
        w = xs_ref[...]
        half = w.shape[1]
        x_ref[:, :half] = lax.bitcast_convert_type(w & jnp.uint32(0xFFFF0000), F32).astype(BF16)
        x_ref[:, half:] = lax.bitcast_convert_type(lax.shift_left(w, jnp.uint32(16)), F32).astype(BF16)

    @pl.when(t < nused_ref[0])
    def _():
        x = x_ref[...]
        tf = w1_ref.shape[2]
        for s0 in range(0, tf, GROUP_W):
            cs = slice(s0, s0 + GROUP_W)
            a = _dot(x, w1_ref[0, :, cs])
            b = _dot(x, w3_ref[0, :, cs])
            hid_ref[:, cs] = (jax.nn.silu(a) * b).astype(BF16)
        part = _dot(hid_ref[...], w2_ref[0])

        @pl.when(c == 0)
        def _():
            acc_ref[...] = part

        @pl.when(c > 0)
        def _():
            acc_ref[...] = acc_ref[...] + part

        @pl.when(c == nc - 1)
        def _():
            o_ref[...] = acc_ref[...]

    @pl.when(jnp.logical_and(t >= nused_ref[0], c == nc - 1))
    def _():
        o_ref[...] = jnp.zeros_like(o_ref)


def _experts(tile_expert, tile_block, nused, xs, w1, w3, w2, nt):
    rows, wd = xs.shape
    dm = 2 * wd
    dff = w1.shape[2]
    tm, tf = TM_EXPERT, TF_EXPERT
    nc = dff // tf
    spare = rows // tm

    def cc(t, c, nu):
        return jnp.where(t < nu[0], c, nc - 1)

    grid_spec = pltpu.PrefetchScalarGridSpec(
        num_scalar_prefetch=3,
        grid=(nt, nc),
        in_specs=[
            pl.BlockSpec((tm, wd), lambda t, c, te, tb, nu: (tb[t], 0)),
            pl.BlockSpec((1, dm, tf), lambda t, c, te, tb, nu: (te[t], 0, cc(t, c, nu))),
            pl.BlockSpec((1, dm, tf), lambda t, c, te, tb, nu: (te[t], 0, cc(t, c, nu))),
            pl.BlockSpec((1, tf, dm), lambda t, c, te, tb, nu: (te[t], cc(t, c, nu), 0)),
        ],
        out_specs=pl.BlockSpec((tm, dm), lambda t, c, te, tb, nu: (jnp.where(t < nu[0], tb[t], spare), 0)),
        scratch_shapes=[pltpu.VMEM((tm, dm), BF16), pltpu.VMEM((tm, tf), BF16), pltpu.VMEM((tm, dm), F32)],
    )
    return pl.pallas_call(
        _experts_kernel,
        out_shape=jax.ShapeDtypeStruct((rows + tm, dm), F32),
        grid_spec=grid_spec,
        compiler_params=_cparams(("arbitrary", "arbitrary")),
        name="moe_experts",
    )(tile_expert, tile_block, nused, xs, w1, w3, w2)


def _moe(x2, g, w_router, w1, w3, w2, tri):
    n, dm = x2.shape
    wr = jnp.zeros((dm, LANES), F32).at[:, :N_EXPERTS].set(w_router.astype(F32))
    wr_hi = wr.astype(BF16)
    wr_lo = (wr - wr_hi.astype(F32)).astype(BF16)
    h, route, counts = _router(x2, g, jnp.stack([wr_hi, wr_lo]), tri)

    tm = TM_EXPERT
    assert n % tm == 0 and TM_PROJ == tm
    cap = n + tm
    d1 = route[:, 0].astype(jnp.int32) * cap + route[:, 4].astype(jnp.int32)
    d2 = route[:, 1].astype(jnp.int32) * cap + route[:, 5].astype(jnp.int32)
    cnt = counts[0, :N_EXPERTS].astype(jnp.int32)

    nt = 2 * n // tm + N_EXPERTS
    ntile = (cnt + tm - 1) // tm
    ends = jnp.cumsum(ntile)
    nused = ends[-1]
    t = jnp.minimum(jnp.arange(nt, dtype=jnp.int32), nused - 1)
    tile_expert = jnp.sum(t[:, None] >= ends[None, :], axis=1).astype(jnp.int32)
    tile_block = tile_expert * (cap // tm) + t - (ends - ntile)[tile_expert]

    xs = _dispatch(d1, d2, cnt, h, cap)
    ys = _experts(tile_expert, tile_block.astype(jnp.int32), nused.reshape(1).astype(jnp.int32), xs, w1, w3, w2, nt)
    return _combine(d1, d2, x2, route, ys)


def kernel(x, norm_mix_g, w_in, ssm_lam_re, ssm_lam_im, ssm_log_dt, ssm_b_re, ssm_b_im, ssm_c_re, ssm_c_im,
           ssm_d, ssm_w_glu, ssm_b_glu, ch_q_norm_g, ch_k_norm_g, ch_rel_bias, df_q_norm_g, df_k_norm_g,
           df_lambda, out_norm_g, w_out, norm_ffn_g, ffn_w1, ffn_w3, ffn_w2, moe_router, moe_w1, moe_w3, moe_w2):
    bsz, seq, dm = x.shape
    depth = w_in.shape[0]
    n = bsz * seq
    nheads = GROUP_W // HEAD_DIM
    ind64 = _block_indicator(GROUP_W, HEAD_DIM)
    ind32 = _block_indicator(GROUP_W, DF_QK_DIM)
    tri = (jnp.arange(TM_PROJ)[:, None] > jnp.arange(TM_PROJ)[None, :]).astype(BF16)
    slopes = [2.0 ** (-8.0 * (h + 1) / nheads) for h in range(nheads)]
    slopes2 = tuple(s * LOG2E for s in slopes)
    df_kb, df_db = _df_bias_tiles(slopes2, TQ_ATT, TK_ATT)
    later = (jnp.arange(TK_ATT)[:, None] < jnp.arange(TK_ATT)[None, :]).astype(BF16)

    x2 = x.reshape(n, dm)
    for layer in range(depth):
        qkg = jnp.stack([
            jnp.tile(ch_q_norm_g[layer].astype(F32), nheads) * (HEAD_DIM ** -0.5 * LOG2E),
            jnp.tile(ch_k_norm_g[layer].astype(F32), nheads),
            jnp.tile(df_q_norm_g[layer].astype(F32).reshape(-1), nheads) * (DF_QK_DIM ** -0.5 * LOG2E),
            jnp.tile(df_k_norm_g[layer].astype(F32).reshape(-1), nheads),
        ])
        u_tm, qk, v_sb, v_ch, v_df = _in_proj(x2, norm_mix_g[layer].reshape(1, dm), w_in[layer].astype(BF16), qkg,
                             ind64, ind32, bsz, seq)

        wb, lre, lim, wc = _s5_params(ssm_lam_re[layer], ssm_lam_im[layer], ssm_log_dt[layer],
                                      ssm_b_re[layer], ssm_b_im[layer], ssm_c_re[layer], ssm_c_im[layer], bsz)
        y_ssm = _s5(u_tm.reshape(seq * bsz, GROUP_W), wb, lre, lim, wc, ssm_d[layer].reshape(1, GROUP_W),
                    ssm_w_glu[layer].astype(BF16), ssm_b_glu[layer].reshape(1, GROUP_W), bsz)
        y_ssm = y_ssm.reshape(seq, bsz * GROUP_W)

        tq, tk = TQ_ATT, TK_ATT
        const3 = lambda b, i: (0, 0, 0)
        y_sb = _attention_call(
            functools.partial(_sb_kernel, tq=tq, tk=tk, nheads=nheads),
            qk, 0, v_sb,
            (later,), [pl.BlockSpec((tk, tk), lambda b, i: (0, 0))],
            [pltpu.VMEM((nheads * tq, GROUP_W), BF16), pltpu.VMEM((SUBLANES, tq), F32),
             pltpu.VMEM((GROUP_W, tq), F32),
             pltpu.VMEM((tk, nheads * tq), F32), pltpu.VMEM((tk, nheads * tq), F32)],
            bsz, seq, "sb_attention")

        bias, nback = _ch_bias_tiles(ch_rel_bias[layer], tq, tk)
        y_ch = _attention_call(
            functools.partial(_ch_kernel, tq=tq, tk=tk, nheads=nheads, nback=nback),
            qk, 2, v_ch,
            (bias,), [pl.BlockSpec(bias.shape, lambda b, i: (0, 0, 0, 0))],
            [pltpu.VMEM((nheads * tq, GROUP_W), BF16), pltpu.VMEM(((nback + 1) * tk, nheads * tq), F32)],
            bsz, seq, "ch_attention")

        lambda_init = 0.8 - 0.6 * math.exp(-0.3 * layer)
        lam_p = df_lambda[layer].astype(F32)
        lam = jnp.exp(jnp.sum(lam_p[0] * lam_p[1])) - jnp.exp(jnp.sum(lam_p[2] * lam_p[3])) + lambda_init
        y_df = _attention_call(
            functools.partial(_df_kernel, tq=tq, tk=tk, nheads=nheads, slopes2=slopes2),
            qk, 4, v_df,
            (df_kb, df_db), [pl.BlockSpec(df_kb.shape, const3), pl.BlockSpec(df_db.shape, const3)],
            [pltpu.VMEM((2 * nheads * tq, GROUP_W), BF16), pltpu.VMEM((2 * nheads, tq), F32),
             pltpu.VMEM((2 * nheads * VA_ROWS, tq), F32),
             pltpu.VMEM((tk, 2 * nheads * tq), F32), pltpu.VMEM((tk, 2 * nheads * tq), F32),
             pltpu.VMEM((2 * nheads, tq), F32), pltpu.VMEM((2 * nheads, tq), F32)],
            bsz, seq, "df_attention", smem=(lam.reshape(1),))

        head_scale = jnp.concatenate([jnp.ones((dm - GROUP_W,), F32),
                                      jnp.full((GROUP_W,), 1.0 - lambda_init, F32)])
        g_out = (out_norm_g[layer].astype(F32) * head_scale).reshape(1, dm)
        x2 = _out_proj(y_ssm, y_sb, y_ch, y_df, g_out, ind64, w_out[layer].astype(BF16), x2, bsz, seq)

        idx = layer // 2
        g_ffn = norm_ffn_g[layer].reshape(1, dm)
        if layer % 2 == 0:
            x2 = _ffn(x2, g_ffn, ffn_w1[idx].astype(BF16), ffn_w3[idx].astype(BF16), ffn_w2[idx].astype(BF16))
        else:
            x2 = _moe(x2, g_ffn, moe_router[idx], moe_w1[idx].astype(BF16), moe_w3[idx].astype(BF16),
                      moe_w2[idx].astype(BF16), tri)
    return x2.reshape(bsz, seq, dm)
```

```python
import functools
import math

import jax
import jax.numpy as jnp
import numpy as np
from jax import lax
from jax.experimental import pallas as pl
from jax.experimental.pallas import tpu as pltpu

F32 = jnp.float32
BF16 = jnp.bfloat16

EPS = 1e-6
NEG_INF = -1e30
HEAD_DIM = 64
GROUP_W = 256
CHUNK = 64
SSM_GROUP = 16
SSM_STATE = 64
DF_QK_DIM = 32
CH_LEFT_CHUNKS = 8
REL_CLIP = 128
N_EXPERTS = 8
LOG2E = 1.4426950408889634

VMEM_LIMIT_BYTES = 56 * 1024 * 1024
SUBLANES = 8
LANES = 128

TM_PROJ = 512
TQ_ATT = 256
TK_ATT = 256
SSM_STEPS = 64
TM_EXPERT = 512
TF_EXPERT = 1792


def _cparams(sem):
    return pltpu.CompilerParams(dimension_semantics=sem, vmem_limit_bytes=VMEM_LIMIT_BYTES)


def _dot(a, b):
    return jnp.dot(a, b, preferred_element_type=F32)


def _dot_nt(a, b):
    return lax.dot_general(a, b, (((1,), (1,)), ((), ())), preferred_element_type=F32)


def _group_ms(v, ind, inv_size):
    return _dot((v * v).astype(BF16), ind) * inv_size


def _block_indicator(width, group):
    r = jnp.arange(width)[:, None] // group
    c = jnp.arange(width)[None, :] // group
    return (r == c).astype(BF16)


def _in_proj_kernel(x_ref, g_ref, w_ref, qkg_ref, ind64_ref, ind32_ref, u_ref, qk_ref, vsb_ref, vch_ref, vdf_ref,
                    *, tk, nheads):
    x = x_ref[...]
    ms = jnp.mean(x * x, axis=-1, keepdims=True)
    h = (x * lax.rsqrt(ms + EPS) * g_ref[...]).astype(BF16)

    def cols(c):
        return _dot(h, w_ref[:, c * GROUP_W:(c + 1) * GROUP_W])

    def put(slot, val):
        qk_ref[:, slot * GROUP_W:(slot + 1) * GROUP_W] = val.astype(BF16)

    def put_values(v_ref, v, ones_row):
        for kt in range(v.shape[0] // tk):
            vt = v[kt * tk:(kt + 1) * tk, :].T
            if not ones_row:
                v_ref[kt] = vt.astype(BF16)
                continue
            pad = jnp.where(_iota2((VA_ROWS - HEAD_DIM, tk), 0) == 0, 1.0, 0.0).astype(BF16)
            for hd in range(nheads):
                v_ref[kt, hd * VA_ROWS:hd * VA_ROWS + HEAD_DIM, :] = vt[hd * HEAD_DIM:(hd + 1) * HEAD_DIM, :].astype(BF16)
                v_ref[kt, hd * VA_ROWS + HEAD_DIM:(hd + 1) * VA_ROWS, :] = pad

    u_ref[...] = cols(0)
    put(0, cols(1) * (HEAD_DIM ** -0.5 * LOG2E))
    put(1, cols(2))
    put_values(vsb_ref, cols(3), False)
    for c, gi in ((4, 0), (5, 1)):
        a = cols(c)
        ms_h = _group_ms(a, ind64_ref[...], 1.0 / HEAD_DIM)
        put(c - 2, a * lax.rsqrt(ms_h + EPS) * qkg_ref[gi:gi + 1, :])
    put_values(vch_ref, cols(6), True)
    for c, gi in ((7, 2), (8, 3)):
        a = cols(c)
        ms_h = _group_ms(a, ind32_ref[...], 1.0 / DF_QK_DIM)
        put(c - 3, a * lax.rsqrt(ms_h + EPS) * qkg_ref[gi:gi + 1, :])
    put_values(vdf_ref, cols(9), True)


def _in_proj(x2, g, w, qkg, ind64, ind32, bsz, seq):
    n, dm = x2.shape
    tm, tk = TM_PROJ, TK_ATT
    ns = seq // tm
    nheads = GROUP_W // HEAD_DIM
    const = lambda i: (0, 0)

    def values(rows):
        return (jax.ShapeDtypeStruct((bsz, seq // tk, rows, tk), BF16),
                pl.BlockSpec((None, tm // tk, rows, tk), lambda i: (i // ns, i % ns, 0, 0)))

    v_shapes, v_specs = zip(values(GROUP_W), values(nheads * VA_ROWS), values(nheads * VA_ROWS))
    return pl.pallas_call(
        functools.partial(_in_proj_kernel, tk=tk, nheads=nheads),
        out_shape=(jax.ShapeDtypeStruct((seq, bsz * GROUP_W), F32),
                   jax.ShapeDtypeStruct((n, 6 * GROUP_W), BF16)) + v_shapes,
        grid=(n // tm,),
        in_specs=[
            pl.BlockSpec((tm, dm), lambda i: (i, 0)),
            pl.BlockSpec((1, dm), const),
            pl.BlockSpec(w.shape, const),
            pl.BlockSpec((4, GROUP_W), const),
            pl.BlockSpec((GROUP_W, GROUP_W), const),
            pl.BlockSpec((GROUP_W, GROUP_W), const),
        ],
        out_specs=(
            pl.BlockSpec((tm, GROUP_W), lambda i: (i % ns, i // ns)),
            pl.BlockSpec((tm, 6 * GROUP_W), lambda i: (i, 0)),
        ) + v_specs,
        compiler_params=_cparams(("arbitrary",)),
        name="in_proj",
    )(x2, g, w, qkg, ind64, ind32)


def _s5_kernel(u_ref, wb_ref, lre_ref, lim_ref, wc_ref, d_ref, wg_ref, bg_ref, y_ref, bu_ref, st_ref,
               *, steps, nstate):
    @pl.when(pl.program_id(0) == 0)
    def _():
        st_ref[...] = jnp.zeros_like(st_ref)

    u = u_ref[...]
    bu_ref[...] = _dot(u.astype(BF16), wb_ref[...])
    lre = lre_ref[...]
    lim = lim_ref[...]

    def step(t, carry):
        sre, sim = carry
        r = pl.multiple_of(t * SUBLANES, SUBLANES)
        bre = bu_ref[pl.ds(r, SUBLANES), 0:nstate]
        bim = bu_ref[pl.ds(r, SUBLANES), nstate:2 * nstate]
        nre = lre * sre - lim * sim + bre
        nim = lre * sim + lim * sre + bim
        bu_ref[pl.ds(r, SUBLANES), 0:nstate] = nre
        bu_ref[pl.ds(r, SUBLANES), nstate:2 * nstate] = nim
        return nre, nim

    sre, sim = lax.fori_loop(0, steps, step, (st_ref[:, 0:nstate], st_ref[:, nstate:2 * nstate]))
    st_ref[:, 0:nstate] = sre
    st_ref[:, nstate:2 * nstate] = sim

    y = _dot(bu_ref[...].astype(BF16), wc_ref[...]) + d_ref[...] * u
    y = jax.nn.gelu(y)
    gate = jax.nn.sigmoid(_dot(y.astype(BF16), wg_ref[...]) + bg_ref[...])
    y_ref[...] = (y * gate).astype(y_ref.dtype)


def _s5(u_tm, wb, lre, lim, wc, d, wg, bg, bsz):
    rows, w = u_tm.shape
    assert bsz == SUBLANES
    blk = SSM_STEPS * bsz
    nstate2 = wb.shape[1]
    const = lambda c: (0, 0)
    return pl.pallas_call(
        functools.partial(_s5_kernel, steps=SSM_STEPS, nstate=nstate2 // 2),
        out_shape=jax.ShapeDtypeStruct((rows, w), BF16),
        grid=(rows // blk,),
        in_specs=[
            pl.BlockSpec((blk, w), lambda c: (c, 0)),
            pl.BlockSpec((w, nstate2), const),
            pl.BlockSpec((bsz, nstate2 // 2), const),
            pl.BlockSpec((bsz, nstate2 // 2), const),
            pl.BlockSpec((nstate2, w), const),
            pl.BlockSpec((1, w), const),
            pl.BlockSpec((w, w), const),
            pl.BlockSpec((1, w), const),
        ],
        out_specs=pl.BlockSpec((blk, w), lambda c: (c, 0)),
        scratch_shapes=[pltpu.VMEM((blk, nstate2), F32), pltpu.VMEM((bsz, nstate2), F32)],
        compiler_params=_cparams(("arbitrary",)),
        name="s5_mixer",
    )(u_tm, wb, lre, lim, wc, d, wg, bg)


def _s5_params(lam_re, lam_im, log_dt, b_re, b_im, c_re, c_im, bsz):
    g, n = lam_re.shape
    dt = jnp.exp(log_dt.astype(F32))[:, None]
    xr = lam_re * dt
    th = lam_im * dt
    er = jnp.exp(xr)
    lbr = er * jnp.cos(th)
    lbi = er * jnp.sin(th)
    ar = jnp.expm1(xr) * jnp.cos(th) - 2.0 * jnp.sin(0.5 * th) ** 2
    ai = lbi
    den = lam_re * lam_re + lam_im * lam_im
    fr = (ar * lam_re + ai * lam_im) / den
    fi = (ai * lam_re - ar * lam_im) / den
    bbr = fr[..., None] * b_re - fi[..., None] * b_im
    bbi = fr[..., None] * b_im + fi[..., None] * b_re
    eye = jnp.eye(g, dtype=F32)
    p = b_re.shape[-1]
    wb_re = jnp.einsum("gnp,gh->gphn", bbr, eye).reshape(g * p, g * n)
    wb_im = jnp.einsum("gnp,gh->gphn", bbi, eye).reshape(g * p, g * n)
    wb = jnp.concatenate([wb_re, wb_im], axis=1).astype(BF16)
    wc_re = jnp.einsum("gpn,gh->gnhp", c_re, eye).reshape(g * n, g * p)
    wc_im = jnp.einsum("gpn,gh->gnhp", c_im, eye).reshape(g * n, g * p)
    wc = jnp.concatenate([wc_re, -wc_im], axis=0).astype(BF16)
    lre = jnp.broadcast_to(lbr.reshape(1, g * n), (bsz, g * n))
    lim = jnp.broadcast_to(lbi.reshape(1, g * n), (bsz, g * n))
    return wb, lre, lim, wc


VA_ROWS = 2 * HEAD_DIM


def _iota2(shape, dim):
    return lax.broadcasted_iota(jnp.int32, shape, dim)


def _stack_masked_q(q_ref, qs_ref, tq, ngroups, width):
    lane = _iota2((1, GROUP_W), 1)
    q = q_ref[...]
    for g in range(ngroups):
        keep = (lane >= g * width) & (lane < (g + 1) * width)
        qs_ref[g * tq:(g + 1) * tq, :] = jnp.where(keep, q, jnp.zeros_like(q))


def _key_scores(k_ref, qs_ref, j, tk):
    row0 = pl.multiple_of(jnp.maximum(j, 0) * tk, tk)
    return _dot_nt(k_ref[pl.ds(row0, tk), :], qs_ref[...])


def _score_prefetch(k_ref, qs_ref, j, dst_ref, tq, tk, bias_of=None, max_ref=None):
    row0 = pl.multiple_of(jnp.maximum(j, 0) * tk, tk)

    def emit(g):
        cols = slice(g * tq, (g + 1) * tq)
        s = _dot_nt(k_ref[pl.ds(row0, tk), :], qs_ref[cols, :])
        if bias_of is not None:
            s = s + bias_of(g)[...]
        dst_ref[:, cols] = s
        if max_ref is not None:
            max_ref[g:g + 1, :] = jnp.max(s, axis=0, keepdims=True)

    return emit


def _paired_key_tiles(i, k_ref, qs_ref, tile_fn, buf_a, buf_b, tq, tk, bias_of=None):
    def body(n, c):
        ja = i - 1 - 2 * n
        tile_fn(ja, buf_a, _score_prefetch(k_ref, qs_ref, ja - 1, buf_b[0], tq, tk, bias_of, buf_b[1]))
        tile_fn(ja - 1, buf_b, _score_prefetch(k_ref, qs_ref, ja - 2, buf_a[0], tq, tk, bias_of, buf_a[1]))
        return c

    lax.fori_loop(0, i // 2, body, 0)

    @pl.when(jnp.bitwise_and(i, 1) == 1)
    def _():
        tile_fn(0, buf_a, lambda g: None)


def _sb_kernel(q_ref, k_ref, vt_ref, later_ref, o_ref, qs_ref, carry_ref, acc_ref, za_ref, zb_ref,
               *, tq, tk, nheads):
    i = pl.program_id(1)
    _stack_masked_q(q_ref, qs_ref, tq, nheads, HEAD_DIM)

    def tile(j, buf, prefetch, diagonal=False):
        zt = buf[0]
        if diagonal:
            before = _iota2((tk, tq), 0) < _iota2((tk, tq), 1)

        def scan_stage(h):
            z = zt[:, h * tq:(h + 1) * tq]
            neg_abs = lax.bitcast_convert_type(
                lax.bitcast_convert_type(z, jnp.uint32) | jnp.uint32(0x80000000), F32)
            sp = jnp.maximum(z, 0.0) + jnp.log2(1.0 + jnp.exp2(neg_abs))
            nl1m = jnp.where(before, sp, 0.0) if diagonal else sp
            inside = _dot(later_ref[...], nl1m.astype(BF16))
            return z - sp, inside, nl1m[0:1, :]

        def value_stage(h, logit, inside, first_row):
            hs = slice(h * HEAD_DIM, (h + 1) * HEAD_DIM)
            between = inside if diagonal else inside + carry_ref[h:h + 1, :]
            w = jnp.exp2(logit - between)
            if diagonal:
                w = jnp.where(before, w, 0.0)
            pv = _dot(vt_ref[j, hs, :], w.astype(BF16))
            total = inside[0:1, :] + first_row
            if diagonal:
                acc_ref[hs, :] = pv
                carry_ref[h:h + 1, :] = total
            else:
                acc_ref[hs, :] = acc_ref[hs, :] + pv
                carry_ref[h:h + 1, :] = carry_ref[h:h + 1, :] + total

        prefetch(0)
        pending = scan_stage(0)
        for h in range(1, nheads):
            prefetch(h)
            upcoming = scan_stage(h)
            value_stage(h - 1, *pending)
            pending = upcoming
        value_stage(nheads - 1, *pending)

    tile(i, (_key_scores(k_ref, qs_ref, i, tk), None), _score_prefetch(k_ref, qs_ref, i - 1, za_ref, tq, tk),
         diagonal=True)
    _paired_key_tiles(i, k_ref, qs_ref, tile, (za_ref, None), (zb_ref, None), tq, tk)
    o_ref[...] = acc_ref[...].T.astype(o_ref.dtype)


def _softmax_tile(buf, tq, off, va, g, first, m_ref, acc_ref):
    st_ref, max_ref = buf
    rows = slice(g * VA_ROWS, (g + 1) * VA_ROWS)
    parts, alphas = [], []
    for c0 in range(0, tq, LANES):
        cs = slice(c0, c0 + LANES)
        sc = slice(g * tq + c0, g * tq + c0 + LANES)
        blk_max = max_ref[g:g + 1, cs]
        if first:
            m_new = blk_max
            shift = m_new
        else:
            m_old = m_ref[g:g + 1, cs]
            m_new = jnp.maximum(m_old, blk_max + off)
            shift = m_new - off
            alphas.append(jnp.exp2(m_old - m_new))
        parts.append(jnp.exp2(st_ref[:, sc] - shift).astype(BF16))
        m_ref[g:g + 1, cs] = m_new
    pv = _dot(va, jnp.concatenate(parts, axis=1))
    if first:
        acc_ref[rows, :] = pv
    else:
        acc_ref[rows, :] = jnp.concatenate(alphas, axis=1) * acc_ref[rows, :] + pv


def _normalised(acc_ref, g):
    base = g * VA_ROWS
    return acc_ref[base:base + HEAD_DIM, :] / acc_ref[base + HEAD_DIM:base + HEAD_DIM + 1, :]


def _ch_kernel(q_ref, k_ref, va_ref, bias_ref, o_ref, qs_ref, z_ref, *, tq, tk, nheads, nback):
    i = pl.program_id(1)
    _stack_masked_q(q_ref, qs_ref, tq, nheads, HEAD_DIM)

    def attend(ntiles):
        tile_max = [[] for _ in range(nheads)]
        for n in range(ntiles):
            row0 = pl.multiple_of((i - n) * tk, tk)
            for h in range(nheads):
                cols = slice(h * tq, (h + 1) * tq)
                s = _dot_nt(k_ref[pl.ds(row0, tk), :], qs_ref[cols, :]) + bias_ref[h, n]
                z_ref[n * tk:(n + 1) * tk, cols] = s
                tile_max[h].append(jnp.max(s, axis=0, keepdims=True))
        outs = []
        for h in range(nheads):
            m = functools.reduce(jnp.maximum, tile_max[h])
            acc = None
            for n in range(ntiles):
                p = jnp.exp2(z_ref[n * tk:(n + 1) * tk, h * tq:(h + 1) * tq] - m).astype(BF16)
                pv = _dot(va_ref[i - n, h * VA_ROWS:(h + 1) * VA_ROWS, :], p)
                acc = pv if acc is None else acc + pv
            outs.append(acc[:HEAD_DIM, :] / acc[HEAD_DIM:HEAD_DIM + 1, :])
        o_ref[...] = jnp.concatenate(outs, axis=0).T.astype(o_ref.dtype)

    for ntiles in range(1, nback + 2):
        pl.when(jnp.minimum(i, nback) + 1 == ntiles)(functools.partial(attend, ntiles))


def _df_kernel(lam_ref, q_ref, k_ref, va_ref, kb_ref, db_ref, o_ref, qs_ref, m_ref, acc_ref,
               za_ref, zb_ref, maxa_ref, maxb_ref, *, tq, tk, nheads, slopes2):
    i = pl.program_id(1)
    _stack_masked_q(q_ref, qs_ref, tq, 2 * nheads, DF_QK_DIM)
    buf_a, buf_b = (za_ref, maxa_ref), (zb_ref, maxb_ref)

    def past_bias(g):
        return kb_ref.at[g // 2]

    def tile(j, buf, prefetch, diagonal=False):
        for g in range(2 * nheads):
            prefetch(g)
            h = g // 2
            va = va_ref[j, h * VA_ROWS:(h + 1) * VA_ROWS, :]
            if diagonal:
                _softmax_tile(buf, tq, 0.0, va, g, True, m_ref, acc_ref)
            else:
                _softmax_tile(buf, tq, slopes2[h] * ((j - i) * tk).astype(F32), va, g, False, m_ref, acc_ref)

    diag = _score_prefetch(k_ref, qs_ref, i, zb_ref, tq, tk, lambda g: db_ref.at[g // 2], maxb_ref)
    for g in range(2 * nheads):
        diag(g)
    tile(i, buf_b, _score_prefetch(k_ref, qs_ref, i - 1, za_ref, tq, tk, past_bias, maxa_ref), diagonal=True)
    _paired_key_tiles(i, k_ref, qs_ref, tile, buf_a, buf_b, tq, tk, past_bias)
    lam = lam_ref[0]
    out = jnp.concatenate([_normalised(acc_ref, 2 * h) - lam * _normalised(acc_ref, 2 * h + 1)
                           for h in range(nheads)], axis=0)
    o_ref[...] = out.T.astype(o_ref.dtype)


def _attention_call(kernel, qk, col0, vt, extra, extra_specs, scratch, bsz, seq, name, smem=()):
    tq, tk = TQ_ATT, TK_ATT
    assert tq == tk
    nq = seq // tq
    vrows = vt.shape[2]
    return pl.pallas_call(
        kernel,
        out_shape=jax.ShapeDtypeStruct((bsz * seq, GROUP_W), BF16),
        grid=(bsz, nq),
        in_specs=[pl.BlockSpec(memory_space=pltpu.SMEM) for _ in smem] + [
            pl.BlockSpec((tq, GROUP_W), lambda b, i: (b * nq + i, col0)),
            pl.BlockSpec((seq, GROUP_W), lambda b, i: (b, col0 + 1)),
            pl.BlockSpec((None, seq // tk, vrows, tk), lambda b, i: (b, 0, 0, 0)),
        ] + extra_specs,
        out_specs=pl.BlockSpec((tq, GROUP_W), lambda b, i: (b * nq + i, 0)),
        scratch_shapes=scratch,
        compiler_params=_cparams(("arbitrary", "arbitrary")),
        name=name,
    )(*smem, qk, qk, vt, *extra)


def _ch_bias_tiles(rel_bias, tq, tk):
    assert tq == tk
    nback = CH_LEFT_CHUNKS * CHUNK // tk
    nh = rel_bias.shape[0]
    n = np.arange(nback + 1)[:, None, None]
    s = np.arange(tk)[None, :, None]
    t = np.arange(tq)[None, None, :]
    delta = (n * tk + t) // CHUNK - s // CHUNK
    valid = (delta >= 0) & (delta <= CH_LEFT_CHUNKS)
    table = rel_bias.astype(F32) * LOG2E
    lo, hi = -(tk - 1), nback * tk + tq - 1
    ext = jnp.concatenate([jnp.repeat(table[:, :1], -REL_CLIP - lo, axis=1), table,
                           jnp.repeat(table[:, -1:], hi - REL_CLIP, axis=1)], axis=1)
    period = tk + tq
    tiles = []
    for m in range(nback + 1):
        w = ext[:, m * tk:m * tk + period - 1]
        w = jnp.concatenate([w, jnp.zeros((nh, 1), F32)], axis=1)
        skew = jnp.tile(w, (1, tk))[:, :tk * (period - 1)].reshape(nh, tk, period - 1)
        tiles.append(skew[:, :, tk - 1:tk - 1 + tq])
    bias = jnp.stack(tiles, axis=1)
    return jnp.where(jnp.asarray(valid)[None], bias, NEG_INF), nback


def _df_bias_tiles(slopes2, tq, tk):
    s = jnp.arange(tk, dtype=jnp.int32)[:, None]
    t = jnp.arange(tq, dtype=jnp.int32)[None, :]
    sl = jnp.asarray(slopes2, F32)[:, None, None]
    kb = sl * jnp.broadcast_to(s, (tk, tq)).astype(F32)
    allowed = (s // CHUNK) <= (t // CHUNK)
    db = jnp.where(allowed[None], sl * (t - jnp.abs(t - s)).astype(F32), NEG_INF)
    return kb, db


def _out_proj_kernel(ys_ref, ysb_ref, ych_ref, ydf_ref, g_ref, ind_ref, w_ref, x_ref, o_ref):
    acc = x_ref[...]
    for gi, y_ref in enumerate((ys_ref, ysb_ref, ych_ref, ydf_ref)):
        cs = slice(gi * GROUP_W, (gi + 1) * GROUP_W)
        y = y_ref[...].astype(F32)
        ms = _group_ms(y, ind_ref[...], 1.0 / HEAD_DIM)
        yn = (y * lax.rsqrt(ms + EPS) * g_ref[:, cs]).astype(BF16)
        acc = acc + _dot(yn, w_ref[cs, :])
    o_ref[...] = acc


def _out_proj(y_ssm_tm, y_sb, y_ch, y_df, g, ind64, w, x2, bsz, seq):
    n, dm = x2.shape
    tm = TM_PROJ
    ns = seq // tm
    const = lambda i: (0, 0)
    tile = pl.BlockSpec((tm, GROUP_W), lambda i: (i, 0))
    return pl.pallas_call(
        _out_proj_kernel,
        out_shape=jax.ShapeDtypeStruct((n, dm), F32),
        grid=(n // tm,),
        in_specs=[
            pl.BlockSpec((tm, GROUP_W), lambda i: (i % ns, i // ns)),
            tile, tile, tile,
            pl.BlockSpec((1, dm), const),
            pl.BlockSpec((GROUP_W, GROUP_W), const),
            pl.BlockSpec((dm, dm), const),
            pl.BlockSpec((tm, dm), lambda i: (i, 0)),
        ],
        out_specs=pl.BlockSpec((tm, dm), lambda i: (i, 0)),
        compiler_params=_cparams(("arbitrary",)),
        name="out_proj",
    )(y_ssm_tm, y_sb, y_ch, y_df, g, ind64, w, x2)


def _ffn_kernel(x_ref, g_ref, w1_ref, w3_ref, w2_ref, o_ref, hid_ref, *, tf):
    x = x_ref[...]
    ms = jnp.mean(x * x, axis=-1, keepdims=True)
    h = (x * lax.rsqrt(ms + EPS) * g_ref[...]).astype(BF16)
    dff = w1_ref.shape[1]
    for c in range(dff // tf):
        cs = slice(c * tf, (c + 1) * tf)
        a = _dot(h, w1_ref[:, cs])
        b = _dot(h, w3_ref[:, cs])
        hid_ref[:, cs] = (jax.nn.silu(a) * b).astype(BF16)
    o_ref[...] = x + _dot(hid_ref[...], w2_ref[...])


def _ffn(x2, g, w1, w3, w2):
    n, dm = x2.shape
    dff = w1.shape[1]
    tm = TM_PROJ
    const = lambda i: (0, 0)
    return pl.pallas_call(
        functools.partial(_ffn_kernel, tf=GROUP_W),
        out_shape=jax.ShapeDtypeStruct((n, dm), F32),
        grid=(n // tm,),
        in_specs=[
            pl.BlockSpec((tm, dm), lambda i: (i, 0)),
            pl.BlockSpec((1, dm), const),
            pl.BlockSpec((dm, dff), const),
            pl.BlockSpec((dm, dff), const),
            pl.BlockSpec((dff, dm), const),
        ],
        out_specs=pl.BlockSpec((tm, dm), lambda i: (i, 0)),
        scratch_shapes=[pltpu.VMEM((tm, dff), BF16)],
        compiler_params=_cparams(("arbitrary",)),
        name="ffn_dense",
    )(x2, g, w1, w3, w2)


def _router_kernel(x_ref, g_ref, wr_ref, tri_ref, h_ref, route_ref, cnt_ref, carry_ref):
    @pl.when(pl.program_id(0) == 0)
    def _():
        carry_ref[...] = jnp.zeros_like(carry_ref)

    x = x_ref[...]
    ms = jnp.mean(x * x, axis=-1, keepdims=True)
    h = x * lax.rsqrt(ms + EPS) * g_ref[...]
    h_hi = h.astype(BF16)
    half = h.shape[1] // 2
    bits = lax.bitcast_convert_type(h_hi.astype(F32), jnp.uint32)
    h_ref[...] = bits[:, :half] | lax.shift_right_logical(bits[:, half:], jnp.uint32(16))
    h_lo = (h - h_hi.astype(F32)).astype(BF16)
    logits = _dot(h_hi, wr_ref[0]) + _dot(h_lo, wr_ref[0]) + _dot(h_hi, wr_ref[1])

    lane = _iota2(logits.shape, 1).astype(F32)
    lg = jnp.where(lane < N_EXPERTS, logits, -jnp.inf)
    m1 = jnp.max(lg, axis=-1, keepdims=True)
    e1 = jnp.min(jnp.where(lg == m1, lane, float(LANES)), axis=-1, keepdims=True)
    lg2 = jnp.where(lane == e1, -jnp.inf, lg)
    m2 = jnp.max(lg2, axis=-1, keepdims=True)
    e2 = jnp.min(jnp.where(lg2 == m2, lane, float(LANES)), axis=-1, keepdims=True)
    t = jnp.exp(m2 - m1)
    g1 = 1.0 / (1.0 + t)
    g2 = t / (1.0 + t)

    hot1 = lane == e1
    hot2 = lane == e2
    sel = jnp.where(hot1 | hot2, 1.0, 0.0).astype(BF16)
    prior = _dot(tri_ref[...], sel) + carry_ref[...]
    r1 = jnp.sum(jnp.where(hot1, prior, 0.0), axis=-1, keepdims=True)
    r2 = jnp.sum(jnp.where(hot2, prior, 0.0), axis=-1, keepdims=True)
    carry_ref[...] = carry_ref[...] + jnp.sum(sel.astype(F32), axis=0, keepdims=True)
    cnt_ref[...] = carry_ref[...]

    out = jnp.where(lane == 0, e1, 0.0)
    out = jnp.where(lane == 1, e2, out)
    out = jnp.where(lane == 2, g1, out)
    out = jnp.where(lane == 3, g2, out)
    out = jnp.where(lane == 4, r1, out)
    out = jnp.where(lane == 5, r2, out)
    route_ref[...] = out


def _router(x2, g, wr2, tri):
    n, dm = x2.shape
    tm = TM_PROJ
    const = lambda i: (0, 0)
    return pl.pallas_call(
        _router_kernel,
        out_shape=(jax.ShapeDtypeStruct((n, dm // 2), jnp.uint32),
                   jax.ShapeDtypeStruct((n, LANES), F32),
                   jax.ShapeDtypeStruct((1, LANES), F32)),
        grid=(n // tm,),
        in_specs=[
            pl.BlockSpec((tm, dm), lambda i: (i, 0)),
            pl.BlockSpec((1, dm), const),
            pl.BlockSpec((2, dm, LANES), lambda i: (0, 0, 0)),
            pl.BlockSpec((tm, tm), const),
        ],
        out_specs=(
            pl.BlockSpec((tm, dm // 2), lambda i: (i, 0)),
            pl.BlockSpec((tm, LANES), lambda i: (i, 0)),
            pl.BlockSpec((1, LANES), const),
        ),
        scratch_shapes=[pltpu.VMEM((1, LANES), F32)],
        compiler_params=_cparams(("arbitrary",)),
        name="moe_router",
    )(x2, g, wr2, tri)


def _row_copy(src_ref, src_row, dst_ref, dst_row, sem):
    return pltpu.make_async_copy(src_ref.at[pl.ds(src_row, 1)], dst_ref.at[pl.ds(dst_row, 1)], sem)


def _dispatch_kernel(d1_ref, d2_ref, cnt_ref, h_ref, xs_ref, zero_ref, sem, tail_sem, *, tm, cap, nexp, tail):
    def issue(r, c):
        _row_copy(h_ref, r, xs_ref, d1_ref[0, 0, r], sem).start()
        _row_copy(h_ref, r, xs_ref, d2_ref[0, 0, r], sem).start(priority=1)
        return c

    lax.fori_loop(0, tm, issue, 0, unroll=8)

    def aligned_end(e):
        return ((cnt_ref[e] + (SUBLANES - 1)) // SUBLANES) * SUBLANES

    def tail_copy(e, k):
        row0 = pl.multiple_of(e * cap + aligned_end(e) + k * tm, SUBLANES)
        return pltpu.make_async_copy(zero_ref, xs_ref.at[pl.ds(row0, tm)], tail_sem)

    def tail_rows(e, fn):
        for k in range(SUBLANES - 1):
            @pl.when(cnt_ref[e] + k < aligned_end(e))
            def _():
                fn(_row_copy(zero_ref, 0, xs_ref, e * cap + cnt_ref[e] + k, tail_sem))

    @pl.when(pl.program_id(0) == pl.num_programs(0) - 1)
    def _():
        zero_ref[...] = jnp.zeros_like(zero_ref)
        for e in range(nexp):
            for k in range(tail // tm):
                tail_copy(e, k).start()
            tail_rows(e, lambda cp: cp.start())
        for e in range(nexp):
            for k in range(tail // tm):
                tail_copy(e, k).wait()
            tail_rows(e, lambda cp: cp.wait())

    for _ in range(2):
        pltpu.make_async_copy(h_ref, xs_ref.at[pl.ds(0, tm)], sem).wait()


def _dispatch(d1, d2, cnt, h, cap, tail):
    n, wd = h.shape
    tm = TM_PROJ
    nt = n // tm
    idx_spec = pl.BlockSpec((1, 1, tm), lambda i: (i, 0, 0), memory_space=pltpu.SMEM)
    return pl.pallas_call(
        functools.partial(_dispatch_kernel, tm=tm, cap=cap, nexp=N_EXPERTS, tail=tail),
        out_shape=jax.ShapeDtypeStruct((N_EXPERTS * cap, wd), h.dtype),
        grid=(nt,),
        in_specs=[idx_spec, idx_spec, pl.BlockSpec(memory_space=pltpu.SMEM),
                  pl.BlockSpec((tm, wd), lambda i: (i, 0))],
        out_specs=pl.BlockSpec(memory_space=pl.ANY),
        scratch_shapes=[pltpu.VMEM((tm, wd), h.dtype), pltpu.SemaphoreType.DMA, pltpu.SemaphoreType.DMA],
        compiler_params=pltpu.CompilerParams(dimension_semantics=("arbitrary",),
                                             vmem_limit_bytes=VMEM_LIMIT_BYTES, disable_bounds_checks=True),
        name="moe_dispatch",
    )(d1.reshape(nt, 1, tm), d2.reshape(nt, 1, tm), cnt, h)


def _combine_kernel(d1_ref, d2_ref, n1_ref, n2_ref, x_ref, route_ref, ys_ref, o_ref, y1_ref, y2_ref, sems, *, tm):
    i = pl.program_id(0)
    slot = jnp.bitwise_and(i, 1)

    def fetch(a_ref, b_ref, s):
        def issue(r, c):
            _row_copy(ys_ref, a_ref[0, 0, r], y1_ref.at[s], r, sems.at[s]).start()
            _row_copy(ys_ref, b_ref[0, 0, r], y2_ref.at[s], r, sems.at[s]).start(priority=1)
            return c

        lax.fori_loop(0, tm, issue, 0, unroll=8)

    @pl.when(i == 0)
    def _():
        fetch(d1_ref, d2_ref, slot)

    @pl.when(i + 1 < pl.num_programs(0))
    def _():
        fetch(n1_ref, n2_ref, 1 - slot)

    for buf in (y1_ref, y2_ref):
        pltpu.make_async_copy(ys_ref.at[pl.ds(0, tm)], buf.at[slot], sems.at[slot]).wait()
    o_ref[...] = x_ref[...] + route_ref[:, 2:3] * y1_ref[slot] + route_ref[:, 3:4] * y2_ref[slot]


def _combine(d1, d2, x2, route, ys):
    n, dm = x2.shape
    tm = TM_PROJ
    nt = n // tm
    idx_spec = pl.BlockSpec((1, 1, tm), lambda i: (i, 0, 0), memory_space=pltpu.SMEM)
    next_spec = pl.BlockSpec((1, 1, tm), lambda i: (jnp.minimum(i + 1, nt - 1), 0, 0), memory_space=pltpu.SMEM)
    d1, d2 = d1.reshape(nt, 1, tm), d2.reshape(nt, 1, tm)
    return pl.pallas_call(
        functools.partial(_combine_kernel, tm=tm),
        out_shape=jax.ShapeDtypeStruct((n, dm), F32),
        grid=(nt,),
        in_specs=[idx_spec, idx_spec, next_spec, next_spec, pl.BlockSpec((tm, dm), lambda i: (i, 0)),
                  pl.BlockSpec((tm, LANES), lambda i: (i, 0)), pl.BlockSpec(memory_space=pl.ANY)],
        out_specs=pl.BlockSpec((tm, dm), lambda i: (i, 0)),
        scratch_shapes=[pltpu.VMEM((2, tm, dm), F32), pltpu.VMEM((2, tm, dm), F32), pltpu.SemaphoreType.DMA((2,))],
        compiler_params=pltpu.CompilerParams(dimension_semantics=("arbitrary",),
                                             vmem_limit_bytes=VMEM_LIMIT_BYTES, disable_bounds_checks=True),
        name="moe_combine",
    )(d1, d2, d1, d2, x2, route, ys)


def _experts_kernel(te_ref, blk_ref, nused_ref, xs_ref, w1_ref, w3_ref, w2_ref, o_ref, x_ref, hid_ref, acc_ref):
    t = pl.program_id(0)
    c = pl.program_id(1)
    nc = pl.num_programs(1)

    @pl.when(jnp.logical_and(t < nused_ref[0], c == 0))
    def _():
        w = xs_ref[...]
        half = w.shape[1]
        x_ref[:, :half] = lax.bitcast_convert_type(w & jnp.uint32(0xFFFF0000), F32).astype(BF16)
        x_ref[:, half:] = lax.bitcast_convert_type(lax.shift_left(w, jnp.uint32(16)), F32).astype(BF16)

    @pl.when(t < nused_ref[0])
    def _():
        x = x_ref[...]
        tf = w1_ref.shape[2]
        for s0 in range(0, tf, GROUP_W):
            cs = slice(s0, s0 + GROUP_W)
            a = _dot(x, w1_ref[0, :, cs])
            b = _dot(x, w3_ref[0, :, cs])
            hid_ref[:, cs] = (jax.nn.silu(a) * b).astype(BF16)
        part = _dot(hid_ref[...], w2_ref[0])

        @pl.when(c == 0)
        def _():
            acc_ref[...] = part

        @pl.when(c > 0)
        def _():
            acc_ref[...] = acc_ref[...] + part

        @pl.when(c == nc - 1)
        def _():
            o_ref[...] = acc_ref[...]

    @pl.when(jnp.logical_and(t >= nused_ref[0], c == nc - 1))
    def _():
        o_ref[...] = jnp.zeros_like(o_ref)


def _experts(tile_expert, tile_block, nused, xs, w1, w3, w2, nt):
    rows, wd = xs.shape
    dm = 2 * wd
    dff = w1.shape[2]
    tm, tf = TM_EXPERT, TF_EXPERT
    nc = dff // tf
    spare = rows // tm

    def cc(t, c, nu):
        return jnp.where(t < nu[0], c, nc - 1)

    grid_spec = pltpu.PrefetchScalarGridSpec(
        num_scalar_prefetch=3,
        grid=(nt, nc),
        in_specs=[
            pl.BlockSpec((tm, wd), lambda t, c, te, tb, nu: (tb[t], 0)),
            pl.BlockSpec((1, dm, tf), lambda t, c, te, tb, nu: (te[t], 0, cc(t, c, nu))),
            pl.BlockSpec((1, dm, tf), lambda t, c, te, tb, nu: (te[t], 0, cc(t, c, nu))),
            pl.BlockSpec((1, tf, dm), lambda t, c, te, tb, nu: (te[t], cc(t, c, nu), 0)),
        ],
        out_specs=pl.BlockSpec((tm, dm), lambda t, c, te, tb, nu: (jnp.where(t < nu[0], tb[t], spare), 0)),
        scratch_shapes=[pltpu.VMEM((tm, dm), BF16), pltpu.VMEM((tm, tf), BF16), pltpu.VMEM((tm, dm), F32)],
    )
    return pl.pallas_call(
        _experts_kernel,
        out_shape=jax.ShapeDtypeStruct((rows + tm, dm), F32),
        grid_spec=grid_spec,
        compiler_params=_cparams(("arbitrary", "arbitrary")),
        name="moe_experts",
    )(tile_expert, tile_block, nused, xs, w1, w3, w2)


def _moe(x2, g, w_router, w1, w3, w2, layer_idx, tri):
    n, dm = x2.shape
    wr = jnp.zeros((dm, LANES), F32).at[:, :N_EXPERTS].set(w_router.astype(F32))
    wr_hi = wr.astype(BF16)
    wr_lo = (wr - wr_hi.astype(F32)).astype(BF16)
    h, route, counts = _router(x2, g, jnp.stack([wr_hi, wr_lo]), tri)

    tm = TM_EXPERT
    assert n % tm == 0 and tm % TM_PROJ == 0
    cap = n + tm
    d1 = route[:, 0].astype(jnp.int32) * cap + route[:, 4].astype(jnp.int32)
    d2 = route[:, 1].astype(jnp.int32) * cap + route[:, 5].astype(jnp.int32)
    cnt = counts[0, :N_EXPERTS].astype(jnp.int32)

    nt = 2 * n // tm + N_EXPERTS
    ntile = (cnt + tm - 1) // tm
    ends = jnp.cumsum(ntile)
    nused = ends[-1]
    t = jnp.minimum(jnp.arange(nt, dtype=jnp.int32), nused - 1)
    tile_expert = jnp.sum(t[:, None] >= ends[None, :], axis=1).astype(jnp.int32)
    tile_block = tile_expert * (cap // tm) + t - (ends - ntile)[tile_expert]

    xs = _dispatch(d1, d2, cnt, h, cap, tm)
    ys = _experts(tile_expert + layer_idx * N_EXPERTS, tile_block.astype(jnp.int32),
                  nused.reshape(1).astype(jnp.int32), xs, w1, w3, w2, nt)
    return _combine(d1, d2, x2, route, ys)


def kernel(x, norm_mix_g, w_in, ssm_lam_re, ssm_lam_im, ssm_log_dt, ssm_b_re, ssm_b_im, ssm_c_re, ssm_c_im,
           ssm_d, ssm_w_glu, ssm_b_glu, ch_q_norm_g, ch_k_norm_g, ch_rel_bias, df_q_norm_g, df_k_norm_g,
           df_lambda, out_norm_g, w_out, norm_ffn_g, ffn_w1, ffn_w3, ffn_w2, moe_router, moe_w1, moe_w3, moe_w2):
    bsz, seq, dm = x.shape
    depth = w_in.shape[0]
    n = bsz * seq
    nheads = GROUP_W // HEAD_DIM
    ind64 = _block_indicator(GROUP_W, HEAD_DIM)
    ind32 = _block_indicator(GROUP_W, DF_QK_DIM)
    tri = (jnp.arange(TM_PROJ)[:, None] > jnp.arange(TM_PROJ)[None, :]).astype(BF16)
    slopes = [2.0 ** (-8.0 * (h + 1) / nheads) for h in range(nheads)]
    slopes2 = tuple(s * LOG2E for s in slopes)
    df_kb, df_db = _df_bias_tiles(slopes2, TQ_ATT, TK_ATT)
    later = (jnp.arange(TK_ATT)[:, None] < jnp.arange(TK_ATT)[None, :]).astype(BF16)

    moe_w = tuple(w.reshape((-1,) + w.shape[2:]).astype(BF16) for w in (moe_w1, moe_w3, moe_w2))

    x2 = x.reshape(n, dm)
    for layer in range(depth):
        qkg = jnp.stack([
            jnp.tile(ch_q_norm_g[layer].astype(F32), nheads) * (HEAD_DIM ** -0.5 * LOG2E),
            jnp.tile(ch_k_norm_g[layer].astype(F32), nheads),
            jnp.tile(df_q_norm_g[layer].astype(F32).reshape(-1), nheads) * (DF_QK_DIM ** -0.5 * LOG2E),
            jnp.tile(df_k_norm_g[layer].astype(F32).reshape(-1), nheads),
        ])
        u_tm, qk, v_sb, v_ch, v_df = _in_proj(x2, norm_mix_g[layer].reshape(1, dm), w_in[layer].astype(BF16), qkg,
                             ind64, ind32, bsz, seq)

        wb, lre, lim, wc = _s5_params(ssm_lam_re[layer], ssm_lam_im[layer], ssm_log_dt[layer],
                                      ssm_b_re[layer], ssm_b_im[layer], ssm_c_re[layer], ssm_c_im[layer], bsz)
        y_ssm = _s5(u_tm.reshape(seq * bsz, GROUP_W), wb, lre, lim, wc, ssm_d[layer].reshape(1, GROUP_W),
                    ssm_w_glu[layer].astype(BF16), ssm_b_glu[layer].reshape(1, GROUP_W), bsz)
        y_ssm = y_ssm.reshape(seq, bsz * GROUP_W)

        tq, tk = TQ_ATT, TK_ATT
        const3 = lambda b, i: (0, 0, 0)
        y_sb = _attention_call(
            functools.partial(_sb_kernel, tq=tq, tk=tk, nheads=nheads),
            qk, 0, v_sb,
            (later,), [pl.BlockSpec((tk, tk), lambda b, i: (0, 0))],
            [pltpu.VMEM((nheads * tq, GROUP_W), BF16), pltpu.VMEM((SUBLANES, tq), F32),
             pltpu.VMEM((GROUP_W, tq), F32),
             pltpu.VMEM((tk, nheads * tq), F32), pltpu.VMEM((tk, nheads * tq), F32)],
            bsz, seq, "sb_attention")

        bias, nback = _ch_bias_tiles(ch_rel_bias[layer], tq, tk)
        y_ch = _attention_call(
            functools.partial(_ch_kernel, tq=tq, tk=tk, nheads=nheads, nback=nback),
            qk, 2, v_ch,
            (bias,), [pl.BlockSpec(bias.shape, lambda b, i: (0, 0, 0, 0))],
            [pltpu.VMEM((nheads * tq, GROUP_W), BF16), pltpu.VMEM(((nback + 1) * tk, nheads * tq), F32)],
            bsz, seq, "ch_attention")

        lambda_init = 0.8 - 0.6 * math.exp(-0.3 * layer)
        lam_p = df_lambda[layer].astype(F32)
        lam = jnp.exp(jnp.sum(lam_p[0] * lam_p[1])) - jnp.exp(jnp.sum(lam_p[2] * lam_p[3])) + lambda_init
        y_df = _attention_call(
            functools.partial(_df_kernel, tq=tq, tk=tk, nheads=nheads, slopes2=slopes2),
            qk, 4, v_df,
            (df_kb, df_db), [pl.BlockSpec(df_kb.shape, const3), pl.BlockSpec(df_db.shape, const3)],
            [pltpu.VMEM((2 * nheads * tq, GROUP_W), BF16), pltpu.VMEM((2 * nheads, tq), F32),
             pltpu.VMEM((2 * nheads * VA_ROWS, tq), F32),
             pltpu.VMEM((tk, 2 * nheads * tq), F32), pltpu.VMEM((tk, 2 * nheads * tq), F32),
             pltpu.VMEM((2 * nheads, tq), F32), pltpu.VMEM((2 * nheads, tq), F32)],
            bsz, seq, "df_attention", smem=(lam.reshape(1),))

        head_scale = jnp.concatenate([jnp.ones((dm - GROUP_W,), F32),
                                      jnp.full((GROUP_W,), 1.0 - lambda_init, F32)])
        g_out = (out_norm_g[layer].astype(F32) * head_scale).reshape(1, dm)
        x2 = _out_proj(y_ssm, y_sb, y_ch, y_df, g_out, ind64, w_out[layer].astype(BF16), x2, bsz, seq)

        idx = layer // 2
        g_ffn = norm_ffn_g[layer].reshape(1, dm)
        if layer % 2 == 0:
            x2 = _ffn(x2, g_ffn, ffn_w1[idx].astype(BF16), ffn_w3[idx].astype(BF16), ffn_w2[idx].astype(BF16))
        else:
            x2 = _moe(x2, g_ffn, moe_router[idx], *moe_w, idx, tri)
    return x2.reshape(bsz, seq, dm)
```

```python
import functools
import math

import jax
import jax.numpy as jnp
import numpy as np
from jax import lax
from jax.experimental import pallas as pl
from jax.experimental.pallas import tpu as pltpu

F32 = jnp.float32
BF16 = jnp.bfloat16

EPS = 1e-6
NEG_INF = -1e30
HEAD_DIM = 64
GROUP_W = 256
CHUNK = 64
SSM_GROUP = 16
SSM_STATE = 64
DF_QK_DIM = 32
CH_LEFT_CHUNKS = 8
REL_CLIP = 128
N_EXPERTS = 8
LOG2E = 1.4426950408889634

VMEM_LIMIT_BYTES = 56 * 1024 * 1024
SUBLANES = 8
LANES = 128

TM_PROJ = 512
TQ_ATT = 256
TK_ATT = 256
SSM_STEPS = 64
TM_EXPERT = 512
TF_EXPERT = 1792


def _cparams(sem):
    return pltpu.CompilerParams(dimension_semantics=sem, vmem_limit_bytes=VMEM_LIMIT_BYTES)


def _dot(a, b):
    return jnp.dot(a, b, preferred_element_type=F32)


def _dot_nt(a, b):
    return lax.dot_general(a, b, (((1,), (1,)), ((), ())), preferred_element_type=F32)


def _group_ms(v, ind, inv_size):
    return _dot((v * v).astype(BF16), ind) * inv_size


def _block_indicator(width, group):
    r = jnp.arange(width)[:, None] // group
    c = jnp.arange(width)[None, :] // group
    return (r == c).astype(BF16)


def _in_proj_kernel(x_ref, g_ref, w_ref, qkg_ref, ind64_ref, ind32_ref, u_ref, qk_ref, vsb_ref, vch_ref, vdf_ref,
                    *, tk, nheads):
    x = x_ref[...]
    ms = jnp.mean(x * x, axis=-1, keepdims=True)
    h = (x * lax.rsqrt(ms + EPS) * g_ref[...]).astype(BF16)

    def cols(c):
        return _dot(h, w_ref[:, c * GROUP_W:(c + 1) * GROUP_W])

    def put(slot, val):
        qk_ref[:, slot * GROUP_W:(slot + 1) * GROUP_W] = val.astype(BF16)

    def put_values(v_ref, v, ones_row):
        for kt in range(v.shape[0] // tk):
            vt = v[kt * tk:(kt + 1) * tk, :].T
            if not ones_row:
                v_ref[kt] = vt.astype(BF16)
                continue
            pad = jnp.where(_iota2((VA_ROWS - HEAD_DIM, tk), 0) == 0, 1.0, 0.0).astype(BF16)
            for hd in range(nheads):
                v_ref[kt, hd * VA_ROWS:hd * VA_ROWS + HEAD_DIM, :] = vt[hd * HEAD_DIM:(hd + 1) * HEAD_DIM, :].astype(BF16)
                v_ref[kt, hd * VA_ROWS + HEAD_DIM:(hd + 1) * VA_ROWS, :] = pad

    u_ref[...] = cols(0)
    put(0, cols(1) * (HEAD_DIM ** -0.5 * LOG2E))
    put(1, cols(2))
    put_values(vsb_ref, cols(3), False)
    for c, gi in ((4, 0), (5, 1)):
        a = cols(c)
        ms_h = _group_ms(a, ind64_ref[...], 1.0 / HEAD_DIM)
        put(c - 2, a * lax.rsqrt(ms_h + EPS) * qkg_ref[gi:gi + 1, :])
    put_values(vch_ref, cols(6), True)
    for c, gi in ((7, 2), (8, 3)):
        a = cols(c)
        ms_h = _group_ms(a, ind32_ref[...], 1.0 / DF_QK_DIM)
        put(c - 3, a * lax.rsqrt(ms_h + EPS) * qkg_ref[gi:gi + 1, :])
    put_values(vdf_ref, cols(9), True)


def _in_proj(x2, g, w, qkg, ind64, ind32, bsz, seq):
    n, dm = x2.shape
    tm, tk = TM_PROJ, TK_ATT
    ns = seq // tm
    nheads = GROUP_W // HEAD_DIM
    const = lambda i: (0, 0)

    def values(rows):
        return (jax.ShapeDtypeStruct((bsz, seq // tk, rows, tk), BF16),
                pl.BlockSpec((None, tm // tk, rows, tk), lambda i: (i // ns, i % ns, 0, 0)))

    v_shapes, v_specs = zip(values(GROUP_W), values(nheads * VA_ROWS), values(nheads * VA_ROWS))
    return pl.pallas_call(
        functools.partial(_in_proj_kernel, tk=tk, nheads=nheads),
        out_shape=(jax.ShapeDtypeStruct((seq, bsz * GROUP_W), F32),
                   jax.ShapeDtypeStruct((n, 6 * GROUP_W), BF16)) + v_shapes,
        grid=(n // tm,),
        in_specs=[
            pl.BlockSpec((tm, dm), lambda i: (i, 0)),
            pl.BlockSpec((1, dm), const),
            pl.BlockSpec(w.shape, const),
            pl.BlockSpec((4, GROUP_W), const),
            pl.BlockSpec((GROUP_W, GROUP_W), const),
            pl.BlockSpec((GROUP_W, GROUP_W), const),
        ],
        out_specs=(
            pl.BlockSpec((tm, GROUP_W), lambda i: (i % ns, i // ns)),
            pl.BlockSpec((tm, 6 * GROUP_W), lambda i: (i, 0)),
        ) + v_specs,
        compiler_params=_cparams(("arbitrary",)),
        name="in_proj",
    )(x2, g, w, qkg, ind64, ind32)


def _s5_kernel(u_ref, wb_ref, lre_ref, lim_ref, wc_ref, d_ref, wg_ref, bg_ref, y_ref, bu_ref, st_ref, rows_ref,
               *, steps, nstate):
    @pl.when(pl.program_id(0) == 0)
    def _():
        st_ref[...] = jnp.zeros_like(st_ref)

    nhalf, bsz = rows_ref.shape[0], rows_ref.shape[1] // steps
    w = nhalf * LANES
    for b in range(bsz):
        for k in range(nhalf):
            c0 = b * w + k * LANES
            rows_ref[k, pl.ds(b, steps, stride=bsz), :] = u_ref[:, c0:c0 + LANES]
    u = jnp.concatenate([rows_ref[k] for k in range(nhalf)], axis=1)
    bu_ref[...] = _dot(u.astype(BF16), wb_ref[...])
    lre = lre_ref[...]
    lim = lim_ref[...]

    def step(t, carry):
        sre, sim = carry
        r = pl.multiple_of(t * SUBLANES, SUBLANES)
        bre = bu_ref[pl.ds(r, SUBLANES), 0:nstate]
        bim = bu_ref[pl.ds(r, SUBLANES), nstate:2 * nstate]
        nre = lre * sre - lim * sim + bre
        nim = lre * sim + lim * sre + bim
        bu_ref[pl.ds(r, SUBLANES), 0:nstate] = nre
        bu_ref[pl.ds(r, SUBLANES), nstate:2 * nstate] = nim
        return nre, nim

    sre, sim = lax.fori_loop(0, steps, step, (st_ref[:, 0:nstate], st_ref[:, nstate:2 * nstate]))
    st_ref[:, 0:nstate] = sre
    st_ref[:, nstate:2 * nstate] = sim

    y = _dot(bu_ref[...].astype(BF16), wc_ref[...]) + d_ref[...] * u
    y = jax.nn.gelu(y)
    gate = jax.nn.sigmoid(_dot(y.astype(BF16), wg_ref[...]) + bg_ref[...])
    out = y * gate
    for k in range(nhalf):
        rows_ref[k] = out[:, k * LANES:(k + 1) * LANES]
    for b in range(bsz):
        for k in range(nhalf):
            c0 = b * w + k * LANES
            y_ref[:, c0:c0 + LANES] = rows_ref[k, pl.ds(b, steps, stride=bsz), :]


def _s5(u_tm, wb, lre, lim, wc, d, wg, bg, bsz):
    seq = u_tm.shape[0]
    w = u_tm.shape[1] // bsz
    assert bsz == SUBLANES
    blk = SSM_STEPS * bsz
    nstate2 = wb.shape[1]
    const = lambda c: (0, 0)
    return pl.pallas_call(
        functools.partial(_s5_kernel, steps=SSM_STEPS, nstate=nstate2 // 2),
        out_shape=jax.ShapeDtypeStruct(u_tm.shape, F32),
        grid=(seq // SSM_STEPS,),
        in_specs=[
            pl.BlockSpec((SSM_STEPS, bsz * w), lambda c: (c, 0)),
            pl.BlockSpec((w, nstate2), const),
            pl.BlockSpec((bsz, nstate2 // 2), const),
            pl.BlockSpec((bsz, nstate2 // 2), const),
            pl.BlockSpec((nstate2, w), const),
            pl.BlockSpec((1, w), const),
            pl.BlockSpec((w, w), const),
            pl.BlockSpec((1, w), const),
        ],
        out_specs=pl.BlockSpec((SSM_STEPS, bsz * w), lambda c: (c, 0)),
        scratch_shapes=[pltpu.VMEM((blk, nstate2), F32), pltpu.VMEM((bsz, nstate2), F32),
                        pltpu.VMEM((w // LANES, blk, LANES), F32)],
        compiler_params=_cparams(("arbitrary",)),
        name="s5_mixer",
    )(u_tm, wb, lre, lim, wc, d, wg, bg)


def _s5_params(lam_re, lam_im, log_dt, b_re, b_im, c_re, c_im, bsz):
    g, n = lam_re.shape
    dt = jnp.exp(log_dt.astype(F32))[:, None]
    xr = lam_re * dt
    th = lam_im * dt
    er = jnp.exp(xr)
    lbr = er * jnp.cos(th)
    lbi = er * jnp.sin(th)
    ar = jnp.expm1(xr) * jnp.cos(th) - 2.0 * jnp.sin(0.5 * th) ** 2
    ai = lbi
    den = lam_re * lam_re + lam_im * lam_im
    fr = (ar * lam_re + ai * lam_im) / den
    fi = (ai * lam_re - ar * lam_im) / den
    bbr = fr[..., None] * b_re - fi[..., None] * b_im
    bbi = fr[..., None] * b_im + fi[..., None] * b_re
    eye = jnp.eye(g, dtype=F32)
    p = b_re.shape[-1]
    wb_re = jnp.einsum("gnp,gh->gphn", bbr, eye).reshape(g * p, g * n)
    wb_im = jnp.einsum("gnp,gh->gphn", bbi, eye).reshape(g * p, g * n)
    wb = jnp.concatenate([wb_re, wb_im], axis=1).astype(BF16)
    wc_re = jnp.einsum("gpn,gh->gnhp", c_re, eye).reshape(g * n, g * p)
    wc_im = jnp.einsum("gpn,gh->gnhp", c_im, eye).reshape(g * n, g * p)
    wc = jnp.concatenate([wc_re, -wc_im], axis=0).astype(BF16)
    lre = jnp.broadcast_to(lbr.reshape(1, g * n), (bsz, g * n))
    lim = jnp.broadcast_to(lbi.reshape(1, g * n), (bsz, g * n))
    return wb, lre, lim, wc


VA_ROWS = 2 * HEAD_DIM


def _iota2(shape, dim):
    return lax.broadcasted_iota(jnp.int32, shape, dim)


def _stack_masked_q(q_ref, qs_ref, tq, ngroups, width):
    lane = _iota2((1, GROUP_W), 1)
    q = q_ref[...]
    for g in range(ngroups):
        keep = (lane >= g * width) & (lane < (g + 1) * width)
        qs_ref[g * tq:(g + 1) * tq, :] = jnp.where(keep, q, jnp.zeros_like(q))


def _key_scores(k_ref, qs_ref, j, tk):
    row0 = pl.multiple_of(jnp.maximum(j, 0) * tk, tk)
    return _dot_nt(k_ref[pl.ds(row0, tk), :], qs_ref[...])


def _score_prefetch(k_ref, qs_ref, j, dst_ref, tq, tk, bias_of=None, max_ref=None):
    row0 = pl.multiple_of(jnp.maximum(j, 0) * tk, tk)

    def emit(g):
        cols = slice(g * tq, (g + 1) * tq)
        s = _dot_nt(k_ref[pl.ds(row0, tk), :], qs_ref[cols, :])
        if bias_of is not None:
            s = s + bias_of(g)[...]
        dst_ref[:, cols] = s
        if max_ref is not None:
            max_ref[g:g + 1, :] = jnp.max(s, axis=0, keepdims=True)

    return emit


def _paired_key_tiles(i, k_ref, qs_ref, tile_fn, buf_a, buf_b, tq, tk, bias_of=None):
    def body(n, c):
        ja = i - 1 - 2 * n
        tile_fn(ja, buf_a, _score_prefetch(k_ref, qs_ref, ja - 1, buf_b[0], tq, tk, bias_of, buf_b[1]))
        tile_fn(ja - 1, buf_b, _score_prefetch(k_ref, qs_ref, ja - 2, buf_a[0], tq, tk, bias_of, buf_a[1]))
        return c

    lax.fori_loop(0, i // 2, body, 0)

    @pl.when(jnp.bitwise_and(i, 1) == 1)
    def _():
        tile_fn(0, buf_a, lambda g: None)


def _sb_kernel(q_ref, k_ref, vt_ref, later_ref, o_ref, qs_ref, carry_ref, acc_ref, za_ref, zb_ref,
               *, tq, tk, nheads):
    i = pl.program_id(1)
    _stack_masked_q(q_ref, qs_ref, tq, nheads, HEAD_DIM)

    def tile(j, buf, prefetch, diagonal=False):
        zt = buf[0]
        if diagonal:
            before = _iota2((tk, tq), 0) < _iota2((tk, tq), 1)

        def scan_stage(h):
            z = zt[:, h * tq:(h + 1) * tq]
            neg_abs = lax.bitcast_convert_type(
                lax.bitcast_convert_type(z, jnp.uint32) | jnp.uint32(0x80000000), F32)
            sp = jnp.maximum(z, 0.0) + jnp.log2(1.0 + jnp.exp2(neg_abs))
            nl1m = jnp.where(before, sp, 0.0) if diagonal else sp
            inside = _dot(later_ref[...], nl1m.astype(BF16))
            return z - sp, inside, nl1m[0:1, :]

        def value_stage(h, logit, inside, first_row):
            hs = slice(h * HEAD_DIM, (h + 1) * HEAD_DIM)
            between = inside if diagonal else inside + carry_ref[h:h + 1, :]
            w = jnp.exp2(logit - between)
            if diagonal:
                w = jnp.where(before, w, 0.0)
            pv = _dot(vt_ref[j, hs, :], w.astype(BF16))
            total = inside[0:1, :] + first_row
            if diagonal:
                acc_ref[hs, :] = pv
                carry_ref[h:h + 1, :] = total
            else:
                acc_ref[hs, :] = acc_ref[hs, :] + pv
                carry_ref[h:h + 1, :] = carry_ref[h:h + 1, :] + total

        prefetch(0)
        pending = scan_stage(0)
        for h in range(1, nheads):
            prefetch(h)
            upcoming = scan_stage(h)
            value_stage(h - 1, *pending)
            pending = upcoming
        value_stage(nheads - 1, *pending)

    tile(i, (_key_scores(k_ref, qs_ref, i, tk), None), _score_prefetch(k_ref, qs_ref, i - 1, za_ref, tq, tk),
         diagonal=True)
    _paired_key_tiles(i, k_ref, qs_ref, tile, (za_ref, None), (zb_ref, None), tq, tk)
    o_ref[...] = acc_ref[...].T.astype(o_ref.dtype)


def _softmax_tile(buf, tq, off, va, g, first, m_ref, acc_ref):
    st_ref, max_ref = buf
    rows = slice(g * VA_ROWS, (g + 1) * VA_ROWS)
    parts, alphas = [], []
    for c0 in range(0, tq, LANES):
        cs = slice(c0, c0 + LANES)
        sc = slice(g * tq + c0, g * tq + c0 + LANES)
        blk_max = max_ref[g:g + 1, cs]
        if first:
            m_new = blk_max
            shift = m_new
        else:
            m_old = m_ref[g:g + 1, cs]
            m_new = jnp.maximum(m_old, blk_max + off)
            shift = m_new - off
            alphas.append(jnp.exp2(m_old - m_new))
        parts.append(jnp.exp2(st_ref[:, sc] - shift).astype(BF16))
        m_ref[g:g + 1, cs] = m_new
    pv = _dot(va, jnp.concatenate(parts, axis=1))
    if first:
        acc_ref[rows, :] = pv
    else:
        acc_ref[rows, :] = jnp.concatenate(alphas, axis=1) * acc_ref[rows, :] + pv


def _normalised(acc_ref, g):
    base = g * VA_ROWS
    return acc_ref[base:base + HEAD_DIM, :] / acc_ref[base + HEAD_DIM:base + HEAD_DIM + 1, :]


def _ch_kernel(q_ref, k_ref, va_ref, bias_ref, o_ref, qs_ref, z_ref, *, tq, tk, nheads, nback):
    i = pl.program_id(1)
    _stack_masked_q(q_ref, qs_ref, tq, nheads, HEAD_DIM)

    def attend(ntiles):
        tile_max = [[] for _ in range(nheads)]
        for n in range(ntiles):
            row0 = pl.multiple_of((i - n) * tk, tk)
            for h in range(nheads):
                cols = slice(h * tq, (h + 1) * tq)
                s = _dot_nt(k_ref[pl.ds(row0, tk), :], qs_ref[cols, :]) + bias_ref[h, n]
                z_ref[n * tk:(n + 1) * tk, cols] = s
                tile_max[h].append(jnp.max(s, axis=0, keepdims=True))
        outs = []
        for h in range(nheads):
            m = functools.reduce(jnp.maximum, tile_max[h])
            acc = None
            for n in range(ntiles):
                p = jnp.exp2(z_ref[n * tk:(n + 1) * tk, h * tq:(h + 1) * tq] - m).astype(BF16)
                pv = _dot(va_ref[i - n, h * VA_ROWS:(h + 1) * VA_ROWS, :], p)
                acc = pv if acc is None else acc + pv
            outs.append(acc[:HEAD_DIM, :] / acc[HEAD_DIM:HEAD_DIM + 1, :])
        o_ref[...] = jnp.concatenate(outs, axis=0).T.astype(o_ref.dtype)

    for ntiles in range(1, nback + 2):
        pl.when(jnp.minimum(i, nback) + 1 == ntiles)(functools.partial(attend, ntiles))


def _df_kernel(lam_ref, q_ref, k_ref, va_ref, kb_ref, db_ref, o_ref, qs_ref, m_ref, acc_ref,
               za_ref, zb_ref, maxa_ref, maxb_ref, *, tq, tk, nheads, slopes2):
    i = pl.program_id(1)
    _stack_masked_q(q_ref, qs_ref, tq, 2 * nheads, DF_QK_DIM)
    buf_a, buf_b = (za_ref, maxa_ref), (zb_ref, maxb_ref)

    def past_bias(g):
        return kb_ref.at[g // 2]

    def tile(j, buf, prefetch, diagonal=False):
        for g in range(2 * nheads):
            prefetch(g)
            h = g // 2
            va = va_ref[j, h * VA_ROWS:(h + 1) * VA_ROWS, :]
            if diagonal:
                _softmax_tile(buf, tq, 0.0, va, g, True, m_ref, acc_ref)
            else:
                _softmax_tile(buf, tq, slopes2[h] * ((j - i) * tk).astype(F32), va, g, False, m_ref, acc_ref)

    diag = _score_prefetch(k_ref, qs_ref, i, zb_ref, tq, tk, lambda g: db_ref.at[g // 2], maxb_ref)
    for g in range(2 * nheads):
        diag(g)
    tile(i, buf_b, _score_prefetch(k_ref, qs_ref, i - 1, za_ref, tq, tk, past_bias, maxa_ref), diagonal=True)
    _paired_key_tiles(i, k_ref, qs_ref, tile, buf_a, buf_b, tq, tk, past_bias)
    lam = lam_ref[0]
    out = jnp.concatenate([_normalised(acc_ref, 2 * h) - lam * _normalised(acc_ref, 2 * h + 1)
                           for h in range(nheads)], axis=0)
    o_ref[...] = out.T.astype(o_ref.dtype)


def _attention_call(kernel, qk, col0, vt, extra, extra_specs, scratch, bsz, seq, name, smem=()):
    tq, tk = TQ_ATT, TK_ATT
    assert tq == tk
    nq = seq // tq
    vrows = vt.shape[2]
    return pl.pallas_call(
        kernel,
        out_shape=jax.ShapeDtypeStruct((bsz * seq, GROUP_W), BF16),
        grid=(bsz, nq),
        in_specs=[pl.BlockSpec(memory_space=pltpu.SMEM) for _ in smem] + [
            pl.BlockSpec((tq, GROUP_W), lambda b, i: (b * nq + i, col0)),
            pl.BlockSpec((seq, GROUP_W), lambda b, i: (b, col0 + 1)),
            pl.BlockSpec((None, seq // tk, vrows, tk), lambda b, i: (b, 0, 0, 0)),
        ] + extra_specs,
        out_specs=pl.BlockSpec((tq, GROUP_W), lambda b, i: (b * nq + i, 0)),
        scratch_shapes=scratch,
        compiler_params=_cparams(("arbitrary", "arbitrary")),
        name=name,
    )(*smem, qk, qk, vt, *extra)


def _ch_bias_tiles(rel_bias, tq, tk):
    assert tq == tk
    nback = CH_LEFT_CHUNKS * CHUNK // tk
    nh = rel_bias.shape[0]
    n = np.arange(nback + 1)[:, None, None]
    s = np.arange(tk)[None, :, None]
    t = np.arange(tq)[None, None, :]
    delta = (n * tk + t) // CHUNK - s // CHUNK
    valid = (delta >= 0) & (delta <= CH_LEFT_CHUNKS)
    table = rel_bias.astype(F32) * LOG2E
    lo, hi = -(tk - 1), nback * tk + tq - 1
    ext = jnp.concatenate([jnp.repeat(table[:, :1], -REL_CLIP - lo, axis=1), table,
                           jnp.repeat(table[:, -1:], hi - REL_CLIP, axis=1)], axis=1)
    period = tk + tq
    tiles = []
    for m in range(nback + 1):
        w = ext[:, m * tk:m * tk + period - 1]
        w = jnp.concatenate([w, jnp.zeros((nh, 1), F32)], axis=1)
        skew = jnp.tile(w, (1, tk))[:, :tk * (period - 1)].reshape(nh, tk, period - 1)
        tiles.append(skew[:, :, tk - 1:tk - 1 + tq])
    bias = jnp.stack(tiles, axis=1)
    return jnp.where(jnp.asarray(valid)[None], bias, NEG_INF), nback


def _df_bias_tiles(slopes2, tq, tk):
    s = jnp.arange(tk, dtype=jnp.int32)[:, None]
    t = jnp.arange(tq, dtype=jnp.int32)[None, :]
    sl = jnp.asarray(slopes2, F32)[:, None, None]
    kb = sl * jnp.broadcast_to(s, (tk, tq)).astype(F32)
    allowed = (s // CHUNK) <= (t // CHUNK)
    db = jnp.where(allowed[None], sl * (t - jnp.abs(t - s)).astype(F32), NEG_INF)
    return kb, db


def _out_proj_kernel(ys_ref, ysb_ref, ych_ref, ydf_ref, g_ref, ind_ref, w_ref, x_ref, o_ref):
    acc = x_ref[...]
    for gi, y_ref in enumerate((ys_ref, ysb_ref, ych_ref, ydf_ref)):
        cs = slice(gi * GROUP_W, (gi + 1) * GROUP_W)
        y = y_ref[...].astype(F32)
        ms = _group_ms(y, ind_ref[...], 1.0 / HEAD_DIM)
        yn = (y * lax.rsqrt(ms + EPS) * g_ref[:, cs]).astype(BF16)
        acc = acc + _dot(yn, w_ref[cs, :])
    o_ref[...] = acc


def _out_proj(y_ssm_tm, y_sb, y_ch, y_df, g, ind64, w, x2, bsz, seq):
    n, dm = x2.shape
    tm = TM_PROJ
    ns = seq // tm
    const = lambda i: (0, 0)
    tile = pl.BlockSpec((tm, GROUP_W), lambda i: (i, 0))
    return pl.pallas_call(
        _out_proj_kernel,
        out_shape=jax.ShapeDtypeStruct((n, dm), F32),
        grid=(n // tm,),
        in_specs=[
            pl.BlockSpec((tm, GROUP_W), lambda i: (i % ns, i // ns)),
            tile, tile, tile,
            pl.BlockSpec((1, dm), const),
            pl.BlockSpec((GROUP_W, GROUP_W), const),
            pl.BlockSpec((dm, dm), const),
            pl.BlockSpec((tm, dm), lambda i: (i, 0)),
        ],
        out_specs=pl.BlockSpec((tm, dm), lambda i: (i, 0)),
        compiler_params=_cparams(("arbitrary",)),
        name="out_proj",
    )(y_ssm_tm, y_sb, y_ch, y_df, g, ind64, w, x2)


def _ffn_kernel(x_ref, g_ref, w1_ref, w3_ref, w2_ref, o_ref, hid_ref, *, tf):
    x = x_ref[...]
    ms = jnp.mean(x * x, axis=-1, keepdims=True)
    h = (x * lax.rsqrt(ms + EPS) * g_ref[...]).astype(BF16)
    dff = w1_ref.shape[1]
    for c in range(dff // tf):
        cs = slice(c * tf, (c + 1) * tf)
        a = _dot(h, w1_ref[:, cs])
        b = _dot(h, w3_ref[:, cs])
        hid_ref[:, cs] = (jax.nn.silu(a) * b).astype(BF16)
    o_ref[...] = x + _dot(hid_ref[...], w2_ref[...])


def _ffn(x2, g, w1, w3, w2):
    n, dm = x2.shape
    dff = w1.shape[1]
    tm = TM_PROJ
    const = lambda i: (0, 0)
    return pl.pallas_call(
        functools.partial(_ffn_kernel, tf=GROUP_W),
        out_shape=jax.ShapeDtypeStruct((n, dm), F32),
        grid=(n // tm,),
        in_specs=[
            pl.BlockSpec((tm, dm), lambda i: (i, 0)),
            pl.BlockSpec((1, dm), const),
            pl.BlockSpec((dm, dff), const),
            pl.BlockSpec((dm, dff), const),
            pl.BlockSpec((dff, dm), const),
        ],
        out_specs=pl.BlockSpec((tm, dm), lambda i: (i, 0)),
        scratch_shapes=[pltpu.VMEM((tm, dff), BF16)],
        compiler_params=_cparams(("arbitrary",)),
        name="ffn_dense",
    )(x2, g, w1, w3, w2)


def _router_kernel(x_ref, g_ref, wr_ref, tri_ref, h_ref, route_ref, cnt_ref, carry_ref):
    @pl.when(pl.program_id(0) == 0)
    def _():
        carry_ref[...] = jnp.zeros_like(carry_ref)

    x = x_ref[...]
    ms = jnp.mean(x * x, axis=-1, keepdims=True)
    h = x * lax.rsqrt(ms + EPS) * g_ref[...]
    h_hi = h.astype(BF16)
    half = h.shape[1] // 2
    bits = lax.bitcast_convert_type(h_hi.astype(F32), jnp.uint32)
    h_ref[...] = bits[:, :half] | lax.shift_right_logical(bits[:, half:], jnp.uint32(16))
    h_lo = (h - h_hi.astype(F32)).astype(BF16)
    logits = _dot(h_hi, wr_ref[0]) + _dot(h_lo, wr_ref[0]) + _dot(h_hi, wr_ref[1])

    lane = _iota2(logits.shape, 1).astype(F32)
    lg = jnp.where(lane < N_EXPERTS, logits, -jnp.inf)
    m1 = jnp.max(lg, axis=-1, keepdims=True)
    e1 = jnp.min(jnp.where(lg == m1, lane, float(LANES)), axis=-1, keepdims=True)
    lg2 = jnp.where(lane == e1, -jnp.inf, lg)
    m2 = jnp.max(lg2, axis=-1, keepdims=True)
    e2 = jnp.min(jnp.where(lg2 == m2, lane, float(LANES)), axis=-1, keepdims=True)
    t = jnp.exp(m2 - m1)
    g1 = 1.0 / (1.0 + t)
    g2 = t / (1.0 + t)

    hot1 = lane == e1
    hot2 = lane == e2
    sel = jnp.where(hot1 | hot2, 1.0, 0.0).astype(BF16)
    prior = _dot(tri_ref[...], sel) + carry_ref[...]
    r1 = jnp.sum(jnp.where(hot1, prior, 0.0), axis=-1, keepdims=True)
    r2 = jnp.sum(jnp.where(hot2, prior, 0.0), axis=-1, keepdims=True)
    carry_ref[...] = carry_ref[...] + jnp.sum(sel.astype(F32), axis=0, keepdims=True)
    cnt_ref[...] = carry_ref[...]

    out = jnp.where(lane == 0, e1, 0.0)
    out = jnp.where(lane == 1, e2, out)
    out = jnp.where(lane == 2, g1, out)
    out = jnp.where(lane == 3, g2, out)
    out = jnp.where(lane == 4, r1, out)
    out = jnp.where(lane == 5, r2, out)
    route_ref[...] = out


def _router(x2, g, wr2, tri):
    n, dm = x2.shape
    tm = TM_PROJ
    const = lambda i: (0, 0)
    return pl.pallas_call(
        _router_kernel,
        out_shape=(jax.ShapeDtypeStruct((n, dm // 2), jnp.uint32),
                   jax.ShapeDtypeStruct((n, LANES), F32),
                   jax.ShapeDtypeStruct((1, LANES), F32)),
        grid=(n // tm,),
        in_specs=[
            pl.BlockSpec((tm, dm), lambda i: (i, 0)),
            pl.BlockSpec((1, dm), const),
            pl.BlockSpec((2, dm, LANES), lambda i: (0, 0, 0)),
            pl.BlockSpec((tm, tm), const),
        ],
        out_specs=(
            pl.BlockSpec((tm, dm // 2), lambda i: (i, 0)),
            pl.BlockSpec((tm, LANES), lambda i: (i, 0)),
            pl.BlockSpec((1, LANES), const),
        ),
        scratch_shapes=[pltpu.VMEM((1, LANES), F32)],
        compiler_params=_cparams(("arbitrary",)),
        name="moe_router",
    )(x2, g, wr2, tri)


def _row_copy(src_ref, src_row, dst_ref, dst_row, sem):
    return pltpu.make_async_copy(src_ref.at[pl.ds(src_row, 1)], dst_ref.at[pl.ds(dst_row, 1)], sem)


def _dispatch_kernel(d1_ref, d2_ref, cnt_ref, h_ref, xs_ref, zero_ref, sem, tail_sem, *, tm, cap, nexp, tail):
    def issue(r, c):
        _row_copy(h_ref, r, xs_ref, d1_ref[0, 0, r], sem).start()
        _row_copy(h_ref, r, xs_ref, d2_ref[0, 0, r], sem).start(priority=1)
        return c

    lax.fori_loop(0, tm, issue, 0, unroll=8)

    def aligned_end(e):
        return ((cnt_ref[e] + (SUBLANES - 1)) // SUBLANES) * SUBLANES

    def tail_copy(e, k):
        row0 = pl.multiple_of(e * cap + aligned_end(e) + k * tm, SUBLANES)
        return pltpu.make_async_copy(zero_ref, xs_ref.at[pl.ds(row0, tm)], tail_sem)

    def tail_rows(e, fn):
        for k in range(SUBLANES - 1):
            @pl.when(cnt_ref[e] + k < aligned_end(e))
            def _():
                fn(_row_copy(zero_ref, 0, xs_ref, e * cap + cnt_ref[e] + k, tail_sem))

    @pl.when(pl.program_id(0) == pl.num_programs(0) - 1)
    def _():
        zero_ref[...] = jnp.zeros_like(zero_ref)
        for e in range(nexp):
            for k in range(tail // tm):
                tail_copy(e, k).start()
            tail_rows(e, lambda cp: cp.start())
        for e in range(nexp):
            for k in range(tail // tm):
                tail_copy(e, k).wait()
            tail_rows(e, lambda cp: cp.wait())

    for _ in range(2):
        pltpu.make_async_copy(h_ref, xs_ref.at[pl.ds(0, tm)], sem).wait()


def _dispatch(d1, d2, cnt, h, cap, tail):
    n, wd = h.shape
    tm = TM_PROJ
    nt = n // tm
    idx_spec = pl.BlockSpec((1, 1, tm), lambda i: (i, 0, 0), memory_space=pltpu.SMEM)
    return pl.pallas_call(
        functools.partial(_dispatch_kernel, tm=tm, cap=cap, nexp=N_EXPERTS, tail=tail),
        out_shape=jax.ShapeDtypeStruct((N_EXPERTS * cap, wd), h.dtype),
        grid=(nt,),
        in_specs=[idx_spec, idx_spec, pl.BlockSpec(memory_space=pltpu.SMEM),
                  pl.BlockSpec((tm, wd), lambda i: (i, 0))],
        out_specs=pl.BlockSpec(memory_space=pl.ANY),
        scratch_shapes=[pltpu.VMEM((tm, wd), h.dtype), pltpu.SemaphoreType.DMA, pltpu.SemaphoreType.DMA],
        compiler_params=pltpu.CompilerParams(dimension_semantics=("arbitrary",),
                                             vmem_limit_bytes=VMEM_LIMIT_BYTES, disable_bounds_checks=True),
        name="moe_dispatch",
    )(d1.reshape(nt, 1, tm), d2.reshape(nt, 1, tm), cnt, h)


def _combine_kernel(d1_ref, d2_ref, n1_ref, n2_ref, x_ref, route_ref, ys_ref, o_ref, y1_ref, y2_ref, sems, *, tm):
    i = pl.program_id(0)
    slot = jnp.bitwise_and(i, 1)

    def fetch(a_ref, b_ref, s):
        def issue(r, c):
            _row_copy(ys_ref, a_ref[0, 0, r], y1_ref.at[s], r, sems.at[s]).start()
            _row_copy(ys_ref, b_ref[0, 0, r], y2_ref.at[s], r, sems.at[s]).start(priority=1)
            return c

        lax.fori_loop(0, tm, issue, 0, unroll=8)

    @pl.when(i == 0)
    def _():
        fetch(d1_ref, d2_ref, slot)

    @pl.when(i + 1 < pl.num_programs(0))
    def _():
        fetch(n1_ref, n2_ref, 1 - slot)

    for buf in (y1_ref, y2_ref):
        pltpu.make_async_copy(ys_ref.at[pl.ds(0, tm)], buf.at[slot], sems.at[slot]).wait()
    o_ref[...] = x_ref[...] + route_ref[:, 2:3] * y1_ref[slot] + route_ref[:, 3:4] * y2_ref[slot]


def _combine(d1, d2, x2, route, ys):
    n, dm = x2.shape
    tm = TM_PROJ
    nt = n // tm
    idx_spec = pl.BlockSpec((1, 1, tm), lambda i: (i, 0, 0), memory_space=pltpu.SMEM)
    next_spec = pl.BlockSpec((1, 1, tm), lambda i: (jnp.minimum(i + 1, nt - 1), 0, 0), memory_space=pltpu.SMEM)
    d1, d2 = d1.reshape(nt, 1, tm), d2.reshape(nt, 1, tm)
    return pl.pallas_call(
        functools.partial(_combine_kernel, tm=tm),
        out_shape=jax.ShapeDtypeStruct((n, dm), F32),
        grid=(nt,),
        in_specs=[idx_spec, idx_spec, next_spec, next_spec, pl.BlockSpec((tm, dm), lambda i: (i, 0)),
                  pl.BlockSpec((tm, LANES), lambda i: (i, 0)), pl.BlockSpec(memory_space=pl.ANY)],
        out_specs=pl.BlockSpec((tm, dm), lambda i: (i, 0)),
        scratch_shapes=[pltpu.VMEM((2, tm, dm), F32), pltpu.VMEM((2, tm, dm), F32), pltpu.SemaphoreType.DMA((2,))],
        compiler_params=pltpu.CompilerParams(dimension_semantics=("arbitrary",),
                                             vmem_limit_bytes=VMEM_LIMIT_BYTES, disable_bounds_checks=True),
        name="moe_combine",
    )(d1, d2, d1, d2, x2, route, ys)


def _experts_kernel(te_ref, blk_ref, nused_ref, xs_ref, w1_ref, w3_ref, w2_ref, o_ref, x_ref, hid_ref, acc_ref):
    t = pl.program_id(0)
    c = pl.program_id(1)
    nc = pl.num_programs(1)

    @pl.when(jnp.logical_and(t < nused_ref[0], c == 0))
    def _():
        w = xs_ref[...]
        half = w.shape[1]
        x_ref[:, :half] = lax.bitcast_convert_type(w & jnp.uint32(0xFFFF0000), F32).astype(BF16)
        x_ref[:, half:] = lax.bitcast_convert_type(lax.shift_left(w, jnp.uint32(16)), F32).astype(BF16)

    @pl.when(t < nused_ref[0])
    def _():
        x = x_ref[...]
        tf = w1_ref.shape[2]
        for s0 in range(0, tf, GROUP_W):
            cs = slice(s0, s0 + GROUP_W)
            a = _dot(x, w1_ref[0, :, cs])
            b = _dot(x, w3_ref[0, :, cs])
            hid_ref[:, cs] = (jax.nn.silu(a) * b).astype(BF16)
        part = _dot(hid_ref[...], w2_ref[0])

        @pl.when(c == 0)
        def _():
            acc_ref[...] = part

        @pl.when(c > 0)
        def _():
            acc_ref[...] = acc_ref[...] + part

        @pl.when(c == nc - 1)
        def _():
            o_ref[...] = acc_ref[...]

    @pl.when(jnp.logical_and(t >= nused_ref[0], c == nc - 1))
    def _():
        o_ref[...] = jnp.zeros_like(o_ref)


def _experts(tile_expert, tile_block, nused, xs, w1, w3, w2, nt):
    rows, wd = xs.shape
    dm = 2 * wd
    dff = w1.shape[2]
    tm, tf = TM_EXPERT, TF_EXPERT
    nc = dff // tf
    spare = rows // tm

    def cc(t, c, nu):
        return jnp.where(t < nu[0], c, nc - 1)

    grid_spec = pltpu.PrefetchScalarGridSpec(
        num_scalar_prefetch=3,
        grid=(nt, nc),
        in_specs=[
            pl.BlockSpec((tm, wd), lambda t, c, te, tb, nu: (tb[t], 0)),
            pl.BlockSpec((1, dm, tf), lambda t, c, te, tb, nu: (te[t], 0, cc(t, c, nu))),
            pl.BlockSpec((1, dm, tf), lambda t, c, te, tb, nu: (te[t], 0, cc(t, c, nu))),
            pl.BlockSpec((1, tf, dm), lambda t, c, te, tb, nu: (te[t], cc(t, c, nu), 0)),
        ],
        out_specs=pl.BlockSpec((tm, dm), lambda t, c, te, tb, nu: (jnp.where(t < nu[0], tb[t], spare), 0)),
        scratch_shapes=[pltpu.VMEM((tm, dm), BF16), pltpu.VMEM((tm, tf), BF16), pltpu.VMEM((tm, dm), F32)],
    )
    return pl.pallas_call(
        _experts_kernel,
        out_shape=jax.ShapeDtypeStruct((rows + tm, dm), F32),
        grid_spec=grid_spec,
        compiler_params=_cparams(("arbitrary", "arbitrary")),
        name="moe_experts",
    )(tile_expert, tile_block, nused, xs, w1, w3, w2)


def _moe(x2, g, w_router, w1, w3, w2, layer_idx, tri):
    n, dm = x2.shape
    wr = jnp.zeros((dm, LANES), F32).at[:, :N_EXPERTS].set(w_router.astype(F32))
    wr_hi = wr.astype(BF16)
    wr_lo = (wr - wr_hi.astype(F32)).astype(BF16)
    h, route, counts = _router(x2, g, jnp.stack([wr_hi, wr_lo]), tri)

    tm = TM_EXPERT
    assert n % tm == 0 and tm % TM_PROJ == 0
    cap = n + tm
    d1 = route[:, 0].astype(jnp.int32) * cap + route[:, 4].astype(jnp.int32)
    d2 = route[:, 1].astype(jnp.int32) * cap + route[:, 5].astype(jnp.int32)
    cnt = counts[0, :N_EXPERTS].astype(jnp.int32)

    nt = 2 * n // tm + N_EXPERTS
    ntile = (cnt + tm - 1) // tm
    ends = jnp.cumsum(ntile)
    nused = ends[-1]
    t = jnp.minimum(jnp.arange(nt, dtype=jnp.int32), nused - 1)
    tile_expert = jnp.sum(t[:, None] >= ends[None, :], axis=1).astype(jnp.int32)
    tile_block = tile_expert * (cap // tm) + t - (ends - ntile)[tile_expert]

    xs = _dispatch(d1, d2, cnt, h, cap, tm)
    ys = _experts(tile_expert + layer_idx * N_EXPERTS, tile_block.astype(jnp.int32),
                  nused.reshape(1).astype(jnp.int32), xs, w1, w3, w2, nt)
    return _combine(d1, d2, x2, route, ys)


def kernel(x, norm_mix_g, w_in, ssm_lam_re, ssm_lam_im, ssm_log_dt, ssm_b_re, ssm_b_im, ssm_c_re, ssm_c_im,
           ssm_d, ssm_w_glu, ssm_b_glu, ch_q_norm_g, ch_k_norm_g, ch_rel_bias, df_q_norm_g, df_k_norm_g,
           df_lambda, out_norm_g, w_out, norm_ffn_g, ffn_w1, ffn_w3, ffn_w2, moe_router, moe_w1, moe_w3, moe_w2):
    bsz, seq, dm = x.shape
    depth = w_in.shape[0]
    n = bsz * seq
    nheads = GROUP_W // HEAD_DIM
    ind64 = _block_indicator(GROUP_W, HEAD_DIM)
    ind32 = _block_indicator(GROUP_W, DF_QK_DIM)
    tri = (jnp.arange(TM_PROJ)[:, None] > jnp.arange(TM_PROJ)[None, :]).astype(BF16)
    slopes = [2.0 ** (-8.0 * (h + 1) / nheads) for h in range(nheads)]
    slopes2 = tuple(s * LOG2E for s in slopes)
    df_kb, df_db = _df_bias_tiles(slopes2, TQ_ATT, TK_ATT)
    later = (jnp.arange(TK_ATT)[:, None] < jnp.arange(TK_ATT)[None, :]).astype(BF16)

    moe_w = tuple(w.reshape((-1,) + w.shape[2:]).astype(BF16) for w in (moe_w1, moe_w3, moe_w2))

    x2 = x.reshape(n, dm)
    for layer in range(depth):
        qkg = jnp.stack([
            jnp.tile(ch_q_norm_g[layer].astype(F32), nheads) * (HEAD_DIM ** -0.5 * LOG2E),
            jnp.tile(ch_k_norm_g[layer].astype(F32), nheads),
            jnp.tile(df_q_norm_g[layer].astype(F32).reshape(-1), nheads) * (DF_QK_DIM ** -0.5 * LOG2E),
            jnp.tile(df_k_norm_g[layer].astype(F32).reshape(-1), nheads),
        ])
        u_tm, qk, v_sb, v_ch, v_df = _in_proj(x2, norm_mix_g[layer].reshape(1, dm), w_in[layer].astype(BF16), qkg,
                             ind64, ind32, bsz, seq)

        wb, lre, lim, wc = _s5_params(ssm_lam_re[layer], ssm_lam_im[layer], ssm_log_dt[layer],
                                      ssm_b_re[layer], ssm_b_im[layer], ssm_c_re[layer], ssm_c_im[layer], bsz)
        y_ssm = _s5(u_tm, wb, lre, lim, wc, ssm_d[layer].reshape(1, GROUP_W),
                    ssm_w_glu[layer].astype(BF16), ssm_b_glu[layer].reshape(1, GROUP_W), bsz)

        tq, tk = TQ_ATT, TK_ATT
        const3 = lambda b, i: (0, 0, 0)
        y_sb = _attention_call(
            functools.partial(_sb_kernel, tq=tq, tk=tk, nheads=nheads),
            qk, 0, v_sb,
            (later,), [pl.BlockSpec((tk, tk), lambda b, i: (0, 0))],
            [pltpu.VMEM((nheads * tq, GROUP_W), BF16), pltpu.VMEM((SUBLANES, tq), F32),
             pltpu.VMEM((GROUP_W, tq), F32),
             pltpu.VMEM((tk, nheads * tq), F32), pltpu.VMEM((tk, nheads * tq), F32)],
            bsz, seq, "sb_attention")

        bias, nback = _ch_bias_tiles(ch_rel_bias[layer], tq, tk)
        y_ch = _attention_call(
            functools.partial(_ch_kernel, tq=tq, tk=tk, nheads=nheads, nback=nback),
            qk, 2, v_ch,
            (bias,), [pl.BlockSpec(bias.shape, lambda b, i: (0, 0, 0, 0))],
            [pltpu.VMEM((nheads * tq, GROUP_W), BF16), pltpu.VMEM(((nback + 1) * tk, nheads * tq), F32)],
            bsz, seq, "ch_attention")

        lambda_init = 0.8 - 0.6 * math.exp(-0.3 * layer)
        lam_p = df_lambda[layer].astype(F32)
        lam = jnp.exp(jnp.sum(lam_p[0] * lam_p[1])) - jnp.exp(jnp.sum(lam_p[2] * lam_p[3])) + lambda_init
        y_df = _attention_call(
            functools.partial(_df_kernel, tq=tq, tk=tk, nheads=nheads, slopes2=slopes2),
            qk, 4, v_df,
            (df_kb, df_db), [pl.BlockSpec(df_kb.shape, const3), pl.BlockSpec(df_db.shape, const3)],
            [pltpu.VMEM((2 * nheads * tq, GROUP_W), BF16), pltpu.VMEM((2 * nheads, tq), F32),
             pltpu.VMEM((2 * nheads * VA_ROWS, tq), F32),
             pltpu.VMEM((tk, 2 * nheads * tq), F32), pltpu.VMEM((tk, 2 * nheads * tq), F32),
             pltpu.VMEM((2 * nheads, tq), F32), pltpu.VMEM((2 * nheads, tq), F32)],
            bsz, seq, "df_attention", smem=(lam.reshape(1),))

        head_scale = jnp.concatenate([jnp.ones((dm - GROUP_W,), F32),
                                      jnp.full((GROUP_W,), 1.0 - lambda_init, F32)])
        g_out = (out_norm_g[layer].astype(F32) * head_scale).reshape(1, dm)
        x2 = _out_proj(y_ssm, y_sb, y_ch, y_df, g_out, ind64, w_out[layer].astype(BF16), x2, bsz, seq)

        idx = layer // 2
        g_ffn = norm_ffn_g[layer].reshape(1, dm)
        if layer % 2 == 0:
            x2 = _ffn(x2, g_ffn, ffn_w1[idx].astype(BF16), ffn_w3[idx].astype(BF16), ffn_w2[idx].astype(BF16))
        else:
            x2 = _moe(x2, g_ffn, moe_router[idx], *moe_w, idx, tri)
    return x2.reshape(bsz, seq, dm)
```

```python
import functools
import math

import jax
import jax.numpy as jnp
import numpy as np
from jax import lax
from jax.experimental import pallas as pl
from jax.experimental.pallas import tpu as pltpu

F32 = jnp.float32
BF16 = jnp.bfloat16

EPS = 1e-6
NEG_INF = -1e30
HEAD_DIM = 64
GROUP_W = 256
CHUNK = 64
SSM_GROUP = 16
SSM_STATE = 64
DF_QK_DIM = 32
CH_LEFT_CHUNKS = 8
REL_CLIP = 128
N_EXPERTS = 8
LOG2E = 1.4426950408889634

VMEM_LIMIT_BYTES = 56 * 1024 * 1024
SUBLANES = 8
LANES = 128

TM_PROJ = 512
TQ_ATT = 256
TK_ATT = 256
SSM_STEPS = 128
TM_EXPERT = 512
TF_EXPERT = 1792


def _cparams(sem):
    return pltpu.CompilerParams(dimension_semantics=sem, vmem_limit_bytes=VMEM_LIMIT_BYTES)


def _dot(a, b):
    return jnp.dot(a, b, preferred_element_type=F32)


def _dot_nt(a, b):
    return lax.dot_general(a, b, (((1,), (1,)), ((), ())), preferred_element_type=F32)


def _group_ms(v, ind, inv_size):
    return _dot((v * v).astype(BF16), ind) * inv_size


def _block_indicator(width, group):
    r = jnp.arange(width)[:, None] // group
    c = jnp.arange(width)[None, :] // group
    return (r == c).astype(BF16)


def _in_proj_kernel(x_ref, g_ref, w_ref, qkg_ref, ind64_ref, ind32_ref, u_ref, qk_ref, vsb_ref, vch_ref, vdf_ref,
                    *, tk, nheads):
    x = x_ref[...]
    ms = jnp.mean(x * x, axis=-1, keepdims=True)
    h = (x * lax.rsqrt(ms + EPS) * g_ref[...]).astype(BF16)

    def cols(c):
        return _dot(h, w_ref[:, c * GROUP_W:(c + 1) * GROUP_W])

    def put(slot, val):
        qk_ref[:, slot * GROUP_W:(slot + 1) * GROUP_W] = val.astype(BF16)

    def put_values(v_ref, v, ones_row):
        for kt in range(v.shape[0] // tk):
            vt = v[kt * tk:(kt + 1) * tk, :].T
            if not ones_row:
                v_ref[kt] = vt.astype(BF16)
                continue
            pad = jnp.where(_iota2((VA_ROWS - HEAD_DIM, tk), 0) == 0, 1.0, 0.0).astype(BF16)
            for hd in range(nheads):
                v_ref[kt, hd * VA_ROWS:hd * VA_ROWS + HEAD_DIM, :] = vt[hd * HEAD_DIM:(hd + 1) * HEAD_DIM, :].astype(BF16)
                v_ref[kt, hd * VA_ROWS + HEAD_DIM:(hd + 1) * VA_ROWS, :] = pad

    u_ref[...] = cols(0)
    put(0, cols(1) * (HEAD_DIM ** -0.5 * LOG2E))
    put(1, cols(2))
    put_values(vsb_ref, cols(3), False)
    for c, gi in ((4, 0), (5, 1)):
        a = cols(c)
        ms_h = _group_ms(a, ind64_ref[...], 1.0 / HEAD_DIM)
        put(c - 2, a * lax.rsqrt(ms_h + EPS) * qkg_ref[gi:gi + 1, :])
    put_values(vch_ref, cols(6), True)
    for c, gi in ((7, 2), (8, 3)):
        a = cols(c)
        ms_h = _group_ms(a, ind32_ref[...], 1.0 / DF_QK_DIM)
        put(c - 3, a * lax.rsqrt(ms_h + EPS) * qkg_ref[gi:gi + 1, :])
    put_values(vdf_ref, cols(9), True)


def _in_proj(x2, g, w, qkg, ind64, ind32, bsz, seq):
    n, dm = x2.shape
    tm, tk = TM_PROJ, TK_ATT
    ns = seq // tm
    nheads = GROUP_W // HEAD_DIM
    const = lambda i: (0, 0)

    def values(rows):
        return (jax.ShapeDtypeStruct((bsz, seq // tk, rows, tk), BF16),
                pl.BlockSpec((None, tm // tk, rows, tk), lambda i: (i // ns, i % ns, 0, 0)))

    v_shapes, v_specs = zip(values(GROUP_W), values(nheads * VA_ROWS), values(nheads * VA_ROWS))
    return pl.pallas_call(
        functools.partial(_in_proj_kernel, tk=tk, nheads=nheads),
        out_shape=(jax.ShapeDtypeStruct((seq, bsz * GROUP_W), F32),
                   jax.ShapeDtypeStruct((n, 6 * GROUP_W), BF16)) + v_shapes,
        grid=(n // tm,),
        in_specs=[
            pl.BlockSpec((tm, dm), lambda i: (i, 0)),
            pl.BlockSpec((1, dm), const),
            pl.BlockSpec(w.shape, const),
            pl.BlockSpec((4, GROUP_W), const),
            pl.BlockSpec((GROUP_W, GROUP_W), const),
            pl.BlockSpec((GROUP_W, GROUP_W), const),
        ],
        out_specs=(
            pl.BlockSpec((tm, GROUP_W), lambda i: (i % ns, i // ns)),
            pl.BlockSpec((tm, 6 * GROUP_W), lambda i: (i, 0)),
        ) + v_specs,
        compiler_params=_cparams(("arbitrary",)),
        name="in_proj",
    )(x2, g, w, qkg, ind64, ind32)


def _s5_kernel(u_ref, wb_ref, lre_ref, lim_ref, wc_ref, d_ref, wg_ref, bg_ref, y_ref, bu_ref, st_ref, rows_ref,
               *, steps, nstate):
    @pl.when(pl.program_id(0) == 0)
    def _():
        st_ref[...] = jnp.zeros_like(st_ref)

    nhalf, bsz = rows_ref.shape[0], rows_ref.shape[1] // steps
    w = nhalf * LANES
    for b in range(bsz):
        for k in range(nhalf):
            c0 = b * w + k * LANES
            rows_ref[k, pl.ds(b, steps, stride=bsz), :] = u_ref[:, c0:c0 + LANES]
    u = jnp.concatenate([rows_ref[k] for k in range(nhalf)], axis=1)
    bu_ref[...] = _dot(u.astype(BF16), wb_ref[...])
    lre = lre_ref[...]
    lim = lim_ref[...]

    def step(t, carry):
        sre, sim = carry
        r = pl.multiple_of(t * SUBLANES, SUBLANES)
        bre = bu_ref[pl.ds(r, SUBLANES), 0:nstate]
        bim = bu_ref[pl.ds(r, SUBLANES), nstate:2 * nstate]
        nre = lre * sre - lim * sim + bre
        nim = lre * sim + lim * sre + bim
        bu_ref[pl.ds(r, SUBLANES), 0:nstate] = nre
        bu_ref[pl.ds(r, SUBLANES), nstate:2 * nstate] = nim
        return nre, nim

    sre, sim = lax.fori_loop(0, steps, step, (st_ref[:, 0:nstate], st_ref[:, nstate:2 * nstate]))
    st_ref[:, 0:nstate] = sre
    st_ref[:, nstate:2 * nstate] = sim

    y = _dot(bu_ref[...].astype(BF16), wc_ref[...]) + d_ref[...] * u
    y = jax.nn.gelu(y)
    gate = jax.nn.sigmoid(_dot(y.astype(BF16), wg_ref[...]) + bg_ref[...])
    out = y * gate
    for k in range(nhalf):
        rows_ref[k] = out[:, k * LANES:(k + 1) * LANES]
    for b in range(bsz):
        for k in range(nhalf):
            c0 = b * w + k * LANES
            y_ref[:, c0:c0 + LANES] = rows_ref[k, pl.ds(b, steps, stride=bsz), :]


def _s5(u_tm, wb, lre, lim, wc, d, wg, bg, bsz):
    seq = u_tm.shape[0]
    w = u_tm.shape[1] // bsz
    assert bsz == SUBLANES
    blk = SSM_STEPS * bsz
    nstate2 = wb.shape[1]
    const = lambda c: (0, 0)
    return pl.pallas_call(
        functools.partial(_s5_kernel, steps=SSM_STEPS, nstate=nstate2 // 2),
        out_shape=jax.ShapeDtypeStruct(u_tm.shape, F32),
        grid=(seq // SSM_STEPS,),
        in_specs=[
            pl.BlockSpec((SSM_STEPS, bsz * w), lambda c: (c, 0)),
            pl.BlockSpec((w, nstate2), const),
            pl.BlockSpec((bsz, nstate2 // 2), const),
            pl.BlockSpec((bsz, nstate2 // 2), const),
            pl.BlockSpec((nstate2, w), const),
            pl.BlockSpec((1, w), const),
            pl.BlockSpec((w, w), const),
            pl.BlockSpec((1, w), const),
        ],
        out_specs=pl.BlockSpec((SSM_STEPS, bsz * w), lambda c: (c, 0)),
        scratch_shapes=[pltpu.VMEM((blk, nstate2), F32), pltpu.VMEM((bsz, nstate2), F32),
                        pltpu.VMEM((w // LANES, blk, LANES), F32)],
        compiler_params=_cparams(("arbitrary",)),
        name="s5_mixer",
    )(u_tm, wb, lre, lim, wc, d, wg, bg)


def _s5_params(lam_re, lam_im, log_dt, b_re, b_im, c_re, c_im, bsz):
    g, n = lam_re.shape
    dt = jnp.exp(log_dt.astype(F32))[:, None]
    xr = lam_re * dt
    th = lam_im * dt
    er = jnp.exp(xr)
    lbr = er * jnp.cos(th)
    lbi = er * jnp.sin(th)
    ar = jnp.expm1(xr) * jnp.cos(th) - 2.0 * jnp.sin(0.5 * th) ** 2
    ai = lbi
    den = lam_re * lam_re + lam_im * lam_im
    fr = (ar * lam_re + ai * lam_im) / den
    fi = (ai * lam_re - ar * lam_im) / den
    bbr = fr[..., None] * b_re - fi[..., None] * b_im
    bbi = fr[..., None] * b_im + fi[..., None] * b_re
    eye = jnp.eye(g, dtype=F32)
    p = b_re.shape[-1]
    wb_re = jnp.einsum("gnp,gh->gphn", bbr, eye).reshape(g * p, g * n)
    wb_im = jnp.einsum("gnp,gh->gphn", bbi, eye).reshape(g * p, g * n)
    wb = jnp.concatenate([wb_re, wb_im], axis=1).astype(BF16)
    wc_re = jnp.einsum("gpn,gh->gnhp", c_re, eye).reshape(g * n, g * p)
    wc_im = jnp.einsum("gpn,gh->gnhp", c_im, eye).reshape(g * n, g * p)
    wc = jnp.concatenate([wc_re, -wc_im], axis=0).astype(BF16)
    lre = jnp.broadcast_to(lbr.reshape(1, g * n), (bsz, g * n))
    lim = jnp.broadcast_to(lbi.reshape(1, g * n), (bsz, g * n))
    return wb, lre, lim, wc


VA_ROWS = 2 * HEAD_DIM


def _iota2(shape, dim):
    return lax.broadcasted_iota(jnp.int32, shape, dim)


def _stack_masked_q(q_ref, qs_ref, tq, ngroups, width):
    lane = _iota2((1, GROUP_W), 1)
    q = q_ref[...]
    for g in range(ngroups):
        keep = (lane >= g * width) & (lane < (g + 1) * width)
        qs_ref[g * tq:(g + 1) * tq, :] = jnp.where(keep, q, jnp.zeros_like(q))


def _key_scores(k_ref, qs_ref, j, tk):
    row0 = pl.multiple_of(jnp.maximum(j, 0) * tk, tk)
    return _dot_nt(k_ref[pl.ds(row0, tk), :], qs_ref[...])


def _score_prefetch(k_ref, qs_ref, j, dst_ref, tq, tk, bias_of=None, max_ref=None):
    row0 = pl.multiple_of(jnp.maximum(j, 0) * tk, tk)

    def emit(g):
        cols = slice(g * tq, (g + 1) * tq)
        s = _dot_nt(k_ref[pl.ds(row0, tk), :], qs_ref[cols, :])
        if bias_of is not None:
            s = s + bias_of(g)[...]
        dst_ref[:, cols] = s
        if max_ref is not None:
            max_ref[g:g + 1, :] = jnp.max(s, axis=0, keepdims=True)

    return emit


def _paired_key_tiles(i, k_ref, qs_ref, tile_fn, buf_a, buf_b, tq, tk, bias_of=None):
    def body(n, c):
        ja = i - 1 - 2 * n
        tile_fn(ja, buf_a, _score_prefetch(k_ref, qs_ref, ja - 1, buf_b[0], tq, tk, bias_of, buf_b[1]))
        tile_fn(ja - 1, buf_b, _score_prefetch(k_ref, qs_ref, ja - 2, buf_a[0], tq, tk, bias_of, buf_a[1]))
        return c

    lax.fori_loop(0, i // 2, body, 0)

    @pl.when(jnp.bitwise_and(i, 1) == 1)
    def _():
        tile_fn(0, buf_a, lambda g: None)


def _sb_kernel(q_ref, k_ref, vt_ref, later_ref, o_ref, qs_ref, carry_ref, acc_ref, za_ref, zb_ref,
               *, tq, tk, nheads):
    i = pl.program_id(1)
    _stack_masked_q(q_ref, qs_ref, tq, nheads, HEAD_DIM)

    def tile(j, buf, prefetch, diagonal=False):
        zt = buf[0]
        if diagonal:
            before = _iota2((tk, tq), 0) < _iota2((tk, tq), 1)

        def scan_stage(h):
            z = zt[:, h * tq:(h + 1) * tq]
            neg_abs = lax.bitcast_convert_type(
                lax.bitcast_convert_type(z, jnp.uint32) | jnp.uint32(0x80000000), F32)
            sp = jnp.maximum(z, 0.0) + jnp.log2(1.0 + jnp.exp2(neg_abs))
            nl1m = jnp.where(before, sp, 0.0) if diagonal else sp
            inside = _dot(later_ref[...], nl1m.astype(BF16))
            return z - sp, inside, nl1m[0:1, :]

        def value_stage(h, logit, inside, first_row):
            hs = slice(h * HEAD_DIM, (h + 1) * HEAD_DIM)
            between = inside if diagonal else inside + carry_ref[h:h + 1, :]
            w = jnp.exp2(logit - between)
            if diagonal:
                w = jnp.where(before, w, 0.0)
            pv = _dot(vt_ref[j, hs, :], w.astype(BF16))
            total = inside[0:1, :] + first_row
            if diagonal:
                acc_ref[hs, :] = pv
                carry_ref[h:h + 1, :] = total
            else:
                acc_ref[hs, :] = acc_ref[hs, :] + pv
                carry_ref[h:h + 1, :] = carry_ref[h:h + 1, :] + total

        prefetch(0)
        pending = scan_stage(0)
        for h in range(1, nheads):
            prefetch(h)
            upcoming = scan_stage(h)
            value_stage(h - 1, *pending)
            pending = upcoming
        value_stage(nheads - 1, *pending)

    tile(i, (_key_scores(k_ref, qs_ref, i, tk), None), _score_prefetch(k_ref, qs_ref, i - 1, za_ref, tq, tk),
         diagonal=True)
    _paired_key_tiles(i, k_ref, qs_ref, tile, (za_ref, None), (zb_ref, None), tq, tk)
    o_ref[...] = acc_ref[...].T.astype(o_ref.dtype)


def _softmax_tile(buf, tq, off, va, g, first, m_ref, acc_ref):
    st_ref, max_ref = buf
    rows = slice(g * VA_ROWS, (g + 1) * VA_ROWS)
    parts, alphas = [], []
    for c0 in range(0, tq, LANES):
        cs = slice(c0, c0 + LANES)
        sc = slice(g * tq + c0, g * tq + c0 + LANES)
        blk_max = max_ref[g:g + 1, cs]
        if first:
            m_new = blk_max
            shift = m_new
        else:
            m_old = m_ref[g:g + 1, cs]
            m_new = jnp.maximum(m_old, blk_max + off)
            shift = m_new - off
            alphas.append(jnp.exp2(m_old - m_new))
        parts.append(jnp.exp2(st_ref[:, sc] - shift).astype(BF16))
        m_ref[g:g + 1, cs] = m_new
    pv = _dot(va, jnp.concatenate(parts, axis=1))
    if first:
        acc_ref[rows, :] = pv
    else:
        acc_ref[rows, :] = jnp.concatenate(alphas, axis=1) * acc_ref[rows, :] + pv


def _normalised(acc_ref, g):
    base = g * VA_ROWS
    return acc_ref[base:base + HEAD_DIM, :] / acc_ref[base + HEAD_DIM:base + HEAD_DIM + 1, :]


def _ch_kernel(q_ref, k_ref, va_ref, bias_ref, o_ref, qs_ref, z_ref, *, tq, tk, nheads, nback):
    i = pl.program_id(1)
    _stack_masked_q(q_ref, qs_ref, tq, nheads, HEAD_DIM)

    def attend(ntiles):
        tile_max = [[] for _ in range(nheads)]
        for n in range(ntiles):
            row0 = pl.multiple_of((i - n) * tk, tk)
            for h in range(nheads):
                cols = slice(h * tq, (h + 1) * tq)
                s = _dot_nt(k_ref[pl.ds(row0, tk), :], qs_ref[cols, :]) + bias_ref[h, n]
                z_ref[n * tk:(n + 1) * tk, cols] = s
                tile_max[h].append(jnp.max(s, axis=0, keepdims=True))
        outs = []
        for h in range(nheads):
            m = functools.reduce(jnp.maximum, tile_max[h])
            acc = None
            for n in range(ntiles):
                p = jnp.exp2(z_ref[n * tk:(n + 1) * tk, h * tq:(h + 1) * tq] - m).astype(BF16)
                pv = _dot(va_ref[i - n, h * VA_ROWS:(h + 1) * VA_ROWS, :], p)
                acc = pv if acc is None else acc + pv
            outs.append(acc[:HEAD_DIM, :] / acc[HEAD_DIM:HEAD_DIM + 1, :])
        o_ref[...] = jnp.concatenate(outs, axis=0).T.astype(o_ref.dtype)

    for ntiles in range(1, nback + 2):
        pl.when(jnp.minimum(i, nback) + 1 == ntiles)(functools.partial(attend, ntiles))


def _df_kernel(lam_ref, q_ref, k_ref, va_ref, kb_ref, db_ref, o_ref, qs_ref, m_ref, acc_ref,
               za_ref, zb_ref, maxa_ref, maxb_ref, *, tq, tk, nheads, slopes2):
    i = pl.program_id(1)
    _stack_masked_q(q_ref, qs_ref, tq, 2 * nheads, DF_QK_DIM)
    buf_a, buf_b = (za_ref, maxa_ref), (zb_ref, maxb_ref)

    def past_bias(g):
        return kb_ref.at[g // 2]

    def tile(j, buf, prefetch, diagonal=False):
        for g in range(2 * nheads):
            prefetch(g)
            h = g // 2
            va = va_ref[j, h * VA_ROWS:(h + 1) * VA_ROWS, :]
            if diagonal:
                _softmax_tile(buf, tq, 0.0, va, g, True, m_ref, acc_ref)
            else:
                _softmax_tile(buf, tq, slopes2[h] * ((j - i) * tk).astype(F32), va, g, False, m_ref, acc_ref)

    diag = _score_prefetch(k_ref, qs_ref, i, zb_ref, tq, tk, lambda g: db_ref.at[g // 2], maxb_ref)
    for g in range(2 * nheads):
        diag(g)
    tile(i, buf_b, _score_prefetch(k_ref, qs_ref, i - 1, za_ref, tq, tk, past_bias, maxa_ref), diagonal=True)
    _paired_key_tiles(i, k_ref, qs_ref, tile, buf_a, buf_b, tq, tk, past_bias)
    lam = lam_ref[0]
    out = jnp.concatenate([_normalised(acc_ref, 2 * h) - lam * _normalised(acc_ref, 2 * h + 1)
                           for h in range(nheads)], axis=0)
    o_ref[...] = out.T.astype(o_ref.dtype)


def _attention_call(kernel, qk, col0, vt, extra, extra_specs, scratch, bsz, seq, name, smem=()):
    tq, tk = TQ_ATT, TK_ATT
    assert tq == tk
    nq = seq // tq
    vrows = vt.shape[2]
    return pl.pallas_call(
        kernel,
        out_shape=jax.ShapeDtypeStruct((bsz * seq, GROUP_W), BF16),
        grid=(bsz, nq),
        in_specs=[pl.BlockSpec(memory_space=pltpu.SMEM) for _ in smem] + [
            pl.BlockSpec((tq, GROUP_W), lambda b, i: (b * nq + i, col0)),
            pl.BlockSpec((seq, GROUP_W), lambda b, i: (b, col0 + 1)),
            pl.BlockSpec((None, seq // tk, vrows, tk), lambda b, i: (b, 0, 0, 0)),
        ] + extra_specs,
        out_specs=pl.BlockSpec((tq, GROUP_W), lambda b, i: (b * nq + i, 0)),
        scratch_shapes=scratch,
        compiler_params=_cparams(("arbitrary", "arbitrary")),
        name=name,
    )(*smem, qk, qk, vt, *extra)


def _ch_bias_tiles(rel_bias, tq, tk):
    assert tq == tk
    nback = CH_LEFT_CHUNKS * CHUNK // tk
    nh = rel_bias.shape[0]
    n = np.arange(nback + 1)[:, None, None]
    s = np.arange(tk)[None, :, None]
    t = np.arange(tq)[None, None, :]
    delta = (n * tk + t) // CHUNK - s // CHUNK
    valid = (delta >= 0) & (delta <= CH_LEFT_CHUNKS)
    table = rel_bias.astype(F32) * LOG2E
    lo, hi = -(tk - 1), nback * tk + tq - 1
    ext = jnp.concatenate([jnp.repeat(table[:, :1], -REL_CLIP - lo, axis=1), table,
                           jnp.repeat(table[:, -1:], hi - REL_CLIP, axis=1)], axis=1)
    period = tk + tq
    tiles = []
    for m in range(nback + 1):
        w = ext[:, m * tk:m * tk + period - 1]
        w = jnp.concatenate([w, jnp.zeros((nh, 1), F32)], axis=1)
        skew = jnp.tile(w, (1, tk))[:, :tk * (period - 1)].reshape(nh, tk, period - 1)
        tiles.append(skew[:, :, tk - 1:tk - 1 + tq])
    bias = jnp.stack(tiles, axis=1)
    return jnp.where(jnp.asarray(valid)[None], bias, NEG_INF), nback


def _df_bias_tiles(slopes2, tq, tk):
    s = jnp.arange(tk, dtype=jnp.int32)[:, None]
    t = jnp.arange(tq, dtype=jnp.int32)[None, :]
    sl = jnp.asarray(slopes2, F32)[:, None, None]
    kb = sl * jnp.broadcast_to(s, (tk, tq)).astype(F32)
    allowed = (s // CHUNK) <= (t // CHUNK)
    db = jnp.where(allowed[None], sl * (t - jnp.abs(t - s)).astype(F32), NEG_INF)
    return kb, db


def _out_proj_kernel(ys_ref, ysb_ref, ych_ref, ydf_ref, g_ref, ind_ref, w_ref, x_ref, o_ref):
    acc = x_ref[...]
    for gi, y_ref in enumerate((ys_ref, ysb_ref, ych_ref, ydf_ref)):
        cs = slice(gi * GROUP_W, (gi + 1) * GROUP_W)
        y = y_ref[...].astype(F32)
        ms = _group_ms(y, ind_ref[...], 1.0 / HEAD_DIM)
        yn = (y * lax.rsqrt(ms + EPS) * g_ref[:, cs]).astype(BF16)
        acc = acc + _dot(yn, w_ref[cs, :])
    o_ref[...] = acc


def _out_proj(y_ssm_tm, y_sb, y_ch, y_df, g, ind64, w, x2, bsz, seq):
    n, dm = x2.shape
    tm = TM_PROJ
    ns = seq // tm
    const = lambda i: (0, 0)
    tile = pl.BlockSpec((tm, GROUP_W), lambda i: (i, 0))
    return pl.pallas_call(
        _out_proj_kernel,
        out_shape=jax.ShapeDtypeStruct((n, dm), F32),
        grid=(n // tm,),
        in_specs=[
            pl.BlockSpec((tm, GROUP_W), lambda i: (i % ns, i // ns)),
            tile, tile, tile,
            pl.BlockSpec((1, dm), const),
            pl.BlockSpec((GROUP_W, GROUP_W), const),
            pl.BlockSpec((dm, dm), const),
            pl.BlockSpec((tm, dm), lambda i: (i, 0)),
        ],
        out_specs=pl.BlockSpec((tm, dm), lambda i: (i, 0)),
        compiler_params=_cparams(("arbitrary",)),
        name="out_proj",
    )(y_ssm_tm, y_sb, y_ch, y_df, g, ind64, w, x2)


def _ffn_kernel(x_ref, g_ref, w1_ref, w3_ref, w2_ref, o_ref, hid_ref, *, tf):
    x = x_ref[...]
    ms = jnp.mean(x * x, axis=-1, keepdims=True)
    h = (x * lax.rsqrt(ms + EPS) * g_ref[...]).astype(BF16)
    dff = w1_ref.shape[1]
    for c in range(dff // tf):
        cs = slice(c * tf, (c + 1) * tf)
        a = _dot(h, w1_ref[:, cs])
        b = _dot(h, w3_ref[:, cs])
        hid_ref[:, cs] = (jax.nn.silu(a) * b).astype(BF16)
    o_ref[...] = x + _dot(hid_ref[...], w2_ref[...])


def _ffn(x2, g, w1, w3, w2):
    n, dm = x2.shape
    dff = w1.shape[1]
    tm = TM_PROJ
    const = lambda i: (0, 0)
    return pl.pallas_call(
        functools.partial(_ffn_kernel, tf=GROUP_W),
        out_shape=jax.ShapeDtypeStruct((n, dm), F32),
        grid=(n // tm,),
        in_specs=[
            pl.BlockSpec((tm, dm), lambda i: (i, 0)),
            pl.BlockSpec((1, dm), const),
            pl.BlockSpec((dm, dff), const),
            pl.BlockSpec((dm, dff), const),
            pl.BlockSpec((dff, dm), const),
        ],
        out_specs=pl.BlockSpec((tm, dm), lambda i: (i, 0)),
        scratch_shapes=[pltpu.VMEM((tm, dff), BF16)],
        compiler_params=_cparams(("arbitrary",)),
        name="ffn_dense",
    )(x2, g, w1, w3, w2)


def _router_kernel(x_ref, g_ref, wr_ref, tri_ref, h_ref, route_ref, cnt_ref, carry_ref):
    @pl.when(pl.program_id(0) == 0)
    def _():
        carry_ref[...] = jnp.zeros_like(carry_ref)

    x = x_ref[...]
    ms = jnp.mean(x * x, axis=-1, keepdims=True)
    h = x * lax.rsqrt(ms + EPS) * g_ref[...]
    h_hi = h.astype(BF16)
    half = h.shape[1] // 2
    bits = lax.bitcast_convert_type(h_hi.astype(F32), jnp.uint32)
    h_ref[...] = bits[:, :half] | lax.shift_right_logical(bits[:, half:], jnp.uint32(16))
    h_lo = (h - h_hi.astype(F32)).astype(BF16)
    logits = _dot(h_hi, wr_ref[0]) + _dot(h_lo, wr_ref[0]) + _dot(h_hi, wr_ref[1])

    lane = _iota2(logits.shape, 1).astype(F32)
    lg = jnp.where(lane < N_EXPERTS, logits, -jnp.inf)
    m1 = jnp.max(lg, axis=-1, keepdims=True)
    e1 = jnp.min(jnp.where(lg == m1, lane, float(LANES)), axis=-1, keepdims=True)
    lg2 = jnp.where(lane == e1, -jnp.inf, lg)
    m2 = jnp.max(lg2, axis=-1, keepdims=True)
    e2 = jnp.min(jnp.where(lg2 == m2, lane, float(LANES)), axis=-1, keepdims=True)
    t = jnp.exp(m2 - m1)
    g1 = 1.0 / (1.0 + t)
    g2 = t / (1.0 + t)

    hot1 = lane == e1
    hot2 = lane == e2
    sel = jnp.where(hot1 | hot2, 1.0, 0.0).astype(BF16)
    prior = _dot(tri_ref[...], sel) + carry_ref[...]
    r1 = jnp.sum(jnp.where(hot1, prior, 0.0), axis=-1, keepdims=True)
    r2 = jnp.sum(jnp.where(hot2, prior, 0.0), axis=-1, keepdims=True)
    carry_ref[...] = carry_ref[...] + jnp.sum(sel.astype(F32), axis=0, keepdims=True)
    cnt_ref[...] = carry_ref[...]

    out = jnp.where(lane == 0, e1, 0.0)
    out = jnp.where(lane == 1, e2, out)
    out = jnp.where(lane == 2, g1, out)
    out = jnp.where(lane == 3, g2, out)
    out = jnp.where(lane == 4, r1, out)
    out = jnp.where(lane == 5, r2, out)
    route_ref[...] = out


def _router(x2, g, wr2, tri):
    n, dm = x2.shape
    tm = TM_PROJ
    const = lambda i: (0, 0)
    return pl.pallas_call(
        _router_kernel,
        out_shape=(jax.ShapeDtypeStruct((n, dm // 2), jnp.uint32),
                   jax.ShapeDtypeStruct((n, LANES), F32),
                   jax.ShapeDtypeStruct((1, LANES), F32)),
        grid=(n // tm,),
        in_specs=[
            pl.BlockSpec((tm, dm), lambda i: (i, 0)),
            pl.BlockSpec((1, dm), const),
            pl.BlockSpec((2, dm, LANES), lambda i: (0, 0, 0)),
            pl.BlockSpec((tm, tm), const),
        ],
        out_specs=(
            pl.BlockSpec((tm, dm // 2), lambda i: (i, 0)),
            pl.BlockSpec((tm, LANES), lambda i: (i, 0)),
            pl.BlockSpec((1, LANES), const),
        ),
        scratch_shapes=[pltpu.VMEM((1, LANES), F32)],
        compiler_params=_cparams(("arbitrary",)),
        name="moe_router",
    )(x2, g, wr2, tri)


def _row_copy(src_ref, src_row, dst_ref, dst_row, sem):
    return pltpu.make_async_copy(src_ref.at[pl.ds(src_row, 1)], dst_ref.at[pl.ds(dst_row, 1)], sem)


def _dispatch_kernel(d1_ref, d2_ref, cnt_ref, h_ref, xs_ref, zero_ref, sem, tail_sem, *, tm, cap, nexp, tail):
    def issue(r, c):
        _row_copy(h_ref, r, xs_ref, d1_ref[0, 0, r], sem).start()
        _row_copy(h_ref, r, xs_ref, d2_ref[0, 0, r], sem).start(priority=1)
        return c

    lax.fori_loop(0, tm, issue, 0, unroll=8)

    def aligned_end(e):
        return ((cnt_ref[e] + (SUBLANES - 1)) // SUBLANES) * SUBLANES

    def tail_copy(e, k):
        row0 = pl.multiple_of(e * cap + aligned_end(e) + k * tm, SUBLANES)
        return pltpu.make_async_copy(zero_ref, xs_ref.at[pl.ds(row0, tm)], tail_sem)

    def tail_rows(e, fn):
        for k in range(SUBLANES - 1):
            @pl.when(cnt_ref[e] + k < aligned_end(e))
            def _():
                fn(_row_copy(zero_ref, 0, xs_ref, e * cap + cnt_ref[e] + k, tail_sem))

    @pl.when(pl.program_id(0) == pl.num_programs(0) - 1)
    def _():
        zero_ref[...] = jnp.zeros_like(zero_ref)
        for e in range(nexp):
            for k in range(tail // tm):
                tail_copy(e, k).start()
            tail_rows(e, lambda cp: cp.start())
        for e in range(nexp):
            for k in range(tail // tm):
                tail_copy(e, k).wait()
            tail_rows(e, lambda cp: cp.wait())

    for _ in range(2):
        pltpu.make_async_copy(h_ref, xs_ref.at[pl.ds(0, tm)], sem).wait()


def _dispatch(d1, d2, cnt, h, cap, tail):
    n, wd = h.shape
    tm = TM_PROJ
    nt = n // tm
    idx_spec = pl.BlockSpec((1, 1, tm), lambda i: (i, 0, 0), memory_space=pltpu.SMEM)
    return pl.pallas_call(
        functools.partial(_dispatch_kernel, tm=tm, cap=cap, nexp=N_EXPERTS, tail=tail),
        out_shape=jax.ShapeDtypeStruct((N_EXPERTS * cap, wd), h.dtype),
        grid=(nt,),
        in_specs=[idx_spec, idx_spec, pl.BlockSpec(memory_space=pltpu.SMEM),
                  pl.BlockSpec((tm, wd), lambda i: (i, 0))],
        out_specs=pl.BlockSpec(memory_space=pl.ANY),
        scratch_shapes=[pltpu.VMEM((tm, wd), h.dtype), pltpu.SemaphoreType.DMA, pltpu.SemaphoreType.DMA],
        compiler_params=pltpu.CompilerParams(dimension_semantics=("arbitrary",),
                                             vmem_limit_bytes=VMEM_LIMIT_BYTES, disable_bounds_checks=True),
        name="moe_dispatch",
    )(d1.reshape(nt, 1, tm), d2.reshape(nt, 1, tm), cnt, h)


def _combine_kernel(d1_ref, d2_ref, n1_ref, n2_ref, x_ref, route_ref, ys_ref, o_ref, y1_ref, y2_ref, sems, *, tm):
    i = pl.program_id(0)
    slot = jnp.bitwise_and(i, 1)

    def fetch(a_ref, b_ref, s):
        def issue(r, c):
            _row_copy(ys_ref, a_ref[0, 0, r], y1_ref.at[s], r, sems.at[s]).start()
            _row_copy(ys_ref, b_ref[0, 0, r], y2_ref.at[s], r, sems.at[s]).start(priority=1)
            return c

        lax.fori_loop(0, tm, issue, 0, unroll=8)

    @pl.when(i == 0)
    def _():
        fetch(d1_ref, d2_ref, slot)

    @pl.when(i + 1 < pl.num_programs(0))
    def _():
        fetch(n1_ref, n2_ref, 1 - slot)

    for buf in (y1_ref, y2_ref):
        pltpu.make_async_copy(ys_ref.at[pl.ds(0, tm)], buf.at[slot], sems.at[slot]).wait()
    o_ref[...] = x_ref[...] + route_ref[:, 2:3] * y1_ref[slot] + route_ref[:, 3:4] * y2_ref[slot]


def _combine(d1, d2, x2, route, ys):
    n, dm = x2.shape
    tm = TM_PROJ
    nt = n // tm
    idx_spec = pl.BlockSpec((1, 1, tm), lambda i: (i, 0, 0), memory_space=pltpu.SMEM)
    next_spec = pl.BlockSpec((1, 1, tm), lambda i: (jnp.minimum(i + 1, nt - 1), 0, 0), memory_space=pltpu.SMEM)
    d1, d2 = d1.reshape(nt, 1, tm), d2.reshape(nt, 1, tm)
    return pl.pallas_call(
        functools.partial(_combine_kernel, tm=tm),
        out_shape=jax.ShapeDtypeStruct((n, dm), F32),
        grid=(nt,),
        in_specs=[idx_spec, idx_spec, next_spec, next_spec, pl.BlockSpec((tm, dm), lambda i: (i, 0)),
                  pl.BlockSpec((tm, LANES), lambda i: (i, 0)), pl.BlockSpec(memory_space=pl.ANY)],
        out_specs=pl.BlockSpec((tm, dm), lambda i: (i, 0)),
        scratch_shapes=[pltpu.VMEM((2, tm, dm), F32), pltpu.VMEM((2, tm, dm), F32), pltpu.SemaphoreType.DMA((2,))],
        compiler_params=pltpu.CompilerParams(dimension_semantics=("arbitrary",),
                                             vmem_limit_bytes=VMEM_LIMIT_BYTES, disable_bounds_checks=True),
        name="moe_combine",
    )(d1, d2, d1, d2, x2, route, ys)


def _experts_kernel(te_ref, blk_ref, nused_ref, xs_ref, w1_ref, w3_ref, w2_ref, o_ref, x_ref, hid_ref, acc_ref):
    t = pl.program_id(0)
    c = pl.program_id(1)
    nc = pl.num_programs(1)

    @pl.when(jnp.logical_and(t < nused_ref[0], c == 0))
    def _():
        w = xs_ref[...]
        half = w.shape[1]
        x_ref[:, :half] = lax.bitcast_convert_type(w & jnp.uint32(0xFFFF0000), F32).astype(BF16)
        x_ref[:, half:] = lax.bitcast_convert_type(lax.shift_left(w, jnp.uint32(16)), F32).astype(BF16)

    @pl.when(t < nused_ref[0])
    def _():
        x = x_ref[...]
        tf = w1_ref.shape[2]
        for s0 in range(0, tf, GROUP_W):
            cs = slice(s0, s0 + GROUP_W)
            a = _dot(x, w1_ref[0, :, cs])
            b = _dot(x, w3_ref[0, :, cs])
            hid_ref[:, cs] = (jax.nn.silu(a) * b).astype(BF16)
        part = _dot(hid_ref[...], w2_ref[0])

        @pl.when(c == 0)
        def _():
            acc_ref[...] = part

        @pl.when(c > 0)
        def _():
            acc_ref[...] = acc_ref[...] + part

        @pl.when(c == nc - 1)
        def _():
            o_ref[...] = acc_ref[...]

    @pl.when(jnp.logical_and(t >= nused_ref[0], c == nc - 1))
    def _():
        o_ref[...] = jnp.zeros_like(o_ref)


def _experts(tile_expert, tile_block, nused, xs, w1, w3, w2, nt):
    rows, wd = xs.shape
    dm = 2 * wd
    dff = w1.shape[2]
    tm, tf = TM_EXPERT, TF_EXPERT
    nc = dff // tf
    spare = rows // tm

    def cc(t, c, nu):
        return jnp.where(t < nu[0], c, nc - 1)

    grid_spec = pltpu.PrefetchScalarGridSpec(
        num_scalar_prefetch=3,
        grid=(nt, nc),
        in_specs=[
            pl.BlockSpec((tm, wd), lambda t, c, te, tb, nu: (tb[t], 0)),
            pl.BlockSpec((1, dm, tf), lambda t, c, te, tb, nu: (te[t], 0, cc(t, c, nu))),
            pl.BlockSpec((1, dm, tf), lambda t, c, te, tb, nu: (te[t], 0, cc(t, c, nu))),
            pl.BlockSpec((1, tf, dm), lambda t, c, te, tb, nu: (te[t], cc(t, c, nu), 0)),
        ],
        out_specs=pl.BlockSpec((tm, dm), lambda t, c, te, tb, nu: (jnp.where(t < nu[0], tb[t], spare), 0)),
        scratch_shapes=[pltpu.VMEM((tm, dm), BF16), pltpu.VMEM((tm, tf), BF16), pltpu.VMEM((tm, dm), F32)],
    )
    return pl.pallas_call(
        _experts_kernel,
        out_shape=jax.ShapeDtypeStruct((rows + tm, dm), F32),
        grid_spec=grid_spec,
        compiler_params=_cparams(("arbitrary", "arbitrary")),
        name="moe_experts",
    )(tile_expert, tile_block, nused, xs, w1, w3, w2)


def _moe(x2, g, w_router, w1, w3, w2, layer_idx, tri):
    n, dm = x2.shape
    wr = jnp.zeros((dm, LANES), F32).at[:, :N_EXPERTS].set(w_router.astype(F32))
    wr_hi = wr.astype(BF16)
    wr_lo = (wr - wr_hi.astype(F32)).astype(BF16)
    h, route, counts = _router(x2, g, jnp.stack([wr_hi, wr_lo]), tri)

    tm = TM_EXPERT
    assert n % tm == 0 and tm % TM_PROJ == 0
    cap = n + tm
    d1 = route[:, 0].astype(jnp.int32) * cap + route[:, 4].astype(jnp.int32)
    d2 = route[:, 1].astype(jnp.int32) * cap + route[:, 5].astype(jnp.int32)
    cnt = counts[0, :N_EXPERTS].astype(jnp.int32)

    nt = 2 * n // tm + N_EXPERTS
    ntile = (cnt + tm - 1) // tm
    ends = jnp.cumsum(ntile)
    nused = ends[-1]
    t = jnp.minimum(jnp.arange(nt, dtype=jnp.int32), nused - 1)
    tile_expert = jnp.sum(t[:, None] >= ends[None, :], axis=1).astype(jnp.int32)
    tile_block = tile_expert * (cap // tm) + t - (ends - ntile)[tile_expert]

    xs = _dispatch(d1, d2, cnt, h, cap, tm)
    ys = _experts(tile_expert + layer_idx * N_EXPERTS, tile_block.astype(jnp.int32),
                  nused.reshape(1).astype(jnp.int32), xs, w1, w3, w2, nt)
    return _combine(d1, d2, x2, route, ys)


def kernel(x, norm_mix_g, w_in, ssm_lam_re, ssm_lam_im, ssm_log_dt, ssm_b_re, ssm_b_im, ssm_c_re, ssm_c_im,
           ssm_d, ssm_w_glu, ssm_b_glu, ch_q_norm_g, ch_k_norm_g, ch_rel_bias, df_q_norm_g, df_k_norm_g,
           df_lambda, out_norm_g, w_out, norm_ffn_g, ffn_w1, ffn_w3, ffn_w2, moe_router, moe_w1, moe_w3, moe_w2):
    bsz, seq, dm = x.shape
    depth = w_in.shape[0]
    n = bsz * seq
    nheads = GROUP_W // HEAD_DIM
    ind64 = _block_indicator(GROUP_W, HEAD_DIM)
    ind32 = _block_indicator(GROUP_W, DF_QK_DIM)
    tri = (jnp.arange(TM_PROJ)[:, None] > jnp.arange(TM_PROJ)[None, :]).astype(BF16)
    slopes = [2.0 ** (-8.0 * (h + 1) / nheads) for h in range(nheads)]
    slopes2 = tuple(s * LOG2E for s in slopes)
    df_kb, df_db = _df_bias_tiles(slopes2, TQ_ATT, TK_ATT)
    later = (jnp.arange(TK_ATT)[:, None] < jnp.arange(TK_ATT)[None, :]).astype(BF16)

    moe_w = tuple(w.reshape((-1,) + w.shape[2:]).astype(BF16) for w in (moe_w1, moe_w3, moe_w2))

    x2 = x.reshape(n, dm)
    for layer in range(depth):
        qkg = jnp.stack([
            jnp.tile(ch_q_norm_g[layer].astype(F32), nheads) * (HEAD_DIM ** -0.5 * LOG2E),
            jnp.tile(ch_k_norm_g[layer].astype(F32), nheads),
            jnp.tile(df_q_norm_g[layer].astype(F32).reshape(-1), nheads) * (DF_QK_DIM ** -0.5 * LOG2E),
            jnp.tile(df_k_norm_g[layer].astype(F32).reshape(-1), nheads),
        ])
        u_tm, qk, v_sb, v_ch, v_df = _in_proj(x2, norm_mix_g[layer].reshape(1, dm), w_in[layer].astype(BF16), qkg,
                             ind64, ind32, bsz, seq)

        wb, lre, lim, wc = _s5_params(ssm_lam_re[layer], ssm_lam_im[layer], ssm_log_dt[layer],
                                      ssm_b_re[layer], ssm_b_im[layer], ssm_c_re[layer], ssm_c_im[layer], bsz)
        y_ssm = _s5(u_tm, wb, lre, lim, wc, ssm_d[layer].reshape(1, GROUP_W),
                    ssm_w_glu[layer].astype(BF16), ssm_b_glu[layer].reshape(1, GROUP_W), bsz)

        tq, tk = TQ_ATT, TK_ATT
        const3 = lambda b, i: (0, 0, 0)
        y_sb = _attention_call(
            functools.partial(_sb_kernel, tq=tq, tk=tk, nheads=nheads),
            qk, 0, v_sb,
            (later,), [pl.BlockSpec((tk, tk), lambda b, i: (0, 0))],
            [pltpu.VMEM((nheads * tq, GROUP_W), BF16), pltpu.VMEM((SUBLANES, tq), F32),
             pltpu.VMEM((GROUP_W, tq), F32),
             pltpu.VMEM((tk, nheads * tq), F32), pltpu.VMEM((tk, nheads * tq), F32)],
            bsz, seq, "sb_attention")

        bias, nback = _ch_bias_tiles(ch_rel_bias[layer], tq, tk)
        y_ch = _attention_call(
            functools.partial(_ch_kernel, tq=tq, tk=tk, nheads=nheads, nback=nback),
            qk, 2, v_ch,
            (bias,), [pl.BlockSpec(bias.shape, lambda b, i: (0, 0, 0, 0))],
            [pltpu.VMEM((nheads * tq, GROUP_W), BF16), pltpu.VMEM(((nback + 1) * tk, nheads * tq), F32)],
            bsz, seq, "ch_attention")

        lambda_init = 0.8 - 0.6 * math.exp(-0.3 * layer)
        lam_p = df_lambda[layer].astype(F32)
        lam = jnp.exp(jnp.sum(lam_p[0] * lam_p[1])) - jnp.exp(jnp.sum(lam_p[2] * lam_p[3])) + lambda_init
        y_df = _attention_call(
            functools.partial(_df_kernel, tq=tq, tk=tk, nheads=nheads, slopes2=slopes2),
            qk, 4, v_df,
            (df_kb, df_db), [pl.BlockSpec(df_kb.shape, const3), pl.BlockSpec(df_db.shape, const3)],
            [pltpu.VMEM((2 * nheads * tq, GROUP_W), BF16), pltpu.VMEM((2 * nheads, tq), F32),
             pltpu.VMEM((2 * nheads * VA_ROWS, tq), F32),
             pltpu.VMEM((tk, 2 * nheads * tq), F32), pltpu.VMEM((tk, 2 * nheads * tq), F32),
             pltpu.VMEM((2 * nheads, tq), F32), pltpu.VMEM((2 * nheads, tq), F32)],
            bsz, seq, "df_attention", smem=(lam.reshape(1),))

        head_scale = jnp.concatenate([jnp.ones((dm - GROUP_W,), F32),
                                      jnp.full((GROUP_W,), 1.0 - lambda_init, F32)])
        g_out = (out_norm_g[layer].astype(F32) * head_scale).reshape(1, dm)
        x2 = _out_proj(y_ssm, y_sb, y_ch, y_df, g_out, ind64, w_out[layer].astype(BF16), x2, bsz, seq)

        idx = layer // 2
        g_ffn = norm_ffn_g[layer].reshape(1, dm)
        if layer % 2 == 0:
            x2 = _ffn(x2, g_ffn, ffn_w1[idx].astype(BF16), ffn_w3[idx].astype(BF16), ffn_w2[idx].astype(BF16))
        else:
            x2 = _moe(x2, g_ffn, moe_router[idx], *moe_w, idx, tri)
    return x2.reshape(bsz, seq, dm)
```

```python
import functools
import math

import jax
import jax.numpy as jnp
import numpy as np
from jax import lax
from jax.experimental import pallas as pl
from jax.experimental.pallas import tpu as pltpu

F32 = jnp.float32
BF16 = jnp.bfloat16

EPS = 1e-6
NEG_INF = -1e30
HEAD_DIM = 64
GROUP_W = 256
CHUNK = 64
SSM_GROUP = 16
SSM_STATE = 64
DF_QK_DIM = 32
CH_LEFT_CHUNKS = 8
REL_CLIP = 128
N_EXPERTS = 8
LOG2E = 1.4426950408889634

VMEM_LIMIT_BYTES = 56 * 1024 * 1024
SUBLANES = 8
LANES = 128

TM_PROJ = 512
TM_MIX = 1024
TQ_ATT = 256
TK_ATT = 256
SSM_STEPS = 64
TM_EXPERT = 512
TF_EXPERT = 1792


def _cparams(sem):
    return pltpu.CompilerParams(dimension_semantics=sem, vmem_limit_bytes=VMEM_LIMIT_BYTES)


def _dot(a, b):
    return jnp.dot(a, b, preferred_element_type=F32)


def _dot_nt(a, b):
    return lax.dot_general(a, b, (((1,), (1,)), ((), ())), preferred_element_type=F32)


def _group_ms(v, ind, inv_size):
    return _dot((v * v).astype(BF16), ind) * inv_size


def _block_indicator(width, group):
    r = jnp.arange(width)[:, None] // group
    c = jnp.arange(width)[None, :] // group
    return (r == c).astype(BF16)


def _in_proj_kernel(x_ref, g_ref, w_ref, qkg_ref, ind64_ref, ind32_ref, u_ref, qk_ref, vsb_ref, vch_ref, vdf_ref,
                    *, tk, nheads):
    x = x_ref[...]
    ms = jnp.mean(x * x, axis=-1, keepdims=True)
    h = (x * lax.rsqrt(ms + EPS) * g_ref[...]).astype(BF16)

    def cols(c):
        return _dot(h, w_ref[:, c * GROUP_W:(c + 1) * GROUP_W])

    def put(slot, val):
        qk_ref[:, slot * GROUP_W:(slot + 1) * GROUP_W] = val.astype(BF16)

    def put_values(v_ref, v, ones_row):
        for kt in range(v.shape[0] // tk):
            vt = v[kt * tk:(kt + 1) * tk, :].T
            if not ones_row:
                v_ref[kt] = vt.astype(BF16)
                continue
            pad = jnp.where(_iota2((VA_ROWS - HEAD_DIM, tk), 0) == 0, 1.0, 0.0).astype(BF16)
            for hd in range(nheads):
                v_ref[kt, hd * VA_ROWS:hd * VA_ROWS + HEAD_DIM, :] = vt[hd * HEAD_DIM:(hd + 1) * HEAD_DIM, :].astype(BF16)
                v_ref[kt, hd * VA_ROWS + HEAD_DIM:(hd + 1) * VA_ROWS, :] = pad

    u_ref[...] = cols(0)
    put(0, cols(1) * (HEAD_DIM ** -0.5 * LOG2E))
    put(1, cols(2))
    put_values(vsb_ref, cols(3), False)
    for c, gi in ((4, 0), (5, 1)):
        a = cols(c)
        ms_h = _group_ms(a, ind64_ref[...], 1.0 / HEAD_DIM)
        put(c - 2, a * lax.rsqrt(ms_h + EPS) * qkg_ref[gi:gi + 1, :])
    put_values(vch_ref, cols(6), True)
    for c, gi in ((7, 2), (8, 3)):
        a = cols(c)
        ms_h = _group_ms(a, ind32_ref[...], 1.0 / DF_QK_DIM)
        put(c - 3, a * lax.rsqrt(ms_h + EPS) * qkg_ref[gi:gi + 1, :])
    put_values(vdf_ref, cols(9), True)


def _in_proj(x2, g, w, qkg, ind64, ind32, bsz, seq):
    n, dm = x2.shape
    tm, tk = TM_MIX, TK_ATT
    ns = seq // tm
    nheads = GROUP_W // HEAD_DIM
    const = lambda i: (0, 0)

    def values(rows):
        return (jax.ShapeDtypeStruct((bsz, seq // tk, rows, tk), BF16),
                pl.BlockSpec((None, tm // tk, rows, tk), lambda i: (i // ns, i % ns, 0, 0)))

    v_shapes, v_specs = zip(values(GROUP_W), values(nheads * VA_ROWS), values(nheads * VA_ROWS))
    return pl.pallas_call(
        functools.partial(_in_proj_kernel, tk=tk, nheads=nheads),
        out_shape=(jax.ShapeDtypeStruct((seq, bsz * GROUP_W), F32),
                   jax.ShapeDtypeStruct((n, 6 * GROUP_W), BF16)) + v_shapes,
        grid=(n // tm,),
        in_specs=[
            pl.BlockSpec((tm, dm), lambda i: (i, 0)),
            pl.BlockSpec((1, dm), const),
            pl.BlockSpec(w.shape, const),
            pl.BlockSpec((4, GROUP_W), const),
            pl.BlockSpec((GROUP_W, GROUP_W), const),
            pl.BlockSpec((GROUP_W, GROUP_W), const),
        ],
        out_specs=(
            pl.BlockSpec((tm, GROUP_W), lambda i: (i % ns, i // ns)),
            pl.BlockSpec((tm, 6 * GROUP_W), lambda i: (i, 0)),
        ) + v_specs,
        compiler_params=_cparams(("arbitrary",)),
        name="in_proj",
    )(x2, g, w, qkg, ind64, ind32)


def _s5_kernel(u_ref, wb_ref, lre_ref, lim_ref, wc_ref, d_ref, wg_ref, bg_ref, y_ref, bu_ref, st_ref, rows_ref,
               *, steps, nstate):
    @pl.when(pl.program_id(0) == 0)
    def _():
        st_ref[...] = jnp.zeros_like(st_ref)

    nhalf, bsz = rows_ref.shape[0], rows_ref.shape[1] // steps
    w = nhalf * LANES
    for b in range(bsz):
        for k in range(nhalf):
            c0 = b * w + k * LANES
            rows_ref[k, pl.ds(b, steps, stride=bsz), :] = u_ref[:, c0:c0 + LANES]
    u = jnp.concatenate([rows_ref[k] for k in range(nhalf)], axis=1)
    bu_ref[...] = _dot(u.astype(BF16), wb_ref[...])
    lre = lre_ref[...]
    lim = lim_ref[...]

    def step(t, carry):
        sre, sim = carry
        r = pl.multiple_of(t * SUBLANES, SUBLANES)
        bre = bu_ref[pl.ds(r, SUBLANES), 0:nstate]
        bim = bu_ref[pl.ds(r, SUBLANES), nstate:2 * nstate]
        nre = lre * sre - lim * sim + bre
        nim = lre * sim + lim * sre + bim
        bu_ref[pl.ds(r, SUBLANES), 0:nstate] = nre
        bu_ref[pl.ds(r, SUBLANES), nstate:2 * nstate] = nim
        return nre, nim

    sre, sim = lax.fori_loop(0, steps, step, (st_ref[:, 0:nstate], st_ref[:, nstate:2 * nstate]))
    st_ref[:, 0:nstate] = sre
    st_ref[:, nstate:2 * nstate] = sim

    y = _dot(bu_ref[...].astype(BF16), wc_ref[...]) + d_ref[...] * u
    y = jax.nn.gelu(y)
    gate = jax.nn.sigmoid(_dot(y.astype(BF16), wg_ref[...]) + bg_ref[...])
    out = y * gate
    for k in range(nhalf):
        rows_ref[k] = out[:, k * LANES:(k + 1) * LANES]
    for b in range(bsz):
        for k in range(nhalf):
            c0 = b * w + k * LANES
            y_ref[:, c0:c0 + LANES] = rows_ref[k, pl.ds(b, steps, stride=bsz), :]


def _s5(u_tm, wb, lre, lim, wc, d, wg, bg, bsz):
    seq = u_tm.shape[0]
    w = u_tm.shape[1] // bsz
    assert bsz == SUBLANES
    blk = SSM_STEPS * bsz
    nstate2 = wb.shape[1]
    const = lambda c: (0, 0)
    return pl.pallas_call(
        functools.partial(_s5_kernel, steps=SSM_STEPS, nstate=nstate2 // 2),
        out_shape=jax.ShapeDtypeStruct(u_tm.shape, F32),
        grid=(seq // SSM_STEPS,),
        in_specs=[
            pl.BlockSpec((SSM_STEPS, bsz * w), lambda c: (c, 0)),
            pl.BlockSpec((w, nstate2), const),
            pl.BlockSpec((bsz, nstate2 // 2), const),
            pl.BlockSpec((bsz, nstate2 // 2), const),
            pl.BlockSpec((nstate2, w), const),
            pl.BlockSpec((1, w), const),
            pl.BlockSpec((w, w), const),
            pl.BlockSpec((1, w), const),
        ],
        out_specs=pl.BlockSpec((SSM_STEPS, bsz * w), lambda c: (c, 0)),
        scratch_shapes=[pltpu.VMEM((blk, nstate2), F32), pltpu.VMEM((bsz, nstate2), F32),
                        pltpu.VMEM((w // LANES, blk, LANES), F32)],
        compiler_params=_cparams(("arbitrary",)),
        name="s5_mixer",
    )(u_tm, wb, lre, lim, wc, d, wg, bg)


def _s5_params(lam_re, lam_im, log_dt, b_re, b_im, c_re, c_im, bsz):
    g, n = lam_re.shape
    dt = jnp.exp(log_dt.astype(F32))[:, None]
    xr = lam_re * dt
    th = lam_im * dt
    er = jnp.exp(xr)
    lbr = er * jnp.cos(th)
    lbi = er * jnp.sin(th)
    ar = jnp.expm1(xr) * jnp.cos(th) - 2.0 * jnp.sin(0.5 * th) ** 2
    ai = lbi
    den = lam_re * lam_re + lam_im * lam_im
    fr = (ar * lam_re + ai * lam_im) / den
    fi = (ai * lam_re - ar * lam_im) / den
    bbr = fr[..., None] * b_re - fi[..., None] * b_im
    bbi = fr[..., None] * b_im + fi[..., None] * b_re
    eye = jnp.eye(g, dtype=F32)
    p = b_re.shape[-1]
    wb_re = jnp.einsum("gnp,gh->gphn", bbr, eye).reshape(g * p, g * n)
    wb_im = jnp.einsum("gnp,gh->gphn", bbi, eye).reshape(g * p, g * n)
    wb = jnp.concatenate([wb_re, wb_im], axis=1).astype(BF16)
    wc_re = jnp.einsum("gpn,gh->gnhp", c_re, eye).reshape(g * n, g * p)
    wc_im = jnp.einsum("gpn,gh->gnhp", c_im, eye).reshape(g * n, g * p)
    wc = jnp.concatenate([wc_re, -wc_im], axis=0).astype(BF16)
    lre = jnp.broadcast_to(lbr.reshape(1, g * n), (bsz, g * n))
    lim = jnp.broadcast_to(lbi.reshape(1, g * n), (bsz, g * n))
    return wb, lre, lim, wc


VA_ROWS = 2 * HEAD_DIM


def _iota2(shape, dim):
    return lax.broadcasted_iota(jnp.int32, shape, dim)


def _stack_masked_q(q_ref, qs_ref, tq, ngroups, width):
    lane = _iota2((1, GROUP_W), 1)
    q = q_ref[...]
    for g in range(ngroups):
        keep = (lane >= g * width) & (lane < (g + 1) * width)
        qs_ref[g * tq:(g + 1) * tq, :] = jnp.where(keep, q, jnp.zeros_like(q))


def _key_scores(k_ref, qs_ref, j, tk):
    row0 = pl.multiple_of(jnp.maximum(j, 0) * tk, tk)
    return _dot_nt(k_ref[pl.ds(row0, tk), :], qs_ref[...])


def _score_prefetch(k_ref, qs_ref, j, dst_ref, tq, tk, bias_of=None, max_ref=None):
    row0 = pl.multiple_of(jnp.maximum(j, 0) * tk, tk)

    def emit(g):
        cols = slice(g * tq, (g + 1) * tq)
        s = _dot_nt(k_ref[pl.ds(row0, tk), :], qs_ref[cols, :])
        if bias_of is not None:
            s = s + bias_of(g)[...]
        dst_ref[:, cols] = s
        if max_ref is not None:
            max_ref[g:g + 1, :] = jnp.max(s, axis=0, keepdims=True)

    return emit


def _paired_key_tiles(i, k_ref, qs_ref, tile_fn, buf_a, buf_b, tq, tk, bias_of=None):
    def body(n, c):
        ja = i - 1 - 2 * n
        tile_fn(ja, buf_a, _score_prefetch(k_ref, qs_ref, ja - 1, buf_b[0], tq, tk, bias_of, buf_b[1]))
        tile_fn(ja - 1, buf_b, _score_prefetch(k_ref, qs_ref, ja - 2, buf_a[0], tq, tk, bias_of, buf_a[1]))
        return c

    lax.fori_loop(0, i // 2, body, 0)

    @pl.when(jnp.bitwise_and(i, 1) == 1)
    def _():
        tile_fn(0, buf_a, lambda g: None)


def _sb_kernel(q_ref, k_ref, vt_ref, later_ref, o_ref, qs_ref, carry_ref, acc_ref, za_ref, zb_ref,
               *, tq, tk, nheads):
    i = pl.program_id(1)
    _stack_masked_q(q_ref, qs_ref, tq, nheads, HEAD_DIM)

    def tile(j, buf, prefetch, diagonal=False):
        zt = buf[0]
        if diagonal:
            before = _iota2((tk, tq), 0) < _iota2((tk, tq), 1)

        def scan_stage(h):
            z = zt[:, h * tq:(h + 1) * tq]
            neg_abs = lax.bitcast_convert_type(
                lax.bitcast_convert_type(z, jnp.uint32) | jnp.uint32(0x80000000), F32)
            sp = jnp.maximum(z, 0.0) + jnp.log2(1.0 + jnp.exp2(neg_abs))
            nl1m = jnp.where(before, sp, 0.0) if diagonal else sp
            inside = _dot(later_ref[...], nl1m.astype(BF16))
            return z - sp, inside, nl1m[0:1, :]

        def value_stage(h, logit, inside, first_row):
            hs = slice(h * HEAD_DIM, (h + 1) * HEAD_DIM)
            between = inside if diagonal else inside + carry_ref[h:h + 1, :]
            w = jnp.exp2(logit - between)
            if diagonal:
                w = jnp.where(before, w, 0.0)
            pv = _dot(vt_ref[j, hs, :], w.astype(BF16))
            total = inside[0:1, :] + first_row
            if diagonal:
                acc_ref[hs, :] = pv
                carry_ref[h:h + 1, :] = total
            else:
                acc_ref[hs, :] = acc_ref[hs, :] + pv
                carry_ref[h:h + 1, :] = carry_ref[h:h + 1, :] + total

        prefetch(0)
        pending = scan_stage(0)
        for h in range(1, nheads):
            prefetch(h)
            upcoming = scan_stage(h)
            value_stage(h - 1, *pending)
            pending = upcoming
        value_stage(nheads - 1, *pending)

    tile(i, (_key_scores(k_ref, qs_ref, i, tk), None), _score_prefetch(k_ref, qs_ref, i - 1, za_ref, tq, tk),
         diagonal=True)
    _paired_key_tiles(i, k_ref, qs_ref, tile, (za_ref, None), (zb_ref, None), tq, tk)
    o_ref[...] = acc_ref[...].T.astype(o_ref.dtype)


def _softmax_tile(buf, tq, off, va, g, first, m_ref, acc_ref):
    st_ref, max_ref = buf
    rows = slice(g * VA_ROWS, (g + 1) * VA_ROWS)
    parts, alphas = [], []
    for c0 in range(0, tq, LANES):
        cs = slice(c0, c0 + LANES)
        sc = slice(g * tq + c0, g * tq + c0 + LANES)
        blk_max = max_ref[g:g + 1, cs]
        if first:
            m_new = blk_max
            shift = m_new
        else:
            m_old = m_ref[g:g + 1, cs]
            m_new = jnp.maximum(m_old, blk_max + off)
            shift = m_new - off
            alphas.append(jnp.exp2(m_old - m_new))
        parts.append(jnp.exp2(st_ref[:, sc] - shift).astype(BF16))
        m_ref[g:g + 1, cs] = m_new
    pv = _dot(va, jnp.concatenate(parts, axis=1))
    if first:
        acc_ref[rows, :] = pv
    else:
        acc_ref[rows, :] = jnp.concatenate(alphas, axis=1) * acc_ref[rows, :] + pv


def _normalised(acc_ref, g):
    base = g * VA_ROWS
    return acc_ref[base:base + HEAD_DIM, :] / acc_ref[base + HEAD_DIM:base + HEAD_DIM + 1, :]


def _ch_kernel(q_ref, k_ref, va_ref, bias_ref, o_ref, qs_ref, z_ref, *, tq, tk, nheads, nback):
    i = pl.program_id(1)
    _stack_masked_q(q_ref, qs_ref, tq, nheads, HEAD_DIM)

    def attend(ntiles):
        tile_max = [[] for _ in range(nheads)]
        for n in range(ntiles):
            row0 = pl.multiple_of((i - n) * tk, tk)
            for h in range(nheads):
                cols = slice(h * tq, (h + 1) * tq)
                s = _dot_nt(k_ref[pl.ds(row0, tk), :], qs_ref[cols, :]) + bias_ref[h, n]
                z_ref[n * tk:(n + 1) * tk, cols] = s
                tile_max[h].append(jnp.max(s, axis=0, keepdims=True))
        outs = []
        for h in range(nheads):
            m = functools.reduce(jnp.maximum, tile_max[h])
            acc = None
            for n in range(ntiles):
                p = jnp.exp2(z_ref[n * tk:(n + 1) * tk, h * tq:(h + 1) * tq] - m).astype(BF16)
                pv = _dot(va_ref[i - n, h * VA_ROWS:(h + 1) * VA_ROWS, :], p)
                acc = pv if acc is None else acc + pv
            outs.append(acc[:HEAD_DIM, :] / acc[HEAD_DIM:HEAD_DIM + 1, :])
        o_ref[...] = jnp.concatenate(outs, axis=0).T.astype(o_ref.dtype)

    for ntiles in range(1, nback + 2):
        pl.when(jnp.minimum(i, nback) + 1 == ntiles)(functools.partial(attend, ntiles))


def _df_kernel(lam_ref, q_ref, k_ref, va_ref, kb_ref, db_ref, o_ref, qs_ref, m_ref, acc_ref,
               za_ref, zb_ref, maxa_ref, maxb_ref, *, tq, tk, nheads, slopes2):
    i = pl.program_id(1)
    _stack_masked_q(q_ref, qs_ref, tq, 2 * nheads, DF_QK_DIM)
    buf_a, buf_b = (za_ref, maxa_ref), (zb_ref, maxb_ref)

    def past_bias(g):
        return kb_ref.at[g // 2]

    def tile(j, buf, prefetch, diagonal=False):
        for g in range(2 * nheads):
            prefetch(g)
            h = g // 2
            va = va_ref[j, h * VA_ROWS:(h + 1) * VA_ROWS, :]
            if diagonal:
                _softmax_tile(buf, tq, 0.0, va, g, True, m_ref, acc_ref)
            else:
                _softmax_tile(buf, tq, slopes2[h] * ((j - i) * tk).astype(F32), va, g, False, m_ref, acc_ref)

    diag = _score_prefetch(k_ref, qs_ref, i, zb_ref, tq, tk, lambda g: db_ref.at[g // 2], maxb_ref)
    for g in range(2 * nheads):
        diag(g)
    tile(i, buf_b, _score_prefetch(k_ref, qs_ref, i - 1, za_ref, tq, tk, past_bias, maxa_ref), diagonal=True)
    _paired_key_tiles(i, k_ref, qs_ref, tile, buf_a, buf_b, tq, tk, past_bias)
    lam = lam_ref[0]
    out = jnp.concatenate([_normalised(acc_ref, 2 * h) - lam * _normalised(acc_ref, 2 * h + 1)
                           for h in range(nheads)], axis=0)
    o_ref[...] = out.T.astype(o_ref.dtype)


def _attention_call(kernel, qk, col0, vt, extra, extra_specs, scratch, bsz, seq, name, smem=()):
    tq, tk = TQ_ATT, TK_ATT
    assert tq == tk
    nq = seq // tq
    vrows = vt.shape[2]
    return pl.pallas_call(
        kernel,
        out_shape=jax.ShapeDtypeStruct((bsz * seq, GROUP_W), BF16),
        grid=(bsz, nq),
        in_specs=[pl.BlockSpec(memory_space=pltpu.SMEM) for _ in smem] + [
            pl.BlockSpec((tq, GROUP_W), lambda b, i: (b * nq + i, col0)),
            pl.BlockSpec((seq, GROUP_W), lambda b, i: (b, col0 + 1)),
            pl.BlockSpec((None, seq // tk, vrows, tk), lambda b, i: (b, 0, 0, 0)),
        ] + extra_specs,
        out_specs=pl.BlockSpec((tq, GROUP_W), lambda b, i: (b * nq + i, 0)),
        scratch_shapes=scratch,
        compiler_params=_cparams(("arbitrary", "arbitrary")),
        name=name,
    )(*smem, qk, qk, vt, *extra)


def _ch_bias_tiles(rel_bias, tq, tk):
    assert tq == tk
    nback = CH_LEFT_CHUNKS * CHUNK // tk
    nh = rel_bias.shape[0]
    n = np.arange(nback + 1)[:, None, None]
    s = np.arange(tk)[None, :, None]
    t = np.arange(tq)[None, None, :]
    delta = (n * tk + t) // CHUNK - s // CHUNK
    valid = (delta >= 0) & (delta <= CH_LEFT_CHUNKS)
    table = rel_bias.astype(F32) * LOG2E
    lo, hi = -(tk - 1), nback * tk + tq - 1
    ext = jnp.concatenate([jnp.repeat(table[:, :1], -REL_CLIP - lo, axis=1), table,
                           jnp.repeat(table[:, -1:], hi - REL_CLIP, axis=1)], axis=1)
    period = tk + tq
    tiles = []
    for m in range(nback + 1):
        w = ext[:, m * tk:m * tk + period - 1]
        w = jnp.concatenate([w, jnp.zeros((nh, 1), F32)], axis=1)
        skew = jnp.tile(w, (1, tk))[:, :tk * (period - 1)].reshape(nh, tk, period - 1)
        tiles.append(skew[:, :, tk - 1:tk - 1 + tq])
    bias = jnp.stack(tiles, axis=1)
    return jnp.where(jnp.asarray(valid)[None], bias, NEG_INF), nback


def _df_bias_tiles(slopes2, tq, tk):
    s = jnp.arange(tk, dtype=jnp.int32)[:, None]
    t = jnp.arange(tq, dtype=jnp.int32)[None, :]
    sl = jnp.asarray(slopes2, F32)[:, None, None]
    kb = sl * jnp.broadcast_to(s, (tk, tq)).astype(F32)
    allowed = (s // CHUNK) <= (t // CHUNK)
    db = jnp.where(allowed[None], sl * (t - jnp.abs(t - s)).astype(F32), NEG_INF)
    return kb, db


def _out_proj_kernel(ys_ref, ysb_ref, ych_ref, ydf_ref, g_ref, ind_ref, w_ref, x_ref, o_ref):
    acc = x_ref[...]
    for gi, y_ref in enumerate((ys_ref, ysb_ref, ych_ref, ydf_ref)):
        cs = slice(gi * GROUP_W, (gi + 1) * GROUP_W)
        y = y_ref[...].astype(F32)
        ms = _group_ms(y, ind_ref[...], 1.0 / HEAD_DIM)
        yn = (y * lax.rsqrt(ms + EPS) * g_ref[:, cs]).astype(BF16)
        acc = acc + _dot(yn, w_ref[cs, :])
    o_ref[...] = acc


def _out_proj(y_ssm_tm, y_sb, y_ch, y_df, g, ind64, w, x2, bsz, seq):
    n, dm = x2.shape
    tm = TM_MIX
    ns = seq // tm
    const = lambda i: (0, 0)
    tile = pl.BlockSpec((tm, GROUP_W), lambda i: (i, 0))
    return pl.pallas_call(
        _out_proj_kernel,
        out_shape=jax.ShapeDtypeStruct((n, dm), F32),
        grid=(n // tm,),
        in_specs=[
            pl.BlockSpec((tm, GROUP_W), lambda i: (i % ns, i // ns)),
            tile, tile, tile,
            pl.BlockSpec((1, dm), const),
            pl.BlockSpec((GROUP_W, GROUP_W), const),
            pl.BlockSpec((dm, dm), const),
            pl.BlockSpec((tm, dm), lambda i: (i, 0)),
        ],
        out_specs=pl.BlockSpec((tm, dm), lambda i: (i, 0)),
        compiler_params=_cparams(("arbitrary",)),
        name="out_proj",
    )(y_ssm_tm, y_sb, y_ch, y_df, g, ind64, w, x2)


def _ffn_kernel(x_ref, g_ref, w1_ref, w3_ref, w2_ref, o_ref, hid_ref, *, tf):
    x = x_ref[...]
    ms = jnp.mean(x * x, axis=-1, keepdims=True)
    h = (x * lax.rsqrt(ms + EPS) * g_ref[...]).astype(BF16)
    dff = w1_ref.shape[1]
    for c in range(dff // tf):
        cs = slice(c * tf, (c + 1) * tf)
        a = _dot(h, w1_ref[:, cs])
        b = _dot(h, w3_ref[:, cs])
        hid_ref[:, cs] = (jax.nn.silu(a) * b).astype(BF16)
    o_ref[...] = x + _dot(hid_ref[...], w2_ref[...])


def _ffn(x2, g, w1, w3, w2):
    n, dm = x2.shape
    dff = w1.shape[1]
    tm = TM_PROJ
    const = lambda i: (0, 0)
    return pl.pallas_call(
        functools.partial(_ffn_kernel, tf=GROUP_W),
        out_shape=jax.ShapeDtypeStruct((n, dm), F32),
        grid=(n // tm,),
        in_specs=[
            pl.BlockSpec((tm, dm), lambda i: (i, 0)),
            pl.BlockSpec((1, dm), const),
            pl.BlockSpec((dm, dff), const),
            pl.BlockSpec((dm, dff), const),
            pl.BlockSpec((dff, dm), const),
        ],
        out_specs=pl.BlockSpec((tm, dm), lambda i: (i, 0)),
        scratch_shapes=[pltpu.VMEM((tm, dff), BF16)],
        compiler_params=_cparams(("arbitrary",)),
        name="ffn_dense",
    )(x2, g, w1, w3, w2)


def _router_kernel(x_ref, g_ref, wr_ref, tri_ref, h_ref, route_ref, cnt_ref, carry_ref):
    @pl.when(pl.program_id(0) == 0)
    def _():
        carry_ref[...] = jnp.zeros_like(carry_ref)

    x = x_ref[...]
    ms = jnp.mean(x * x, axis=-1, keepdims=True)
    h = x * lax.rsqrt(ms + EPS) * g_ref[...]
    h_hi = h.astype(BF16)
    half = h.shape[1] // 2
    bits = lax.bitcast_convert_type(h_hi.astype(F32), jnp.uint32)
    h_ref[...] = bits[:, :half] | lax.shift_right_logical(bits[:, half:], jnp.uint32(16))
    h_lo = (h - h_hi.astype(F32)).astype(BF16)
    logits = _dot(h_hi, wr_ref[0]) + _dot(h_lo, wr_ref[0]) + _dot(h_hi, wr_ref[1])

    lane = _iota2(logits.shape, 1).astype(F32)
    lg = jnp.where(lane < N_EXPERTS, logits, -jnp.inf)
    m1 = jnp.max(lg, axis=-1, keepdims=True)
    e1 = jnp.min(jnp.where(lg == m1, lane, float(LANES)), axis=-1, keepdims=True)
    lg2 = jnp.where(lane == e1, -jnp.inf, lg)
    m2 = jnp.max(lg2, axis=-1, keepdims=True)
    e2 = jnp.min(jnp.where(lg2 == m2, lane, float(LANES)), axis=-1, keepdims=True)
    t = jnp.exp(m2 - m1)
    g1 = 1.0 / (1.0 + t)
    g2 = t / (1.0 + t)

    hot1 = lane == e1
    hot2 = lane == e2
    sel = jnp.where(hot1 | hot2, 1.0, 0.0).astype(BF16)
    prior = _dot(tri_ref[...], sel) + carry_ref[...]
    r1 = jnp.sum(jnp.where(hot1, prior, 0.0), axis=-1, keepdims=True)
    r2 = jnp.sum(jnp.where(hot2, prior, 0.0), axis=-1, keepdims=True)
    carry_ref[...] = carry_ref[...] + jnp.sum(sel.astype(F32), axis=0, keepdims=True)
    cnt_ref[...] = carry_ref[...]

    out = jnp.where(lane == 0, e1, 0.0)
    out = jnp.where(lane == 1, e2, out)
    out = jnp.where(lane == 2, g1, out)
    out = jnp.where(lane == 3, g2, out)
    out = jnp.where(lane == 4, r1, out)
    out = jnp.where(lane == 5, r2, out)
    route_ref[...] = out


def _router(x2, g, wr2, tri):
    n, dm = x2.shape
    tm = TM_PROJ
    const = lambda i: (0, 0)
    return pl.pallas_call(
        _router_kernel,
        out_shape=(jax.ShapeDtypeStruct((n, dm // 2), jnp.uint32),
                   jax.ShapeDtypeStruct((n, LANES), F32),
                   jax.ShapeDtypeStruct((1, LANES), F32)),
        grid=(n // tm,),
        in_specs=[
            pl.BlockSpec((tm, dm), lambda i: (i, 0)),
            pl.BlockSpec((1, dm), const),
            pl.BlockSpec((2, dm, LANES), lambda i: (0, 0, 0)),
            pl.BlockSpec((tm, tm), const),
        ],
        out_specs=(
            pl.BlockSpec((tm, dm // 2), lambda i: (i, 0)),
            pl.BlockSpec((tm, LANES), lambda i: (i, 0)),
            pl.BlockSpec((1, LANES), const),
        ),
        scratch_shapes=[pltpu.VMEM((1, LANES), F32)],
        compiler_params=_cparams(("arbitrary",)),
        name="moe_router",
    )(x2, g, wr2, tri)


def _row_copy(src_ref, src_row, dst_ref, dst_row, sem):
    return pltpu.make_async_copy(src_ref.at[pl.ds(src_row, 1)], dst_ref.at[pl.ds(dst_row, 1)], sem)


def _dispatch_kernel(d1_ref, d2_ref, cnt_ref, h_ref, xs_ref, zero_ref, sem, tail_sem, *, tm, cap, nexp, tail):
    def issue(r, c):
        _row_copy(h_ref, r, xs_ref, d1_ref[0, 0, r], sem).start()
        _row_copy(h_ref, r, xs_ref, d2_ref[0, 0, r], sem).start(priority=1)
        return c

    lax.fori_loop(0, tm, issue, 0, unroll=8)

    def aligned_end(e):
        return ((cnt_ref[e] + (SUBLANES - 1)) // SUBLANES) * SUBLANES

    def tail_copy(e, k):
        row0 = pl.multiple_of(e * cap + aligned_end(e) + k * tm, SUBLANES)
        return pltpu.make_async_copy(zero_ref, xs_ref.at[pl.ds(row0, tm)], tail_sem)

    def tail_rows(e, fn):
        for k in range(SUBLANES - 1):
            @pl.when(cnt_ref[e] + k < aligned_end(e))
            def _():
                fn(_row_copy(zero_ref, 0, xs_ref, e * cap + cnt_ref[e] + k, tail_sem))

    @pl.when(pl.program_id(0) == pl.num_programs(0) - 1)
    def _():
        zero_ref[...] = jnp.zeros_like(zero_ref)
        for e in range(nexp):
            for k in range(tail // tm):
                tail_copy(e, k).start()
            tail_rows(e, lambda cp: cp.start())
        for e in range(nexp):
            for k in range(tail // tm):
                tail_copy(e, k).wait()
            tail_rows(e, lambda cp: cp.wait())

    for _ in range(2):
        pltpu.make_async_copy(h_ref, xs_ref.at[pl.ds(0, tm)], sem).wait()


def _dispatch(d1, d2, cnt, h, cap, tail):
    n, wd = h.shape
    tm = TM_PROJ
    nt = n // tm
    idx_spec = pl.BlockSpec((1, 1, tm), lambda i: (i, 0, 0), memory_space=pltpu.SMEM)
    return pl.pallas_call(
        functools.partial(_dispatch_kernel, tm=tm, cap=cap, nexp=N_EXPERTS, tail=tail),
        out_shape=jax.ShapeDtypeStruct((N_EXPERTS * cap, wd), h.dtype),
        grid=(nt,),
        in_specs=[idx_spec, idx_spec, pl.BlockSpec(memory_space=pltpu.SMEM),
                  pl.BlockSpec((tm, wd), lambda i: (i, 0))],
        out_specs=pl.BlockSpec(memory_space=pl.ANY),
        scratch_shapes=[pltpu.VMEM((tm, wd), h.dtype), pltpu.SemaphoreType.DMA, pltpu.SemaphoreType.DMA],
        compiler_params=pltpu.CompilerParams(dimension_semantics=("arbitrary",),
                                             vmem_limit_bytes=VMEM_LIMIT_BYTES, disable_bounds_checks=True),
        name="moe_dispatch",
    )(d1.reshape(nt, 1, tm), d2.reshape(nt, 1, tm), cnt, h)


def _combine_kernel(d1_ref, d2_ref, n1_ref, n2_ref, x_ref, route_ref, ys_ref, o_ref, y1_ref, y2_ref, sems, *, tm):
    i = pl.program_id(0)
    slot = jnp.bitwise_and(i, 1)

    def fetch(a_ref, b_ref, s):
        def issue(r, c):
            _row_copy(ys_ref, a_ref[0, 0, r], y1_ref.at[s], r, sems.at[s]).start()
            _row_copy(ys_ref, b_ref[0, 0, r], y2_ref.at[s], r, sems.at[s]).start(priority=1)
            return c

        lax.fori_loop(0, tm, issue, 0, unroll=8)

    @pl.when(i == 0)
    def _():
        fetch(d1_ref, d2_ref, slot)

    @pl.when(i + 1 < pl.num_programs(0))
    def _():
        fetch(n1_ref, n2_ref, 1 - slot)

    for buf in (y1_ref, y2_ref):
        pltpu.make_async_copy(ys_ref.at[pl.ds(0, tm)], buf.at[slot], sems.at[slot]).wait()
    o_ref[...] = x_ref[...] + route_ref[:, 2:3] * y1_ref[slot] + route_ref[:, 3:4] * y2_ref[slot]


def _combine(d1, d2, x2, route, ys):
    n, dm = x2.shape
    tm = TM_PROJ
    nt = n // tm
    idx_spec = pl.BlockSpec((1, 1, tm), lambda i: (i, 0, 0), memory_space=pltpu.SMEM)
    next_spec = pl.BlockSpec((1, 1, tm), lambda i: (jnp.minimum(i + 1, nt - 1), 0, 0), memory_space=pltpu.SMEM)
    d1, d2 = d1.reshape(nt, 1, tm), d2.reshape(nt, 1, tm)
    return pl.pallas_call(
        functools.partial(_combine_kernel, tm=tm),
        out_shape=jax.ShapeDtypeStruct((n, dm), F32),
        grid=(nt,),
        in_specs=[idx_spec, idx_spec, next_spec, next_spec, pl.BlockSpec((tm, dm), lambda i: (i, 0)),
                  pl.BlockSpec((tm, LANES), lambda i: (i, 0)), pl.BlockSpec(memory_space=pl.ANY)],
        out_specs=pl.BlockSpec((tm, dm), lambda i: (i, 0)),
        scratch_shapes=[pltpu.VMEM((2, tm, dm), F32), pltpu.VMEM((2, tm, dm), F32), pltpu.SemaphoreType.DMA((2,))],
        compiler_params=pltpu.CompilerParams(dimension_semantics=("arbitrary",),
                                             vmem_limit_bytes=VMEM_LIMIT_BYTES, disable_bounds_checks=True),
        name="moe_combine",
    )(d1, d2, d1, d2, x2, route, ys)


def _experts_kernel(te_ref, blk_ref, nused_ref, xs_ref, w1_ref, w3_ref, w2_ref, o_ref, x_ref, hid_ref, acc_ref):
    t = pl.program_id(0)
    c = pl.program_id(1)
    nc = pl.num_programs(1)

    @pl.when(jnp.logical_and(t < nused_ref[0], c == 0))
    def _():
        w = xs_ref[...]
        half = w.shape[1]
        x_ref[:, :half] = lax.bitcast_convert_type(w & jnp.uint32(0xFFFF0000), F32).astype(BF16)
        x_ref[:, half:] = lax.bitcast_convert_type(lax.shift_left(w, jnp.uint32(16)), F32).astype(BF16)

    @pl.when(t < nused_ref[0])
    def _():
        x = x_ref[...]
        tf = w1_ref.shape[2]
        for s0 in range(0, tf, GROUP_W):
            cs = slice(s0, s0 + GROUP_W)
            a = _dot(x, w1_ref[0, :, cs])
            b = _dot(x, w3_ref[0, :, cs])
            hid_ref[:, cs] = (jax.nn.silu(a) * b).astype(BF16)
        part = _dot(hid_ref[...], w2_ref[0])

        @pl.when(c == 0)
        def _():
            acc_ref[...] = part

        @pl.when(c > 0)
        def _():
            acc_ref[...] = acc_ref[...] + part

        @pl.when(c == nc - 1)
        def _():
            o_ref[...] = acc_ref[...]

    @pl.when(jnp.logical_and(t >= nused_ref[0], c == nc - 1))
    def _():
        o_ref[...] = jnp.zeros_like(o_ref)


def _experts(tile_expert, tile_block, nused, xs, w1, w3, w2, nt):
    rows, wd = xs.shape
    dm = 2 * wd
    dff = w1.shape[2]
    tm, tf = TM_EXPERT, TF_EXPERT
    nc = dff // tf
    spare = rows // tm

    def cc(t, c, nu):
        return jnp.where(t < nu[0], c, nc - 1)

    grid_spec = pltpu.PrefetchScalarGridSpec(
        num_scalar_prefetch=3,
        grid=(nt, nc),
        in_specs=[
            pl.BlockSpec((tm, wd), lambda t, c, te, tb, nu: (tb[t], 0)),
            pl.BlockSpec((1, dm, tf), lambda t, c, te, tb, nu: (te[t], 0, cc(t, c, nu))),
            pl.BlockSpec((1, dm, tf), lambda t, c, te, tb, nu: (te[t], 0, cc(t, c, nu))),
            pl.BlockSpec((1, tf, dm), lambda t, c, te, tb, nu: (te[t], cc(t, c, nu), 0)),
        ],
        out_specs=pl.BlockSpec((tm, dm), lambda t, c, te, tb, nu: (jnp.where(t < nu[0], tb[t], spare), 0)),
        scratch_shapes=[pltpu.VMEM((tm, dm), BF16), pltpu.VMEM((tm, tf), BF16), pltpu.VMEM((tm, dm), F32)],
    )
    return pl.pallas_call(
        _experts_kernel,
        out_shape=jax.ShapeDtypeStruct((rows + tm, dm), F32),
        grid_spec=grid_spec,
        compiler_params=_cparams(("arbitrary", "arbitrary")),
        name="moe_experts",
    )(tile_expert, tile_block, nused, xs, w1, w3, w2)


def _moe(x2, g, w_router, w1, w3, w2, layer_idx, tri):
    n, dm = x2.shape
    wr = jnp.zeros((dm, LANES), F32).at[:, :N_EXPERTS].set(w_router.astype(F32))
    wr_hi = wr.astype(BF16)
    wr_lo = (wr - wr_hi.astype(F32)).astype(BF16)
    h, route, counts = _router(x2, g, jnp.stack([wr_hi, wr_lo]), tri)

    tm = TM_EXPERT
    assert n % tm == 0 and tm % TM_PROJ == 0
    cap = n + tm
    d1 = route[:, 0].astype(jnp.int32) * cap + route[:, 4].astype(jnp.int32)
    d2 = route[:, 1].astype(jnp.int32) * cap + route[:, 5].astype(jnp.int32)
    cnt = counts[0, :N_EXPERTS].astype(jnp.int32)

    nt = 2 * n // tm + N_EXPERTS
    ntile = (cnt + tm - 1) // tm
    ends = jnp.cumsum(ntile)
    nused = ends[-1]
    t = jnp.minimum(jnp.arange(nt, dtype=jnp.int32), nused - 1)
    tile_expert = jnp.sum(t[:, None] >= ends[None, :], axis=1).astype(jnp.int32)
    tile_block = tile_expert * (cap // tm) + t - (ends - ntile)[tile_expert]

    xs = _dispatch(d1, d2, cnt, h, cap, tm)
    ys = _experts(tile_expert + layer_idx * N_EXPERTS, tile_block.astype(jnp.int32),
                  nused.reshape(1).astype(jnp.int32), xs, w1, w3, w2, nt)
    return _combine(d1, d2, x2, route, ys)


def kernel(x, norm_mix_g, w_in, ssm_lam_re, ssm_lam_im, ssm_log_dt, ssm_b_re, ssm_b_im, ssm_c_re, ssm_c_im,
           ssm_d, ssm_w_glu, ssm_b_glu, ch_q_norm_g, ch_k_norm_g, ch_rel_bias, df_q_norm_g, df_k_norm_g,
           df_lambda, out_norm_g, w_out, norm_ffn_g, ffn_w1, ffn_w3, ffn_w2, moe_router, moe_w1, moe_w3, moe_w2):
    bsz, seq, dm = x.shape
    depth = w_in.shape[0]
    n = bsz * seq
    nheads = GROUP_W // HEAD_DIM
    ind64 = _block_indicator(GROUP_W, HEAD_DIM)
    ind32 = _block_indicator(GROUP_W, DF_QK_DIM)
    tri = (jnp.arange(TM_PROJ)[:, None] > jnp.arange(TM_PROJ)[None, :]).astype(BF16)
    slopes = [2.0 ** (-8.0 * (h + 1) / nheads) for h in range(nheads)]
    slopes2 = tuple(s * LOG2E for s in slopes)
    df_kb, df_db = _df_bias_tiles(slopes2, TQ_ATT, TK_ATT)
    later = (jnp.arange(TK_ATT)[:, None] < jnp.arange(TK_ATT)[None, :]).astype(BF16)

    moe_w = tuple(w.reshape((-1,) + w.shape[2:]).astype(BF16) for w in (moe_w1, moe_w3, moe_w2))

    x2 = x.reshape(n, dm)
    for layer in range(depth):
        qkg = jnp.stack([
            jnp.tile(ch_q_norm_g[layer].astype(F32), nheads) * (HEAD_DIM ** -0.5 * LOG2E),
            jnp.tile(ch_k_norm_g[layer].astype(F32), nheads),
            jnp.tile(df_q_norm_g[layer].astype(F32).reshape(-1), nheads) * (DF_QK_DIM ** -0.5 * LOG2E),
            jnp.tile(df_k_norm_g[layer].astype(F32).reshape(-1), nheads),
        ])
        u_tm, qk, v_sb, v_ch, v_df = _in_proj(x2, norm_mix_g[layer].reshape(1, dm), w_in[layer].astype(BF16), qkg,
                             ind64, ind32, bsz, seq)

        wb, lre, lim, wc = _s5_params(ssm_lam_re[layer], ssm_lam_im[layer], ssm_log_dt[layer],
                                      ssm_b_re[layer], ssm_b_im[layer], ssm_c_re[layer], ssm_c_im[layer], bsz)
        y_ssm = _s5(u_tm, wb, lre, lim, wc, ssm_d[layer].reshape(1, GROUP_W),
                    ssm_w_glu[layer].astype(BF16), ssm_b_glu[layer].reshape(1, GROUP_W), bsz)

        tq, tk = TQ_ATT, TK_ATT
        const3 = lambda b, i: (0, 0, 0)
        y_sb = _attention_call(
            functools.partial(_sb_kernel, tq=tq, tk=tk, nheads=nheads),
            qk, 0, v_sb,
            (later,), [pl.BlockSpec((tk, tk), lambda b, i: (0, 0))],
            [pltpu.VMEM((nheads * tq, GROUP_W), BF16), pltpu.VMEM((SUBLANES, tq), F32),
             pltpu.VMEM((GROUP_W, tq), F32),
             pltpu.VMEM((tk, nheads * tq), F32), pltpu.VMEM((tk, nheads * tq), F32)],
            bsz, seq, "sb_attention")

        bias, nback = _ch_bias_tiles(ch_rel_bias[layer], tq, tk)
        y_ch = _attention_call(
            functools.partial(_ch_kernel, tq=tq, tk=tk, nheads=nheads, nback=nback),
            qk, 2, v_ch,
            (bias,), [pl.BlockSpec(bias.shape, lambda b, i: (0, 0, 0, 0))],
            [pltpu.VMEM((nheads * tq, GROUP_W), BF16), pltpu.VMEM(((nback + 1) * tk, nheads * tq), F32)],
            bsz, seq, "ch_attention")

        lambda_init = 0.8 - 0.6 * math.exp(-0.3 * layer)
        lam_p = df_lambda[layer].astype(F32)
        lam = jnp.exp(jnp.sum(lam_p[0] * lam_p[1])) - jnp.exp(jnp.sum(lam_p[2] * lam_p[3])) + lambda_init
        y_df = _attention_call(
            functools.partial(_df_kernel, tq=tq, tk=tk, nheads=nheads, slopes2=slopes2),
            qk, 4, v_df,
            (df_kb, df_db), [pl.BlockSpec(df_kb.shape, const3), pl.BlockSpec(df_db.shape, const3)],
            [pltpu.VMEM((2 * nheads * tq, GROUP_W), BF16), pltpu.VMEM((2 * nheads, tq), F32),
             pltpu.VMEM((2 * nheads * VA_ROWS, tq), F32),
             pltpu.VMEM((tk, 2 * nheads * tq), F32), pltpu.VMEM((tk, 2 * nheads * tq), F32),
             pltpu.VMEM((2 * nheads, tq), F32), pltpu.VMEM((2 * nheads, tq), F32)],
            bsz, seq, "df_attention", smem=(lam.reshape(1),))

        head_scale = jnp.concatenate([jnp.ones((dm - GROUP_W,), F32),
                                      jnp.full((GROUP_W,), 1.0 - lambda_init, F32)])
        g_out = (out_norm_g[layer].astype(F32) * head_scale).reshape(1, dm)
        x2 = _out_proj(y_ssm, y_sb, y_ch, y_df, g_out, ind64, w_out[layer].astype(BF16), x2, bsz, seq)

        idx = layer // 2
        g_ffn = norm_ffn_g[layer].reshape(1, dm)
        if layer % 2 == 0:
            x2 = _ffn(x2, g_ffn, ffn_w1[idx].astype(BF16), ffn_w3[idx].astype(BF16), ffn_w2[idx].astype(BF16))
        else:
            x2 = _moe(x2, g_ffn, moe_router[idx], *moe_w, idx, tri)
    return x2.reshape(bsz, seq, dm)
```

```python
import functools
import math

import jax
import jax.numpy as jnp
import numpy as np
from jax import lax
from jax.experimental import pallas as pl
from jax.experimental.pallas import tpu as pltpu

F32 = jnp.float32
BF16 = jnp.bfloat16

EPS = 1e-6
NEG_INF = -1e30
HEAD_DIM = 64
GROUP_W = 256
CHUNK = 64
SSM_GROUP = 16
SSM_STATE = 64
DF_QK_DIM = 32
CH_LEFT_CHUNKS = 8
REL_CLIP = 128
N_EXPERTS = 8
LOG2E = 1.4426950408889634

VMEM_LIMIT_BYTES = 56 * 1024 * 1024
SUBLANES = 8
LANES = 128

TM_PROJ = 512
TM_MIX = 1024
TQ_ATT = 256
TK_ATT = 256
SSM_STEPS = 64
TM_EXPERT = 512
TF_EXPERT = 1792


def _cparams(sem):
    return pltpu.CompilerParams(dimension_semantics=sem, vmem_limit_bytes=VMEM_LIMIT_BYTES)


def _dot(a, b):
    return jnp.dot(a, b, preferred_element_type=F32)


def _dot_nt(a, b):
    return lax.dot_general(a, b, (((1,), (1,)), ((), ())), preferred_element_type=F32)


def _group_ms(v, ind, inv_size):
    return _dot((v * v).astype(BF16), ind) * inv_size


def _block_indicator(width, group):
    r = jnp.arange(width)[:, None] // group
    c = jnp.arange(width)[None, :] // group
    return (r == c).astype(BF16)


def _in_proj_kernel(x_ref, g_ref, w_ref, qkg_ref, ind64_ref, ind32_ref, u_ref, qk_ref, vsb_ref, vch_ref, vdf_ref,
                    *, tk, nheads):
    x = x_ref[...]
    ms = jnp.mean(x * x, axis=-1, keepdims=True)
    h = (x * lax.rsqrt(ms + EPS) * g_ref[...]).astype(BF16)

    def cols(c):
        return _dot(h, w_ref[:, c * GROUP_W:(c + 1) * GROUP_W])

    def put(slot, val):
        qk_ref[:, slot * GROUP_W:(slot + 1) * GROUP_W] = val.astype(BF16)

    def put_values(v_ref, v, ones_row):
        for kt in range(v.shape[0] // tk):
            vt = v[kt * tk:(kt + 1) * tk, :].T
            if not ones_row:
                v_ref[kt] = vt.astype(BF16)
                continue
            pad = jnp.where(_iota2((VA_ROWS - HEAD_DIM, tk), 0) == 0, 1.0, 0.0).astype(BF16)
            for hd in range(nheads):
                v_ref[kt, hd * VA_ROWS:hd * VA_ROWS + HEAD_DIM, :] = vt[hd * HEAD_DIM:(hd + 1) * HEAD_DIM, :].astype(BF16)
                v_ref[kt, hd * VA_ROWS + HEAD_DIM:(hd + 1) * VA_ROWS, :] = pad

    u_ref[...] = cols(0)
    put(0, cols(1) * (HEAD_DIM ** -0.5 * LOG2E))
    put(1, cols(2))
    put_values(vsb_ref, cols(3), False)
    for c, gi in ((4, 0), (5, 1)):
        a = cols(c)
        ms_h = _group_ms(a, ind64_ref[...], 1.0 / HEAD_DIM)
        put(c - 2, a * lax.rsqrt(ms_h + EPS) * qkg_ref[gi:gi + 1, :])
    put_values(vch_ref, cols(6), True)
    for c, gi in ((7, 2), (8, 3)):
        a = cols(c)
        ms_h = _group_ms(a, ind32_ref[...], 1.0 / DF_QK_DIM)
        put(c - 3, a * lax.rsqrt(ms_h + EPS) * qkg_ref[gi:gi + 1, :])
    put_values(vdf_ref, cols(9), True)


def _in_proj(x2, g, w, qkg, ind64, ind32, bsz, seq):
    n, dm = x2.shape
    tm, tk = TM_MIX, TK_ATT
    ns = seq // tm
    nheads = GROUP_W // HEAD_DIM
    const = lambda i: (0, 0)

    def values(rows):
        return (jax.ShapeDtypeStruct((bsz, seq // tk, rows, tk), BF16),
                pl.BlockSpec((None, tm // tk, rows, tk), lambda i: (i // ns, i % ns, 0, 0)))

    v_shapes, v_specs = zip(values(GROUP_W), values(nheads * VA_ROWS), values(nheads * VA_ROWS))
    return pl.pallas_call(
        functools.partial(_in_proj_kernel, tk=tk, nheads=nheads),
        out_shape=(jax.ShapeDtypeStruct((seq, bsz * GROUP_W), F32),
                   jax.ShapeDtypeStruct((n, 6 * GROUP_W), BF16)) + v_shapes,
        grid=(n // tm,),
        in_specs=[
            pl.BlockSpec((tm, dm), lambda i: (i, 0)),
            pl.BlockSpec((1, dm), const),
            pl.BlockSpec(w.shape, const),
            pl.BlockSpec((4, GROUP_W), const),
            pl.BlockSpec((GROUP_W, GROUP_W), const),
            pl.BlockSpec((GROUP_W, GROUP_W), const),
        ],
        out_specs=(
            pl.BlockSpec((tm, GROUP_W), lambda i: (i % ns, i // ns)),
            pl.BlockSpec((tm, 6 * GROUP_W), lambda i: (i, 0)),
        ) + v_specs,
        compiler_params=_cparams(("arbitrary",)),
        name="in_proj",
    )(x2, g, w, qkg, ind64, ind32)


def _s5_kernel(u_ref, wb_ref, lre_ref, lim_ref, wc_ref, d_ref, wg_ref, bg_ref, y_ref, bu_ref, st_ref, rows_ref,
               *, steps, nstate):
    @pl.when(pl.program_id(0) == 0)
    def _():
        st_ref[...] = jnp.zeros_like(st_ref)

    nhalf, bsz = rows_ref.shape[0], rows_ref.shape[1] // steps
    w = nhalf * LANES
    for b in range(bsz):
        for k in range(nhalf):
            c0 = b * w + k * LANES
            rows_ref[k, pl.ds(b, steps, stride=bsz), :] = u_ref[:, c0:c0 + LANES]
    u = jnp.concatenate([rows_ref[k] for k in range(nhalf)], axis=1)
    bu_ref[...] = _dot(u.astype(BF16), wb_ref[...])
    lre = lre_ref[...]
    lim = lim_ref[...]

    def step(t, carry):
        sre, sim = carry
        r = pl.multiple_of(t * SUBLANES, SUBLANES)
        bre = bu_ref[pl.ds(r, SUBLANES), 0:nstate]
        bim = bu_ref[pl.ds(r, SUBLANES), nstate:2 * nstate]
        nre = lre * sre - lim * sim + bre
        nim = lre * sim + lim * sre + bim
        bu_ref[pl.ds(r, SUBLANES), 0:nstate] = nre
        bu_ref[pl.ds(r, SUBLANES), nstate:2 * nstate] = nim
        return nre, nim

    sre, sim = lax.fori_loop(0, steps, step, (st_ref[:, 0:nstate], st_ref[:, nstate:2 * nstate]))
    st_ref[:, 0:nstate] = sre
    st_ref[:, nstate:2 * nstate] = sim

    y = _dot(bu_ref[...].astype(BF16), wc_ref[...]) + d_ref[...] * u
    y = jax.nn.gelu(y)
    gate = jax.nn.sigmoid(_dot(y.astype(BF16), wg_ref[...]) + bg_ref[...])
    out = y * gate
    for k in range(nhalf):
        rows_ref[k] = out[:, k * LANES:(k + 1) * LANES]
    for b in range(bsz):
        for k in range(nhalf):
            c0 = b * w + k * LANES
            y_ref[:, c0:c0 + LANES] = rows_ref[k, pl.ds(b, steps, stride=bsz), :]


def _s5(u_tm, wb, lre, lim, wc, d, wg, bg, bsz):
    seq = u_tm.shape[0]
    w = u_tm.shape[1] // bsz
    assert bsz == SUBLANES
    blk = SSM_STEPS * bsz
    nstate2 = wb.shape[1]
    const = lambda c: (0, 0)
    return pl.pallas_call(
        functools.partial(_s5_kernel, steps=SSM_STEPS, nstate=nstate2 // 2),
        out_shape=jax.ShapeDtypeStruct(u_tm.shape, F32),
        grid=(seq // SSM_STEPS,),
        in_specs=[
            pl.BlockSpec((SSM_STEPS, bsz * w), lambda c: (c, 0)),
            pl.BlockSpec((w, nstate2), const),
            pl.BlockSpec((bsz, nstate2 // 2), const),
            pl.BlockSpec((bsz, nstate2 // 2), const),
            pl.BlockSpec((nstate2, w), const),
            pl.BlockSpec((1, w), const),
            pl.BlockSpec((w, w), const),
            pl.BlockSpec((1, w), const),
        ],
        out_specs=pl.BlockSpec((SSM_STEPS, bsz * w), lambda c: (c, 0)),
        scratch_shapes=[pltpu.VMEM((blk, nstate2), F32), pltpu.VMEM((bsz, nstate2), F32),
                        pltpu.VMEM((w // LANES, blk, LANES), F32)],
        compiler_params=_cparams(("arbitrary",)),
        name="s5_mixer",
    )(u_tm, wb, lre, lim, wc, d, wg, bg)


def _s5_params(lam_re, lam_im, log_dt, b_re, b_im, c_re, c_im, bsz):
    g, n = lam_re.shape
    dt = jnp.exp(log_dt.astype(F32))[:, None]
    xr = lam_re * dt
    th = lam_im * dt
    er = jnp.exp(xr)
    lbr = er * jnp.cos(th)
    lbi = er * jnp.sin(th)
    ar = jnp.expm1(xr) * jnp.cos(th) - 2.0 * jnp.sin(0.5 * th) ** 2
    ai = lbi
    den = lam_re * lam_re + lam_im * lam_im
    fr = (ar * lam_re + ai * lam_im) / den
    fi = (ai * lam_re - ar * lam_im) / den
    bbr = fr[..., None] * b_re - fi[..., None] * b_im
    bbi = fr[..., None] * b_im + fi[..., None] * b_re
    eye = jnp.eye(g, dtype=F32)
    p = b_re.shape[-1]
    wb_re = jnp.einsum("gnp,gh->gphn", bbr, eye).reshape(g * p, g * n)
    wb_im = jnp.einsum("gnp,gh->gphn", bbi, eye).reshape(g * p, g * n)
    wb = jnp.concatenate([wb_re, wb_im], axis=1).astype(BF16)
    wc_re = jnp.einsum("gpn,gh->gnhp", c_re, eye).reshape(g * n, g * p)
    wc_im = jnp.einsum("gpn,gh->gnhp", c_im, eye).reshape(g * n, g * p)
    wc = jnp.concatenate([wc_re, -wc_im], axis=0).astype(BF16)
    lre = jnp.broadcast_to(lbr.reshape(1, g * n), (bsz, g * n))
    lim = jnp.broadcast_to(lbi.reshape(1, g * n), (bsz, g * n))
    return wb, lre, lim, wc


VA_ROWS = 2 * HEAD_DIM


def _iota2(shape, dim):
    return lax.broadcasted_iota(jnp.int32, shape, dim)


def _stack_masked_q(q_ref, qs_ref, tq, ngroups, width):
    lane = _iota2((1, GROUP_W), 1)
    q = q_ref[...]
    for g in range(ngroups):
        keep = (lane >= g * width) & (lane < (g + 1) * width)
        qs_ref[g * tq:(g + 1) * tq, :] = jnp.where(keep, q, jnp.zeros_like(q))


def _key_scores(k_ref, qs_ref, j, tk):
    row0 = pl.multiple_of(jnp.maximum(j, 0) * tk, tk)
    return _dot_nt(k_ref[pl.ds(row0, tk), :], qs_ref[...])


def _score_prefetch(k_ref, qs_ref, j, dst_ref, tq, tk, bias_of=None, max_ref=None):
    row0 = pl.multiple_of(jnp.maximum(j, 0) * tk, tk)

    def emit(g):
        cols = slice(g * tq, (g + 1) * tq)
        s = _dot_nt(k_ref[pl.ds(row0, tk), :], qs_ref[cols, :])
        if bias_of is not None:
            s = s + bias_of(g)[...]
        dst_ref[:, cols] = s
        if max_ref is not None:
            max_ref[g:g + 1, :] = jnp.max(s, axis=0, keepdims=True)

    return emit


def _paired_key_tiles(i, k_ref, qs_ref, tile_fn, buf_a, buf_b, tq, tk, bias_of=None):
    def body(n, c):
        ja = i - 1 - 2 * n
        tile_fn(ja, buf_a, _score_prefetch(k_ref, qs_ref, ja - 1, buf_b[0], tq, tk, bias_of, buf_b[1]))
        tile_fn(ja - 1, buf_b, _score_prefetch(k_ref, qs_ref, ja - 2, buf_a[0], tq, tk, bias_of, buf_a[1]))
        return c

    lax.fori_loop(0, i // 2, body, 0)

    @pl.when(jnp.bitwise_and(i, 1) == 1)
    def _():
        tile_fn(0, buf_a, lambda g: None)


def _sb_kernel(q_ref, k_ref, vt_ref, later_ref, o_ref, qs_ref, carry_ref, acc_ref, za_ref, zb_ref,
               *, tq, tk, nheads):
    i = pl.program_id(1)
    _stack_masked_q(q_ref, qs_ref, tq, nheads, HEAD_DIM)

    def tile(j, buf, prefetch, diagonal=False):
        zt = buf[0]
        if diagonal:
            before = _iota2((tk, tq), 0) < _iota2((tk, tq), 1)

        def scan_stage(h):
            z = zt[:, h * tq:(h + 1) * tq]
            neg_abs = lax.bitcast_convert_type(
                lax.bitcast_convert_type(z, jnp.uint32) | jnp.uint32(0x80000000), F32)
            sp = jnp.maximum(z, 0.0) + jnp.log2(1.0 + jnp.exp2(neg_abs))
            nl1m = jnp.where(before, sp, 0.0) if diagonal else sp
            inside = _dot(later_ref[...], nl1m.astype(BF16))
            return z - sp, inside, nl1m[0:1, :]

        def value_stage(h, logit, inside, first_row):
            hs = slice(h * HEAD_DIM, (h + 1) * HEAD_DIM)
            between = inside if diagonal else inside + carry_ref[h:h + 1, :]
            w = jnp.exp2(logit - between)
            if diagonal:
                w = jnp.where(before, w, 0.0)
            pv = _dot(vt_ref[j, hs, :], w.astype(BF16))
            total = inside[0:1, :] + first_row
            if diagonal:
                acc_ref[hs, :] = pv
                carry_ref[h:h + 1, :] = total
            else:
                acc_ref[hs, :] = acc_ref[hs, :] + pv
                carry_ref[h:h + 1, :] = carry_ref[h:h + 1, :] + total

        prefetch(0)
        pending = scan_stage(0)
        for h in range(1, nheads):
            prefetch(h)
            upcoming = scan_stage(h)
            value_stage(h - 1, *pending)
            pending = upcoming
        value_stage(nheads - 1, *pending)

    tile(i, (_key_scores(k_ref, qs_ref, i, tk), None), _score_prefetch(k_ref, qs_ref, i - 1, za_ref, tq, tk),
         diagonal=True)
    _paired_key_tiles(i, k_ref, qs_ref, tile, (za_ref, None), (zb_ref, None), tq, tk)
    o_ref[...] = acc_ref[...].T.astype(o_ref.dtype)


def _softmax_tile(buf, tq, off, va, g, first, m_ref, acc_ref):
    st_ref, max_ref = buf
    rows = slice(g * VA_ROWS, (g + 1) * VA_ROWS)
    parts, alphas = [], []
    for c0 in range(0, tq, LANES):
        cs = slice(c0, c0 + LANES)
        sc = slice(g * tq + c0, g * tq + c0 + LANES)
        blk_max = max_ref[g:g + 1, cs]
        if first:
            m_new = blk_max
            shift = m_new
        else:
            m_old = m_ref[g:g + 1, cs]
            m_new = jnp.maximum(m_old, blk_max + off)
            shift = m_new - off
            alphas.append(jnp.exp2(m_old - m_new))
        parts.append(jnp.exp2(st_ref[:, sc] - shift).astype(BF16))
        m_ref[g:g + 1, cs] = m_new
    pv = _dot(va, jnp.concatenate(parts, axis=1))
    if first:
        acc_ref[rows, :] = pv
    else:
        acc_ref[rows, :] = jnp.concatenate(alphas, axis=1) * acc_ref[rows, :] + pv


def _normalised(acc_ref, g):
    base = g * VA_ROWS
    return acc_ref[base:base + HEAD_DIM, :] / acc_ref[base + HEAD_DIM:base + HEAD_DIM + 1, :]


def _ch_kernel(q_ref, k_ref, va_ref, bias_ref, o_ref, qs_ref, z_ref, *, tq, tk, nheads, nback):
    i = pl.program_id(1)
    _stack_masked_q(q_ref, qs_ref, tq, nheads, HEAD_DIM)

    def attend(ntiles):
        tile_max = [[] for _ in range(nheads)]
        for n in range(ntiles):
            row0 = pl.multiple_of((i - n) * tk, tk)
            for h in range(nheads):
                cols = slice(h * tq, (h + 1) * tq)
                s = _dot_nt(k_ref[pl.ds(row0, tk), :], qs_ref[cols, :]) + bias_ref[h, n]
                z_ref[n * tk:(n + 1) * tk, cols] = s
                tile_max[h].append(jnp.max(s, axis=0, keepdims=True))
        outs = []
        for h in range(nheads):
            m = functools.reduce(jnp.maximum, tile_max[h])
            acc = None
            for n in range(ntiles):
                p = jnp.exp2(z_ref[n * tk:(n + 1) * tk, h * tq:(h + 1) * tq] - m).astype(BF16)
                pv = _dot(va_ref[i - n, h * VA_ROWS:(h + 1) * VA_ROWS, :], p)
                acc = pv if acc is None else acc + pv
            outs.append(acc[:HEAD_DIM, :] / acc[HEAD_DIM:HEAD_DIM + 1, :])
        o_ref[...] = jnp.concatenate(outs, axis=0).T.astype(o_ref.dtype)

    for ntiles in range(1, nback + 2):
        pl.when(jnp.minimum(i, nback) + 1 == ntiles)(functools.partial(attend, ntiles))


def _df_kernel(lam_ref, q_ref, k_ref, va_ref, kb_ref, db_ref, o_ref, qs_ref, m_ref, acc_ref,
               za_ref, zb_ref, maxa_ref, maxb_ref, *, tq, tk, nheads, slopes2):
    i = pl.program_id(1)
    _stack_masked_q(q_ref, qs_ref, tq, 2 * nheads, DF_QK_DIM)
    buf_a, buf_b = (za_ref, maxa_ref), (zb_ref, maxb_ref)

    def past_bias(g):
        return kb_ref.at[g // 2]

    def tile(j, buf, prefetch, diagonal=False):
        for g in range(2 * nheads):
            prefetch(g)
            h = g // 2
            va = va_ref[j, h * VA_ROWS:(h + 1) * VA_ROWS, :]
            if diagonal:
                _softmax_tile(buf, tq, 0.0, va, g, True, m_ref, acc_ref)
            else:
                _softmax_tile(buf, tq, slopes2[h] * ((j - i) * tk).astype(F32), va, g, False, m_ref, acc_ref)

    diag = _score_prefetch(k_ref, qs_ref, i, zb_ref, tq, tk, lambda g: db_ref.at[g // 2], maxb_ref)
    for g in range(2 * nheads):
        diag(g)
    tile(i, buf_b, _score_prefetch(k_ref, qs_ref, i - 1, za_ref, tq, tk, past_bias, maxa_ref), diagonal=True)
    _paired_key_tiles(i, k_ref, qs_ref, tile, buf_a, buf_b, tq, tk, past_bias)
    lam = lam_ref[0]
    out = jnp.concatenate([_normalised(acc_ref, 2 * h) - lam * _normalised(acc_ref, 2 * h + 1)
                           for h in range(nheads)], axis=0)
    o_ref[...] = out.T.astype(o_ref.dtype)


def _attention_call(kernel, qk, col0, vt, extra, extra_specs, scratch, bsz, seq, name, smem=()):
    tq, tk = TQ_ATT, TK_ATT
    assert tq == tk
    nq = seq // tq
    vrows = vt.shape[2]
    return pl.pallas_call(
        kernel,
        out_shape=jax.ShapeDtypeStruct((bsz * seq, GROUP_W), BF16),
        grid=(bsz, nq),
        in_specs=[pl.BlockSpec(memory_space=pltpu.SMEM) for _ in smem] + [
            pl.BlockSpec((tq, GROUP_W), lambda b, i: (b * nq + i, col0)),
            pl.BlockSpec((seq, GROUP_W), lambda b, i: (b, col0 + 1)),
            pl.BlockSpec((None, seq // tk, vrows, tk), lambda b, i: (b, 0, 0, 0)),
        ] + extra_specs,
        out_specs=pl.BlockSpec((tq, GROUP_W), lambda b, i: (b * nq + i, 0)),
        scratch_shapes=scratch,
        compiler_params=_cparams(("arbitrary", "arbitrary")),
        name=name,
    )(*smem, qk, qk, vt, *extra)


def _ch_bias_tiles(rel_bias, tq, tk):
    assert tq == tk
    nback = CH_LEFT_CHUNKS * CHUNK // tk
    nh = rel_bias.shape[0]
    n = np.arange(nback + 1)[:, None, None]
    s = np.arange(tk)[None, :, None]
    t = np.arange(tq)[None, None, :]
    delta = (n * tk + t) // CHUNK - s // CHUNK
    valid = (delta >= 0) & (delta <= CH_LEFT_CHUNKS)
    table = rel_bias.astype(F32) * LOG2E
    lo, hi = -(tk - 1), nback * tk + tq - 1
    ext = jnp.concatenate([jnp.repeat(table[:, :1], -REL_CLIP - lo, axis=1), table,
                           jnp.repeat(table[:, -1:], hi - REL_CLIP, axis=1)], axis=1)
    period = tk + tq
    tiles = []
    for m in range(nback + 1):
        w = ext[:, m * tk:m * tk + period - 1]
        w = jnp.concatenate([w, jnp.zeros((nh, 1), F32)], axis=1)
        skew = jnp.tile(w, (1, tk))[:, :tk * (period - 1)].reshape(nh, tk, period - 1)
        tiles.append(skew[:, :, tk - 1:tk - 1 + tq])
    bias = jnp.stack(tiles, axis=1)
    return jnp.where(jnp.asarray(valid)[None], bias, NEG_INF), nback


def _df_bias_tiles(slopes2, tq, tk):
    s = jnp.arange(tk, dtype=jnp.int32)[:, None]
    t = jnp.arange(tq, dtype=jnp.int32)[None, :]
    sl = jnp.asarray(slopes2, F32)[:, None, None]
    kb = sl * jnp.broadcast_to(s, (tk, tq)).astype(F32)
    allowed = (s // CHUNK) <= (t // CHUNK)
    db = jnp.where(allowed[None], sl * (t - jnp.abs(t - s)).astype(F32), NEG_INF)
    return kb, db


def _out_proj_kernel(ys_ref, ysb_ref, ych_ref, ydf_ref, g_ref, ind_ref, w_ref, x_ref, o_ref):
    acc = x_ref[...]
    for gi, y_ref in enumerate((ys_ref, ysb_ref, ych_ref, ydf_ref)):
        cs = slice(gi * GROUP_W, (gi + 1) * GROUP_W)
        y = y_ref[...].astype(F32)
        ms = _group_ms(y, ind_ref[...], 1.0 / HEAD_DIM)
        yn = (y * lax.rsqrt(ms + EPS) * g_ref[:, cs]).astype(BF16)
        acc = acc + _dot(yn, w_ref[cs, :])
    o_ref[...] = acc


def _out_proj(y_ssm_tm, y_sb, y_ch, y_df, g, ind64, w, x2, bsz, seq):
    n, dm = x2.shape
    tm = TM_MIX
    ns = seq // tm
    const = lambda i: (0, 0)
    tile = pl.BlockSpec((tm, GROUP_W), lambda i: (i, 0))
    return pl.pallas_call(
        _out_proj_kernel,
        out_shape=jax.ShapeDtypeStruct((n, dm), F32),
        grid=(n // tm,),
        in_specs=[
            pl.BlockSpec((tm, GROUP_W), lambda i: (i % ns, i // ns)),
            tile, tile, tile,
            pl.BlockSpec((1, dm), const),
            pl.BlockSpec((GROUP_W, GROUP_W), const),
            pl.BlockSpec((dm, dm), const),
            pl.BlockSpec((tm, dm), lambda i: (i, 0)),
        ],
        out_specs=pl.BlockSpec((tm, dm), lambda i: (i, 0)),
        compiler_params=_cparams(("arbitrary",)),
        name="out_proj",
    )(y_ssm_tm, y_sb, y_ch, y_df, g, ind64, w, x2)


def _ffn_kernel(x_ref, g_ref, w1_ref, w3_ref, w2_ref, o_ref, hid_ref, *, tf):
    x = x_ref[...]
    ms = jnp.mean(x * x, axis=-1, keepdims=True)
    h = (x * lax.rsqrt(ms + EPS) * g_ref[...]).astype(BF16)
    dff = w1_ref.shape[1]
    for c in range(dff // tf):
        cs = slice(c * tf, (c + 1) * tf)
        a = _dot(h, w1_ref[:, cs])
        b = _dot(h, w3_ref[:, cs])
        hid_ref[:, cs] = (jax.nn.silu(a) * b).astype(BF16)
    o_ref[...] = x + _dot(hid_ref[...], w2_ref[...])


def _ffn(x2, g, w1, w3, w2):
    n, dm = x2.shape
    dff = w1.shape[1]
    tm = TM_PROJ
    const = lambda i: (0, 0)
    return pl.pallas_call(
        functools.partial(_ffn_kernel, tf=GROUP_W),
        out_shape=jax.ShapeDtypeStruct((n, dm), F32),
        grid=(n // tm,),
        in_specs=[
            pl.BlockSpec((tm, dm), lambda i: (i, 0)),
            pl.BlockSpec((1, dm), const),
            pl.BlockSpec((dm, dff), const),
            pl.BlockSpec((dm, dff), const),
            pl.BlockSpec((dff, dm), const),
        ],
        out_specs=pl.BlockSpec((tm, dm), lambda i: (i, 0)),
        scratch_shapes=[pltpu.VMEM((tm, dff), BF16)],
        compiler_params=_cparams(("arbitrary",)),
        name="ffn_dense",
    )(x2, g, w1, w3, w2)


def _mix_ffn_kernel(ys_ref, ysb_ref, ych_ref, ydf_ref, go_ref, ind_ref, wo_ref, x_ref, gf_ref, w1_ref, w3_ref,
                    w2_ref, o_ref, xmid_ref, hid_ref, *, tf):
    _out_proj_kernel(ys_ref, ysb_ref, ych_ref, ydf_ref, go_ref, ind_ref, wo_ref, x_ref, xmid_ref)
    _ffn_kernel(xmid_ref, gf_ref, w1_ref, w3_ref, w2_ref, o_ref, hid_ref, tf=tf)


def _mix_ffn(y_ssm_tm, y_sb, y_ch, y_df, g_out, ind64, w_out, x2, g_ffn, w1, w3, w2, bsz, seq):
    n, dm = x2.shape
    dff = w1.shape[1]
    tm = TM_PROJ
    ns = seq // tm
    const = lambda i: (0, 0)
    tile = pl.BlockSpec((tm, GROUP_W), lambda i: (i, 0))
    return pl.pallas_call(
        functools.partial(_mix_ffn_kernel, tf=GROUP_W),
        out_shape=jax.ShapeDtypeStruct((n, dm), F32),
        grid=(n // tm,),
        in_specs=[
            pl.BlockSpec((tm, GROUP_W), lambda i: (i % ns, i // ns)),
            tile, tile, tile,
            pl.BlockSpec((1, dm), const),
            pl.BlockSpec((GROUP_W, GROUP_W), const),
            pl.BlockSpec((dm, dm), const),
            pl.BlockSpec((tm, dm), lambda i: (i, 0)),
            pl.BlockSpec((1, dm), const),
            pl.BlockSpec((dm, dff), const),
            pl.BlockSpec((dm, dff), const),
            pl.BlockSpec((dff, dm), const),
        ],
        out_specs=pl.BlockSpec((tm, dm), lambda i: (i, 0)),
        scratch_shapes=[pltpu.VMEM((tm, dm), F32), pltpu.VMEM((tm, dff), BF16)],
        compiler_params=_cparams(("arbitrary",)),
        name="mix_ffn",
    )(y_ssm_tm, y_sb, y_ch, y_df, g_out, ind64, w_out, x2, g_ffn, w1, w3, w2)


def _router_kernel(x_ref, g_ref, wr_ref, tri_ref, h_ref, route_ref, cnt_ref, carry_ref):
    @pl.when(pl.program_id(0) == 0)
    def _():
        carry_ref[...] = jnp.zeros_like(carry_ref)

    x = x_ref[...]
    ms = jnp.mean(x * x, axis=-1, keepdims=True)
    h = x * lax.rsqrt(ms + EPS) * g_ref[...]
    h_hi = h.astype(BF16)
    half = h.shape[1] // 2
    bits = lax.bitcast_convert_type(h_hi.astype(F32), jnp.uint32)
    h_ref[...] = bits[:, :half] | lax.shift_right_logical(bits[:, half:], jnp.uint32(16))
    h_lo = (h - h_hi.astype(F32)).astype(BF16)
    logits = _dot(h_hi, wr_ref[0]) + _dot(h_lo, wr_ref[0]) + _dot(h_hi, wr_ref[1])

    lane = _iota2(logits.shape, 1).astype(F32)
    lg = jnp.where(lane < N_EXPERTS, logits, -jnp.inf)
    m1 = jnp.max(lg, axis=-1, keepdims=True)
    e1 = jnp.min(jnp.where(lg == m1, lane, float(LANES)), axis=-1, keepdims=True)
    lg2 = jnp.where(lane == e1, -jnp.inf, lg)
    m2 = jnp.max(lg2, axis=-1, keepdims=True)
    e2 = jnp.min(jnp.where(lg2 == m2, lane, float(LANES)), axis=-1, keepdims=True)
    t = jnp.exp(m2 - m1)
    g1 = 1.0 / (1.0 + t)
    g2 = t / (1.0 + t)

    hot1 = lane == e1
    hot2 = lane == e2
    sel = jnp.where(hot1 | hot2, 1.0, 0.0).astype(BF16)
    prior = _dot(tri_ref[...], sel) + carry_ref[...]
    r1 = jnp.sum(jnp.where(hot1, prior, 0.0), axis=-1, keepdims=True)
    r2 = jnp.sum(jnp.where(hot2, prior, 0.0), axis=-1, keepdims=True)
    carry_ref[...] = carry_ref[...] + jnp.sum(sel.astype(F32), axis=0, keepdims=True)
    cnt_ref[...] = carry_ref[...]

    out = jnp.where(lane == 0, e1, 0.0)
    out = jnp.where(lane == 1, e2, out)
    out = jnp.where(lane == 2, g1, out)
    out = jnp.where(lane == 3, g2, out)
    out = jnp.where(lane == 4, r1, out)
    out = jnp.where(lane == 5, r2, out)
    route_ref[...] = out


def _router(x2, g, wr2, tri):
    n, dm = x2.shape
    tm = TM_PROJ
    const = lambda i: (0, 0)
    return pl.pallas_call(
        _router_kernel,
        out_shape=(jax.ShapeDtypeStruct((n, dm // 2), jnp.uint32),
                   jax.ShapeDtypeStruct((n, LANES), F32),
                   jax.ShapeDtypeStruct((1, LANES), F32)),
        grid=(n // tm,),
        in_specs=[
            pl.BlockSpec((tm, dm), lambda i: (i, 0)),
            pl.BlockSpec((1, dm), const),
            pl.BlockSpec((2, dm, LANES), lambda i: (0, 0, 0)),
            pl.BlockSpec((tm, tm), const),
        ],
        out_specs=(
            pl.BlockSpec((tm, dm // 2), lambda i: (i, 0)),
            pl.BlockSpec((tm, LANES), lambda i: (i, 0)),
            pl.BlockSpec((1, LANES), const),
        ),
        scratch_shapes=[pltpu.VMEM((1, LANES), F32)],
        compiler_params=_cparams(("arbitrary",)),
        name="moe_router",
    )(x2, g, wr2, tri)


def _row_copy(src_ref, src_row, dst_ref, dst_row, sem):
    return pltpu.make_async_copy(src_ref.at[pl.ds(src_row, 1)], dst_ref.at[pl.ds(dst_row, 1)], sem)


def _dispatch_kernel(d1_ref, d2_ref, cnt_ref, h_ref, xs_ref, zero_ref, sem, tail_sem, *, tm, cap, nexp, tail):
    def issue(r, c):
        _row_copy(h_ref, r, xs_ref, d1_ref[0, 0, r], sem).start()
        _row_copy(h_ref, r, xs_ref, d2_ref[0, 0, r], sem).start(priority=1)
        return c

    lax.fori_loop(0, tm, issue, 0, unroll=8)

    def aligned_end(e):
        return ((cnt_ref[e] + (SUBLANES - 1)) // SUBLANES) * SUBLANES

    def tail_copy(e, k):
        row0 = pl.multiple_of(e * cap + aligned_end(e) + k * tm, SUBLANES)
        return pltpu.make_async_copy(zero_ref, xs_ref.at[pl.ds(row0, tm)], tail_sem)

    def tail_rows(e, fn):
        for k in range(SUBLANES - 1):
            @pl.when(cnt_ref[e] + k < aligned_end(e))
            def _():
                fn(_row_copy(zero_ref, 0, xs_ref, e * cap + cnt_ref[e] + k, tail_sem))

    @pl.when(pl.program_id(0) == pl.num_programs(0) - 1)
    def _():
        zero_ref[...] = jnp.zeros_like(zero_ref)
        for e in range(nexp):
            for k in range(tail // tm):
                tail_copy(e, k).start()
            tail_rows(e, lambda cp: cp.start())
        for e in range(nexp):
            for k in range(tail // tm):
                tail_copy(e, k).wait()
            tail_rows(e, lambda cp: cp.wait())

    for _ in range(2):
        pltpu.make_async_copy(h_ref, xs_ref.at[pl.ds(0, tm)], sem).wait()


def _dispatch(d1, d2, cnt, h, cap, tail):
    n, wd = h.shape
    tm = TM_PROJ
    nt = n // tm
    idx_spec = pl.BlockSpec((1, 1, tm), lambda i: (i, 0, 0), memory_space=pltpu.SMEM)
    return pl.pallas_call(
        functools.partial(_dispatch_kernel, tm=tm, cap=cap, nexp=N_EXPERTS, tail=tail),
        out_shape=jax.ShapeDtypeStruct((N_EXPERTS * cap, wd), h.dtype),
        grid=(nt,),
        in_specs=[idx_spec, idx_spec, pl.BlockSpec(memory_space=pltpu.SMEM),
                  pl.BlockSpec((tm, wd), lambda i: (i, 0))],
        out_specs=pl.BlockSpec(memory_space=pl.ANY),
        scratch_shapes=[pltpu.VMEM((tm, wd), h.dtype), pltpu.SemaphoreType.DMA, pltpu.SemaphoreType.DMA],
        compiler_params=pltpu.CompilerParams(dimension_semantics=("arbitrary",),
                                             vmem_limit_bytes=VMEM_LIMIT_BYTES, disable_bounds_checks=True),
        name="moe_dispatch",
    )(d1.reshape(nt, 1, tm), d2.reshape(nt, 1, tm), cnt, h)


def _combine_kernel(d1_ref, d2_ref, n1_ref, n2_ref, x_ref, route_ref, ys_ref, o_ref, y1_ref, y2_ref, sems, *, tm):
    i = pl.program_id(0)
    slot = jnp.bitwise_and(i, 1)

    def fetch(a_ref, b_ref, s):
        def issue(r, c):
            _row_copy(ys_ref, a_ref[0, 0, r], y1_ref.at[s], r, sems.at[s]).start()
            _row_copy(ys_ref, b_ref[0, 0, r], y2_ref.at[s], r, sems.at[s]).start(priority=1)
            return c

        lax.fori_loop(0, tm, issue, 0, unroll=8)

    @pl.when(i == 0)
    def _():
        fetch(d1_ref, d2_ref, slot)

    @pl.when(i + 1 < pl.num_programs(0))
    def _():
        fetch(n1_ref, n2_ref, 1 - slot)

    for buf in (y1_ref, y2_ref):
        pltpu.make_async_copy(ys_ref.at[pl.ds(0, tm)], buf.at[slot], sems.at[slot]).wait()
    o_ref[...] = x_ref[...] + route_ref[:, 2:3] * y1_ref[slot] + route_ref[:, 3:4] * y2_ref[slot]


def _combine(d1, d2, x2, route, ys):
    n, dm = x2.shape
    tm = TM_PROJ
    nt = n // tm
    idx_spec = pl.BlockSpec((1, 1, tm), lambda i: (i, 0, 0), memory_space=pltpu.SMEM)
    next_spec = pl.BlockSpec((1, 1, tm), lambda i: (jnp.minimum(i + 1, nt - 1), 0, 0), memory_space=pltpu.SMEM)
    d1, d2 = d1.reshape(nt, 1, tm), d2.reshape(nt, 1, tm)
    return pl.pallas_call(
        functools.partial(_combine_kernel, tm=tm),
        out_shape=jax.ShapeDtypeStruct((n, dm), F32),
        grid=(nt,),
        in_specs=[idx_spec, idx_spec, next_spec, next_spec, pl.BlockSpec((tm, dm), lambda i: (i, 0)),
                  pl.BlockSpec((tm, LANES), lambda i: (i, 0)), pl.BlockSpec(memory_space=pl.ANY)],
        out_specs=pl.BlockSpec((tm, dm), lambda i: (i, 0)),
        scratch_shapes=[pltpu.VMEM((2, tm, dm), F32), pltpu.VMEM((2, tm, dm), F32), pltpu.SemaphoreType.DMA((2,))],
        compiler_params=pltpu.CompilerParams(dimension_semantics=("arbitrary",),
                                             vmem_limit_bytes=VMEM_LIMIT_BYTES, disable_bounds_checks=True),
        name="moe_combine",
    )(d1, d2, d1, d2, x2, route, ys)


def _experts_kernel(te_ref, blk_ref, nused_ref, xs_ref, w1_ref, w3_ref, w2_ref, o_ref, x_ref, hid_ref, acc_ref):
    t = pl.program_id(0)
    c = pl.program_id(1)
    nc = pl.num_programs(1)

    @pl.when(jnp.logical_and(t < nused_ref[0], c == 0))
    def _():
        w = xs_ref[...]
        half = w.shape[1]
        x_ref[:, :half] = lax.bitcast_convert_type(w & jnp.uint32(0xFFFF0000), F32).astype(BF16)
        x_ref[:, half:] = lax.bitcast_convert_type(lax.shift_left(w, jnp.uint32(16)), F32).astype(BF16)

    @pl.when(t < nused_ref[0])
    def _():
        x = x_ref[...]
        tf = w1_ref.shape[2]
        for s0 in range(0, tf, GROUP_W):
            cs = slice(s0, s0 + GROUP_W)
            a = _dot(x, w1_ref[0, :, cs])
            b = _dot(x, w3_ref[0, :, cs])
            hid_ref[:, cs] = (jax.nn.silu(a) * b).astype(BF16)
        part = _dot(hid_ref[...], w2_ref[0])

        @pl.when(c == 0)
        def _():
            acc_ref[...] = part

        @pl.when(c > 0)
        def _():
            acc_ref[...] = acc_ref[...] + part

        @pl.when(c == nc - 1)
        def _():
            o_ref[...] = acc_ref[...]

    @pl.when(jnp.logical_and(t >= nused_ref[0], c == nc - 1))
    def _():
        o_ref[...] = jnp.zeros_like(o_ref)


def _experts(tile_expert, tile_block, nused, xs, w1, w3, w2, nt):
    rows, wd = xs.shape
    dm = 2 * wd
    dff = w1.shape[2]
    tm, tf = TM_EXPERT, TF_EXPERT
    nc = dff // tf
    spare = rows // tm

    def cc(t, c, nu):
        return jnp.where(t < nu[0], c, nc - 1)

    grid_spec = pltpu.PrefetchScalarGridSpec(
        num_scalar_prefetch=3,
        grid=(nt, nc),
        in_specs=[
            pl.BlockSpec((tm, wd), lambda t, c, te, tb, nu: (tb[t], 0)),
            pl.BlockSpec((1, dm, tf), lambda t, c, te, tb, nu: (te[t], 0, cc(t, c, nu))),
            pl.BlockSpec((1, dm, tf), lambda t, c, te, tb, nu: (te[t], 0, cc(t, c, nu))),
            pl.BlockSpec((1, tf, dm), lambda t, c, te, tb, nu: (te[t], cc(t, c, nu), 0)),
        ],
        out_specs=pl.BlockSpec((tm, dm), lambda t, c, te, tb, nu: (jnp.where(t < nu[0], tb[t], spare), 0)),
        scratch_shapes=[pltpu.VMEM((tm, dm), BF16), pltpu.VMEM((tm, tf), BF16), pltpu.VMEM((tm, dm), F32)],
    )
    return pl.pallas_call(
        _experts_kernel,
        out_shape=jax.ShapeDtypeStruct((rows + tm, dm), F32),
        grid_spec=grid_spec,
        compiler_params=_cparams(("arbitrary", "arbitrary")),
        name="moe_experts",
    )(tile_expert, tile_block, nused, xs, w1, w3, w2)


def _moe(x2, g, w_router, w1, w3, w2, layer_idx, tri):
    n, dm = x2.shape
    wr = jnp.zeros((dm, LANES), F32).at[:, :N_EXPERTS].set(w_router.astype(F32))
    wr_hi = wr.astype(BF16)
    wr_lo = (wr - wr_hi.astype(F32)).astype(BF16)
    h, route, counts = _router(x2, g, jnp.stack([wr_hi, wr_lo]), tri)

    tm = TM_EXPERT
    assert n % tm == 0 and tm % TM_PROJ == 0
    cap = n + tm
    d1 = route[:, 0].astype(jnp.int32) * cap + route[:, 4].astype(jnp.int32)
    d2 = route[:, 1].astype(jnp.int32) * cap + route[:, 5].astype(jnp.int32)
    cnt = counts[0, :N_EXPERTS].astype(jnp.int32)

    nt = 2 * n // tm + N_EXPERTS
    ntile = (cnt + tm - 1) // tm
    ends = jnp.cumsum(ntile)
    nused = ends[-1]
    t = jnp.minimum(jnp.arange(nt, dtype=jnp.int32), nused - 1)
    tile_expert = jnp.sum(t[:, None] >= ends[None, :], axis=1).astype(jnp.int32)
    tile_block = tile_expert * (cap // tm) + t - (ends - ntile)[tile_expert]

    xs = _dispatch(d1, d2, cnt, h, cap, tm)
    ys = _experts(tile_expert + layer_idx * N_EXPERTS, tile_block.astype(jnp.int32),
                  nused.reshape(1).astype(jnp.int32), xs, w1, w3, w2, nt)
    return _combine(d1, d2, x2, route, ys)


def kernel(x, norm_mix_g, w_in, ssm_lam_re, ssm_lam_im, ssm_log_dt, ssm_b_re, ssm_b_im, ssm_c_re, ssm_c_im,
           ssm_d, ssm_w_glu, ssm_b_glu, ch_q_norm_g, ch_k_norm_g, ch_rel_bias, df_q_norm_g, df_k_norm_g,
           df_lambda, out_norm_g, w_out, norm_ffn_g, ffn_w1, ffn_w3, ffn_w2, moe_router, moe_w1, moe_w3, moe_w2):
    bsz, seq, dm = x.shape
    depth = w_in.shape[0]
    n = bsz * seq
    nheads = GROUP_W // HEAD_DIM
    ind64 = _block_indicator(GROUP_W, HEAD_DIM)
    ind32 = _block_indicator(GROUP_W, DF_QK_DIM)
    tri = (jnp.arange(TM_PROJ)[:, None] > jnp.arange(TM_PROJ)[None, :]).astype(BF16)
    slopes = [2.0 ** (-8.0 * (h + 1) / nheads) for h in range(nheads)]
    slopes2 = tuple(s * LOG2E for s in slopes)
    df_kb, df_db = _df_bias_tiles(slopes2, TQ_ATT, TK_ATT)
    later = (jnp.arange(TK_ATT)[:, None] < jnp.arange(TK_ATT)[None, :]).astype(BF16)

    moe_w = tuple(w.reshape((-1,) + w.shape[2:]).astype(BF16) for w in (moe_w1, moe_w3, moe_w2))

    x2 = x.reshape(n, dm)
    for layer in range(depth):
        qkg = jnp.stack([
            jnp.tile(ch_q_norm_g[layer].astype(F32), nheads) * (HEAD_DIM ** -0.5 * LOG2E),
            jnp.tile(ch_k_norm_g[layer].astype(F32), nheads),
            jnp.tile(df_q_norm_g[layer].astype(F32).reshape(-1), nheads) * (DF_QK_DIM ** -0.5 * LOG2E),
            jnp.tile(df_k_norm_g[layer].astype(F32).reshape(-1), nheads),
        ])
        u_tm, qk, v_sb, v_ch, v_df = _in_proj(x2, norm_mix_g[layer].reshape(1, dm), w_in[layer].astype(BF16), qkg,
                             ind64, ind32, bsz, seq)

        wb, lre, lim, wc = _s5_params(ssm_lam_re[layer], ssm_lam_im[layer], ssm_log_dt[layer],
                                      ssm_b_re[layer], ssm_b_im[layer], ssm_c_re[layer], ssm_c_im[layer], bsz)
        y_ssm = _s5(u_tm, wb, lre, lim, wc, ssm_d[layer].reshape(1, GROUP_W),
                    ssm_w_glu[layer].astype(BF16), ssm_b_glu[layer].reshape(1, GROUP_W), bsz)

        tq, tk = TQ_ATT, TK_ATT
        const3 = lambda b, i: (0, 0, 0)
        y_sb = _attention_call(
            functools.partial(_sb_kernel, tq=tq, tk=tk, nheads=nheads),
            qk, 0, v_sb,
            (later,), [pl.BlockSpec((tk, tk), lambda b, i: (0, 0))],
            [pltpu.VMEM((nheads * tq, GROUP_W), BF16), pltpu.VMEM((SUBLANES, tq), F32),
             pltpu.VMEM((GROUP_W, tq), F32),
             pltpu.VMEM((tk, nheads * tq), F32), pltpu.VMEM((tk, nheads * tq), F32)],
            bsz, seq, "sb_attention")

        bias, nback = _ch_bias_tiles(ch_rel_bias[layer], tq, tk)
        y_ch = _attention_call(
            functools.partial(_ch_kernel, tq=tq, tk=tk, nheads=nheads, nback=nback),
            qk, 2, v_ch,
            (bias,), [pl.BlockSpec(bias.shape, lambda b, i: (0, 0, 0, 0))],
            [pltpu.VMEM((nheads * tq, GROUP_W), BF16), pltpu.VMEM(((nback + 1) * tk, nheads * tq), F32)],
            bsz, seq, "ch_attention")

        lambda_init = 0.8 - 0.6 * math.exp(-0.3 * layer)
        lam_p = df_lambda[layer].astype(F32)
        lam = jnp.exp(jnp.sum(lam_p[0] * lam_p[1])) - jnp.exp(jnp.sum(lam_p[2] * lam_p[3])) + lambda_init
        y_df = _attention_call(
            functools.partial(_df_kernel, tq=tq, tk=tk, nheads=nheads, slopes2=slopes2),
            qk, 4, v_df,
            (df_kb, df_db), [pl.BlockSpec(df_kb.shape, const3), pl.BlockSpec(df_db.shape, const3)],
            [pltpu.VMEM((2 * nheads * tq, GROUP_W), BF16), pltpu.VMEM((2 * nheads, tq), F32),
             pltpu.VMEM((2 * nheads * VA_ROWS, tq), F32),
             pltpu.VMEM((tk, 2 * nheads * tq), F32), pltpu.VMEM((tk, 2 * nheads * tq), F32),
             pltpu.VMEM((2 * nheads, tq), F32), pltpu.VMEM((2 * nheads, tq), F32)],
            bsz, seq, "df_attention", smem=(lam.reshape(1),))

        head_scale = jnp.concatenate([jnp.ones((dm - GROUP_W,), F32),
                                      jnp.full((GROUP_W,), 1.0 - lambda_init, F32)])
        g_out = (out_norm_g[layer].astype(F32) * head_scale).reshape(1, dm)
        mixed = (y_ssm, y_sb, y_ch, y_df, g_out, ind64, w_out[layer].astype(BF16), x2)

        idx = layer // 2
        g_ffn = norm_ffn_g[layer].reshape(1, dm)
        if layer % 2 == 0:
            x2 = _mix_ffn(*mixed, g_ffn, ffn_w1[idx].astype(BF16), ffn_w3[idx].astype(BF16),
                          ffn_w2[idx].astype(BF16), bsz, seq)
        else:
            x2 = _out_proj(*mixed, bsz, seq)
            x2 = _moe(x2, g_ffn, moe_router[idx], *moe_w, idx, tri)
    return x2.reshape(bsz, seq, dm)
```
